```python
import math
import jax
import jax.numpy as jnp
from jax import lax
import numpy as np

D_MODEL = 1024
BATCH = 32
SEQ = 256
DEPTH = 4
DEC_BATCH = 8
DEC_SEQ = 4096
PAST_LEN = 512

GRID_W = 64
N_AH = (DEPTH + 1) // 2
N_DN = DEPTH // 2
H_A = 8
KVH_A = 2
G_A = H_A // KVH_A
HD_A = 64
WINDOW = 128
ATTN_BLOCK = 128
ROPE_THETA = 10000.0
ROPE_AXIS_DIM = HD_A // 2
HY_C = 512
HY_ORDER = 2
SHORT_W = 3
HY_EMB = 33
HY_BANDS = (HY_EMB - 1) // 2
HY_FORD = 64
HY_MIN_DECAY = math.log(1e-2) / 1.5
HY_MAX_DECAY = math.log(1e-2) / 0.3
AH_IN = H_A * HD_A + 2 * KVH_A * HD_A + 3 * HY_C
AH_OUT = H_A * HD_A + HY_C
H_C = 8
DK_C = 128
DV_C = 128
DN_CHUNK = 64
DN_IN = 2 * H_C * DK_C + 2 * H_C * DV_C + 4 * H_C
DN_OUT = H_C * DV_C
D_FF = 2816
NORM_EPS = 1e-6
NEG_INF = -1e30

kernel_name = 'hybrid_diffusion_swa_hyena_gdn_step'


def rms_norm(x, g):
    xf = x.astype(jnp.float32)
    y = xf * lax.rsqrt(jnp.mean(xf * xf, axis=-1, keepdims=True) + NORM_EPS)
    return (y * g.astype(jnp.float32)).astype(x.dtype)


def l2_norm(x):
    return x * lax.rsqrt(jnp.sum(x * x, axis=-1, keepdims=True) + NORM_EPS)


def ada_modulation(cond, w, b):
    m = jax.nn.silu(cond) @ w + b
    return m.reshape(cond.shape[0], 1, 9, D_MODEL)


def ada_norm(x, g, mod, j):
    return rms_norm(x, g) * (1 + mod[:, :, 3 * j + 1]) + mod[:, :, 3 * j]


def half_ffn(x, g, mod, j, w13, w2):
    gt, up = jnp.split(ada_norm(x, g, mod, j) @ w13, 2, axis=-1)
    return x + 0.5 * mod[:, :, 3 * j + 2] * ((jax.nn.silu(gt) * up) @ w2)


def depthwise_conv(x, w):
    k = w.shape[0]
    return lax.conv_general_dilated(x, w.astype(x.dtype)[:, None, :], window_strides=(1,),
                                    padding=[(k // 2, k // 2)],
                                    dimension_numbers=('NWC', 'WIO', 'NWC'),
                                    feature_group_count=x.shape[-1])


def axial_rope(length):
    rows = length // GRID_W
    r, col = jnp.meshgrid(jnp.arange(rows), jnp.arange(GRID_W), indexing='ij')
    inv = ROPE_THETA ** (-jnp.arange(0, ROPE_AXIS_DIM, 2, dtype=jnp.float32) / ROPE_AXIS_DIM)
    ang = jnp.concatenate([r.reshape(-1, 1).astype(jnp.float32) * inv,
                           col.reshape(-1, 1).astype(jnp.float32) * inv], axis=-1)
    return jnp.cos(ang), jnp.sin(ang)


def apply_rope(x, cos, sin):
    half = x.shape[-1] // 2
    shape = (1, x.shape[1]) + (1,) * (x.ndim - 3) + (half,)
    c = cos.reshape(shape).astype(x.dtype)
    s = sin.reshape(shape).astype(x.dtype)
    x1, x2 = x[..., :half], x[..., half:]
    return jnp.concatenate([x1 * c - x2 * s, x2 * c + x1 * s], axis=-1)


def sink_softmax(logits, sink):
    s = jnp.broadcast_to(sink.astype(jnp.float32).reshape(1, KVH_A, G_A, 1, 1), logits.shape[:-1] + (1,))
    return jax.nn.softmax(jnp.concatenate([s, logits], axis=-1), axis=-1)[..., 1:]


def context_attention(q, k, v, sink):
    b, length = q.shape[:2]
    qb = jnp.moveaxis(q.reshape(b, length // ATTN_BLOCK, ATTN_BLOCK, KVH_A, G_A, HD_A), 1, 0) * (HD_A ** -0.5)

    def block(qi):
        s = jnp.einsum('bqkgd,bskd->bkgqs', qi, k).astype(jnp.float32)
        p = sink_softmax(s, sink).astype(v.dtype)
        return jnp.einsum('bkgqs,bskd->bqkgd', p, v)

    out = lax.map(block, qb)
    return jnp.moveaxis(out, 0, 1).reshape(b, length, H_A * HD_A)


def window_attention(q, k, v, ck, cv, sink):
    b, length = q.shape[:2]
    nb = length // ATTN_BLOCK
    n_ctx = ck.shape[1]
    pad = ((0, 0), (ATTN_BLOCK, ATTN_BLOCK), (0, 0), (0, 0))
    kp = jnp.pad(k, pad)
    vp = jnp.pad(v, pad)
    qb = jnp.moveaxis(q.reshape(b, nb, ATTN_BLOCK, KVH_A, G_A, HD_A), 1, 0) * (HD_A ** -0.5)
    q_off = jnp.arange(ATTN_BLOCK)
    k_off = jnp.arange(3 * ATTN_BLOCK)

    def block(args):
        i, qi = args
        ks = lax.dynamic_slice_in_dim(kp, i * ATTN_BLOCK, 3 * ATTN_BLOCK, axis=1)
        vs = lax.dynamic_slice_in_dim(vp, i * ATTN_BLOCK, 3 * ATTN_BLOCK, axis=1)
        q_pos = i * ATTN_BLOCK + q_off
        k_pos = (i - 1) * ATTN_BLOCK + k_off
        valid = ((jnp.abs(q_pos[:, None] - k_pos[None, :]) <= WINDOW)
                 & (k_pos >= 0)[None, :] & (k_pos < length)[None, :])
        s_loc = jnp.where(valid, jnp.einsum('bqkgd,bskd->bkgqs', qi, ks).astype(jnp.float32), NEG_INF)
        s_ctx = jnp.einsum('bqkgd,bskd->bkgqs', qi, ck).astype(jnp.float32)
        p = sink_softmax(jnp.concatenate([s_ctx, s_loc], axis=-1), sink).astype(v.dtype)
        return (jnp.einsum('bkgqs,bskd->bqkgd', p[..., :n_ctx], cv)
                + jnp.einsum('bkgqs,bskd->bqkgd', p[..., n_ctx:], vs))

    out = lax.map(block, (jnp.arange(nb), qb))
    return jnp.moveaxis(out, 0, 1).reshape(b, length, H_A * HD_A)


def hyena_filter(length, w1, b1, f1, w2, b2, f2, w3):
    t = jnp.linspace(0.0, 1.0, length, dtype=jnp.float32)[:, None]
    w = 2 * math.pi * jnp.arange(length, dtype=jnp.float32)[:, None] / length
    f = jnp.linspace(1e-4, HY_BANDS - 1, HY_BANDS, dtype=jnp.float32)[None, :]
    z = jnp.concatenate([t, jnp.cos(f * w), -jnp.sin(f * w)], axis=-1)
    h = jnp.sin(f1.astype(jnp.float32) * (z @ w1.astype(jnp.float32) + b1.astype(jnp.float32)))
    h = jnp.sin(f2.astype(jnp.float32) * (h @ w2.astype(jnp.float32) + b2.astype(jnp.float32)))
    h = h @ w3.astype(jnp.float32)
    deltas = jnp.abs(jnp.linspace(HY_MIN_DECAY, HY_MAX_DECAY, h.shape[-1], dtype=jnp.float32))
    h = (h * jnp.exp(-t * deltas)).reshape(length, 2, HY_ORDER, HY_C)
    kern = jnp.concatenate([h[:, 0], jnp.zeros((1, HY_ORDER, HY_C), jnp.float32), h[:0:-1, 1]], axis=0)
    return kern * lax.rsqrt(jnp.sum(kern * kern, axis=0, keepdims=True) + NORM_EPS)


def hyena_mixer(u3, conv_w, conv_b, w1, b1, f1, w2, b2, f2, w3, bias):
    length = u3.shape[1]
    u3 = depthwise_conv(u3, conv_w) + conv_b
    x1, x2, v = jnp.split(u3, 3, axis=-1)
    kf = jnp.fft.rfft(hyena_filter(length, w1, b1, f1, w2, b2, f2, w3), n=2 * length, axis=0)
    z = v.astype(jnp.float32)
    for o, gate in enumerate((x1, x2)):
        y = jnp.fft.irfft(jnp.fft.rfft(z, n=2 * length, axis=1) * kf[None, :, o], n=2 * length, axis=1)[:, :length]
        z = gate.astype(jnp.float32) * (y + z * bias[o].astype(jnp.float32))
    return z.astype(u3.dtype)


def attn_hyena_mixer(h, ctx_kv, w_in, w_out, q_norm, k_norm, sink, conv_w, conv_b,
                     w1, b1, f1, w2, b2, f2, w3, hy_bias):
    b, length, _ = h.shape
    q, k, v, u3 = jnp.split(h @ w_in, [H_A * HD_A, H_A * HD_A + KVH_A * HD_A,
                                       H_A * HD_A + 2 * KVH_A * HD_A], axis=-1)
    q = rms_norm(q.reshape(b, length, KVH_A, G_A, HD_A), q_norm)
    k = rms_norm(k.reshape(b, length, KVH_A, HD_A), k_norm)
    v = v.reshape(b, length, KVH_A, HD_A)
    if ctx_kv is None:
        a = context_attention(q, k, v, sink)
    else:
        cos, sin = axial_rope(length)
        a = window_attention(apply_rope(q, cos, sin), apply_rope(k, cos, sin), v, ctx_kv[0], ctx_kv[1], sink)
    y = hyena_mixer(u3, conv_w, conv_b, w1, b1, f1, w2, b2, f2, w3, hy_bias)
    return jnp.concatenate([a, y], axis=-1) @ w_out, k, v


def chunk_gated_delta_rule(q, k, v, g, beta, s0):
    b, length, h, dk = q.shape
    dv = v.shape[-1]
    n = length // DN_CHUNK

    def chunks(t):
        t = t.reshape((b, n, DN_CHUNK, h) + t.shape[3:])
        return jnp.moveaxis(t, (1, 3), (0, 2))

    qc, kc, vc = chunks(q * (dk ** -0.5)), chunks(k), chunks(v)
    gcum = jnp.cumsum(chunks(g), axis=-1)
    bc = chunks(beta)[..., None]
    idx = jnp.arange(DN_CHUNK)
    lower = idx[:, None] >= idx[None, :]
    decay = jnp.where(lower, jnp.exp(jnp.where(lower, gcum[..., :, None] - gcum[..., None, :], 0.0)), 0.0)
    kb = kc * bc
    lmat = jnp.where(idx[:, None] > idx[None, :], jnp.einsum('nbhik,nbhjk->nbhij', kb, kc) * decay, 0.0)
    rhs = jnp.concatenate([vc * bc, kb * jnp.exp(gcum)[..., None]], axis=-1)
    sol = lax.linalg.triangular_solve(lmat + jnp.eye(DN_CHUNK, dtype=jnp.float32), rhs,
                                      left_side=True, lower=True, unit_diagonal=True)
    u, w = sol[..., :dv], sol[..., dv:]

    def step(s, inp):
        qi, ki, ui, wi, gi, di = inp
        v_new = ui - jnp.einsum('bhck,bhkv->bhcv', wi, s)
        o = (jnp.einsum('bhck,bhkv->bhcv', qi * jnp.exp(gi)[..., None], s)
             + jnp.einsum('bhij,bhjv->bhiv', jnp.einsum('bhik,bhjk->bhij', qi, ki) * di, v_new))
        gl = gi[..., -1]
        s = (s * jnp.exp(gl)[..., None, None]
             + jnp.einsum('bhck,bhcv->bhkv', ki * jnp.exp(gl[..., None] - gi)[..., None], v_new))
        return s, o

    s_final, o = lax.scan(step, s0, (qc, kc, u, w, gcum, decay))
    o = jnp.moveaxis(o, (0, 2), (1, 3)).reshape(b, length, h, dv)
    return o, s_final


def deltanet_mixer(h, s0_f, s0_b, w_in, w_out, conv_w, a_log, dt_bias, norm_g):
    b, length, _ = h.shape
    nq = H_C * DK_C
    qkv, z, bl, al = jnp.split(h @ w_in, [2 * nq + H_C * DV_C, 2 * nq + 2 * H_C * DV_C,
                                          2 * nq + 2 * H_C * DV_C + 2 * H_C], axis=-1)
    qkv = jax.nn.silu(depthwise_conv(qkv, conv_w)).astype(jnp.float32)
    q, k, v = jnp.split(qkv, [nq, 2 * nq], axis=-1)
    q = l2_norm(q.reshape(b, length, H_C, DK_C))
    k = l2_norm(k.reshape(b, length, H_C, DK_C))
    v = v.reshape(b, length, H_C, DV_C)
    beta = jax.nn.sigmoid(bl.astype(jnp.float32)).reshape(b, length, 2, H_C)
    g = -jnp.exp(a_log.astype(jnp.float32)) * jax.nn.softplus(
        al.astype(jnp.float32).reshape(b, length, 2, H_C) + dt_bias.astype(jnp.float32))
    o_f, s_f = chunk_gated_delta_rule(q, k, v, g[:, :, 0], beta[:, :, 0], s0_f.astype(jnp.float32))
    o_b, s_b = chunk_gated_delta_rule(q[:, ::-1], k[:, ::-1], v[:, ::-1], g[:, ::-1, 1], beta[:, ::-1, 1],
                                      s0_b.astype(jnp.float32))
    o = rms_norm(o_f + o_b[:, ::-1], norm_g) * jax.nn.silu(z.astype(jnp.float32).reshape(b, length, H_C, DV_C))
    return o.reshape(b, length, DN_OUT).astype(h.dtype) @ w_out, s_f, s_b


def setup_inputs(seed: int = 0) -> dict:
    key = jax.random.key(seed)
    keys = iter(jax.random.split(key, 48))

    def nrm(shape, scale=1.0):
        return scale * jax.random.normal(next(keys), shape, jnp.float32)

    def gain(shape):
        return 1.0 + nrm(shape, 0.05)

    x_prompt = nrm((BATCH, SEQ, D_MODEL))
    x_sample = nrm((DEC_BATCH, DEC_SEQ, D_MODEL))
    cache_k = nrm((DEC_BATCH, N_AH, PAST_LEN, KVH_A, HD_A))
    cache_v = nrm((DEC_BATCH, N_AH, PAST_LEN, KVH_A, HD_A))
    state_fwd = nrm((DEC_BATCH, N_DN, H_C, DK_C, DV_C), 0.5)
    state_bwd = nrm((DEC_BATCH, N_DN, H_C, DK_C, DV_C), 0.5)
    c = nrm((DEC_BATCH, D_MODEL))
    c_ctx = nrm((D_MODEL,))
    norm_g = gain((DEPTH, 3, D_MODEL))
    ada_w = nrm((DEPTH, D_MODEL, 9 * D_MODEL), 0.5 * D_MODEL ** -0.5)
    ada_b = nrm((DEPTH, 9 * D_MODEL), 0.02)
    ffn_w13 = nrm((DEPTH, 2, D_MODEL, 2 * D_FF), D_MODEL ** -0.5)
    ffn_w2 = nrm((DEPTH, 2, D_FF, D_MODEL), D_FF ** -0.5)
    mx_w_in = nrm((N_AH, D_MODEL, AH_IN), D_MODEL ** -0.5)
    mx_w_out = nrm((N_AH, AH_OUT, D_MODEL), AH_OUT ** -0.5)
    q_norm = gain((N_AH, HD_A))
    k_norm = gain((N_AH, HD_A))
    attn_sink = nrm((N_AH, H_A), 0.5)
    hy_conv_w = nrm((N_AH, SHORT_W, 3 * HY_C), SHORT_W ** -0.5)
    hy_conv_b = nrm((N_AH, 3 * HY_C), 0.02)
    hy_w1 = nrm((N_AH, HY_EMB, HY_FORD), HY_EMB ** -0.5)
    hy_b1 = nrm((N_AH, HY_FORD), 0.1)
    hy_freq1 = gain((N_AH, HY_FORD))
    hy_w2 = nrm((N_AH, HY_FORD, HY_FORD), HY_FORD ** -0.5)
    hy_b2 = nrm((N_AH, HY_FORD), 0.1)
    hy_freq2 = gain((N_AH, HY_FORD))
    hy_w3 = nrm((N_AH, HY_FORD, 2 * HY_ORDER * HY_C), HY_FORD ** -0.5)
    hy_bias = nrm((N_AH, HY_ORDER, HY_C))
    dn_w_in = nrm((N_DN, D_MODEL, DN_IN), D_MODEL ** -0.5)
    dn_w_out = nrm((N_DN, DN_OUT, D_MODEL), DN_OUT ** -0.5)
    dn_conv_w = nrm((N_DN, SHORT_W, 2 * H_C * DK_C + H_C * DV_C), SHORT_W ** -0.5)
    dn_a_log = jnp.log(jax.random.uniform(next(keys), (N_DN, 2, H_C), jnp.float32, 1.0, 16.0))
    dt = jax.random.uniform(next(keys), (N_DN, 2, H_C), jnp.float32, 0.001, 0.1)
    dn_dt_bias = dt + jnp.log(-jnp.expm1(-dt))
    dn_norm_g = gain((N_DN, DV_C))
    return {'x_prompt': x_prompt, 'x_sample': x_sample, 'cache_k': cache_k, 'cache_v': cache_v,
            'state_fwd': state_fwd, 'state_bwd': state_bwd, 'c': c, 'c_ctx': c_ctx,
            'norm_g': norm_g, 'ada_w': ada_w, 'ada_b': ada_b, 'ffn_w13': ffn_w13, 'ffn_w2': ffn_w2,
            'mx_w_in': mx_w_in, 'mx_w_out': mx_w_out, 'q_norm': q_norm, 'k_norm': k_norm,
            'attn_sink': attn_sink, 'hy_conv_w': hy_conv_w, 'hy_conv_b': hy_conv_b,
            'hy_w1': hy_w1, 'hy_b1': hy_b1, 'hy_freq1': hy_freq1, 'hy_w2': hy_w2, 'hy_b2': hy_b2,
            'hy_freq2': hy_freq2, 'hy_w3': hy_w3, 'hy_bias': hy_bias,
            'dn_w_in': dn_w_in, 'dn_w_out': dn_w_out, 'dn_conv_w': dn_conv_w,
            'dn_a_log': dn_a_log, 'dn_dt_bias': dn_dt_bias, 'dn_norm_g': dn_norm_g}


def reference(x_prompt, x_sample, cache_k, cache_v, state_fwd, state_bwd, c, c_ctx,
              norm_g, ada_w, ada_b, ffn_w13, ffn_w2, mx_w_in, mx_w_out, q_norm, k_norm,
              attn_sink, hy_conv_w, hy_conv_b, hy_w1, hy_b1, hy_freq1, hy_w2, hy_b2,
              hy_freq2, hy_w3, hy_bias, dn_w_in, dn_w_out, dn_conv_w, dn_a_log, dn_dt_bias,
              dn_norm_g):
    xp, xs = x_prompt, x_sample
    new_k, new_v, new_sf, new_sb = [], [], [], []
    zero_state = jnp.zeros((xp.shape[0], H_C, DK_C, DV_C), jnp.float32)

    def ah_mixer(h, ctx_kv, i):
        return attn_hyena_mixer(h, ctx_kv, mx_w_in[i], mx_w_out[i], q_norm[i], k_norm[i], attn_sink[i],
                                hy_conv_w[i], hy_conv_b[i], hy_w1[i], hy_b1[i], hy_freq1[i],
                                hy_w2[i], hy_b2[i], hy_freq2[i], hy_w3[i], hy_bias[i])

    def dn_mixer(h, s0_f, s0_b, i):
        return deltanet_mixer(h, s0_f, s0_b, dn_w_in[i], dn_w_out[i], dn_conv_w[i],
                              dn_a_log[i], dn_dt_bias[i], dn_norm_g[i])

    for layer in range(DEPTH):
        mp = ada_modulation(c_ctx[None, :], ada_w[layer], ada_b[layer])
        ms = ada_modulation(c, ada_w[layer], ada_b[layer])
        xp = half_ffn(xp, norm_g[layer, 0], mp, 0, ffn_w13[layer, 0], ffn_w2[layer, 0])
        xs = half_ffn(xs, norm_g[layer, 0], ms, 0, ffn_w13[layer, 0], ffn_w2[layer, 0])
        hp = ada_norm(xp, norm_g[layer, 1], mp, 1)
        hs = ada_norm(xs, norm_g[layer, 1], ms, 1)
        i = layer // 2
        if layer % 2 == 0:
            o_p, k_p, v_p = ah_mixer(hp, None, i)
            o_s, _, _ = ah_mixer(hs, (cache_k[:, i], cache_v[:, i]), i)
            new_k.append(k_p)
            new_v.append(v_p)
        else:
            o_p, s_f, s_b = dn_mixer(hp, zero_state, zero_state, i)
            o_s, _, _ = dn_mixer(hs, state_fwd[:, i], state_bwd[:, i], i)
            new_sf.append(s_f.astype(xp.dtype))
            new_sb.append(s_b.astype(xp.dtype))
        xp = xp + mp[:, :, 5] * o_p
        xs = xs + ms[:, :, 5] * o_s
        xp = half_ffn(xp, norm_g[layer, 2], mp, 2, ffn_w13[layer, 1], ffn_w2[layer, 1])
        xs = half_ffn(xs, norm_g[layer, 2], ms, 2, ffn_w13[layer, 1], ffn_w2[layer, 1])

    new_cache_k = jnp.stack(new_k, axis=1)
    new_cache_v = jnp.stack(new_v, axis=1)
    new_state_fwd = jnp.stack(new_sf, axis=1)
    new_state_bwd = jnp.stack(new_sb, axis=1)
    return (xp, xs, new_cache_k, new_cache_v, new_state_fwd, new_state_bwd)
```

```python
import functools
import math

import jax
import jax.numpy as jnp
from jax import lax
from jax.experimental import pallas as pl
from jax.experimental.pallas import tpu as pltpu

D_MODEL = 1024
DEPTH = 4
GRID_W = 64
H_A = 8
KVH_A = 2
G_A = H_A // KVH_A
HD_A = 64
WINDOW = 128
ATTN_BLOCK = 128
ROPE_THETA = 10000.0
ROPE_AXIS_DIM = HD_A // 2
HY_C = 512
HY_ORDER = 2
HY_EMB = 33
HY_BANDS = (HY_EMB - 1) // 2
HY_MIN_DECAY = math.log(1e-2) / 1.5
HY_MAX_DECAY = math.log(1e-2) / 0.3
H_C = 8
DK_C = 128
DV_C = 128
DN_CHUNK = 64
DN_OUT = H_C * DV_C
D_FF = 2816
NORM_EPS = 1e-6
NEG_INF = -1e30

F32 = jnp.float32
BF16 = jnp.bfloat16

FFN_TM = 512
FFN_CHUNK = 256
VMEM_LIMIT = 56 * 1024 * 1024


def _ffn_kernel(x_ref, mod_ref, g_ref, w1_ref, w3_ref, w2_ref, o_ref, acc_ref, *, j, n_chunks):
    x = x_ref[...]
    y = x * lax.rsqrt(jnp.mean(x * x, axis=-1, keepdims=True) + NORM_EPS) * g_ref[...]
    m = mod_ref[0]
    shift, scale, gate = m[3 * j:3 * j + 1], m[3 * j + 1:3 * j + 2], m[3 * j + 2:3 * j + 3]
    h = (y * (1.0 + scale) + shift).astype(BF16)
    acc_ref[...] = jnp.zeros_like(acc_ref)

    def body(c, carry):
        gt = jnp.dot(h, w1_ref[c], preferred_element_type=F32)
        up = jnp.dot(h, w3_ref[c], preferred_element_type=F32)
        a = (jax.nn.silu(gt) * up).astype(BF16)
        acc_ref[...] += jnp.dot(a, w2_ref[c], preferred_element_type=F32)
        return carry

    lax.fori_loop(0, n_chunks, body, 0)
    o_ref[...] = x + 0.5 * gate * acc_ref[...]


def _ffn(x, mod, row_of_tile, g, w13, w2, j):
    t = x.shape[0]
    n_chunks = D_FF // FFN_CHUNK
    w1 = w13[:, :D_FF].astype(BF16).reshape(D_MODEL, n_chunks, FFN_CHUNK).transpose(1, 0, 2)
    w3 = w13[:, D_FF:].astype(BF16).reshape(D_MODEL, n_chunks, FFN_CHUNK).transpose(1, 0, 2)
    w2c = w2.astype(BF16).reshape(n_chunks, FFN_CHUNK, D_MODEL)
    const3 = lambda i: (0, 0, 0)
    return pl.pallas_call(
        functools.partial(_ffn_kernel, j=j, n_chunks=n_chunks),
        out_shape=jax.ShapeDtypeStruct((t, D_MODEL), F32),
        grid=(t // FFN_TM,),
        in_specs=[
            pl.BlockSpec((FFN_TM, D_MODEL), lambda i: (i, 0)),
            pl.BlockSpec((1, 9, D_MODEL), lambda i: (row_of_tile(i), 0, 0)),
            pl.BlockSpec((1, D_MODEL), lambda i: (0, 0)),
            pl.BlockSpec((n_chunks, D_MODEL, FFN_CHUNK), const3),
            pl.BlockSpec((n_chunks, D_MODEL, FFN_CHUNK), const3),
            pl.BlockSpec((n_chunks, FFN_CHUNK, D_MODEL), const3),
        ],
        out_specs=pl.BlockSpec((FFN_TM, D_MODEL), lambda i: (i, 0)),
        scratch_shapes=[pltpu.VMEM((FFN_TM, D_MODEL), F32)],
        compiler_params=pltpu.CompilerParams(dimension_semantics=("arbitrary",),
                                             vmem_limit_bytes=VMEM_LIMIT),
        name=f"ffn{j}",
    )(x, mod, g.reshape(1, D_MODEL), w1, w3, w2c)


def _rms_norm(x, g):
    xf = x.astype(F32)
    y = xf * lax.rsqrt(jnp.mean(xf * xf, axis=-1, keepdims=True) + NORM_EPS)
    return (y * g.astype(F32)).astype(x.dtype)


def _l2_norm(x):
    return x * lax.rsqrt(jnp.sum(x * x, axis=-1, keepdims=True) + NORM_EPS)


def _ada_norm(x, g, mod, j):
    return _rms_norm(x, g) * (1 + mod[:, :, 3 * j + 1]) + mod[:, :, 3 * j]


def _depthwise_conv(x, w):
    k = w.shape[0]
    return lax.conv_general_dilated(x, w.astype(x.dtype)[:, None, :], window_strides=(1,),
                                    padding=[(k // 2, k // 2)],
                                    dimension_numbers=('NWC', 'WIO', 'NWC'),
                                    feature_group_count=x.shape[-1])


def _axial_rope(length):
    rows = length // GRID_W
    r, col = jnp.meshgrid(jnp.arange(rows), jnp.arange(GRID_W), indexing='ij')
    inv = ROPE_THETA ** (-jnp.arange(0, ROPE_AXIS_DIM, 2, dtype=F32) / ROPE_AXIS_DIM)
    ang = jnp.concatenate([r.reshape(-1, 1).astype(F32) * inv,
                           col.reshape(-1, 1).astype(F32) * inv], axis=-1)
    return jnp.cos(ang), jnp.sin(ang)


def _apply_rope(x, cos, sin):
    half = x.shape[-1] // 2
    shape = (1, x.shape[1]) + (1,) * (x.ndim - 3) + (half,)
    c = cos.reshape(shape).astype(x.dtype)
    s = sin.reshape(shape).astype(x.dtype)
    x1, x2 = x[..., :half], x[..., half:]
    return jnp.concatenate([x1 * c - x2 * s, x2 * c + x1 * s], axis=-1)


def _sink_softmax(logits, sink):
    s = jnp.broadcast_to(sink.astype(F32).reshape(1, KVH_A, G_A, 1, 1), logits.shape[:-1] + (1,))
    return jax.nn.softmax(jnp.concatenate([s, logits], axis=-1), axis=-1)[..., 1:]


def _context_attention(q, k, v, sink):
    b, length = q.shape[:2]
    qb = jnp.moveaxis(q.reshape(b, length // ATTN_BLOCK, ATTN_BLOCK, KVH_A, G_A, HD_A), 1, 0) * (HD_A ** -0.5)

    def block(qi):
        s = jnp.einsum('bqkgd,bskd->bkgqs', qi, k).astype(F32)
        p = _sink_softmax(s, sink).astype(v.dtype)
        return jnp.einsum('bkgqs,bskd->bqkgd', p, v)

    out = lax.map(block, qb)
    return jnp.moveaxis(out, 0, 1).reshape(b, length, H_A * HD_A)


def _window_attention(q, k, v, ck, cv, sink):
    b, length = q.shape[:2]
    nb = length // ATTN_BLOCK
    n_ctx = ck.shape[1]
    pad = ((0, 0), (ATTN_BLOCK, ATTN_BLOCK), (0, 0), (0, 0))
    kp = jnp.pad(k, pad)
    vp = jnp.pad(v, pad)
    qb = jnp.moveaxis(q.reshape(b, nb, ATTN_BLOCK, KVH_A, G_A, HD_A), 1, 0) * (HD_A ** -0.5)
    q_off = jnp.arange(ATTN_BLOCK)
    k_off = jnp.arange(3 * ATTN_BLOCK)

    def block(args):
        i, qi = args
        ks = lax.dynamic_slice_in_dim(kp, i * ATTN_BLOCK, 3 * ATTN_BLOCK, axis=1)
        vs = lax.dynamic_slice_in_dim(vp, i * ATTN_BLOCK, 3 * ATTN_BLOCK, axis=1)
        q_pos = i * ATTN_BLOCK + q_off
        k_pos = (i - 1) * ATTN_BLOCK + k_off
        valid = ((jnp.abs(q_pos[:, None] - k_pos[None, :]) <= WINDOW)
                 & (k_pos >= 0)[None, :] & (k_pos < length)[None, :])
        s_loc = jnp.where(valid, jnp.einsum('bqkgd,bskd->bkgqs', qi, ks).astype(F32), NEG_INF)
        s_ctx = jnp.einsum('bqkgd,bskd->bkgqs', qi, ck).astype(F32)
        p = _sink_softmax(jnp.concatenate([s_ctx, s_loc], axis=-1), sink).astype(v.dtype)
        return (jnp.einsum('bkgqs,bskd->bqkgd', p[..., :n_ctx], cv)
                + jnp.einsum('bkgqs,bskd->bqkgd', p[..., n_ctx:], vs))

    out = lax.map(block, (jnp.arange(nb), qb))
    return jnp.moveaxis(out, 0, 1).reshape(b, length, H_A * HD_A)


def _hyena_filter(length, w1, b1, f1, w2, b2, f2, w3):
    t = jnp.linspace(0.0, 1.0, length, dtype=F32)[:, None]
    w = 2 * math.pi * jnp.arange(length, dtype=F32)[:, None] / length
    f = jnp.linspace(1e-4, HY_BANDS - 1, HY_BANDS, dtype=F32)[None, :]
    z = jnp.concatenate([t, jnp.cos(f * w), -jnp.sin(f * w)], axis=-1)
    h = jnp.sin(f1.astype(F32) * (z @ w1.astype(F32) + b1.astype(F32)))
    h = jnp.sin(f2.astype(F32) * (h @ w2.astype(F32) + b2.astype(F32)))
    h = h @ w3.astype(F32)
    deltas = jnp.abs(jnp.linspace(HY_MIN_DECAY, HY_MAX_DECAY, h.shape[-1], dtype=F32))
    h = (h * jnp.exp(-t * deltas)).reshape(length, 2, HY_ORDER, HY_C)
    kern = jnp.concatenate([h[:, 0], jnp.zeros((1, HY_ORDER, HY_C), F32), h[:0:-1, 1]], axis=0)
    return kern * lax.rsqrt(jnp.sum(kern * kern, axis=0, keepdims=True) + NORM_EPS)


def _hyena_mixer(u3, conv_w, conv_b, w1, b1, f1, w2, b2, f2, w3, bias):
    length = u3.shape[1]
    u3 = _depthwise_conv(u3, conv_w) + conv_b
    x1, x2, v = jnp.split(u3, 3, axis=-1)
    kf = jnp.fft.rfft(_hyena_filter(length, w1, b1, f1, w2, b2, f2, w3), n=2 * length, axis=0)
    z = v.astype(F32)
    for o, gate in enumerate((x1, x2)):
        y = jnp.fft.irfft(jnp.fft.rfft(z, n=2 * length, axis=1) * kf[None, :, o], n=2 * length, axis=1)[:, :length]
        z = gate.astype(F32) * (y + z * bias[o].astype(F32))
    return z.astype(u3.dtype)


def _attn_hyena_mixer(h, ctx_kv, w_in, w_out, q_norm, k_norm, sink, conv_w, conv_b,
                      w1, b1, f1, w2, b2, f2, w3, hy_bias):
    b, length, _ = h.shape
    q, k, v, u3 = jnp.split(h @ w_in, [H_A * HD_A, H_A * HD_A + KVH_A * HD_A,
                                       H_A * HD_A + 2 * KVH_A * HD_A], axis=-1)
    q = _rms_norm(q.reshape(b, length, KVH_A, G_A, HD_A), q_norm)
    k = _rms_norm(k.reshape(b, length, KVH_A, HD_A), k_norm)
    v = v.reshape(b, length, KVH_A, HD_A)
    if ctx_kv is None:
        a = _context_attention(q, k, v, sink)
    else:
        cos, sin = _axial_rope(length)
        a = _window_attention(_apply_rope(q, cos, sin), _apply_rope(k, cos, sin), v, ctx_kv[0], ctx_kv[1], sink)
    y = _hyena_mixer(u3, conv_w, conv_b, w1, b1, f1, w2, b2, f2, w3, hy_bias)
    return jnp.concatenate([a, y], axis=-1) @ w_out, k, v


def _chunk_gated_delta_rule(q, k, v, g, beta, s0):
    b, length, h, dk = q.shape
    dv = v.shape[-1]
    n = length // DN_CHUNK

    def chunks(t):
        t = t.reshape((b, n, DN_CHUNK, h) + t.shape[3:])
        return jnp.moveaxis(t, (1, 3), (0, 2))

    qc, kc, vc = chunks(q * (dk ** -0.5)), chunks(k), chunks(v)
    gcum = jnp.cumsum(chunks(g), axis=-1)
    bc = chunks(beta)[..., None]
    idx = jnp.arange(DN_CHUNK)
    lower = idx[:, None] >= idx[None, :]
    decay = jnp.where(lower, jnp.exp(jnp.where(lower, gcum[..., :, None] - gcum[..., None, :], 0.0)), 0.0)
    kb = kc * bc
    lmat = jnp.where(idx[:, None] > idx[None, :], jnp.einsum('nbhik,nbhjk->nbhij', kb, kc) * decay, 0.0)
    rhs = jnp.concatenate([vc * bc, kb * jnp.exp(gcum)[..., None]], axis=-1)
    sol = lax.linalg.triangular_solve(lmat + jnp.eye(DN_CHUNK, dtype=F32), rhs,
                                      left_side=True, lower=True, unit_diagonal=True)
    u, w = sol[..., :dv], sol[..., dv:]

    def step(s, inp):
        qi, ki, ui, wi, gi, di = inp
        v_new = ui - jnp.einsum('bhck,bhkv->bhcv', wi, s)
        o = (jnp.einsum('bhck,bhkv->bhcv', qi * jnp.exp(gi)[..., None], s)
             + jnp.einsum('bhij,bhjv->bhiv', jnp.einsum('bhik,bhjk->bhij', qi, ki) * di, v_new))
        gl = gi[..., -1]
        s = (s * jnp.exp(gl)[..., None, None]
             + jnp.einsum('bhck,bhcv->bhkv', ki * jnp.exp(gl[..., None] - gi)[..., None], v_new))
        return s, o

    s_final, o = lax.scan(step, s0, (qc, kc, u, w, gcum, decay))
    o = jnp.moveaxis(o, (0, 2), (1, 3)).reshape(b, length, h, dv)
    return o, s_final


def _deltanet_mixer(h, s0_f, s0_b, w_in, w_out, conv_w, a_log, dt_bias, norm_g):
    b, length, _ = h.shape
    nq = H_C * DK_C
    qkv, z, bl, al = jnp.split(h @ w_in, [2 * nq + H_C * DV_C, 2 * nq + 2 * H_C * DV_C,
                                          2 * nq + 2 * H_C * DV_C + 2 * H_C], axis=-1)
    qkv = jax.nn.silu(_depthwise_conv(qkv, conv_w)).astype(F32)
    q, k, v = jnp.split(qkv, [nq, 2 * nq], axis=-1)
    q = _l2_norm(q.reshape(b, length, H_C, DK_C))
    k = _l2_norm(k.reshape(b, length, H_C, DK_C))
    v = v.reshape(b, length, H_C, DV_C)
    beta = jax.nn.sigmoid(bl.astype(F32)).reshape(b, length, 2, H_C)
    g = -jnp.exp(a_log.astype(F32)) * jax.nn.softplus(
        al.astype(F32).reshape(b, length, 2, H_C) + dt_bias.astype(F32))
    o_f, s_f = _chunk_gated_delta_rule(q, k, v, g[:, :, 0], beta[:, :, 0], s0_f.astype(F32))
    o_b, s_b = _chunk_gated_delta_rule(q[:, ::-1], k[:, ::-1], v[:, ::-1], g[:, ::-1, 1], beta[:, ::-1, 1],
                                       s0_b.astype(F32))
    o = _rms_norm(o_f + o_b[:, ::-1], norm_g) * jax.nn.silu(z.astype(F32).reshape(b, length, H_C, DV_C))
    return o.reshape(b, length, DN_OUT).astype(h.dtype) @ w_out, s_f, s_b


def kernel(x_prompt, x_sample, cache_k, cache_v, state_fwd, state_bwd, c, c_ctx, norm_g, ada_w, ada_b, ffn_w13, ffn_w2, mx_w_in, mx_w_out, q_norm, k_norm, attn_sink, hy_conv_w, hy_conv_b, hy_w1, hy_b1, hy_freq1, hy_w2, hy_b2, hy_freq2, hy_w3, hy_bias, dn_w_in, dn_w_out, dn_conv_w, dn_a_log, dn_dt_bias, dn_norm_g):
    bp, lp, _ = x_prompt.shape
    bs, ls, _ = x_sample.shape
    tp, ts = bp * lp, bs * ls
    xp, xs = x_prompt, x_sample
    new_k, new_v, new_sf, new_sb = [], [], [], []
    zero_state = jnp.zeros((bp, H_C, DK_C, DV_C), F32)
    cond = jnp.concatenate([c, c_ctx[None, :]], axis=0)
    tiles_p, tiles_per_s = tp // FFN_TM, ls // FFN_TM

    def row_of_tile(i):
        return jnp.where(i < tiles_p, bs, (i - tiles_p) // tiles_per_s)

    def ffn_both(xp, xs, mod, layer, j, which):
        x = jnp.concatenate([xp.reshape(tp, D_MODEL), xs.reshape(ts, D_MODEL)], axis=0)
        y = _ffn(x, mod, row_of_tile, norm_g[layer, j], ffn_w13[layer, which], ffn_w2[layer, which], j)
        return y[:tp].reshape(bp, lp, D_MODEL), y[tp:].reshape(bs, ls, D_MODEL)

    for layer in range(DEPTH):
        mod = (jax.nn.silu(cond) @ ada_w[layer] + ada_b[layer]).reshape(bs + 1, 9, D_MODEL)
        mp, ms = mod[bs:].reshape(1, 1, 9, D_MODEL), mod[:bs].reshape(bs, 1, 9, D_MODEL)
        xp, xs = ffn_both(xp, xs, mod, layer, 0, 0)
        hp = _ada_norm(xp, norm_g[layer, 1], mp, 1)
        hs = _ada_norm(xs, norm_g[layer, 1], ms, 1)
        i = layer // 2
        if layer % 2 == 0:
            args = (mx_w_in[i], mx_w_out[i], q_norm[i], k_norm[i], attn_sink[i], hy_conv_w[i], hy_conv_b[i],
                    hy_w1[i], hy_b1[i], hy_freq1[i], hy_w2[i], hy_b2[i], hy_freq2[i], hy_w3[i], hy_bias[i])
            o_p, k_p, v_p = _attn_hyena_mixer(hp, None, *args)
            o_s, _, _ = _attn_hyena_mixer(hs, (cache_k[:, i], cache_v[:, i]), *args)
            new_k.append(k_p)
            new_v.append(v_p)
        else:
            args = (dn_w_in[i], dn_w_out[i], dn_conv_w[i], dn_a_log[i], dn_dt_bias[i], dn_norm_g[i])
            o_p, s_f, s_b = _deltanet_mixer(hp, zero_state, zero_state, *args)
            o_s, _, _ = _deltanet_mixer(hs, state_fwd[:, i], state_bwd[:, i], *args)
            new_sf.append(s_f.astype(xp.dtype))
            new_sb.append(s_b.astype(xp.dtype))
        xp = xp + mp[:, :, 5] * o_p
        xs = xs + ms[:, :, 5] * o_s
        xp, xs = ffn_both(xp, xs, mod, layer, 2, 1)

    return (xp, xs, jnp.stack(new_k, axis=1), jnp.stack(new_v, axis=1),
            jnp.stack(new_sf, axis=1), jnp.stack(new_sb, axis=1))
```

```python
import functools
import math

import jax
import jax.numpy as jnp
from jax import lax
from jax.experimental import pallas as pl
from jax.experimental.pallas import tpu as pltpu

D_MODEL = 1024
DEPTH = 4
GRID_W = 64
H_A = 8
KVH_A = 2
G_A = H_A // KVH_A
HD_A = 64
WINDOW = 128
ATTN_BLOCK = 128
ROPE_THETA = 10000.0
ROPE_AXIS_DIM = HD_A // 2
HY_C = 512
HY_ORDER = 2
HY_EMB = 33
HY_BANDS = (HY_EMB - 1) // 2
HY_MIN_DECAY = math.log(1e-2) / 1.5
HY_MAX_DECAY = math.log(1e-2) / 0.3
H_C = 8
DK_C = 128
DV_C = 128
DN_CHUNK = 64
DN_OUT = H_C * DV_C
D_FF = 2816
NORM_EPS = 1e-6
NEG_INF = -1e30

F32 = jnp.float32
BF16 = jnp.bfloat16

TOKEN_TILE = 512
FFN_CHUNK = 256
DN_BLOCK = 256
ATTN_TQ = 256
VMEM_LIMIT = 56 * 1024 * 1024


def _ffn_kernel(x_ref, mod_ref, g_ref, w1_ref, w3_ref, w2_ref, o_ref, acc_ref, *, j, n_chunks):
    x = x_ref[...]
    y = x * lax.rsqrt(jnp.mean(x * x, axis=-1, keepdims=True) + NORM_EPS) * g_ref[...]
    m = mod_ref[0]
    shift, scale, gate = m[3 * j:3 * j + 1], m[3 * j + 1:3 * j + 2], m[3 * j + 2:3 * j + 3]
    h = (y * (1.0 + scale) + shift).astype(BF16)
    acc_ref[...] = jnp.zeros_like(acc_ref)

    def body(c, carry):
        gt = jnp.dot(h, w1_ref[c], preferred_element_type=F32)
        up = jnp.dot(h, w3_ref[c], preferred_element_type=F32)
        a = (jax.nn.silu(gt) * up).astype(BF16)
        acc_ref[...] += jnp.dot(a, w2_ref[c], preferred_element_type=F32)
        return carry

    lax.fori_loop(0, n_chunks, body, 0)
    o_ref[...] = x + 0.5 * gate * acc_ref[...]


def _ffn(x, mod, row_of_tile, g, w13, w2, j, *, tm):
    t = x.shape[0]
    n_chunks = D_FF // FFN_CHUNK
    w1 = w13[:, :D_FF].astype(BF16).reshape(D_MODEL, n_chunks, FFN_CHUNK).transpose(1, 0, 2)
    w3 = w13[:, D_FF:].astype(BF16).reshape(D_MODEL, n_chunks, FFN_CHUNK).transpose(1, 0, 2)
    w2c = w2.astype(BF16).reshape(n_chunks, FFN_CHUNK, D_MODEL)
    const3 = lambda i: (0, 0, 0)
    return pl.pallas_call(
        functools.partial(_ffn_kernel, j=j, n_chunks=n_chunks),
        out_shape=jax.ShapeDtypeStruct((t, D_MODEL), F32),
        grid=(t // tm,),
        in_specs=[
            pl.BlockSpec((tm, D_MODEL), lambda i: (i, 0)),
            pl.BlockSpec((1, 9, D_MODEL), lambda i: (row_of_tile(i), 0, 0)),
            pl.BlockSpec((1, D_MODEL), lambda i: (0, 0)),
            pl.BlockSpec((n_chunks, D_MODEL, FFN_CHUNK), const3),
            pl.BlockSpec((n_chunks, D_MODEL, FFN_CHUNK), const3),
            pl.BlockSpec((n_chunks, FFN_CHUNK, D_MODEL), const3),
        ],
        out_specs=pl.BlockSpec((tm, D_MODEL), lambda i: (i, 0)),
        scratch_shapes=[pltpu.VMEM((tm, D_MODEL), F32)],
        compiler_params=pltpu.CompilerParams(dimension_semantics=("arbitrary",),
                                             vmem_limit_bytes=VMEM_LIMIT),
        name=f"ffn{j}",
    )(x, mod, g.reshape(1, D_MODEL), w1, w3, w2c)


def _bdot(a, b):
    return jnp.dot(a.astype(BF16), b.astype(BF16), preferred_element_type=F32)


def _bdot_nt(a, b):
    return lax.dot_general(a.astype(BF16), b.astype(BF16), (((1,), (1,)), ((), ())),
                           preferred_element_type=F32)


def _bdot_tn(a, b):
    return lax.dot_general(a.astype(BF16), b.astype(BF16), (((0,), (0,)), ((), ())),
                           preferred_element_type=F32)


def _split3(x):
    hi = x.astype(BF16)
    r1 = x - hi.astype(F32)
    mid = r1.astype(BF16)
    lo = (r1 - mid.astype(F32)).astype(BF16)
    return hi, mid, lo


def _ada_h(x, g_row, m, j):
    y = x * lax.rsqrt(jnp.mean(x * x, axis=-1, keepdims=True) + NORM_EPS) * g_row
    return y * (1.0 + m[3 * j + 1:3 * j + 2]) + m[3 * j:3 * j + 1]


def _seq_edges(i, tm, n_tiles_p, tp, lp, ls, shape, axis):
    is_p = i < n_tiles_p
    seq_len = jnp.where(is_p, lp, ls)
    t0 = i * tm - jnp.where(is_p, 0, tp)
    base = lax.rem(t0, seq_len)
    pos = (base + lax.broadcasted_iota(jnp.int32, shape, axis)).astype(F32)
    lf = seq_len.astype(F32)
    rem = pos - jnp.floor((pos + 0.5) / lf) * lf
    return rem == 0.0, rem == lf - 1.0


def _halo_specs(tm, t):
    nb = t // 8
    prev = pl.BlockSpec((8, D_MODEL), lambda i: (jnp.maximum(i * (tm // 8) - 1, 0), 0))
    nxt = pl.BlockSpec((8, D_MODEL), lambda i: (jnp.minimum((i + 1) * (tm // 8), nb - 1), 0))
    return prev, nxt


DN_QKV = 2 * H_C * DK_C + H_C * DV_C
DN_CW = 512


def _dn_inproj_kernel(xp_ref, x_ref, xn_ref, mod_ref, g_ref, w_ref, cw_ref, ab_ref,
                      q_ref, k_ref, v_ref, z_ref, bg_ref, pext_ref, *, tm, n_tiles_p, tp, lp, ls):
    i = pl.program_id(0)
    xe = jnp.concatenate([xp_ref[...], x_ref[...], xn_ref[...]], axis=0)
    h = _ada_h(xe, g_ref[...], mod_ref[0], 1).astype(BF16)
    first, last = _seq_edges(i, tm, n_tiles_p, tp, lp, ls, (tm, 1), 0)
    nq = H_C * DK_C
    for c0 in range(0, DN_QKV, DN_CW):
        pext_ref[...] = jnp.dot(h, w_ref[:, c0:c0 + DN_CW], preferred_element_type=F32)
        cw = cw_ref[:, c0:c0 + DN_CW]
        c = (jnp.where(first, 0.0, pext_ref[7:tm + 7, :]) * cw[0:1]
             + pext_ref[8:tm + 8, :] * cw[1:2]
             + jnp.where(last, 0.0, pext_ref[9:tm + 9, :]) * cw[2:3])
        a = jax.nn.silu(c)
        for b0 in range(0, DN_CW, DK_C):
            col = c0 + b0
            blk = a[:, b0:b0 + DK_C]
            if col < 2 * nq:
                blk = blk * lax.rsqrt(jnp.sum(blk * blk, axis=-1, keepdims=True) + NORM_EPS)
            if col < nq:
                q_ref[:, col:col + DK_C] = blk * (DK_C ** -0.5)
            elif col < 2 * nq:
                k_ref[:, col - nq:col - nq + DK_C] = blk
            else:
                v_ref[:, col - 2 * nq:col - 2 * nq + DK_C] = blk
    hm = h[8:tm + 8]
    for c0 in range(0, DN_OUT, DN_CW):
        z_ref[:, c0:c0 + DN_CW] = jnp.dot(hm, w_ref[:, DN_QKV + c0:DN_QKV + c0 + DN_CW],
                                           preferred_element_type=F32)
    r = jnp.dot(hm, w_ref[:, DN_QKV + DN_OUT:], preferred_element_type=F32)
    lane = lax.broadcasted_iota(jnp.int32, r.shape, 1)
    xs = r + ab_ref[1:2]
    softplus = jnp.maximum(xs, 0.0) + jnp.log1p(jnp.exp(-jnp.abs(xs)))
    bg_ref[...] = jnp.where(lane < 2 * H_C, jax.nn.sigmoid(r),
                            jnp.where(lane < 4 * H_C, -ab_ref[0:1] * softplus, 0.0))


def _dn_inproj(x, mod, row_of_tile, g, w_in, conv_w, a_log, dt_bias, *, tm, tp, lp, ls):
    t = x.shape[0]
    pad = 128 - 4 * H_C
    w = jnp.pad(w_in, ((0, 0), (0, pad))).astype(BF16)
    ab = jnp.zeros((2, 128), F32)
    ab = ab.at[0, 2 * H_C:4 * H_C].set(jnp.exp(a_log.astype(F32)).reshape(-1))
    ab = ab.at[1, 2 * H_C:4 * H_C].set(dt_bias.astype(F32).reshape(-1))
    prev, nxt = _halo_specs(tm, t)
    tile = lambda i: (i, 0)
    const = lambda i: (0, 0)
    wide = jax.ShapeDtypeStruct((t, DN_OUT), F32)
    return pl.pallas_call(
        functools.partial(_dn_inproj_kernel, tm=tm, n_tiles_p=tp // tm, tp=tp, lp=lp, ls=ls),
        out_shape=(wide, wide, wide, wide, jax.ShapeDtypeStruct((t, 128), F32)),
        grid=(t // tm,),
        in_specs=[prev, pl.BlockSpec((tm, D_MODEL), tile), nxt,
                  pl.BlockSpec((1, 9, D_MODEL), lambda i: (row_of_tile(i), 0, 0)),
                  pl.BlockSpec((1, D_MODEL), const),
                  pl.BlockSpec(w.shape, const),
                  pl.BlockSpec(conv_w.shape, const),
                  pl.BlockSpec((2, 128), const)],
        out_specs=(pl.BlockSpec((tm, DN_OUT), tile),) * 4 + (pl.BlockSpec((tm, 128), tile),),
        scratch_shapes=[pltpu.VMEM((tm + 16, DN_CW), F32)],
        compiler_params=pltpu.CompilerParams(dimension_semantics=("arbitrary",),
                                             vmem_limit_bytes=VMEM_LIMIT),
        name="dn_inproj",
    )(x, x, x, mod, g.reshape(1, D_MODEL), w, conv_w.astype(F32), ab)


DN_GROUP = 4


def _dn_chunk(qc, kc, vc, bgc, s_ref, rev):
    n = DN_CHUNK
    r_i = lax.broadcasted_iota(jnp.int32, (n, n), 0)
    c_i = lax.broadcasted_iota(jnp.int32, (n, n), 1)
    tri = jnp.where((r_i <= c_i) if rev else (r_i >= c_i), 1.0, 0.0).astype(BF16)
    gct = sum(jnp.dot(tri, p, preferred_element_type=F32) for p in _split3(bgc))
    rows = DN_GROUP * n
    rr = lax.broadcasted_iota(jnp.int32, (rows, rows), 0)
    cc = lax.broadcasted_iota(jnp.int32, (rows, rows), 1)
    same = (rr // n) == (cc // n)
    incl = same & ((rr <= cc) if rev else (rr >= cc))
    strict = same & ((rr < cc) if rev else (rr > cc))
    lane = lax.broadcasted_iota(jnp.int32, (rows, 128), 1)
    last = 0 if rev else n - 1
    outs = []
    for gi in range(H_C // DN_GROUP):
        heads = range(gi * DN_GROUP, (gi + 1) * DN_GROUP)
        sb = [(H_C if rev else 0) + h for h in heads]
        sg = [2 * H_C + (H_C if rev else 0) + h for h in heads]
        beta = jnp.concatenate([bgc[:, s:s + 1] for s in sb], axis=0)
        gc = jnp.concatenate([gct[:, s:s + 1] for s in sg], axis=0)
        gl = jnp.concatenate([jnp.broadcast_to(gct[last:last + 1, s:s + 1], (n, 1)) for s in sg], axis=0)
        q4 = jnp.concatenate([qc[:, h * DK_C:(h + 1) * DK_C] for h in heads], axis=0)
        k4 = jnp.concatenate([kc[:, h * DK_C:(h + 1) * DK_C] for h in heads], axis=0)
        v4 = jnp.concatenate([vc[:, h * DV_C:(h + 1) * DV_C] for h in heads], axis=0)
        hi, mid, lo = (p.astype(F32) for p in _split3(gc))
        one = jnp.where(lane < 6, 1.0, 0.0)
        u_m = jnp.where(lane == 0, hi, jnp.where(lane == 1, mid, jnp.where(lane == 2, lo, one)))
        v_m = jnp.where(lane == 3, -hi, jnp.where(lane == 4, -mid, jnp.where(lane == 5, -lo, one)))
        gd = _bdot_nt(jnp.where(lane < 6, u_m, 0.0), jnp.where(lane < 6, v_m, 0.0))
        decay = jnp.where(incl, jnp.exp(jnp.where(incl, gd, 0.0)), 0.0)
        kb = k4 * beta
        lm = jnp.where(strict, _bdot_nt(kb, k4) * decay, 0.0)
        x = jnp.concatenate([v4 * beta, kb * jnp.exp(gc)], axis=1)
        lh = lm.astype(BF16)
        ll = (lm - lh.astype(F32)).astype(BF16)
        ps = [-lh]
        for _ in range(5):
            ps.append(jnp.dot(ps[-1], ps[-1], preferred_element_type=F32).astype(BF16))

        def apply_inv(y):
            for p in ps:
                y = y + jnp.dot(p, y.astype(BF16), preferred_element_type=F32)
            return y

        x0 = apply_inv(x)
        xh = x0.astype(BF16)
        xl = (x0 - xh.astype(F32)).astype(BF16)
        res = x - x0 - (jnp.dot(lh, xh, preferred_element_type=F32) + jnp.dot(lh, xl, preferred_element_type=F32)
                        + jnp.dot(ll, xh, preferred_element_type=F32))
        x = x0 + apply_inv(res)
        u4, w4 = x[:, :DV_C], x[:, DV_C:]
        qe = q4 * jnp.exp(gc)
        ke = k4 * jnp.exp(gl - gc)
        aqk = jnp.where(incl, _bdot_nt(q4, k4) * decay, 0.0)
        vn, qs = [], []
        for j, h in enumerate(heads):
            sl = slice(j * n, (j + 1) * n)
            st = s_ref[h]
            wq = _bdot(jnp.concatenate([w4[sl], qe[sl]], axis=0), st)
            vn.append(u4[sl] - wq[:n])
            qs.append(wq[n:])
        vn4 = jnp.concatenate(vn, axis=0)
        o4 = jnp.concatenate(qs, axis=0) + _bdot(aqk, vn4)
        for j, h in enumerate(heads):
            sl = slice(j * n, (j + 1) * n)
            s_ref[h] = s_ref[h] * jnp.exp(gl[j * n:j * n + 1]) + _bdot_tn(ke[sl], vn[j])
            outs.append(o4[sl])
    return outs


def _dn_seq_kernel(*refs, rev, n_chunks, blocks_per_seq, zero_init, has_prev, add_to, write_state):
    refs = list(refs)
    q_ref, k_ref, v_ref, bg_ref = refs[:4]
    del refs[:4]
    s0_ref = None if zero_init else refs.pop(0)
    oin_ref = refs.pop(0) if has_prev else None
    o_ref = refs.pop(0)
    sf_ref = refs.pop(0) if write_state else None
    s_ref = refs.pop(0)
    i = pl.program_id(0)
    blk = (pl.num_programs(0) - 1 - i) if rev else i
    start = (blocks_per_seq - 1) if rev else 0

    @pl.when(lax.rem(blk, blocks_per_seq) == start)
    def _():
        if zero_init:
            s_ref[...] = jnp.zeros_like(s_ref)
        else:
            s_ref[...] = s0_ref[0]

    def body(ci, carry):
        c = (n_chunks - 1 - ci) if rev else ci
        rows = pl.ds(pl.multiple_of(c * DN_CHUNK, DN_CHUNK), DN_CHUNK)
        outs = _dn_chunk(q_ref[rows, :], k_ref[rows, :], v_ref[rows, :], bg_ref[rows, :], s_ref, rev)
        for h, o in enumerate(outs):
            cols = slice(h * DV_C, (h + 1) * DV_C)
            if add_to:
                o_ref[rows, cols] = oin_ref[rows, cols] + o
            else:
                o_ref[rows, cols] = o
        return carry

    lax.fori_loop(0, n_chunks, body, 0)
    if write_state:
        sf_ref[0] = s_ref[...]


def _dn_seq(q, k, v, bg, s0, o_prev, *, rev, row0, n_seq, seq_len, ct, add_to, write_state):
    t = q.shape[0]
    bps = seq_len // ct
    nblk = n_seq * bps
    b0 = row0 // ct
    zero_init = s0 is None

    def blk_of(i):
        return (nblk - 1 - i) if rev else i

    rows = lambda i: (b0 + blk_of(i), 0)
    in_specs = [pl.BlockSpec((ct, DN_OUT), rows)] * 3 + [pl.BlockSpec((ct, 128), rows)]
    args = [q, k, v, bg]
    if not zero_init:
        in_specs.append(pl.BlockSpec((1, H_C, DK_C, DV_C), lambda i: (blk_of(i) // bps, 0, 0, 0)))
        args.append(s0)
    aliases = {}
    if o_prev is not None:
        in_specs.append(pl.BlockSpec((ct, DN_OUT), rows) if add_to else pl.BlockSpec(memory_space=pl.ANY))
        aliases = {len(args): 0}
        args.append(o_prev)
    out_shape = [jax.ShapeDtypeStruct((t, DN_OUT), F32)]
    out_specs = [pl.BlockSpec((ct, DN_OUT), rows)]
    if write_state:
        out_shape.append(jax.ShapeDtypeStruct((n_seq, H_C, DK_C, DV_C), F32))
        out_specs.append(pl.BlockSpec((1, H_C, DK_C, DV_C), lambda i: (blk_of(i) // bps, 0, 0, 0)))
    kern = functools.partial(_dn_seq_kernel, rev=rev, n_chunks=ct // DN_CHUNK, blocks_per_seq=bps,
                             zero_init=zero_init, has_prev=o_prev is not None, add_to=add_to,
                             write_state=write_state)
    res = pl.pallas_call(
        kern, out_shape=tuple(out_shape), grid=(nblk,), in_specs=in_specs, out_specs=tuple(out_specs),
        scratch_shapes=[pltpu.VMEM((H_C, DK_C, DV_C), F32)],
        input_output_aliases=aliases,
        compiler_params=pltpu.CompilerParams(dimension_semantics=("arbitrary",),
                                             vmem_limit_bytes=VMEM_LIMIT),
        name="dn_seq_" + ("bwd" if rev else "fwd"),
    )(*args)
    return res if write_state else (res[0], None)


def _dn_outproj_kernel(x_ref, o_ref, z_ref, mod_ref, ng_ref, w_ref, y_ref):
    o = o_ref[...]
    parts = []
    for h in range(H_C):
        blk = o[:, h * DV_C:(h + 1) * DV_C]
        parts.append(blk * lax.rsqrt(jnp.mean(blk * blk, axis=-1, keepdims=True) + NORM_EPS))
    y = jnp.concatenate(parts, axis=1) * ng_ref[...] * jax.nn.silu(z_ref[...])
    gate = mod_ref[0][5:6]
    y_ref[...] = x_ref[...] + gate * jnp.dot(y.astype(BF16), w_ref[...], preferred_element_type=F32)


def _dn_outproj(x, o, z, mod, row_of_tile, norm_g, w_out, *, tm):
    t = x.shape[0]
    tile = lambda i: (i, 0)
    const = lambda i: (0, 0)
    return pl.pallas_call(
        _dn_outproj_kernel,
        out_shape=jax.ShapeDtypeStruct((t, D_MODEL), F32),
        grid=(t // tm,),
        in_specs=[pl.BlockSpec((tm, D_MODEL), tile), pl.BlockSpec((tm, DN_OUT), tile),
                  pl.BlockSpec((tm, DN_OUT), tile),
                  pl.BlockSpec((1, 9, D_MODEL), lambda i: (row_of_tile(i), 0, 0)),
                  pl.BlockSpec((1, DN_OUT), const), pl.BlockSpec((DN_OUT, D_MODEL), const)],
        out_specs=pl.BlockSpec((tm, D_MODEL), tile),
        compiler_params=pltpu.CompilerParams(dimension_semantics=("arbitrary",),
                                             vmem_limit_bytes=VMEM_LIMIT),
        name="dn_outproj",
    )(x, o, z, mod, jnp.tile(norm_g.astype(F32), H_C).reshape(1, DN_OUT), w_out.astype(BF16))


def _dn_layer(x, mod, row_of_tile, g, w_in, w_out, conv_w, a_log, dt_bias, norm_g, s0_f, s0_b,
              *, tm, ct, tp, lp, ls):
    t = x.shape[0]
    q, k, v, z, bg = _dn_inproj(x, mod, row_of_tile, g, w_in, conv_w, a_log, dt_bias, tm=tm, tp=tp, lp=lp, ls=ls)
    bp, bs = tp // lp, (t - tp) // ls
    common = dict(row0=0, n_seq=bp, seq_len=lp, ct=min(ct, lp), write_state=True)
    o, sf = _dn_seq(q, k, v, bg, None, None, rev=False, add_to=False, **common)
    o, sb = _dn_seq(q, k, v, bg, None, o, rev=True, add_to=True, **common)
    common = dict(row0=tp, n_seq=bs, seq_len=ls, ct=ct, write_state=False)
    o, _ = _dn_seq(q, k, v, bg, s0_f.astype(F32), o, rev=False, add_to=False, **common)
    o, _ = _dn_seq(q, k, v, bg, s0_b.astype(F32), o, rev=True, add_to=True, **common)
    return _dn_outproj(x, o, z, mod, row_of_tile, norm_g, w_out, tm=tm), sf, sb


AH_Q = H_A * HD_A
AH_KV = KVH_A * HD_A
AH_U3 = 3 * HY_C


def _group_mean_sq(x, bd):
    sq = x * x
    hi = sq.astype(BF16)
    lo = (sq - hi.astype(F32)).astype(BF16)
    return (jnp.dot(hi, bd, preferred_element_type=F32) + jnp.dot(lo, bd, preferred_element_type=F32)) * (1.0 / HD_A)


def _rope_lanes(x, cos_t, sin_t):
    w = x.shape[1]
    lane = lax.broadcasted_iota(jnp.int32, x.shape, 1)
    low = lax.rem(lane, HD_A) < HD_A // 2
    partner = jnp.where(low, pltpu.roll(x, w - HD_A // 2, 1), pltpu.roll(x, HD_A // 2, 1))
    reps = w // cos_t.shape[1]
    return x * jnp.concatenate([cos_t] * reps, axis=1) + partner * jnp.concatenate([sin_t] * reps, axis=1)


def _rep4(x):
    lane = lax.broadcasted_iota(jnp.int32, x.shape, 1)
    sw = pltpu.roll(x, HD_A, 1)
    a = jnp.where(lane < HD_A, x, sw)
    b = jnp.where(lane < HD_A, sw, x)
    return jnp.concatenate([a, a, b, b], axis=1)


def _ah_inproj_kernel(xp_ref, x_ref, xn_ref, mod_ref, g_ref, w_ref, wut_ref, gains_ref, cos_ref, sin_ref,
                      bdq_ref, cwb_ref, q_ref, kt_ref, vt_ref, kn_ref, vn_ref, u_ref, vb_ref,
                      *, tm, n_tiles_p, tp, lp, ls):
    i = pl.program_id(0)
    is_p = i < n_tiles_p
    xe = jnp.concatenate([xp_ref[...], x_ref[...], xn_ref[...]], axis=0)
    he = _ada_h(xe, g_ref[...], mod_ref[0], 1).astype(BF16)
    hm = he[8:tm + 8]
    p = jnp.dot(hm, w_ref[...], preferred_element_type=F32)
    cos_t, sin_t = cos_ref[...], sin_ref[...]
    q = p[:, :AH_Q]
    q = q * lax.rsqrt(_group_mean_sq(q, bdq_ref[...]) + NORM_EPS) * gains_ref[0:1, :]
    q = jnp.where(is_p, q, _rope_lanes(q, cos_t, sin_t))
    q_ref[...] = (q * (HD_A ** -0.5)).astype(BF16)
    k = p[:, AH_Q:AH_Q + AH_KV]
    k = k * lax.rsqrt(_group_mean_sq(k, bdq_ref[:AH_KV, :AH_KV]) + NORM_EPS) * gains_ref[1:2, :AH_KV]
    kn_ref[...] = k
    k = jnp.where(is_p, k, _rope_lanes(k, cos_t, sin_t))
    kt_ref[...] = _rep4(k).astype(BF16)
    v = p[:, AH_Q + AH_KV:]
    vn_ref[...] = v
    vt_ref[...] = _rep4(v).astype(BF16)
    halo = jnp.concatenate([he[0:8], he[tm + 8:tm + 16], jnp.zeros((112, D_MODEL), BF16)], axis=0)
    first, last = _seq_edges(i, tm, n_tiles_p, tp, lp, ls, (1, tm), 1)
    lane = lax.broadcasted_iota(jnp.int32, (1, tm), 1)
    for r0 in range(0, AH_U3, HY_C):
        wu = wut_ref[r0:r0 + HY_C, :]
        u = lax.dot_general(wu, hm, (((1,), (1,)), ((), ())), preferred_element_type=F32)
        uh = lax.dot_general(wu, halo, (((1,), (1,)), ((), ())), preferred_element_type=F32)
        left = jnp.where(lane == 0, uh[:, 7:8], pltpu.roll(u, 1, 1))
        right = jnp.where(lane == tm - 1, uh[:, 8:9], pltpu.roll(u, tm - 1, 1))
        cwb = cwb_ref[r0:r0 + HY_C, :]
        c = (jnp.where(first, 0.0, left) * cwb[:, 0:1] + u * cwb[:, 1:2]
             + jnp.where(last, 0.0, right) * cwb[:, 2:3] + cwb[:, 3:4])
        u_ref[r0:r0 + HY_C, :] = c
        if r0 == 2 * HY_C:
            vb_ref[...] = c.astype(BF16)


def _rope_tables(ls):
    rows = ls // GRID_W
    r, col = jnp.meshgrid(jnp.arange(rows), jnp.arange(GRID_W), indexing='ij')
    inv = ROPE_THETA ** (-jnp.arange(0, ROPE_AXIS_DIM, 2, dtype=F32) / ROPE_AXIS_DIM)
    ang = jnp.concatenate([r.reshape(-1, 1).astype(F32) * inv, col.reshape(-1, 1).astype(F32) * inv], axis=-1)
    cos, sin = jnp.cos(ang), jnp.sin(ang)
    return jnp.concatenate([cos, cos] * 2, axis=1), jnp.concatenate([-sin, sin] * 2, axis=1)


def _ah_inproj(x, mod, row_of_tile, g, w_in, q_norm, k_norm, conv_w, conv_b, rope, *, tm, tp, lp, ls):
    t = x.shape[0]
    w = w_in[:, :AH_Q + 2 * AH_KV].astype(BF16)
    wut = w_in[:, AH_Q + 2 * AH_KV:].T.astype(BF16)
    gains = jnp.stack([jnp.tile(q_norm.astype(F32), H_A), jnp.tile(k_norm.astype(F32), H_A)])
    gid = jnp.arange(AH_Q) // HD_A
    bdq = (gid[:, None] == gid[None, :]).astype(BF16)
    cwb = jnp.concatenate([conv_w.astype(F32).T, conv_b.astype(F32)[:, None],
                           jnp.zeros((AH_U3, 4), F32)], axis=1)
    n_tiles_p = tp // tm
    prev, nxt = _halo_specs(tm, t)
    tile = lambda i: (i, 0)
    const = lambda i: (0, 0)
    rope_blk = lambda i: (jnp.where(i < n_tiles_p, 0, lax.rem(i * tm - tp, ls) // tm), 0)
    bf = lambda n: jax.ShapeDtypeStruct((t, n), BF16)
    return pl.pallas_call(
        functools.partial(_ah_inproj_kernel, tm=tm, n_tiles_p=n_tiles_p, tp=tp, lp=lp, ls=ls),
        out_shape=(bf(AH_Q), bf(AH_Q), bf(AH_Q), jax.ShapeDtypeStruct((t, AH_KV), F32),
                   jax.ShapeDtypeStruct((t, AH_KV), F32), jax.ShapeDtypeStruct((AH_U3, t), F32),
                   jax.ShapeDtypeStruct((HY_C, t), BF16)),
        grid=(t // tm,),
        in_specs=[prev, pl.BlockSpec((tm, D_MODEL), tile), nxt,
                  pl.BlockSpec((1, 9, D_MODEL), lambda i: (row_of_tile(i), 0, 0)),
                  pl.BlockSpec((1, D_MODEL), const), pl.BlockSpec(w.shape, const),
                  pl.BlockSpec(wut.shape, const), pl.BlockSpec(gains.shape, const),
                  pl.BlockSpec((tm, 128), rope_blk), pl.BlockSpec((tm, 128), rope_blk),
                  pl.BlockSpec(bdq.shape, const), pl.BlockSpec(cwb.shape, const)],
        out_specs=(pl.BlockSpec((tm, AH_Q), tile),) * 3 + (pl.BlockSpec((tm, AH_KV), tile),) * 2
        + (pl.BlockSpec((AH_U3, tm), lambda i: (0, i)), pl.BlockSpec((HY_C, tm), lambda i: (0, i))),
        compiler_params=pltpu.CompilerParams(dimension_semantics=("arbitrary",),
                                             vmem_limit_bytes=VMEM_LIMIT),
        name="ah_inproj",
    )(x, x, x, mod, g.reshape(1, D_MODEL), w, wut, gains, rope[0], rope[1], bdq, cwb)


def _attn_group(q, kt, vt, sink_ref, g, valid):
    tq = q.shape[0]
    lane = lax.broadcasted_iota(jnp.int32, q.shape, 1)
    qs = jnp.concatenate([jnp.where(lane // HD_A == j, q, jnp.zeros_like(q)) for j in range(G_A)], axis=0)
    s = lax.dot_general(qs, kt, (((1,), (1,)), ((), ())), preferred_element_type=F32)
    if valid is not None:
        s = jnp.where(jnp.concatenate([valid] * G_A, axis=0), s, NEG_INF)
    sink = jnp.concatenate([jnp.broadcast_to(sink_ref[g * G_A + j:g * G_A + j + 1, 0:1], (tq, 1))
                            for j in range(G_A)], axis=0)
    m = jnp.maximum(jnp.max(s, axis=-1, keepdims=True), sink)
    p = jnp.exp(s - m)
    den = jnp.sum(p, axis=-1, keepdims=True) + jnp.exp(sink - m)
    pv = jnp.dot(p.astype(BF16), vt, preferred_element_type=F32) / den
    lane_o = lax.broadcasted_iota(jnp.int32, (tq, G_A * HD_A), 1)
    out = jnp.zeros((tq, G_A * HD_A), F32)
    for j in range(G_A):
        out = jnp.where(lane_o // HD_A == j, pv[j * tq:(j + 1) * tq], out)
    return out


def _attn_ctx_kernel(q_ref, kt_ref, vt_ref, sink_ref, o_ref):
    w = G_A * HD_A
    for g in range(KVH_A):
        cols = slice(g * w, (g + 1) * w)
        o_ref[:, cols] = _attn_group(q_ref[:, cols], kt_ref[:, cols], vt_ref[:, cols], sink_ref, g,
                                     None).astype(BF16)


def _attn_win_kernel(q_ref, kp_ref, km_ref, kn_ref, vp_ref, vm_ref, vn_ref, ck_ref, cv_ref, sink_ref,
                     prev_ref, o_ref, *, tq, ls):
    del prev_ref
    i = pl.program_id(1)
    n_ctx = ck_ref.shape[1]
    w = G_A * HD_A
    q_pos = i * tq + lax.broadcasted_iota(jnp.int32, (tq, n_ctx + tq + 2 * WINDOW), 0)
    col = lax.broadcasted_iota(jnp.int32, (tq, n_ctx + tq + 2 * WINDOW), 1)
    k_pos = i * tq - WINDOW + (col - n_ctx)
    valid = (col < n_ctx) | ((jnp.abs(q_pos - k_pos) <= WINDOW) & (k_pos >= 0) & (k_pos < ls))
    for g in range(KVH_A):
        cols = slice(g * w, (g + 1) * w)
        kt = jnp.concatenate([ck_ref[0, :, cols], kp_ref[:, cols], km_ref[:, cols], kn_ref[:, cols]], axis=0)
        vt = jnp.concatenate([cv_ref[0, :, cols], vp_ref[:, cols], vm_ref[:, cols], vn_ref[:, cols]], axis=0)
        o_ref[:, cols] = _attn_group(q_ref[:, cols], kt, vt, sink_ref, g, valid).astype(BF16)


def _attention(q, kt, vt, ck, cv, sink, *, tq, tp, lp, ls):
    t = q.shape[0]
    bp, bs = tp // lp, (t - tp) // ls
    sink_rows = jnp.broadcast_to(sink.astype(F32)[:, None], (H_A, 128))
    params = pltpu.CompilerParams(dimension_semantics=("arbitrary",), vmem_limit_bytes=VMEM_LIMIT)
    seq = lambda b: (b, 0)
    a = pl.pallas_call(
        _attn_ctx_kernel, out_shape=jax.ShapeDtypeStruct((t, AH_Q), BF16), grid=(bp,),
        in_specs=[pl.BlockSpec((lp, AH_Q), seq)] * 3 + [pl.BlockSpec((H_A, 128), lambda b: (0, 0))],
        out_specs=pl.BlockSpec((lp, AH_Q), seq), compiler_params=params, name="attn_ctx",
    )(q, kt, vt, sink_rows)
    nq = ls // tq
    wb = tq // WINDOW
    n128 = t // WINDOW
    main = lambda b, i: ((tp + b * ls) // tq + i, 0)
    prev = lambda b, i: (jnp.maximum((tp + b * ls) // WINDOW + i * wb - 1, 0), 0)
    nxt = lambda b, i: (jnp.minimum((tp + b * ls) // WINDOW + (i + 1) * wb, n128 - 1), 0)
    ctx = lambda b, i: (b, 0, 0)
    kv_specs = [pl.BlockSpec((WINDOW, AH_Q), prev), pl.BlockSpec((tq, AH_Q), main),
                pl.BlockSpec((WINDOW, AH_Q), nxt)]
    n_ctx = ck.shape[1]
    return pl.pallas_call(
        functools.partial(_attn_win_kernel, tq=tq, ls=ls),
        out_shape=jax.ShapeDtypeStruct((t, AH_Q), BF16), grid=(bs, nq),
        in_specs=[pl.BlockSpec((tq, AH_Q), main)] + kv_specs + kv_specs
        + [pl.BlockSpec((1, n_ctx, AH_Q), ctx)] * 2
        + [pl.BlockSpec((H_A, 128), lambda b, i: (0, 0)), pl.BlockSpec(memory_space=pl.ANY)],
        out_specs=pl.BlockSpec((tq, AH_Q), main),
        input_output_aliases={10: 0},
        compiler_params=pltpu.CompilerParams(dimension_semantics=("arbitrary", "arbitrary"),
                                             vmem_limit_bytes=VMEM_LIMIT),
        name="attn_win",
    )(q, kt, kt, kt, vt, vt, vt, ck, cv, sink_rows, a)


DFT_SUB = 64
DFT_HALF = 256


def _dft_gen_kernel(o_ref, cl_ref, sl_ref, *, length):
    a = pl.program_id(0)
    n2 = 4 * length

    def angles(n_vec, shape):
        k = lax.broadcasted_iota(jnp.int32, shape, 1)
        ph = lax.rem(n_vec * (2 * k + 1), n2)
        return ph.astype(F32) * (2.0 * math.pi / n2)

    @pl.when(a == 0)
    def _():
        th = angles(lax.broadcasted_iota(jnp.int32, (DFT_SUB, length), 0), (DFT_SUB, length))
        cl_ref[...] = jnp.cos(th)
        sl_ref[...] = jnp.sin(th)

    th = angles(jnp.full((8, length), a * DFT_SUB, jnp.int32), (8, length))[0:1]
    ch, sh = jnp.cos(th), jnp.sin(th)
    c = ch * cl_ref[...] - sh * sl_ref[...]
    ns = -(sh * cl_ref[...] + ch * sl_ref[...])
    for jt in range(length // DFT_HALF):
        src = slice(jt * DFT_HALF, (jt + 1) * DFT_HALF)
        o_ref[:, 2 * jt * DFT_HALF:(2 * jt + 1) * DFT_HALF] = c[:, src].astype(BF16)
        o_ref[:, (2 * jt + 1) * DFT_HALF:(2 * jt + 2) * DFT_HALF] = ns[:, src].astype(BF16)


def _dft_matrix(length):
    return pl.pallas_call(
        functools.partial(_dft_gen_kernel, length=length),
        out_shape=jax.ShapeDtypeStruct((length, 2 * length), BF16), grid=(length // DFT_SUB,),
        out_specs=pl.BlockSpec((DFT_SUB, 2 * length), lambda a: (a, 0)),
        scratch_shapes=[pltpu.VMEM((DFT_SUB, length), F32)] * 2,
        compiler_params=pltpu.CompilerParams(dimension_semantics=("arbitrary",),
                                             vmem_limit_bytes=VMEM_LIMIT),
        name="dft_gen",
    )()


HYF_RB = 256


def _hy_filter_kernel(w1_ref, w2_ref, w3f_ref, w3b_ref, cols_ref, f_ref, o_ref, a_ref, b_ref, *, length):
    rc, jt = pl.program_id(0), pl.program_id(1)

    @pl.when(jt == 0)
    def _():
        m = lax.broadcasted_iota(jnp.int32, (8, length), 1).astype(F32)[0:1]
        t = m * (1.0 / (length - 1))
        w = m * (2.0 * math.pi / length)
        band = lax.broadcasted_iota(jnp.int32, (HY_BANDS, 1), 0).astype(F32)
        fb = 1e-4 + band * ((HY_BANDS - 1 - 1e-4) / (HY_BANDS - 1))
        z = jnp.concatenate([t, jnp.cos(fb * w), -jnp.sin(fb * w),
                             jnp.zeros((40 - HY_EMB, length), F32)], axis=0)
        cols = cols_ref[...]
        h = jnp.sin(cols[:, 1:2] * (_bdot(w1_ref[...], z) + cols[:, 0:1]))
        h = jnp.sin(cols[:, 3:4] * (_bdot(w2_ref[...], h) + cols[:, 2:3]))
        n_out = 2 * HY_ORDER * HY_C
        row = (rc * HYF_RB + lax.broadcasted_iota(jnp.int32, (HYF_RB, 1), 0)).astype(F32)
        step = (HY_MAX_DECAY - HY_MIN_DECAY) / (n_out - 1)
        d_f = jnp.abs(HY_MIN_DECAY + row * step)
        d_b = jnp.abs(HY_MIN_DECAY + (row + HY_ORDER * HY_C) * step)
        hf = _bdot(w3f_ref[...], h) * jnp.exp(-t * d_f)
        hb = jnp.where(m == 0.0, 0.0, _bdot(w3b_ref[...], h) * jnp.exp(-t * d_b))
        ssq = jnp.sum(hf * hf, axis=-1, keepdims=True) + jnp.sum(hb * hb, axis=-1, keepdims=True)
        scale = lax.rsqrt(ssq + NORM_EPS) * (1.0 / length)
        a_ref[...] = ((hf + hb) * scale).astype(BF16)
        b_ref[...] = ((hf - hb) * scale).astype(BF16)

    o_ref[:, :DFT_HALF] = jnp.dot(a_ref[...], f_ref[:, :DFT_HALF], preferred_element_type=F32)
    o_ref[:, DFT_HALF:] = jnp.dot(b_ref[...], f_ref[:, DFT_HALF:], preferred_element_type=F32)


def _hy_filter(fmat, w1, b1, f1, w2, b2, f2, w3, *, length):
    w1t = jnp.pad(w1.astype(F32).T, ((0, 0), (0, 40 - HY_EMB)))
    w3t = w3.astype(F32).T
    cols = jnp.stack([b1, f1, b2, f2] + [jnp.zeros_like(b1)] * 4, axis=1).astype(F32)
    n_rows = HY_ORDER * HY_C
    const = lambda rc, jt: (0, 0)
    return pl.pallas_call(
        functools.partial(_hy_filter_kernel, length=length),
        out_shape=jax.ShapeDtypeStruct((n_rows, 2 * length), F32),
        grid=(n_rows // HYF_RB, length // DFT_HALF),
        in_specs=[pl.BlockSpec(w1t.shape, const), pl.BlockSpec((w2.shape[1], w2.shape[0]), const),
                  pl.BlockSpec((HYF_RB, w3t.shape[1]), lambda rc, jt: (rc, 0)),
                  pl.BlockSpec((HYF_RB, w3t.shape[1]), lambda rc, jt: (n_rows // HYF_RB + rc, 0)),
                  pl.BlockSpec(cols.shape, const),
                  pl.BlockSpec((length, 2 * DFT_HALF), lambda rc, jt: (0, jt))],
        out_specs=pl.BlockSpec((HYF_RB, 2 * DFT_HALF), lambda rc, jt: (rc, jt)),
        scratch_shapes=[pltpu.VMEM((HYF_RB, length), BF16)] * 2,
        compiler_params=pltpu.CompilerParams(dimension_semantics=("arbitrary", "arbitrary"),
                                             vmem_limit_bytes=VMEM_LIMIT),
        name="hy_filter",
    )(w1t, w2.astype(F32).T, w3t, w3t, cols, fmat)


def _hy_fwd_kernel(z_ref, f_ref, k_ref, y_ref):
    zt = jnp.dot(z_ref[...], f_ref[...], preferred_element_type=F32)
    zr, zi = zt[:, :DFT_HALF], zt[:, DFT_HALF:]
    kr, ki = k_ref[:, :DFT_HALF], k_ref[:, DFT_HALF:]
    y_ref[:, :DFT_HALF] = (zr * kr - zi * ki).astype(BF16)
    y_ref[:, DFT_HALF:] = (zr * ki + zi * kr).astype(BF16)


def _hy_fwd(zb, fmat, kspec, order, *, lane0, n_seq, length):
    sb0 = lane0 // length
    return pl.pallas_call(
        _hy_fwd_kernel,
        out_shape=jax.ShapeDtypeStruct((n_seq * HY_C, 2 * length), BF16),
        grid=(n_seq, length // DFT_HALF),
        in_specs=[pl.BlockSpec((HY_C, length), lambda s, jt: (0, sb0 + s)),
                  pl.BlockSpec((length, 2 * DFT_HALF), lambda s, jt: (0, jt)),
                  pl.BlockSpec((HY_C, 2 * DFT_HALF), lambda s, jt: (order, jt))],
        out_specs=pl.BlockSpec((HY_C, 2 * DFT_HALF), lambda s, jt: (s, jt)),
        compiler_params=pltpu.CompilerParams(dimension_semantics=("arbitrary", "arbitrary"),
                                             vmem_limit_bytes=VMEM_LIMIT),
        name="hy_fwd",
    )(zb, fmat, kspec)


def _hy_inv_kernel(y_ref, f_ref, gate_ref, z_ref, bias_ref, *rest, with_prev, with_bf16):
    rest = list(rest)
    if with_prev:
        rest.pop(0)
    o_ref = rest.pop(0)
    y = lax.dot_general(y_ref[...], f_ref[...], (((1,), (1,)), ((), ())), preferred_element_type=F32)
    out = gate_ref[...] * (y + z_ref[...] * bias_ref[...])
    o_ref[...] = out
    if with_bf16:
        rest.pop(0)[...] = out.astype(BF16)


def _hy_inv(yspec, fmat, gate_arr, gate_rb, z_arr, z_rb, bias, prev, *, lane0, n_seq, length, tt, t_total,
            with_bf16):
    nt = length // tt
    lane_blk = lambda s, j: lane0 // tt + s * nt + j
    in_specs = [pl.BlockSpec((HY_C, 2 * length), lambda s, j: (s, 0)),
                pl.BlockSpec((tt, 2 * length), lambda s, j: (j, 0)),
                pl.BlockSpec((HY_C, tt), lambda s, j: (gate_rb, lane_blk(s, j))),
                pl.BlockSpec((HY_C, tt), lambda s, j: (z_rb, lane_blk(s, j))),
                pl.BlockSpec((HY_C, 1), lambda s, j: (0, 0))]
    args = [yspec, fmat, gate_arr, z_arr, bias.astype(F32).reshape(HY_C, 1)]
    aliases = {}
    if prev is not None:
        in_specs.append(pl.BlockSpec(memory_space=pl.ANY))
        aliases = {len(args): 0}
        args.append(prev)
    out_spec = pl.BlockSpec((HY_C, tt), lambda s, j: (0, lane_blk(s, j)))
    out_shape = [jax.ShapeDtypeStruct((HY_C, t_total), F32)]
    out_specs = [out_spec]
    if with_bf16:
        out_shape.append(jax.ShapeDtypeStruct((HY_C, t_total), BF16))
        out_specs.append(out_spec)
    res = pl.pallas_call(
        functools.partial(_hy_inv_kernel, with_prev=prev is not None, with_bf16=with_bf16),
        out_shape=tuple(out_shape), grid=(n_seq, nt), in_specs=in_specs, out_specs=tuple(out_specs),
        input_output_aliases=aliases,
        compiler_params=pltpu.CompilerParams(dimension_semantics=("arbitrary", "arbitrary"),
                                             vmem_limit_bytes=VMEM_LIMIT),
        name="hy_inv",
    )(*args)
    return res


def _hyena_group(ut, vb, fmat, kspec, hy_bias, y_prev, *, lane0, n_seq, length, t_total):
    tt = min(512, length)
    geo = dict(lane0=lane0, n_seq=n_seq, length=length)
    y1 = _hy_fwd(vb, fmat, kspec, 0, **geo)
    z2, z2b = _hy_inv(y1, fmat, ut, 0, ut, 2, hy_bias[0], None, tt=tt, t_total=t_total, with_bf16=True, **geo)
    y2 = _hy_fwd(z2b, fmat, kspec, 1, **geo)
    return _hy_inv(y2, fmat, ut, 1, z2, 0, hy_bias[1], y_prev, tt=tt, t_total=t_total, with_bf16=False,
                   **geo)[0]


def _ah_outproj_kernel(x_ref, a_ref, yt_ref, mod_ref, wa_ref, wy_ref, o_ref):
    o = jnp.dot(a_ref[...], wa_ref[...], preferred_element_type=F32)
    o = o + lax.dot_general(yt_ref[...].astype(BF16), wy_ref[...], (((0,), (0,)), ((), ())),
                            preferred_element_type=F32)
    o_ref[...] = x_ref[...] + mod_ref[0][5:6] * o


def _ah_outproj(x, a, yt, mod, row_of_tile, w_out, *, tm):
    t = x.shape[0]
    tile = lambda i: (i, 0)
    const = lambda i: (0, 0)
    wa, wy = w_out[:AH_Q].astype(BF16), w_out[AH_Q:].astype(BF16)
    return pl.pallas_call(
        _ah_outproj_kernel, out_shape=jax.ShapeDtypeStruct((t, D_MODEL), F32), grid=(t // tm,),
        in_specs=[pl.BlockSpec((tm, D_MODEL), tile), pl.BlockSpec((tm, AH_Q), tile),
                  pl.BlockSpec((HY_C, tm), lambda i: (0, i)),
                  pl.BlockSpec((1, 9, D_MODEL), lambda i: (row_of_tile(i), 0, 0)),
                  pl.BlockSpec(wa.shape, const), pl.BlockSpec(wy.shape, const)],
        out_specs=pl.BlockSpec((tm, D_MODEL), tile),
        compiler_params=pltpu.CompilerParams(dimension_semantics=("arbitrary",),
                                             vmem_limit_bytes=VMEM_LIMIT),
        name="ah_outproj",
    )(x, a, yt, mod, wa, wy)


def _ah_layer(x, mod, row_of_tile, g, w_in, w_out, q_norm, k_norm, sink, conv_w, conv_b, hy_bias,
              ck, cv, rope, fmats, kspecs, *, tm, tq, tp, lp, ls):
    t = x.shape[0]
    q, kt, vt, kn, vn, ut, vb = _ah_inproj(x, mod, row_of_tile, g, w_in, q_norm, k_norm, conv_w, conv_b, rope,
                                           tm=tm, tp=tp, lp=lp, ls=ls)
    a = _attention(q, kt, vt, ck, cv, sink, tq=tq, tp=tp, lp=lp, ls=ls)
    yt = _hyena_group(ut, vb, fmats[0], kspecs[0], hy_bias, None, lane0=0, n_seq=tp // lp, length=lp, t_total=t)
    yt = _hyena_group(ut, vb, fmats[1], kspecs[1], hy_bias, yt, lane0=tp, n_seq=(t - tp) // ls, length=ls,
                      t_total=t)
    return _ah_outproj(x, a, yt, mod, row_of_tile, w_out, tm=tm), kn[:tp], vn[:tp]


def _rep4_ctx(c):
    b, s = c.shape[:2]
    return jnp.broadcast_to(c[:, :, :, None, :], (b, s, KVH_A, G_A, HD_A)).reshape(b, s, AH_Q).astype(BF16)


def _ada_kernel(c_ref, w_ref, b_ref, o_ref):
    s = jax.nn.silu(c_ref[...]).astype(BF16)
    o_ref[0] = jnp.dot(s, w_ref[0].astype(BF16), preferred_element_type=F32) + b_ref[0]


def _ada_mod(cond, ada_w, ada_b):
    depth, _, n = ada_w.shape
    rows = 16
    cp = jnp.pad(cond.astype(F32), ((0, rows - cond.shape[0]), (0, 0)))
    out = pl.pallas_call(
        _ada_kernel, out_shape=jax.ShapeDtypeStruct((depth, rows, n), F32),
        grid=(depth, n // D_MODEL),
        in_specs=[pl.BlockSpec((rows, D_MODEL), lambda l, j: (0, 0)),
                  pl.BlockSpec((1, D_MODEL, D_MODEL), lambda l, j: (l, 0, j)),
                  pl.BlockSpec((1, 1, D_MODEL), lambda l, j: (l, 0, j))],
        out_specs=pl.BlockSpec((1, rows, D_MODEL), lambda l, j: (l, 0, j)),
        compiler_params=pltpu.CompilerParams(dimension_semantics=("arbitrary", "arbitrary"),
                                             vmem_limit_bytes=VMEM_LIMIT),
        name="ada_mod",
    )(cp, ada_w, ada_b.reshape(depth, 1, n))
    return out.reshape(depth, rows, n // D_MODEL, D_MODEL)


def kernel(x_prompt, x_sample, cache_k, cache_v, state_fwd, state_bwd, c, c_ctx, norm_g, ada_w, ada_b, ffn_w13, ffn_w2, mx_w_in, mx_w_out, q_norm, k_norm, attn_sink, hy_conv_w, hy_conv_b, hy_w1, hy_b1, hy_freq1, hy_w2, hy_b2, hy_freq2, hy_w3, hy_bias, dn_w_in, dn_w_out, dn_conv_w, dn_a_log, dn_dt_bias, dn_norm_g):
    bp, lp, _ = x_prompt.shape
    bs, ls, _ = x_sample.shape
    tp, ts = bp * lp, bs * ls
    assert tp % ls == 0 or tp % lp == 0
    tm = math.gcd(TOKEN_TILE, math.gcd(tp, ls))
    ct = math.gcd(DN_BLOCK, math.gcd(lp, ls))
    tq = math.gcd(ATTN_TQ, ls)
    x = jnp.concatenate([x_prompt.reshape(tp, D_MODEL), x_sample.reshape(ts, D_MODEL)], axis=0)
    cond = jnp.concatenate([c, c_ctx[None, :]], axis=0)
    mods = _ada_mod(cond, ada_w, ada_b)
    tiles_p, tiles_per_s = tp // tm, ls // tm

    def row_of_tile(i):
        return jnp.where(i < tiles_p, bs, (i - tiles_p) // tiles_per_s)

    rope = _rope_tables(ls)
    fmats = (_dft_matrix(lp), _dft_matrix(ls))
    new_k, new_v, new_sf, new_sb = [], [], [], []
    for layer in range(DEPTH):
        mod = mods[layer]
        i = layer // 2
        x = _ffn(x, mod, row_of_tile, norm_g[layer, 0], ffn_w13[layer, 0], ffn_w2[layer, 0], 0, tm=tm)
        if layer % 2 == 0:
            kspecs = tuple(_hy_filter(f, hy_w1[i], hy_b1[i], hy_freq1[i], hy_w2[i], hy_b2[i], hy_freq2[i],
                                      hy_w3[i], length=n) for f, n in zip(fmats, (lp, ls)))
            x, k_p, v_p = _ah_layer(x, mod, row_of_tile, norm_g[layer, 1], mx_w_in[i], mx_w_out[i], q_norm[i],
                                    k_norm[i], attn_sink[i], hy_conv_w[i], hy_conv_b[i], hy_bias[i],
                                    _rep4_ctx(cache_k[:, i]), _rep4_ctx(cache_v[:, i]), rope, fmats, kspecs,
                                    tm=tm, tq=tq, tp=tp, lp=lp, ls=ls)
            new_k.append(k_p.reshape(bp, lp, KVH_A, HD_A))
            new_v.append(v_p.reshape(bp, lp, KVH_A, HD_A))
        else:
            x, s_f, s_b = _dn_layer(x, mod, row_of_tile, norm_g[layer, 1], dn_w_in[i], dn_w_out[i],
                                    dn_conv_w[i], dn_a_log[i], dn_dt_bias[i], dn_norm_g[i],
                                    state_fwd[:, i], state_bwd[:, i], tm=tm, ct=ct, tp=tp, lp=lp, ls=ls)
            new_sf.append(s_f)
            new_sb.append(s_b)
        x = _ffn(x, mod, row_of_tile, norm_g[layer, 2], ffn_w13[layer, 1], ffn_w2[layer, 1], 2, tm=tm)

    return (x[:tp].reshape(bp, lp, D_MODEL), x[tp:].reshape(bs, ls, D_MODEL),
            jnp.stack(new_k, axis=1), jnp.stack(new_v, axis=1),
            jnp.stack(new_sf, axis=1), jnp.stack(new_sb, axis=1))
```

```python
import functools
import math

import jax
import jax.numpy as jnp
from jax import lax
from jax.experimental import pallas as pl
from jax.experimental.pallas import tpu as pltpu

D_MODEL = 1024
DEPTH = 4
GRID_W = 64
H_A = 8
KVH_A = 2
G_A = H_A // KVH_A
HD_A = 64
WINDOW = 128
ATTN_BLOCK = 128
ROPE_THETA = 10000.0
ROPE_AXIS_DIM = HD_A // 2
HY_C = 512
HY_ORDER = 2
HY_EMB = 33
HY_BANDS = (HY_EMB - 1) // 2
HY_MIN_DECAY = math.log(1e-2) / 1.5
HY_MAX_DECAY = math.log(1e-2) / 0.3
H_C = 8
DK_C = 128
DV_C = 128
DN_CHUNK = 64
DN_OUT = H_C * DV_C
D_FF = 2816
NORM_EPS = 1e-6
NEG_INF = -1e30

F32 = jnp.float32
BF16 = jnp.bfloat16

TOKEN_TILE = 512
FFN_CHUNK = 256
DN_BLOCK = 128
ATTN_TQ = 256
VMEM_LIMIT = 56 * 1024 * 1024


def _ffn_kernel(x_ref, mod_ref, g_ref, w1_ref, w3_ref, w2_ref, o_ref, acc_ref, *, j, n_chunks):
    x = x_ref[...]
    y = x * lax.rsqrt(jnp.mean(x * x, axis=-1, keepdims=True) + NORM_EPS) * g_ref[...]
    m = mod_ref[0]
    shift, scale, gate = m[3 * j:3 * j + 1], m[3 * j + 1:3 * j + 2], m[3 * j + 2:3 * j + 3]
    h = (y * (1.0 + scale) + shift).astype(BF16)
    acc_ref[...] = jnp.zeros_like(acc_ref)

    def body(c, carry):
        gt = jnp.dot(h, w1_ref[c], preferred_element_type=F32)
        up = jnp.dot(h, w3_ref[c], preferred_element_type=F32)
        a = (jax.nn.silu(gt) * up).astype(BF16)
        acc_ref[...] += jnp.dot(a, w2_ref[c], preferred_element_type=F32)
        return carry

    lax.fori_loop(0, n_chunks, body, 0, unroll=True)
    o_ref[...] = x + 0.5 * gate * acc_ref[...]


def _ffn(x, mod, row_of_tile, g, w13, w2, j, *, tm):
    t = x.shape[0]
    n_chunks = D_FF // FFN_CHUNK
    w1 = w13[:, :D_FF].astype(BF16).reshape(D_MODEL, n_chunks, FFN_CHUNK).transpose(1, 0, 2)
    w3 = w13[:, D_FF:].astype(BF16).reshape(D_MODEL, n_chunks, FFN_CHUNK).transpose(1, 0, 2)
    w2c = w2.astype(BF16).reshape(n_chunks, FFN_CHUNK, D_MODEL)
    const3 = lambda i: (0, 0, 0)
    return pl.pallas_call(
        functools.partial(_ffn_kernel, j=j, n_chunks=n_chunks),
        out_shape=jax.ShapeDtypeStruct((t, D_MODEL), F32),
        grid=(t // tm,),
        in_specs=[
            pl.BlockSpec((tm, D_MODEL), lambda i: (i, 0)),
            pl.BlockSpec((1, 9, D_MODEL), lambda i: (row_of_tile(i), 0, 0)),
            pl.BlockSpec((1, D_MODEL), lambda i: (0, 0)),
            pl.BlockSpec((n_chunks, D_MODEL, FFN_CHUNK), const3),
            pl.BlockSpec((n_chunks, D_MODEL, FFN_CHUNK), const3),
            pl.BlockSpec((n_chunks, FFN_CHUNK, D_MODEL), const3),
        ],
        out_specs=pl.BlockSpec((tm, D_MODEL), lambda i: (i, 0)),
        scratch_shapes=[pltpu.VMEM((tm, D_MODEL), F32)],
        compiler_params=pltpu.CompilerParams(dimension_semantics=("arbitrary",),
                                             vmem_limit_bytes=VMEM_LIMIT),
        name=f"ffn{j}",
    )(x, mod, g.reshape(1, D_MODEL), w1, w3, w2c)


def _bdot(a, b):
    return jnp.dot(a.astype(BF16), b.astype(BF16), preferred_element_type=F32)


def _bdot_nt(a, b):
    return lax.dot_general(a.astype(BF16), b.astype(BF16), (((1,), (1,)), ((), ())),
                           preferred_element_type=F32)


def _bdot_tn(a, b):
    return lax.dot_general(a.astype(BF16), b.astype(BF16), (((0,), (0,)), ((), ())),
                           preferred_element_type=F32)


def _split3(x):
    hi = x.astype(BF16)
    r1 = x - hi.astype(F32)
    mid = r1.astype(BF16)
    lo = (r1 - mid.astype(F32)).astype(BF16)
    return hi, mid, lo


def _ada_h(x, g_row, m, j):
    y = x * lax.rsqrt(jnp.mean(x * x, axis=-1, keepdims=True) + NORM_EPS) * g_row
    return y * (1.0 + m[3 * j + 1:3 * j + 2]) + m[3 * j:3 * j + 1]


def _seq_edges(i, tm, n_tiles_p, tp, lp, ls, shape, axis):
    is_p = i < n_tiles_p
    seq_len = jnp.where(is_p, lp, ls)
    t0 = i * tm - jnp.where(is_p, 0, tp)
    base = lax.rem(t0, seq_len)
    pos = (base + lax.broadcasted_iota(jnp.int32, shape, axis)).astype(F32)
    lf = seq_len.astype(F32)
    rem = pos - jnp.floor((pos + 0.5) / lf) * lf
    return rem == 0.0, rem == lf - 1.0


def _halo_specs(tm, t):
    nb = t // 8
    prev = pl.BlockSpec((8, D_MODEL), lambda i: (jnp.maximum(i * (tm // 8) - 1, 0), 0))
    nxt = pl.BlockSpec((8, D_MODEL), lambda i: (jnp.minimum((i + 1) * (tm // 8), nb - 1), 0))
    return prev, nxt


DN_QKV = 2 * H_C * DK_C + H_C * DV_C
DN_CW = 512


def _dn_inproj_kernel(xp_ref, x_ref, xn_ref, mod_ref, g_ref, w_ref, cw_ref, ab_ref,
                      q_ref, k_ref, v_ref, z_ref, bg_ref, pext_ref, *, tm, n_tiles_p, tp, lp, ls):
    i = pl.program_id(0)
    xe = jnp.concatenate([xp_ref[...], x_ref[...], xn_ref[...]], axis=0)
    h = _ada_h(xe, g_ref[...], mod_ref[0], 1).astype(BF16)
    first, last = _seq_edges(i, tm, n_tiles_p, tp, lp, ls, (tm, 1), 0)
    nq = H_C * DK_C
    for c0 in range(0, DN_QKV, DN_CW):
        pext_ref[...] = jnp.dot(h, w_ref[:, c0:c0 + DN_CW], preferred_element_type=F32)
        cw = cw_ref[:, c0:c0 + DN_CW]
        c = (jnp.where(first, 0.0, pext_ref[7:tm + 7, :]) * cw[0:1]
             + pext_ref[8:tm + 8, :] * cw[1:2]
             + jnp.where(last, 0.0, pext_ref[9:tm + 9, :]) * cw[2:3])
        a = jax.nn.silu(c)
        for b0 in range(0, DN_CW, DK_C):
            col = c0 + b0
            blk = a[:, b0:b0 + DK_C]
            if col < 2 * nq:
                blk = blk * lax.rsqrt(jnp.sum(blk * blk, axis=-1, keepdims=True) + NORM_EPS)
            if col < nq:
                q_ref[:, col:col + DK_C] = blk * (DK_C ** -0.5)
            elif col < 2 * nq:
                k_ref[:, col - nq:col - nq + DK_C] = blk
            else:
                v_ref[:, col - 2 * nq:col - 2 * nq + DK_C] = blk
    hm = h[8:tm + 8]
    for c0 in range(0, DN_OUT, DN_CW):
        z_ref[:, c0:c0 + DN_CW] = jnp.dot(hm, w_ref[:, DN_QKV + c0:DN_QKV + c0 + DN_CW],
                                           preferred_element_type=F32)
    r = jnp.dot(hm, w_ref[:, DN_QKV + DN_OUT:], preferred_element_type=F32)
    lane = lax.broadcasted_iota(jnp.int32, r.shape, 1)
    xs = r + ab_ref[1:2]
    softplus = jnp.maximum(xs, 0.0) + jnp.log1p(jnp.exp(-jnp.abs(xs)))
    bg_ref[...] = jnp.where(lane < 2 * H_C, jax.nn.sigmoid(r),
                            jnp.where(lane < 4 * H_C, -ab_ref[0:1] * softplus, 0.0))


def _dn_inproj(x, mod, row_of_tile, g, w_in, conv_w, a_log, dt_bias, *, tm, tp, lp, ls):
    t = x.shape[0]
    pad = 128 - 4 * H_C
    w = jnp.pad(w_in, ((0, 0), (0, pad))).astype(BF16)
    ab = jnp.zeros((2, 128), F32)
    ab = ab.at[0, 2 * H_C:4 * H_C].set(jnp.exp(a_log.astype(F32)).reshape(-1))
    ab = ab.at[1, 2 * H_C:4 * H_C].set(dt_bias.astype(F32).reshape(-1))
    prev, nxt = _halo_specs(tm, t)
    tile = lambda i: (i, 0)
    const = lambda i: (0, 0)
    wide = jax.ShapeDtypeStruct((t, DN_OUT), F32)
    return pl.pallas_call(
        functools.partial(_dn_inproj_kernel, tm=tm, n_tiles_p=tp // tm, tp=tp, lp=lp, ls=ls),
        out_shape=(wide, wide, wide, wide, jax.ShapeDtypeStruct((t, 128), F32)),
        grid=(t // tm,),
        in_specs=[prev, pl.BlockSpec((tm, D_MODEL), tile), nxt,
                  pl.BlockSpec((1, 9, D_MODEL), lambda i: (row_of_tile(i), 0, 0)),
                  pl.BlockSpec((1, D_MODEL), const),
                  pl.BlockSpec(w.shape, const),
                  pl.BlockSpec(conv_w.shape, const),
                  pl.BlockSpec((2, 128), const)],
        out_specs=(pl.BlockSpec((tm, DN_OUT), tile),) * 4 + (pl.BlockSpec((tm, 128), tile),),
        scratch_shapes=[pltpu.VMEM((tm + 16, DN_CW), F32)],
        compiler_params=pltpu.CompilerParams(dimension_semantics=("arbitrary",),
                                             vmem_limit_bytes=VMEM_LIMIT),
        name="dn_inproj",
    )(x, x, x, mod, g.reshape(1, D_MODEL), w, conv_w.astype(F32), ab)


DN_GROUP = 4


def _dn_masks(rev):
    n = DN_CHUNK
    r_i = lax.broadcasted_iota(jnp.int32, (n, n), 0)
    c_i = lax.broadcasted_iota(jnp.int32, (n, n), 1)
    tri = jnp.where((r_i <= c_i) if rev else (r_i >= c_i), 1.0, 0.0).astype(BF16)
    rows = DN_GROUP * n
    rr = lax.broadcasted_iota(jnp.int32, (rows, rows), 0)
    cc = lax.broadcasted_iota(jnp.int32, (rows, rows), 1)
    same = (rr // n) == (cc // n)
    incl = same & ((rr <= cc) if rev else (rr >= cc))
    strict = same & ((rr < cc) if rev else (rr > cc))
    eye = jnp.where(rr == cc, 1.0, 0.0)
    return tri, incl, strict, eye


def _dn_prepare(qc, kc, vc, bgc, masks, rev):
    n = DN_CHUNK
    tri, incl, strict, eye = masks
    gct = sum(jnp.dot(tri, p, preferred_element_type=F32) for p in _split3(bgc))
    rows = DN_GROUP * n
    lane = lax.broadcasted_iota(jnp.int32, (rows, 128), 1)
    last = 0 if rev else n - 1
    prep = []
    for gi in range(H_C // DN_GROUP):
        heads = range(gi * DN_GROUP, (gi + 1) * DN_GROUP)
        sb = [(H_C if rev else 0) + h for h in heads]
        sg = [2 * H_C + (H_C if rev else 0) + h for h in heads]
        beta = jnp.concatenate([bgc[:, s:s + 1] for s in sb], axis=0)
        gc = jnp.concatenate([gct[:, s:s + 1] for s in sg], axis=0)
        gl = jnp.concatenate([jnp.broadcast_to(gct[last:last + 1, s:s + 1], (n, 1)) for s in sg], axis=0)
        q4 = jnp.concatenate([qc[:, h * DK_C:(h + 1) * DK_C] for h in heads], axis=0)
        k4 = jnp.concatenate([kc[:, h * DK_C:(h + 1) * DK_C] for h in heads], axis=0)
        v4 = jnp.concatenate([vc[:, h * DV_C:(h + 1) * DV_C] for h in heads], axis=0)
        hi, mid, lo = (p.astype(F32) for p in _split3(gc))
        one = jnp.where(lane < 6, 1.0, 0.0)
        u_m = jnp.where(lane == 0, hi, jnp.where(lane == 1, mid, jnp.where(lane == 2, lo, one)))
        v_m = jnp.where(lane == 3, -hi, jnp.where(lane == 4, -mid, jnp.where(lane == 5, -lo, one)))
        gd = _bdot_nt(jnp.where(lane < 6, u_m, 0.0), jnp.where(lane < 6, v_m, 0.0))
        decay = jnp.where(incl, jnp.exp(jnp.where(incl, gd, 0.0)), 0.0)
        kb = k4 * beta
        lm = jnp.where(strict, _bdot_nt(kb, k4) * decay, 0.0)
        x = jnp.concatenate([v4 * beta, kb * jnp.exp(gc)], axis=1)
        qe = q4 * jnp.exp(gc)
        ke = (k4 * jnp.exp(gl - gc)).astype(BF16)
        aqk = jnp.where(incl, _bdot_nt(q4, k4) * decay, 0.0).astype(BF16)
        prep.append(dict(lm=lm, x=x, qe=qe, ke=ke, aqk=aqk, egl=jnp.exp(gl)))
    return prep


def _dn_solve(groups, eye):
    n = DN_CHUNK
    dot = functools.partial(jnp.dot, preferred_element_type=F32)
    for g in groups:
        g["lh"] = g["lm"].astype(BF16)
        g["ll"] = (g["lm"] - g["lh"].astype(F32)).astype(BF16)
        g["p"] = -g["lh"]
        g["t"] = eye - g["lm"]
    for _ in range(5):
        for g in groups:
            g["p"] = dot(g["p"], g["p"]).astype(BF16)
        for g in groups:
            g["t"] = g["t"] + dot(g["t"].astype(BF16), g["p"])
    for g in groups:
        g["tb"] = g["t"].astype(BF16)
        g["x0"] = dot(g["tb"], g["x"].astype(BF16))
    for g in groups:
        xh = g["x0"].astype(BF16)
        xl = (g["x0"] - xh.astype(F32)).astype(BF16)
        g["res"] = g["x"] - g["x0"] - (dot(g["lh"], xh) + dot(g["lh"], xl) + dot(g["ll"], xh))
    out = []
    for g in groups:
        x = g["x0"] + dot(g["tb"], g["res"].astype(BF16))
        u4, w4, qe = x[:, :DV_C], x[:, DV_C:], g["qe"]
        wqe = [jnp.concatenate([w4[j * n:(j + 1) * n], qe[j * n:(j + 1) * n]], axis=0).astype(BF16)
               for j in range(DN_GROUP)]
        out.append((u4, wqe, g["ke"], g["aqk"], g["egl"]))
    return out


def _dn_advance(prep, s_ref):
    n = DN_CHUNK
    outs = []
    for gi, (u4, wqe, ke, aqk, egl) in enumerate(prep):
        vn, qs = [], []
        for j in range(DN_GROUP):
            wq = jnp.dot(wqe[j], s_ref[gi * DN_GROUP + j].astype(BF16), preferred_element_type=F32)
            vn.append(u4[j * n:(j + 1) * n] - wq[:n])
            qs.append(wq[n:])
        o4 = jnp.concatenate(qs, axis=0) + jnp.dot(aqk, jnp.concatenate(vn, axis=0).astype(BF16),
                                                   preferred_element_type=F32)
        for j in range(DN_GROUP):
            h = gi * DN_GROUP + j
            sl = slice(j * n, (j + 1) * n)
            s_ref[h] = s_ref[h] * egl[j * n:j * n + 1] + lax.dot_general(
                ke[sl], vn[j].astype(BF16), (((0,), (0,)), ((), ())), preferred_element_type=F32)
            outs.append(o4[sl])
    return outs


def _dn_scan_kernel(*refs, n_chunks, blocks_per_seq, zero_init, has_prev, write_state):
    refs = list(refs)
    fwd_in, bwd_in = refs[:4], refs[4:8]
    del refs[:8]
    s0f_ref, s0b_ref = (None, None) if zero_init else (refs.pop(0), refs.pop(0))
    if has_prev:
        del refs[:2]
    of_ref, ob_ref = refs.pop(0), refs.pop(0)
    sfo_ref, sbo_ref = (refs.pop(0), refs.pop(0)) if write_state else (None, None)
    sf_ref, sb_ref = refs
    i = pl.program_id(0)
    f_in_seq = lax.rem(i, blocks_per_seq)
    r_in_seq = lax.rem(pl.num_programs(0) - 1 - i, blocks_per_seq)

    @pl.when(f_in_seq == 0)
    def _():
        sf_ref[...] = jnp.zeros_like(sf_ref) if zero_init else s0f_ref[0]

    @pl.when(r_in_seq == blocks_per_seq - 1)
    def _():
        sb_ref[...] = jnp.zeros_like(sb_ref) if zero_init else s0b_ref[0]

    def chunk(io, c):
        rows = slice(c * DN_CHUNK, (c + 1) * DN_CHUNK)
        return [r[rows, :] for r in io]

    mf, mb = _dn_masks(False), _dn_masks(True)
    raw = [_dn_prepare(*chunk(fwd_in, c), mf, False) for c in range(n_chunks)]
    raw += [_dn_prepare(*chunk(bwd_in, c), mb, True) for c in range(n_chunks)]
    n_groups = H_C // DN_GROUP
    solved = _dn_solve([g for r in raw for g in r], mf[3])
    per_chunk = [solved[j * n_groups:(j + 1) * n_groups] for j in range(2 * n_chunks)]
    prep_f, prep_b = per_chunk[:n_chunks], per_chunk[n_chunks:]
    for step in range(n_chunks):
        for o_ref, prep, s_ref, c in ((of_ref, prep_f, sf_ref, step),
                                      (ob_ref, prep_b, sb_ref, n_chunks - 1 - step)):
            for h, o in enumerate(_dn_advance(prep[c], s_ref)):
                o_ref[c * DN_CHUNK:(c + 1) * DN_CHUNK, h * DV_C:(h + 1) * DV_C] = o

    if write_state:
        @pl.when(f_in_seq == blocks_per_seq - 1)
        def _():
            sfo_ref[0] = sf_ref[...]

        @pl.when(r_in_seq == 0)
        def _():
            sbo_ref[0] = sb_ref[...]


def _dn_scan(q, k, v, bg, s0_f, s0_b, prev, *, row0, n_seq, seq_len, ct, write_state):
    t = q.shape[0]
    bps = seq_len // ct
    nblk = n_seq * bps
    b0 = row0 // ct
    zero_init = s0_f is None
    fwd = lambda i: (b0 + i, 0)
    bwd = lambda i: (b0 + nblk - 1 - i, 0)
    seq_f = lambda i: (i // bps, 0, 0, 0)
    seq_b = lambda i: ((nblk - 1 - i) // bps, 0, 0, 0)
    in_specs, args = [], []
    for rows in (fwd, bwd):
        in_specs += [pl.BlockSpec((ct, DN_OUT), rows)] * 3 + [pl.BlockSpec((ct, 128), rows)]
        args += [q, k, v, bg]
    state_blk = (1, H_C, DK_C, DV_C)
    if not zero_init:
        in_specs += [pl.BlockSpec(state_blk, seq_f), pl.BlockSpec(state_blk, seq_b)]
        args += [s0_f, s0_b]
    aliases = {}
    if prev is not None:
        aliases = {len(args): 0, len(args) + 1: 1}
        in_specs += [pl.BlockSpec(memory_space=pl.ANY)] * 2
        args += list(prev)
    out_shape = [jax.ShapeDtypeStruct((t, DN_OUT), F32)] * 2
    out_specs = [pl.BlockSpec((ct, DN_OUT), fwd), pl.BlockSpec((ct, DN_OUT), bwd)]
    if write_state:
        out_shape += [jax.ShapeDtypeStruct((n_seq,) + state_blk[1:], F32)] * 2
        out_specs += [pl.BlockSpec(state_blk, seq_f), pl.BlockSpec(state_blk, seq_b)]
    kern = functools.partial(_dn_scan_kernel, n_chunks=ct // DN_CHUNK, blocks_per_seq=bps,
                             zero_init=zero_init, has_prev=prev is not None, write_state=write_state)
    return pl.pallas_call(
        kern, out_shape=tuple(out_shape), grid=(nblk,), in_specs=in_specs, out_specs=tuple(out_specs),
        scratch_shapes=[pltpu.VMEM((H_C, DK_C, DV_C), F32)] * 2,
        input_output_aliases=aliases,
        compiler_params=pltpu.CompilerParams(dimension_semantics=("arbitrary",),
                                             vmem_limit_bytes=VMEM_LIMIT),
        name="dn_scan",
    )(*args)


def _dn_outproj_kernel(x_ref, of_ref, ob_ref, z_ref, mod_ref, ng_ref, w_ref, y_ref):
    o = of_ref[...] + ob_ref[...]
    parts = []
    for h in range(H_C):
        blk = o[:, h * DV_C:(h + 1) * DV_C]
        parts.append(blk * lax.rsqrt(jnp.mean(blk * blk, axis=-1, keepdims=True) + NORM_EPS))
    y = jnp.concatenate(parts, axis=1) * ng_ref[...] * jax.nn.silu(z_ref[...])
    gate = mod_ref[0][5:6]
    y_ref[...] = x_ref[...] + gate * jnp.dot(y.astype(BF16), w_ref[...], preferred_element_type=F32)


def _dn_outproj(x, o_f, o_b, z, mod, row_of_tile, norm_g, w_out, *, tm):
    t = x.shape[0]
    tile = lambda i: (i, 0)
    const = lambda i: (0, 0)
    return pl.pallas_call(
        _dn_outproj_kernel,
        out_shape=jax.ShapeDtypeStruct((t, D_MODEL), F32),
        grid=(t // tm,),
        in_specs=[pl.BlockSpec((tm, D_MODEL), tile), pl.BlockSpec((tm, DN_OUT), tile),
                  pl.BlockSpec((tm, DN_OUT), tile), pl.BlockSpec((tm, DN_OUT), tile),
                  pl.BlockSpec((1, 9, D_MODEL), lambda i: (row_of_tile(i), 0, 0)),
                  pl.BlockSpec((1, DN_OUT), const), pl.BlockSpec((DN_OUT, D_MODEL), const)],
        out_specs=pl.BlockSpec((tm, D_MODEL), tile),
        compiler_params=pltpu.CompilerParams(dimension_semantics=("arbitrary",),
                                             vmem_limit_bytes=VMEM_LIMIT),
        name="dn_outproj",
    )(x, o_f, o_b, z, mod, jnp.tile(norm_g.astype(F32), H_C).reshape(1, DN_OUT), w_out.astype(BF16))


def _dn_layer(x, mod, row_of_tile, g, w_in, w_out, conv_w, a_log, dt_bias, norm_g, s0_f, s0_b,
              *, tm, ct, tp, lp, ls):
    t = x.shape[0]
    q, k, v, z, bg = _dn_inproj(x, mod, row_of_tile, g, w_in, conv_w, a_log, dt_bias, tm=tm, tp=tp, lp=lp, ls=ls)
    bp, bs = tp // lp, (t - tp) // ls
    o_f, o_b, sf, sb = _dn_scan(q, k, v, bg, None, None, None, row0=0, n_seq=bp, seq_len=lp, ct=ct,
                                write_state=True)
    o_f, o_b = _dn_scan(q, k, v, bg, s0_f.astype(F32), s0_b.astype(F32), (o_f, o_b), row0=tp, n_seq=bs,
                        seq_len=ls, ct=ct, write_state=False)
    return _dn_outproj(x, o_f, o_b, z, mod, row_of_tile, norm_g, w_out, tm=tm), sf, sb


AH_Q = H_A * HD_A
AH_KV = KVH_A * HD_A
AH_U3 = 3 * HY_C


def _group_mean_sq(x, bd):
    sq = x * x
    hi = sq.astype(BF16)
    lo = (sq - hi.astype(F32)).astype(BF16)
    return (jnp.dot(hi, bd, preferred_element_type=F32) + jnp.dot(lo, bd, preferred_element_type=F32)) * (1.0 / HD_A)


def _rope_lanes(x, cos_t, sin_t):
    w = x.shape[1]
    lane = lax.broadcasted_iota(jnp.int32, x.shape, 1)
    low = lax.rem(lane, HD_A) < HD_A // 2
    partner = jnp.where(low, pltpu.roll(x, w - HD_A // 2, 1), pltpu.roll(x, HD_A // 2, 1))
    reps = w // cos_t.shape[1]
    return x * jnp.concatenate([cos_t] * reps, axis=1) + partner * jnp.concatenate([sin_t] * reps, axis=1)


def _rep4(x):
    lane = lax.broadcasted_iota(jnp.int32, x.shape, 1)
    sw = pltpu.roll(x, HD_A, 1)
    a = jnp.where(lane < HD_A, x, sw)
    b = jnp.where(lane < HD_A, sw, x)
    return jnp.concatenate([a, a, b, b], axis=1)


def _ah_inproj_kernel(xp_ref, x_ref, xn_ref, mod_ref, g_ref, w_ref, wut_ref, gains_ref, cos_ref, sin_ref,
                      bdq_ref, cwb_ref, q_ref, kt_ref, vt_ref, kn_ref, vn_ref, u_ref, vb_ref,
                      *, tm, n_tiles_p, tp, lp, ls):
    i = pl.program_id(0)
    is_p = i < n_tiles_p
    xe = jnp.concatenate([xp_ref[...], x_ref[...], xn_ref[...]], axis=0)
    he = _ada_h(xe, g_ref[...], mod_ref[0], 1).astype(BF16)
    hm = he[8:tm + 8]
    p = jnp.dot(hm, w_ref[...], preferred_element_type=F32)
    cos_t, sin_t = cos_ref[...], sin_ref[...]
    q = p[:, :AH_Q]
    q = q * lax.rsqrt(_group_mean_sq(q, bdq_ref[...]) + NORM_EPS) * gains_ref[0:1, :]
    q = jnp.where(is_p, q, _rope_lanes(q, cos_t, sin_t))
    q_ref[...] = (q * (HD_A ** -0.5)).astype(BF16)
    k = p[:, AH_Q:AH_Q + AH_KV]
    k = k * lax.rsqrt(_group_mean_sq(k, bdq_ref[:AH_KV, :AH_KV]) + NORM_EPS) * gains_ref[1:2, :AH_KV]
    kn_ref[...] = k
    k = jnp.where(is_p, k, _rope_lanes(k, cos_t, sin_t))
    kt_ref[...] = _rep4(k).astype(BF16)
    v = p[:, AH_Q + AH_KV:]
    vn_ref[...] = v
    vt_ref[...] = _rep4(v).astype(BF16)
    halo = jnp.concatenate([he[0:8], he[tm + 8:tm + 16], jnp.zeros((112, D_MODEL), BF16)], axis=0)
    first, last = _seq_edges(i, tm, n_tiles_p, tp, lp, ls, (1, tm), 1)
    lane = lax.broadcasted_iota(jnp.int32, (1, tm), 1)
    for r0 in range(0, AH_U3, HY_C):
        wu = wut_ref[r0:r0 + HY_C, :]
        u = lax.dot_general(wu, hm, (((1,), (1,)), ((), ())), preferred_element_type=F32)
        uh = lax.dot_general(wu, halo, (((1,), (1,)), ((), ())), preferred_element_type=F32)
        left = jnp.where(lane == 0, uh[:, 7:8], pltpu.roll(u, 1, 1))
        right = jnp.where(lane == tm - 1, uh[:, 8:9], pltpu.roll(u, tm - 1, 1))
        cwb = cwb_ref[r0:r0 + HY_C, :]
        c = (jnp.where(first, 0.0, left) * cwb[:, 0:1] + u * cwb[:, 1:2]
             + jnp.where(last, 0.0, right) * cwb[:, 2:3] + cwb[:, 3:4])
        u_ref[r0:r0 + HY_C, :] = c
        if r0 == 2 * HY_C:
            vb_ref[...] = c.astype(BF16)


def _rope_tables(ls):
    rows = ls // GRID_W
    r, col = jnp.meshgrid(jnp.arange(rows), jnp.arange(GRID_W), indexing='ij')
    inv = ROPE_THETA ** (-jnp.arange(0, ROPE_AXIS_DIM, 2, dtype=F32) / ROPE_AXIS_DIM)
    ang = jnp.concatenate([r.reshape(-1, 1).astype(F32) * inv, col.reshape(-1, 1).astype(F32) * inv], axis=-1)
    cos, sin = jnp.cos(ang), jnp.sin(ang)
    return jnp.concatenate([cos, cos] * 2, axis=1), jnp.concatenate([-sin, sin] * 2, axis=1)


def _ah_inproj(x, mod, row_of_tile, g, w_in, q_norm, k_norm, conv_w, conv_b, rope, *, tm, tp, lp, ls):
    t = x.shape[0]
    w = w_in[:, :AH_Q + 2 * AH_KV].astype(BF16)
    wut = w_in[:, AH_Q + 2 * AH_KV:].T.astype(BF16)
    gains = jnp.stack([jnp.tile(q_norm.astype(F32), H_A), jnp.tile(k_norm.astype(F32), H_A)])
    gid = jnp.arange(AH_Q) // HD_A
    bdq = (gid[:, None] == gid[None, :]).astype(BF16)
    cwb = jnp.concatenate([conv_w.astype(F32).T, conv_b.astype(F32)[:, None],
                           jnp.zeros((AH_U3, 4), F32)], axis=1)
    n_tiles_p = tp // tm
    prev, nxt = _halo_specs(tm, t)
    tile = lambda i: (i, 0)
    const = lambda i: (0, 0)
    rope_blk = lambda i: (jnp.where(i < n_tiles_p, 0, lax.rem(i * tm - tp, ls) // tm), 0)
    bf = lambda n: jax.ShapeDtypeStruct((t, n), BF16)
    return pl.pallas_call(
        functools.partial(_ah_inproj_kernel, tm=tm, n_tiles_p=n_tiles_p, tp=tp, lp=lp, ls=ls),
        out_shape=(bf(AH_Q), bf(AH_Q), bf(AH_Q), jax.ShapeDtypeStruct((t, AH_KV), F32),
                   jax.ShapeDtypeStruct((t, AH_KV), F32), jax.ShapeDtypeStruct((AH_U3, t), F32),
                   jax.ShapeDtypeStruct((HY_C, t), BF16)),
        grid=(t // tm,),
        in_specs=[prev, pl.BlockSpec((tm, D_MODEL), tile), nxt,
                  pl.BlockSpec((1, 9, D_MODEL), lambda i: (row_of_tile(i), 0, 0)),
                  pl.BlockSpec((1, D_MODEL), const), pl.BlockSpec(w.shape, const),
                  pl.BlockSpec(wut.shape, const), pl.BlockSpec(gains.shape, const),
                  pl.BlockSpec((tm, 128), rope_blk), pl.BlockSpec((tm, 128), rope_blk),
                  pl.BlockSpec(bdq.shape, const), pl.BlockSpec(cwb.shape, const)],
        out_specs=(pl.BlockSpec((tm, AH_Q), tile),) * 3 + (pl.BlockSpec((tm, AH_KV), tile),) * 2
        + (pl.BlockSpec((AH_U3, tm), lambda i: (0, i)), pl.BlockSpec((HY_C, tm), lambda i: (0, i))),
        compiler_params=pltpu.CompilerParams(dimension_semantics=("arbitrary",),
                                             vmem_limit_bytes=VMEM_LIMIT),
        name="ah_inproj",
    )(x, x, x, mod, g.reshape(1, D_MODEL), w, wut, gains, rope[0], rope[1], bdq, cwb)


def _attn_group(q, kt, vt, sink_ref, g, valid):
    tq = q.shape[0]
    lane = lax.broadcasted_iota(jnp.int32, q.shape, 1)
    qs = jnp.concatenate([jnp.where(lane // HD_A == j, q, jnp.zeros_like(q)) for j in range(G_A)], axis=0)
    s = lax.dot_general(qs, kt, (((1,), (1,)), ((), ())), preferred_element_type=F32)
    if valid is not None:
        s = jnp.where(jnp.concatenate([valid] * G_A, axis=0), s, NEG_INF)
    sink = jnp.concatenate([jnp.broadcast_to(sink_ref[g * G_A + j:g * G_A + j + 1, 0:1], (tq, 1))
                            for j in range(G_A)], axis=0)
    m = jnp.maximum(jnp.max(s, axis=-1, keepdims=True), sink)
    p = jnp.exp(s - m)
    den = jnp.sum(p, axis=-1, keepdims=True) + jnp.exp(sink - m)
    pv = jnp.dot(p.astype(BF16), vt, preferred_element_type=F32) / den
    lane_o = lax.broadcasted_iota(jnp.int32, (tq, G_A * HD_A), 1)
    out = jnp.zeros((tq, G_A * HD_A), F32)
    for j in range(G_A):
        out = jnp.where(lane_o // HD_A == j, pv[j * tq:(j + 1) * tq], out)
    return out


def _attn_ctx_kernel(q_ref, kt_ref, vt_ref, sink_ref, o_ref):
    w = G_A * HD_A
    for g in range(KVH_A):
        cols = slice(g * w, (g + 1) * w)
        o_ref[:, cols] = _attn_group(q_ref[:, cols], kt_ref[:, cols], vt_ref[:, cols], sink_ref, g,
                                     None).astype(BF16)


def _attn_win_kernel(q_ref, kp_ref, km_ref, kn_ref, vp_ref, vm_ref, vn_ref, ck_ref, cv_ref, sink_ref,
                     prev_ref, o_ref, *, tq, ls):
    del prev_ref
    i = pl.program_id(1)
    n_ctx = ck_ref.shape[1]
    w = G_A * HD_A
    q_pos = i * tq + lax.broadcasted_iota(jnp.int32, (tq, n_ctx + tq + 2 * WINDOW), 0)
    col = lax.broadcasted_iota(jnp.int32, (tq, n_ctx + tq + 2 * WINDOW), 1)
    k_pos = i * tq - WINDOW + (col - n_ctx)
    valid = (col < n_ctx) | ((jnp.abs(q_pos - k_pos) <= WINDOW) & (k_pos >= 0) & (k_pos < ls))
    for g in range(KVH_A):
        cols = slice(g * w, (g + 1) * w)
        kt = jnp.concatenate([ck_ref[0, :, cols], kp_ref[:, cols], km_ref[:, cols], kn_ref[:, cols]], axis=0)
        vt = jnp.concatenate([cv_ref[0, :, cols], vp_ref[:, cols], vm_ref[:, cols], vn_ref[:, cols]], axis=0)
        o_ref[:, cols] = _attn_group(q_ref[:, cols], kt, vt, sink_ref, g, valid).astype(BF16)


def _attention(q, kt, vt, ck, cv, sink, *, tq, tp, lp, ls):
    t = q.shape[0]
    bp, bs = tp // lp, (t - tp) // ls
    sink_rows = jnp.broadcast_to(sink.astype(F32)[:, None], (H_A, 128))
    params = pltpu.CompilerParams(dimension_semantics=("arbitrary",), vmem_limit_bytes=VMEM_LIMIT)
    seq = lambda b: (b, 0)
    a = pl.pallas_call(
        _attn_ctx_kernel, out_shape=jax.ShapeDtypeStruct((t, AH_Q), BF16), grid=(bp,),
        in_specs=[pl.BlockSpec((lp, AH_Q), seq)] * 3 + [pl.BlockSpec((H_A, 128), lambda b: (0, 0))],
        out_specs=pl.BlockSpec((lp, AH_Q), seq), compiler_params=params, name="attn_ctx",
    )(q, kt, vt, sink_rows)
    nq = ls // tq
    wb = tq // WINDOW
    n128 = t // WINDOW
    main = lambda b, i: ((tp + b * ls) // tq + i, 0)
    prev = lambda b, i: (jnp.maximum((tp + b * ls) // WINDOW + i * wb - 1, 0), 0)
    nxt = lambda b, i: (jnp.minimum((tp + b * ls) // WINDOW + (i + 1) * wb, n128 - 1), 0)
    ctx = lambda b, i: (b, 0, 0)
    kv_specs = [pl.BlockSpec((WINDOW, AH_Q), prev), pl.BlockSpec((tq, AH_Q), main),
                pl.BlockSpec((WINDOW, AH_Q), nxt)]
    n_ctx = ck.shape[1]
    return pl.pallas_call(
        functools.partial(_attn_win_kernel, tq=tq, ls=ls),
        out_shape=jax.ShapeDtypeStruct((t, AH_Q), BF16), grid=(bs, nq),
        in_specs=[pl.BlockSpec((tq, AH_Q), main)] + kv_specs + kv_specs
        + [pl.BlockSpec((1, n_ctx, AH_Q), ctx)] * 2
        + [pl.BlockSpec((H_A, 128), lambda b, i: (0, 0)), pl.BlockSpec(memory_space=pl.ANY)],
        out_specs=pl.BlockSpec((tq, AH_Q), main),
        input_output_aliases={10: 0},
        compiler_params=pltpu.CompilerParams(dimension_semantics=("arbitrary", "arbitrary"),
                                             vmem_limit_bytes=VMEM_LIMIT),
        name="attn_win",
    )(q, kt, kt, kt, vt, vt, vt, ck, cv, sink_rows, a)


DFT_SUB = 64
DFT_HALF = 256


def _dft_gen_kernel(o_ref, cl_ref, sl_ref, *, length):
    a = pl.program_id(0)
    n2 = 4 * length

    def angles(n_vec, shape):
        k = lax.broadcasted_iota(jnp.int32, shape, 1)
        ph = lax.rem(n_vec * (2 * k + 1), n2)
        return ph.astype(F32) * (2.0 * math.pi / n2)

    @pl.when(a == 0)
    def _():
        th = angles(lax.broadcasted_iota(jnp.int32, (DFT_SUB, length), 0), (DFT_SUB, length))
        cl_ref[...] = jnp.cos(th)
        sl_ref[...] = jnp.sin(th)

    th = angles(jnp.full((8, length), a * DFT_SUB, jnp.int32), (8, length))[0:1]
    ch, sh = jnp.cos(th), jnp.sin(th)
    c = ch * cl_ref[...] - sh * sl_ref[...]
    ns = -(sh * cl_ref[...] + ch * sl_ref[...])
    for jt in range(length // DFT_HALF):
        src = slice(jt * DFT_HALF, (jt + 1) * DFT_HALF)
        o_ref[:, 2 * jt * DFT_HALF:(2 * jt + 1) * DFT_HALF] = c[:, src].astype(BF16)
        o_ref[:, (2 * jt + 1) * DFT_HALF:(2 * jt + 2) * DFT_HALF] = ns[:, src].astype(BF16)


def _dft_matrix(length):
    return pl.pallas_call(
        functools.partial(_dft_gen_kernel, length=length),
        out_shape=jax.ShapeDtypeStruct((length, 2 * length), BF16), grid=(length // DFT_SUB,),
        out_specs=pl.BlockSpec((DFT_SUB, 2 * length), lambda a: (a, 0)),
        scratch_shapes=[pltpu.VMEM((DFT_SUB, length), F32)] * 2,
        compiler_params=pltpu.CompilerParams(dimension_semantics=("arbitrary",),
                                             vmem_limit_bytes=VMEM_LIMIT),
        name="dft_gen",
    )()


HYF_RB = 256


def _hy_filter_kernel(w1_ref, w2_ref, w3f_ref, w3b_ref, cols_ref, f_ref, o_ref, a_ref, b_ref, *, length):
    rc, jt = pl.program_id(0), pl.program_id(1)

    @pl.when(jt == 0)
    def _():
        m = lax.broadcasted_iota(jnp.int32, (8, length), 1).astype(F32)[0:1]
        t = m * (1.0 / (length - 1))
        w = m * (2.0 * math.pi / length)
        band = lax.broadcasted_iota(jnp.int32, (HY_BANDS, 1), 0).astype(F32)
        fb = 1e-4 + band * ((HY_BANDS - 1 - 1e-4) / (HY_BANDS - 1))
        z = jnp.concatenate([t, jnp.cos(fb * w), -jnp.sin(fb * w),
                             jnp.zeros((40 - HY_EMB, length), F32)], axis=0)
        cols = cols_ref[...]
        h = jnp.sin(cols[:, 1:2] * (_bdot(w1_ref[...], z) + cols[:, 0:1]))
        h = jnp.sin(cols[:, 3:4] * (_bdot(w2_ref[...], h) + cols[:, 2:3]))
        n_out = 2 * HY_ORDER * HY_C
        row = (rc * HYF_RB + lax.broadcasted_iota(jnp.int32, (HYF_RB, 1), 0)).astype(F32)
        step = (HY_MAX_DECAY - HY_MIN_DECAY) / (n_out - 1)
        d_f = jnp.abs(HY_MIN_DECAY + row * step)
        d_b = jnp.abs(HY_MIN_DECAY + (row + HY_ORDER * HY_C) * step)
        hf = _bdot(w3f_ref[...], h) * jnp.exp(-t * d_f)
        hb = jnp.where(m == 0.0, 0.0, _bdot(w3b_ref[...], h) * jnp.exp(-t * d_b))
        ssq = jnp.sum(hf * hf, axis=-1, keepdims=True) + jnp.sum(hb * hb, axis=-1, keepdims=True)
        scale = lax.rsqrt(ssq + NORM_EPS) * (1.0 / length)
        a_ref[...] = ((hf + hb) * scale).astype(BF16)
        b_ref[...] = ((hf - hb) * scale).astype(BF16)

    o_ref[:, :DFT_HALF] = jnp.dot(a_ref[...], f_ref[:, :DFT_HALF], preferred_element_type=F32)
    o_ref[:, DFT_HALF:] = jnp.dot(b_ref[...], f_ref[:, DFT_HALF:], preferred_element_type=F32)


def _hy_filter(fmat, w1, b1, f1, w2, b2, f2, w3, *, length):
    w1t = jnp.pad(w1.astype(F32).T, ((0, 0), (0, 40 - HY_EMB)))
    w3t = w3.astype(F32).T
    cols = jnp.stack([b1, f1, b2, f2] + [jnp.zeros_like(b1)] * 4, axis=1).astype(F32)
    n_rows = HY_ORDER * HY_C
    const = lambda rc, jt: (0, 0)
    return pl.pallas_call(
        functools.partial(_hy_filter_kernel, length=length),
        out_shape=jax.ShapeDtypeStruct((n_rows, 2 * length), F32),
        grid=(n_rows // HYF_RB, length // DFT_HALF),
        in_specs=[pl.BlockSpec(w1t.shape, const), pl.BlockSpec((w2.shape[1], w2.shape[0]), const),
                  pl.BlockSpec((HYF_RB, w3t.shape[1]), lambda rc, jt: (rc, 0)),
                  pl.BlockSpec((HYF_RB, w3t.shape[1]), lambda rc, jt: (n_rows // HYF_RB + rc, 0)),
                  pl.BlockSpec(cols.shape, const),
                  pl.BlockSpec((length, 2 * DFT_HALF), lambda rc, jt: (0, jt))],
        out_specs=pl.BlockSpec((HYF_RB, 2 * DFT_HALF), lambda rc, jt: (rc, jt)),
        scratch_shapes=[pltpu.VMEM((HYF_RB, length), BF16)] * 2,
        compiler_params=pltpu.CompilerParams(dimension_semantics=("arbitrary", "arbitrary"),
                                             vmem_limit_bytes=VMEM_LIMIT),
        name="hy_filter",
    )(w1t, w2.astype(F32).T, w3t, w3t, cols, fmat)


def _hy_fwd_kernel(z_ref, f_ref, k_ref, y_ref):
    zt = jnp.dot(z_ref[...], f_ref[...], preferred_element_type=F32)
    zr, zi = zt[:, :DFT_HALF], zt[:, DFT_HALF:]
    kr, ki = k_ref[:, :DFT_HALF], k_ref[:, DFT_HALF:]
    y_ref[:, :DFT_HALF] = (zr * kr - zi * ki).astype(BF16)
    y_ref[:, DFT_HALF:] = (zr * ki + zi * kr).astype(BF16)


def _hy_fwd(zb, fmat, kspec, order, *, lane0, n_seq, length):
    sb0 = lane0 // length
    return pl.pallas_call(
        _hy_fwd_kernel,
        out_shape=jax.ShapeDtypeStruct((n_seq * HY_C, 2 * length), BF16),
        grid=(n_seq, length // DFT_HALF),
        in_specs=[pl.BlockSpec((HY_C, length), lambda s, jt: (0, sb0 + s)),
                  pl.BlockSpec((length, 2 * DFT_HALF), lambda s, jt: (0, jt)),
                  pl.BlockSpec((HY_C, 2 * DFT_HALF), lambda s, jt: (order, jt))],
        out_specs=pl.BlockSpec((HY_C, 2 * DFT_HALF), lambda s, jt: (s, jt)),
        compiler_params=pltpu.CompilerParams(dimension_semantics=("arbitrary", "arbitrary"),
                                             vmem_limit_bytes=VMEM_LIMIT),
        name="hy_fwd",
    )(zb, fmat, kspec)


def _hy_inv_kernel(y_ref, f_ref, gate_ref, z_ref, bias_ref, *rest, with_prev, with_bf16):
    rest = list(rest)
    if with_prev:
        rest.pop(0)
    o_ref = rest.pop(0)
    y = lax.dot_general(y_ref[...], f_ref[...], (((1,), (1,)), ((), ())), preferred_element_type=F32)
    out = gate_ref[...] * (y + z_ref[...] * bias_ref[...])
    o_ref[...] = out
    if with_bf16:
        rest.pop(0)[...] = out.astype(BF16)


def _hy_inv(yspec, fmat, gate_arr, gate_rb, z_arr, z_rb, bias, prev, *, lane0, n_seq, length, tt, t_total,
            with_bf16):
    nt = length // tt
    lane_blk = lambda s, j: lane0 // tt + s * nt + j
    in_specs = [pl.BlockSpec((HY_C, 2 * length), lambda s, j: (s, 0)),
                pl.BlockSpec((tt, 2 * length), lambda s, j: (j, 0)),
                pl.BlockSpec((HY_C, tt), lambda s, j: (gate_rb, lane_blk(s, j))),
                pl.BlockSpec((HY_C, tt), lambda s, j: (z_rb, lane_blk(s, j))),
                pl.BlockSpec((HY_C, 1), lambda s, j: (0, 0))]
    args = [yspec, fmat, gate_arr, z_arr, bias.astype(F32).reshape(HY_C, 1)]
    aliases = {}
    if prev is not None:
        in_specs.append(pl.BlockSpec(memory_space=pl.ANY))
        aliases = {len(args): 0}
        args.append(prev)
    out_spec = pl.BlockSpec((HY_C, tt), lambda s, j: (0, lane_blk(s, j)))
    out_shape = [jax.ShapeDtypeStruct((HY_C, t_total), F32)]
    out_specs = [out_spec]
    if with_bf16:
        out_shape.append(jax.ShapeDtypeStruct((HY_C, t_total), BF16))
        out_specs.append(out_spec)
    res = pl.pallas_call(
        functools.partial(_hy_inv_kernel, with_prev=prev is not None, with_bf16=with_bf16),
        out_shape=tuple(out_shape), grid=(n_seq, nt), in_specs=in_specs, out_specs=tuple(out_specs),
        input_output_aliases=aliases,
        compiler_params=pltpu.CompilerParams(dimension_semantics=("arbitrary", "arbitrary"),
                                             vmem_limit_bytes=VMEM_LIMIT),
        name="hy_inv",
    )(*args)
    return res


def _hyena_group(ut, vb, fmat, kspec, hy_bias, y_prev, *, lane0, n_seq, length, t_total):
    tt = min(512, length)
    geo = dict(lane0=lane0, n_seq=n_seq, length=length)
    y1 = _hy_fwd(vb, fmat, kspec, 0, **geo)
    z2, z2b = _hy_inv(y1, fmat, ut, 0, ut, 2, hy_bias[0], None, tt=tt, t_total=t_total, with_bf16=True, **geo)
    y2 = _hy_fwd(z2b, fmat, kspec, 1, **geo)
    return _hy_inv(y2, fmat, ut, 1, z2, 0, hy_bias[1], y_prev, tt=tt, t_total=t_total, with_bf16=False,
                   **geo)[0]


def _ah_outproj_kernel(x_ref, a_ref, yt_ref, mod_ref, wa_ref, wy_ref, o_ref):
    o = jnp.dot(a_ref[...], wa_ref[...], preferred_element_type=F32)
    o = o + lax.dot_general(yt_ref[...].astype(BF16), wy_ref[...], (((0,), (0,)), ((), ())),
                            preferred_element_type=F32)
    o_ref[...] = x_ref[...] + mod_ref[0][5:6] * o


def _ah_outproj(x, a, yt, mod, row_of_tile, w_out, *, tm):
    t = x.shape[0]
    tile = lambda i: (i, 0)
    const = lambda i: (0, 0)
    wa, wy = w_out[:AH_Q].astype(BF16), w_out[AH_Q:].astype(BF16)
    return pl.pallas_call(
        _ah_outproj_kernel, out_shape=jax.ShapeDtypeStruct((t, D_MODEL), F32), grid=(t // tm,),
        in_specs=[pl.BlockSpec((tm, D_MODEL), tile), pl.BlockSpec((tm, AH_Q), tile),
                  pl.BlockSpec((HY_C, tm), lambda i: (0, i)),
                  pl.BlockSpec((1, 9, D_MODEL), lambda i: (row_of_tile(i), 0, 0)),
                  pl.BlockSpec(wa.shape, const), pl.BlockSpec(wy.shape, const)],
        out_specs=pl.BlockSpec((tm, D_MODEL), tile),
        compiler_params=pltpu.CompilerParams(dimension_semantics=("arbitrary",),
                                             vmem_limit_bytes=VMEM_LIMIT),
        name="ah_outproj",
    )(x, a, yt, mod, wa, wy)


def _ah_layer(x, mod, row_of_tile, g, w_in, w_out, q_norm, k_norm, sink, conv_w, conv_b, hy_bias,
              ck, cv, rope, fmats, kspecs, *, tm, tq, tp, lp, ls):
    t = x.shape[0]
    q, kt, vt, kn, vn, ut, vb = _ah_inproj(x, mod, row_of_tile, g, w_in, q_norm, k_norm, conv_w, conv_b, rope,
                                           tm=tm, tp=tp, lp=lp, ls=ls)
    a = _attention(q, kt, vt, ck, cv, sink, tq=tq, tp=tp, lp=lp, ls=ls)
    yt = _hyena_group(ut, vb, fmats[0], kspecs[0], hy_bias, None, lane0=0, n_seq=tp // lp, length=lp, t_total=t)
    yt = _hyena_group(ut, vb, fmats[1], kspecs[1], hy_bias, yt, lane0=tp, n_seq=(t - tp) // ls, length=ls,
                      t_total=t)
    return _ah_outproj(x, a, yt, mod, row_of_tile, w_out, tm=tm), kn[:tp], vn[:tp]


def _rep4_ctx(c):
    b, s = c.shape[:2]
    return jnp.broadcast_to(c[:, :, :, None, :], (b, s, KVH_A, G_A, HD_A)).reshape(b, s, AH_Q).astype(BF16)


def _ada_kernel(c_ref, w_ref, b_ref, o_ref):
    s = jax.nn.silu(c_ref[...]).astype(BF16)
    o_ref[0] = jnp.dot(s, w_ref[0].astype(BF16), preferred_element_type=F32) + b_ref[0]


def _ada_mod(cond, ada_w, ada_b):
    depth, _, n = ada_w.shape
    rows = 16
    cp = jnp.pad(cond.astype(F32), ((0, rows - cond.shape[0]), (0, 0)))
    out = pl.pallas_call(
        _ada_kernel, out_shape=jax.ShapeDtypeStruct((depth, rows, n), F32),
        grid=(depth, n // D_MODEL),
        in_specs=[pl.BlockSpec((rows, D_MODEL), lambda l, j: (0, 0)),
                  pl.BlockSpec((1, D_MODEL, D_MODEL), lambda l, j: (l, 0, j)),
                  pl.BlockSpec((1, 1, D_MODEL), lambda l, j: (l, 0, j))],
        out_specs=pl.BlockSpec((1, rows, D_MODEL), lambda l, j: (l, 0, j)),
        compiler_params=pltpu.CompilerParams(dimension_semantics=("arbitrary", "arbitrary"),
                                             vmem_limit_bytes=VMEM_LIMIT),
        name="ada_mod",
    )(cp, ada_w, ada_b.reshape(depth, 1, n))
    return out.reshape(depth, rows, n // D_MODEL, D_MODEL)


def kernel(x_prompt, x_sample, cache_k, cache_v, state_fwd, state_bwd, c, c_ctx, norm_g, ada_w, ada_b, ffn_w13, ffn_w2, mx_w_in, mx_w_out, q_norm, k_norm, attn_sink, hy_conv_w, hy_conv_b, hy_w1, hy_b1, hy_freq1, hy_w2, hy_b2, hy_freq2, hy_w3, hy_bias, dn_w_in, dn_w_out, dn_conv_w, dn_a_log, dn_dt_bias, dn_norm_g):
    bp, lp, _ = x_prompt.shape
    bs, ls, _ = x_sample.shape
    tp, ts = bp * lp, bs * ls
    assert tp % ls == 0 or tp % lp == 0
    tm = math.gcd(TOKEN_TILE, math.gcd(tp, ls))
    ct = math.gcd(DN_BLOCK, math.gcd(lp, ls))
    tq = math.gcd(ATTN_TQ, ls)
    x = jnp.concatenate([x_prompt.reshape(tp, D_MODEL), x_sample.reshape(ts, D_MODEL)], axis=0)
    cond = jnp.concatenate([c, c_ctx[None, :]], axis=0)
    mods = _ada_mod(cond, ada_w, ada_b)
    tiles_p, tiles_per_s = tp // tm, ls // tm

    def row_of_tile(i):
        return jnp.where(i < tiles_p, bs, (i - tiles_p) // tiles_per_s)

    rope = _rope_tables(ls)
    fmats = (_dft_matrix(lp), _dft_matrix(ls))
    new_k, new_v, new_sf, new_sb = [], [], [], []
    for layer in range(DEPTH):
        mod = mods[layer]
        i = layer // 2
        x = _ffn(x, mod, row_of_tile, norm_g[layer, 0], ffn_w13[layer, 0], ffn_w2[layer, 0], 0, tm=tm)
        if layer % 2 == 0:
            kspecs = tuple(_hy_filter(f, hy_w1[i], hy_b1[i], hy_freq1[i], hy_w2[i], hy_b2[i], hy_freq2[i],
                                      hy_w3[i], length=n) for f, n in zip(fmats, (lp, ls)))
            x, k_p, v_p = _ah_layer(x, mod, row_of_tile, norm_g[layer, 1], mx_w_in[i], mx_w_out[i], q_norm[i],
                                    k_norm[i], attn_sink[i], hy_conv_w[i], hy_conv_b[i], hy_bias[i],
                                    _rep4_ctx(cache_k[:, i]), _rep4_ctx(cache_v[:, i]), rope, fmats, kspecs,
                                    tm=tm, tq=tq, tp=tp, lp=lp, ls=ls)
            new_k.append(k_p.reshape(bp, lp, KVH_A, HD_A))
            new_v.append(v_p.reshape(bp, lp, KVH_A, HD_A))
        else:
            x, s_f, s_b = _dn_layer(x, mod, row_of_tile, norm_g[layer, 1], dn_w_in[i], dn_w_out[i],
                                    dn_conv_w[i], dn_a_log[i], dn_dt_bias[i], dn_norm_g[i],
                                    state_fwd[:, i], state_bwd[:, i], tm=tm, ct=ct, tp=tp, lp=lp, ls=ls)
            new_sf.append(s_f)
            new_sb.append(s_b)
        x = _ffn(x, mod, row_of_tile, norm_g[layer, 2], ffn_w13[layer, 1], ffn_w2[layer, 1], 2, tm=tm)

    return (x[:tp].reshape(bp, lp, D_MODEL), x[tp:].reshape(bs, ls, D_MODEL),
            jnp.stack(new_k, axis=1), jnp.stack(new_v, axis=1),
            jnp.stack(new_sf, axis=1), jnp.stack(new_sb, axis=1))
```

```python
import functools
import math

import jax
import jax.numpy as jnp
from jax import lax
from jax.experimental import pallas as pl
from jax.experimental.pallas import tpu as pltpu

D_MODEL = 1024
DEPTH = 4
GRID_W = 64
H_A = 8
KVH_A = 2
G_A = H_A // KVH_A
HD_A = 64
WINDOW = 128
ATTN_BLOCK = 128
ROPE_THETA = 10000.0
ROPE_AXIS_DIM = HD_A // 2
HY_C = 512
HY_ORDER = 2
HY_EMB = 33
HY_BANDS = (HY_EMB - 1) // 2
HY_MIN_DECAY = math.log(1e-2) / 1.5
HY_MAX_DECAY = math.log(1e-2) / 0.3
H_C = 8
DK_C = 128
DV_C = 128
DN_CHUNK = 64
DN_OUT = H_C * DV_C
D_FF = 2816
NORM_EPS = 1e-6
NEG_INF = -1e30

F32 = jnp.float32
BF16 = jnp.bfloat16

TOKEN_TILE = 512
FFN_CHUNK = 256
DN_BLOCK = 128
ATTN_TQ = 256
CTX_SEQS = 4
VMEM_LIMIT = 56 * 1024 * 1024


def _ffn_kernel(x_ref, mod_ref, g_ref, w1_ref, w3_ref, w2_ref, o_ref, acc_ref, *, j, n_chunks):
    x = x_ref[...]
    y = x * lax.rsqrt(jnp.mean(x * x, axis=-1, keepdims=True) + NORM_EPS) * g_ref[...]
    m = mod_ref[0]
    shift, scale, gate = m[3 * j:3 * j + 1], m[3 * j + 1:3 * j + 2], m[3 * j + 2:3 * j + 3]
    h = (y * (1.0 + scale) + shift).astype(BF16)
    acc_ref[...] = jnp.zeros_like(acc_ref)

    def body(c, carry):
        gt = jnp.dot(h, w1_ref[c], preferred_element_type=F32)
        up = jnp.dot(h, w3_ref[c], preferred_element_type=F32)
        a = (jax.nn.silu(gt) * up).astype(BF16)
        acc_ref[...] += jnp.dot(a, w2_ref[c], preferred_element_type=F32)
        return carry

    lax.fori_loop(0, n_chunks, body, 0, unroll=True)
    o_ref[...] = x + 0.5 * gate * acc_ref[...]


def _ffn(x, mod, row_of_tile, g, w13, w2, j, *, tm):
    t = x.shape[0]
    n_chunks = D_FF // FFN_CHUNK
    w1 = w13[:, :D_FF].astype(BF16).reshape(D_MODEL, n_chunks, FFN_CHUNK).transpose(1, 0, 2)
    w3 = w13[:, D_FF:].astype(BF16).reshape(D_MODEL, n_chunks, FFN_CHUNK).transpose(1, 0, 2)
    w2c = w2.astype(BF16).reshape(n_chunks, FFN_CHUNK, D_MODEL)
    const3 = lambda i: (0, 0, 0)
    return pl.pallas_call(
        functools.partial(_ffn_kernel, j=j, n_chunks=n_chunks),
        out_shape=jax.ShapeDtypeStruct((t, D_MODEL), F32),
        grid=(t // tm,),
        in_specs=[
            pl.BlockSpec((tm, D_MODEL), lambda i: (i, 0)),
            pl.BlockSpec((1, 9, D_MODEL), lambda i: (row_of_tile(i), 0, 0)),
            pl.BlockSpec((1, D_MODEL), lambda i: (0, 0)),
            pl.BlockSpec((n_chunks, D_MODEL, FFN_CHUNK), const3),
            pl.BlockSpec((n_chunks, D_MODEL, FFN_CHUNK), const3),
            pl.BlockSpec((n_chunks, FFN_CHUNK, D_MODEL), const3),
        ],
        out_specs=pl.BlockSpec((tm, D_MODEL), lambda i: (i, 0)),
        scratch_shapes=[pltpu.VMEM((tm, D_MODEL), F32)],
        compiler_params=pltpu.CompilerParams(dimension_semantics=("arbitrary",),
                                             vmem_limit_bytes=VMEM_LIMIT),
        name=f"ffn{j}",
    )(x, mod, g.reshape(1, D_MODEL), w1, w3, w2c)


def _bdot(a, b):
    return jnp.dot(a.astype(BF16), b.astype(BF16), preferred_element_type=F32)


def _bdot_nt(a, b):
    return lax.dot_general(a.astype(BF16), b.astype(BF16), (((1,), (1,)), ((), ())),
                           preferred_element_type=F32)


def _bdot_tn(a, b):
    return lax.dot_general(a.astype(BF16), b.astype(BF16), (((0,), (0,)), ((), ())),
                           preferred_element_type=F32)


def _split3(x):
    hi = x.astype(BF16)
    r1 = x - hi.astype(F32)
    mid = r1.astype(BF16)
    lo = (r1 - mid.astype(F32)).astype(BF16)
    return hi, mid, lo


def _ada_h(x, g_row, m, j):
    y = x * lax.rsqrt(jnp.mean(x * x, axis=-1, keepdims=True) + NORM_EPS) * g_row
    return y * (1.0 + m[3 * j + 1:3 * j + 2]) + m[3 * j:3 * j + 1]


def _seq_edges(i, tm, n_tiles_p, tp, lp, ls, shape, axis):
    is_p = i < n_tiles_p
    seq_len = jnp.where(is_p, lp, ls)
    t0 = i * tm - jnp.where(is_p, 0, tp)
    base = lax.rem(t0, seq_len)
    pos = (base + lax.broadcasted_iota(jnp.int32, shape, axis)).astype(F32)
    lf = seq_len.astype(F32)
    rem = pos - jnp.floor((pos + 0.5) / lf) * lf
    return rem == 0.0, rem == lf - 1.0


def _halo_specs(tm, t):
    nb = t // 8
    prev = pl.BlockSpec((8, D_MODEL), lambda i: (jnp.maximum(i * (tm // 8) - 1, 0), 0))
    nxt = pl.BlockSpec((8, D_MODEL), lambda i: (jnp.minimum((i + 1) * (tm // 8), nb - 1), 0))
    return prev, nxt


DN_QKV = 2 * H_C * DK_C + H_C * DV_C
DN_CW = 512


def _dn_inproj_kernel(xp_ref, x_ref, xn_ref, mod_ref, g_ref, w_ref, cw_ref, ab_ref,
                      q_ref, k_ref, v_ref, z_ref, bg_ref, pext_ref, *, tm, n_tiles_p, tp, lp, ls):
    i = pl.program_id(0)
    xe = jnp.concatenate([xp_ref[...], x_ref[...], xn_ref[...]], axis=0)
    h = _ada_h(xe, g_ref[...], mod_ref[0], 1).astype(BF16)
    first, last = _seq_edges(i, tm, n_tiles_p, tp, lp, ls, (tm, 1), 0)
    nq = H_C * DK_C
    for c0 in range(0, DN_QKV, DN_CW):
        pext_ref[...] = jnp.dot(h, w_ref[:, c0:c0 + DN_CW], preferred_element_type=F32)
        cw = cw_ref[:, c0:c0 + DN_CW]
        c = (jnp.where(first, 0.0, pext_ref[7:tm + 7, :]) * cw[0:1]
             + pext_ref[8:tm + 8, :] * cw[1:2]
             + jnp.where(last, 0.0, pext_ref[9:tm + 9, :]) * cw[2:3])
        a = jax.nn.silu(c)
        for b0 in range(0, DN_CW, DK_C):
            col = c0 + b0
            blk = a[:, b0:b0 + DK_C]
            if col < 2 * nq:
                blk = blk * lax.rsqrt(jnp.sum(blk * blk, axis=-1, keepdims=True) + NORM_EPS)
            if col < nq:
                q_ref[:, col:col + DK_C] = blk * (DK_C ** -0.5)
            elif col < 2 * nq:
                k_ref[:, col - nq:col - nq + DK_C] = blk
            else:
                v_ref[:, col - 2 * nq:col - 2 * nq + DK_C] = blk
    hm = h[8:tm + 8]
    for c0 in range(0, DN_OUT, DN_CW):
        z_ref[:, c0:c0 + DN_CW] = jnp.dot(hm, w_ref[:, DN_QKV + c0:DN_QKV + c0 + DN_CW],
                                           preferred_element_type=F32)
    r = jnp.dot(hm, w_ref[:, DN_QKV + DN_OUT:], preferred_element_type=F32)
    lane = lax.broadcasted_iota(jnp.int32, r.shape, 1)
    xs = r + ab_ref[1:2]
    softplus = jnp.maximum(xs, 0.0) + jnp.log1p(jnp.exp(-jnp.abs(xs)))
    bg_ref[...] = jnp.where(lane < 2 * H_C, jax.nn.sigmoid(r),
                            jnp.where(lane < 4 * H_C, -ab_ref[0:1] * softplus, 0.0))


def _dn_inproj(x, mod, row_of_tile, g, w_in, conv_w, a_log, dt_bias, *, tm, tp, lp, ls):
    t = x.shape[0]
    pad = 128 - 4 * H_C
    w = jnp.pad(w_in, ((0, 0), (0, pad))).astype(BF16)
    ab = jnp.zeros((2, 128), F32)
    ab = ab.at[0, 2 * H_C:4 * H_C].set(jnp.exp(a_log.astype(F32)).reshape(-1))
    ab = ab.at[1, 2 * H_C:4 * H_C].set(dt_bias.astype(F32).reshape(-1))
    prev, nxt = _halo_specs(tm, t)
    tile = lambda i: (i, 0)
    const = lambda i: (0, 0)
    wide = jax.ShapeDtypeStruct((t, DN_OUT), F32)
    return pl.pallas_call(
        functools.partial(_dn_inproj_kernel, tm=tm, n_tiles_p=tp // tm, tp=tp, lp=lp, ls=ls),
        out_shape=(wide, wide, wide, wide, jax.ShapeDtypeStruct((t, 128), F32)),
        grid=(t // tm,),
        in_specs=[prev, pl.BlockSpec((tm, D_MODEL), tile), nxt,
                  pl.BlockSpec((1, 9, D_MODEL), lambda i: (row_of_tile(i), 0, 0)),
                  pl.BlockSpec((1, D_MODEL), const),
                  pl.BlockSpec(w.shape, const),
                  pl.BlockSpec(conv_w.shape, const),
                  pl.BlockSpec((2, 128), const)],
        out_specs=(pl.BlockSpec((tm, DN_OUT), tile),) * 4 + (pl.BlockSpec((tm, 128), tile),),
        scratch_shapes=[pltpu.VMEM((tm + 16, DN_CW), F32)],
        compiler_params=pltpu.CompilerParams(dimension_semantics=("arbitrary",),
                                             vmem_limit_bytes=VMEM_LIMIT),
        name="dn_inproj",
    )(x, x, x, mod, g.reshape(1, D_MODEL), w, conv_w.astype(F32), ab)


DN_GROUP = 4


def _dn_masks(rev):
    n = DN_CHUNK
    r_i = lax.broadcasted_iota(jnp.int32, (n, n), 0)
    c_i = lax.broadcasted_iota(jnp.int32, (n, n), 1)
    tri = jnp.where((r_i <= c_i) if rev else (r_i >= c_i), 1.0, 0.0).astype(BF16)
    rows = DN_GROUP * n
    rr = lax.broadcasted_iota(jnp.int32, (rows, rows), 0)
    cc = lax.broadcasted_iota(jnp.int32, (rows, rows), 1)
    same = (rr // n) == (cc // n)
    incl = same & ((rr <= cc) if rev else (rr >= cc))
    strict = same & ((rr < cc) if rev else (rr > cc))
    eye = jnp.where(rr == cc, 1.0, 0.0)
    return tri, incl, strict, eye


def _dn_prepare(qc, kc, vc, bgc, masks, rev):
    n = DN_CHUNK
    tri, incl, strict, eye = masks
    gct = sum(jnp.dot(tri, p, preferred_element_type=F32) for p in _split3(bgc))
    rows = DN_GROUP * n
    lane = lax.broadcasted_iota(jnp.int32, (rows, 128), 1)
    last = 0 if rev else n - 1
    prep = []
    for gi in range(H_C // DN_GROUP):
        heads = range(gi * DN_GROUP, (gi + 1) * DN_GROUP)
        sb = [(H_C if rev else 0) + h for h in heads]
        sg = [2 * H_C + (H_C if rev else 0) + h for h in heads]
        beta = jnp.concatenate([bgc[:, s:s + 1] for s in sb], axis=0)
        gc = jnp.concatenate([gct[:, s:s + 1] for s in sg], axis=0)
        gl = jnp.concatenate([jnp.broadcast_to(gct[last:last + 1, s:s + 1], (n, 1)) for s in sg], axis=0)
        q4 = jnp.concatenate([qc[:, h * DK_C:(h + 1) * DK_C] for h in heads], axis=0)
        k4 = jnp.concatenate([kc[:, h * DK_C:(h + 1) * DK_C] for h in heads], axis=0)
        v4 = jnp.concatenate([vc[:, h * DV_C:(h + 1) * DV_C] for h in heads], axis=0)
        hi, mid, lo = (p.astype(F32) for p in _split3(gc))
        one = jnp.where(lane < 6, 1.0, 0.0)
        u_m = jnp.where(lane == 0, hi, jnp.where(lane == 1, mid, jnp.where(lane == 2, lo, one)))
        v_m = jnp.where(lane == 3, -hi, jnp.where(lane == 4, -mid, jnp.where(lane == 5, -lo, one)))
        gd = _bdot_nt(jnp.where(lane < 6, u_m, 0.0), jnp.where(lane < 6, v_m, 0.0))
        decay = jnp.where(incl, jnp.exp(jnp.where(incl, gd, 0.0)), 0.0)
        kb = k4 * beta
        lm = jnp.where(strict, _bdot_nt(kb, k4) * decay, 0.0)
        x = jnp.concatenate([v4 * beta, kb * jnp.exp(gc)], axis=1)
        qe = q4 * jnp.exp(gc)
        ke = (k4 * jnp.exp(gl - gc)).astype(BF16)
        aqk = jnp.where(incl, _bdot_nt(q4, k4) * decay, 0.0).astype(BF16)
        prep.append(dict(lm=lm, x=x, qe=qe, ke=ke, aqk=aqk, egl=jnp.exp(gl)))
    return prep


def _dn_solve(groups, eye):
    n = DN_CHUNK
    dot = functools.partial(jnp.dot, preferred_element_type=F32)
    for g in groups:
        g["lh"] = g["lm"].astype(BF16)
        g["ll"] = (g["lm"] - g["lh"].astype(F32)).astype(BF16)
        g["p"] = -g["lh"]
        g["t"] = eye - g["lm"]
    for _ in range(5):
        for g in groups:
            g["p"] = dot(g["p"], g["p"]).astype(BF16)
        for g in groups:
            g["t"] = g["t"] + dot(g["t"].astype(BF16), g["p"])
    for g in groups:
        g["tb"] = g["t"].astype(BF16)
        g["x0"] = dot(g["tb"], g["x"].astype(BF16))
    for g in groups:
        xh = g["x0"].astype(BF16)
        xl = (g["x0"] - xh.astype(F32)).astype(BF16)
        g["res"] = g["x"] - g["x0"] - (dot(g["lh"], xh) + dot(g["lh"], xl) + dot(g["ll"], xh))
    out = []
    for g in groups:
        x = g["x0"] + dot(g["tb"], g["res"].astype(BF16))
        u4, w4, qe = x[:, :DV_C], x[:, DV_C:], g["qe"]
        wqe = [jnp.concatenate([w4[j * n:(j + 1) * n], qe[j * n:(j + 1) * n]], axis=0).astype(BF16)
               for j in range(DN_GROUP)]
        out.append((u4, wqe, g["ke"], g["aqk"], g["egl"]))
    return out


def _dn_advance(prep, s_ref):
    n = DN_CHUNK
    outs = []
    for gi, (u4, wqe, ke, aqk, egl) in enumerate(prep):
        vn, qs = [], []
        for j in range(DN_GROUP):
            wq = jnp.dot(wqe[j], s_ref[gi * DN_GROUP + j].astype(BF16), preferred_element_type=F32)
            vn.append(u4[j * n:(j + 1) * n] - wq[:n])
            qs.append(wq[n:])
        o4 = jnp.concatenate(qs, axis=0) + jnp.dot(aqk, jnp.concatenate(vn, axis=0).astype(BF16),
                                                   preferred_element_type=F32)
        for j in range(DN_GROUP):
            h = gi * DN_GROUP + j
            sl = slice(j * n, (j + 1) * n)
            s_ref[h] = s_ref[h] * egl[j * n:j * n + 1] + lax.dot_general(
                ke[sl], vn[j].astype(BF16), (((0,), (0,)), ((), ())), preferred_element_type=F32)
            outs.append(o4[sl])
    return outs


def _dn_scan_kernel(*refs, n_chunks, blocks_per_seq, zero_init, has_prev, write_state):
    refs = list(refs)
    fwd_in, bwd_in = refs[:4], refs[4:8]
    del refs[:8]
    s0f_ref, s0b_ref = (None, None) if zero_init else (refs.pop(0), refs.pop(0))
    if has_prev:
        del refs[:2]
    of_ref, ob_ref = refs.pop(0), refs.pop(0)
    sfo_ref, sbo_ref = (refs.pop(0), refs.pop(0)) if write_state else (None, None)
    sf_ref, sb_ref = refs
    i = pl.program_id(0)
    f_in_seq = lax.rem(i, blocks_per_seq)
    r_in_seq = lax.rem(pl.num_programs(0) - 1 - i, blocks_per_seq)

    @pl.when(f_in_seq == 0)
    def _():
        sf_ref[...] = jnp.zeros_like(sf_ref) if zero_init else s0f_ref[0]

    @pl.when(r_in_seq == blocks_per_seq - 1)
    def _():
        sb_ref[...] = jnp.zeros_like(sb_ref) if zero_init else s0b_ref[0]

    def chunk(io, c):
        rows = slice(c * DN_CHUNK, (c + 1) * DN_CHUNK)
        return [r[rows, :] for r in io]

    mf, mb = _dn_masks(False), _dn_masks(True)
    raw = [_dn_prepare(*chunk(fwd_in, c), mf, False) for c in range(n_chunks)]
    raw += [_dn_prepare(*chunk(bwd_in, c), mb, True) for c in range(n_chunks)]
    n_groups = H_C // DN_GROUP
    solved = _dn_solve([g for r in raw for g in r], mf[3])
    per_chunk = [solved[j * n_groups:(j + 1) * n_groups] for j in range(2 * n_chunks)]
    prep_f, prep_b = per_chunk[:n_chunks], per_chunk[n_chunks:]
    for step in range(n_chunks):
        for o_ref, prep, s_ref, c in ((of_ref, prep_f, sf_ref, step),
                                      (ob_ref, prep_b, sb_ref, n_chunks - 1 - step)):
            for h, o in enumerate(_dn_advance(prep[c], s_ref)):
                o_ref[c * DN_CHUNK:(c + 1) * DN_CHUNK, h * DV_C:(h + 1) * DV_C] = o

    if write_state:
        @pl.when(f_in_seq == blocks_per_seq - 1)
        def _():
            sfo_ref[0] = sf_ref[...]

        @pl.when(r_in_seq == 0)
        def _():
            sbo_ref[0] = sb_ref[...]


def _dn_scan(q, k, v, bg, s0_f, s0_b, prev, *, row0, n_seq, seq_len, ct, write_state):
    t = q.shape[0]
    bps = seq_len // ct
    nblk = n_seq * bps
    b0 = row0 // ct
    zero_init = s0_f is None
    fwd = lambda i: (b0 + i, 0)
    bwd = lambda i: (b0 + nblk - 1 - i, 0)
    seq_f = lambda i: (i // bps, 0, 0, 0)
    seq_b = lambda i: ((nblk - 1 - i) // bps, 0, 0, 0)
    in_specs, args = [], []
    for rows in (fwd, bwd):
        in_specs += [pl.BlockSpec((ct, DN_OUT), rows)] * 3 + [pl.BlockSpec((ct, 128), rows)]
        args += [q, k, v, bg]
    state_blk = (1, H_C, DK_C, DV_C)
    if not zero_init:
        in_specs += [pl.BlockSpec(state_blk, seq_f), pl.BlockSpec(state_blk, seq_b)]
        args += [s0_f, s0_b]
    aliases = {}
    if prev is not None:
        aliases = {len(args): 0, len(args) + 1: 1}
        in_specs += [pl.BlockSpec(memory_space=pl.ANY)] * 2
        args += list(prev)
    out_shape = [jax.ShapeDtypeStruct((t, DN_OUT), F32)] * 2
    out_specs = [pl.BlockSpec((ct, DN_OUT), fwd), pl.BlockSpec((ct, DN_OUT), bwd)]
    if write_state:
        out_shape += [jax.ShapeDtypeStruct((n_seq,) + state_blk[1:], F32)] * 2
        out_specs += [pl.BlockSpec(state_blk, seq_f), pl.BlockSpec(state_blk, seq_b)]
    kern = functools.partial(_dn_scan_kernel, n_chunks=ct // DN_CHUNK, blocks_per_seq=bps,
                             zero_init=zero_init, has_prev=prev is not None, write_state=write_state)
    return pl.pallas_call(
        kern, out_shape=tuple(out_shape), grid=(nblk,), in_specs=in_specs, out_specs=tuple(out_specs),
        scratch_shapes=[pltpu.VMEM((H_C, DK_C, DV_C), F32)] * 2,
        input_output_aliases=aliases,
        compiler_params=pltpu.CompilerParams(dimension_semantics=("arbitrary",),
                                             vmem_limit_bytes=VMEM_LIMIT),
        name="dn_scan",
    )(*args)


def _dn_outproj_kernel(x_ref, of_ref, ob_ref, z_ref, mod_ref, ng_ref, w_ref, y_ref):
    o = of_ref[...] + ob_ref[...]
    parts = []
    for h in range(H_C):
        blk = o[:, h * DV_C:(h + 1) * DV_C]
        parts.append(blk * lax.rsqrt(jnp.mean(blk * blk, axis=-1, keepdims=True) + NORM_EPS))
    y = jnp.concatenate(parts, axis=1) * ng_ref[...] * jax.nn.silu(z_ref[...])
    gate = mod_ref[0][5:6]
    y_ref[...] = x_ref[...] + gate * jnp.dot(y.astype(BF16), w_ref[...], preferred_element_type=F32)


def _dn_outproj(x, o_f, o_b, z, mod, row_of_tile, norm_g, w_out, *, tm):
    t = x.shape[0]
    tile = lambda i: (i, 0)
    const = lambda i: (0, 0)
    return pl.pallas_call(
        _dn_outproj_kernel,
        out_shape=jax.ShapeDtypeStruct((t, D_MODEL), F32),
        grid=(t // tm,),
        in_specs=[pl.BlockSpec((tm, D_MODEL), tile), pl.BlockSpec((tm, DN_OUT), tile),
                  pl.BlockSpec((tm, DN_OUT), tile), pl.BlockSpec((tm, DN_OUT), tile),
                  pl.BlockSpec((1, 9, D_MODEL), lambda i: (row_of_tile(i), 0, 0)),
                  pl.BlockSpec((1, DN_OUT), const), pl.BlockSpec((DN_OUT, D_MODEL), const)],
        out_specs=pl.BlockSpec((tm, D_MODEL), tile),
        compiler_params=pltpu.CompilerParams(dimension_semantics=("arbitrary",),
                                             vmem_limit_bytes=VMEM_LIMIT),
        name="dn_outproj",
    )(x, o_f, o_b, z, mod, jnp.tile(norm_g.astype(F32), H_C).reshape(1, DN_OUT), w_out.astype(BF16))


def _dn_layer(x, mod, row_of_tile, g, w_in, w_out, conv_w, a_log, dt_bias, norm_g, s0_f, s0_b,
              *, tm, ct, tp, lp, ls):
    t = x.shape[0]
    q, k, v, z, bg = _dn_inproj(x, mod, row_of_tile, g, w_in, conv_w, a_log, dt_bias, tm=tm, tp=tp, lp=lp, ls=ls)
    bp, bs = tp // lp, (t - tp) // ls
    o_f, o_b, sf, sb = _dn_scan(q, k, v, bg, None, None, None, row0=0, n_seq=bp, seq_len=lp, ct=ct,
                                write_state=True)
    o_f, o_b = _dn_scan(q, k, v, bg, s0_f.astype(F32), s0_b.astype(F32), (o_f, o_b), row0=tp, n_seq=bs,
                        seq_len=ls, ct=ct, write_state=False)
    return _dn_outproj(x, o_f, o_b, z, mod, row_of_tile, norm_g, w_out, tm=tm), sf, sb


AH_Q = H_A * HD_A
AH_KV = KVH_A * HD_A
AH_U3 = 3 * HY_C


def _group_mean_sq(x, bd):
    sq = x * x
    hi = sq.astype(BF16)
    lo = (sq - hi.astype(F32)).astype(BF16)
    return (jnp.dot(hi, bd, preferred_element_type=F32) + jnp.dot(lo, bd, preferred_element_type=F32)) * (1.0 / HD_A)


def _rope_lanes(x, cos_t, sin_t):
    w = x.shape[1]
    lane = lax.broadcasted_iota(jnp.int32, x.shape, 1)
    low = lax.rem(lane, HD_A) < HD_A // 2
    partner = jnp.where(low, pltpu.roll(x, w - HD_A // 2, 1), pltpu.roll(x, HD_A // 2, 1))
    reps = w // cos_t.shape[1]
    return x * jnp.concatenate([cos_t] * reps, axis=1) + partner * jnp.concatenate([sin_t] * reps, axis=1)


def _rep4(x):
    lane = lax.broadcasted_iota(jnp.int32, x.shape, 1)
    sw = pltpu.roll(x, HD_A, 1)
    a = jnp.where(lane < HD_A, x, sw)
    b = jnp.where(lane < HD_A, sw, x)
    return jnp.concatenate([a, a, b, b], axis=1)


def _ah_inproj_kernel(xp_ref, x_ref, xn_ref, mod_ref, g_ref, w_ref, wut_ref, gains_ref, cos_ref, sin_ref,
                      bdq_ref, cwb_ref, perm_ref, q_ref, kt_ref, vt_ref, kn_ref, vn_ref, u_ref, vb_ref,
                      *, tm, n_tiles_p, tp, lp, ls):
    i = pl.program_id(0)
    is_p = i < n_tiles_p
    xe = jnp.concatenate([xp_ref[...], x_ref[...], xn_ref[...]], axis=0)
    he = _ada_h(xe, g_ref[...], mod_ref[0], 1).astype(BF16)
    hm = he[8:tm + 8]
    p = jnp.dot(hm, w_ref[...], preferred_element_type=F32)
    cos_t, sin_t = cos_ref[...], sin_ref[...]
    q = p[:, :AH_Q]
    q = q * lax.rsqrt(_group_mean_sq(q, bdq_ref[...]) + NORM_EPS) * gains_ref[0:1, :]
    q = jnp.where(is_p, q, _rope_lanes(q, cos_t, sin_t))
    q_ref[...] = (q * (HD_A ** -0.5)).astype(BF16)
    k = p[:, AH_Q:AH_Q + AH_KV]
    k = k * lax.rsqrt(_group_mean_sq(k, bdq_ref[:AH_KV, :AH_KV]) + NORM_EPS) * gains_ref[1:2, :AH_KV]
    kn_ref[...] = k
    k = jnp.where(is_p, k, _rope_lanes(k, cos_t, sin_t))
    kt_ref[...] = _rep4(k).astype(BF16)
    v = p[:, AH_Q + AH_KV:]
    vn_ref[...] = v
    vt_ref[...] = _rep4(v).astype(BF16)
    halo = jnp.concatenate([he[0:8], he[tm + 8:tm + 16], jnp.zeros((112, D_MODEL), BF16)], axis=0)
    nt = (((1,), (1,)), ((), ()))

    def store(r0, c):
        u_ref[r0:r0 + HY_C, :] = c
        if r0 == 2 * HY_C:
            vb_ref[...] = c.astype(BF16)

    @pl.when(is_p)
    def _():
        first, last = _seq_edges(i, tm, n_tiles_p, tp, lp, ls, (1, tm), 1)
        lane = lax.broadcasted_iota(jnp.int32, (1, tm), 1)
        for r0 in range(0, AH_U3, HY_C):
            wu = wut_ref[r0:r0 + HY_C, :]
            u = lax.dot_general(wu, hm, nt, preferred_element_type=F32)
            uh = lax.dot_general(wu, halo, nt, preferred_element_type=F32)
            left = jnp.where(lane == 0, uh[:, 7:8], pltpu.roll(u, 1, 1))
            right = jnp.where(lane == tm - 1, uh[:, 8:9], pltpu.roll(u, tm - 1, 1))
            cwb = cwb_ref[r0:r0 + HY_C, :]
            store(r0, jnp.where(first, 0.0, left) * cwb[:, 0:1] + u * cwb[:, 1:2]
                  + jnp.where(last, 0.0, right) * cwb[:, 2:3] + cwb[:, 3:4])

    @pl.when(jnp.logical_not(is_p))
    def _():
        half = tm // 2
        hs = jnp.dot(perm_ref[...], hm, preferred_element_type=F32).astype(BF16)
        t0 = i * tm - tp
        seq_start = lax.rem(t0, ls) == 0
        seq_end = lax.rem(t0 + tm, ls) == 0
        lane = lax.broadcasted_iota(jnp.int32, (1, half), 1)
        for r0 in range(0, AH_U3, HY_C):
            wu = wut_ref[r0:r0 + HY_C, :]
            u = lax.dot_general(wu, hs, nt, preferred_element_type=F32)
            uh = lax.dot_general(wu, halo, nt, preferred_element_type=F32)
            ev, od = u[:, :half], u[:, half:]
            prev_tok = jnp.where(seq_start, 0.0, uh[:, 7:8])
            next_tok = jnp.where(seq_end, 0.0, uh[:, 8:9])
            od_before = jnp.where(lane == 0, prev_tok, pltpu.roll(od, 1, 1))
            ev_after = jnp.where(lane == half - 1, next_tok, pltpu.roll(ev, half - 1, 1))
            cwb = cwb_ref[r0:r0 + HY_C, :]
            w0, w1, w2, b = cwb[:, 0:1], cwb[:, 1:2], cwb[:, 2:3], cwb[:, 3:4]
            store(r0, jnp.concatenate([od_before * w0 + ev * w1 + od * w2 + b,
                                       ev * w0 + od * w1 + ev_after * w2 + b], axis=1))


def _split_perm(tm):
    tok = jnp.arange(tm)
    pos = jnp.where(tok % 2 == 0, tok // 2, tm // 2 + tok // 2)
    return (jnp.arange(tm)[:, None] == pos[None, :]).astype(BF16)


def _rope_tables(ls):
    rows = ls // GRID_W
    r, col = jnp.meshgrid(jnp.arange(rows), jnp.arange(GRID_W), indexing='ij')
    inv = ROPE_THETA ** (-jnp.arange(0, ROPE_AXIS_DIM, 2, dtype=F32) / ROPE_AXIS_DIM)
    ang = jnp.concatenate([r.reshape(-1, 1).astype(F32) * inv, col.reshape(-1, 1).astype(F32) * inv], axis=-1)
    cos, sin = jnp.cos(ang), jnp.sin(ang)
    return jnp.concatenate([cos, cos] * 2, axis=1), jnp.concatenate([-sin, sin] * 2, axis=1)


def _ah_inproj(x, mod, row_of_tile, g, w_in, q_norm, k_norm, conv_w, conv_b, rope, *, tm, tp, lp, ls):
    t = x.shape[0]
    w = w_in[:, :AH_Q + 2 * AH_KV].astype(BF16)
    wut = w_in[:, AH_Q + 2 * AH_KV:].T.astype(BF16)
    gains = jnp.stack([jnp.tile(q_norm.astype(F32), H_A), jnp.tile(k_norm.astype(F32), H_A)])
    gid = jnp.arange(AH_Q) // HD_A
    bdq = (gid[:, None] == gid[None, :]).astype(BF16)
    cwb = jnp.concatenate([conv_w.astype(F32).T, conv_b.astype(F32)[:, None],
                           jnp.zeros((AH_U3, 4), F32)], axis=1)
    n_tiles_p = tp // tm
    prev, nxt = _halo_specs(tm, t)
    tile = lambda i: (i, 0)
    const = lambda i: (0, 0)
    rope_blk = lambda i: (jnp.where(i < n_tiles_p, 0, lax.rem(i * tm - tp, ls) // tm), 0)
    bf = lambda n: jax.ShapeDtypeStruct((t, n), BF16)
    return pl.pallas_call(
        functools.partial(_ah_inproj_kernel, tm=tm, n_tiles_p=n_tiles_p, tp=tp, lp=lp, ls=ls),
        out_shape=(bf(AH_Q), bf(AH_Q), bf(AH_Q), jax.ShapeDtypeStruct((t, AH_KV), F32),
                   jax.ShapeDtypeStruct((t, AH_KV), F32), jax.ShapeDtypeStruct((AH_U3, t), F32),
                   jax.ShapeDtypeStruct((HY_C, t), BF16)),
        grid=(t // tm,),
        in_specs=[prev, pl.BlockSpec((tm, D_MODEL), tile), nxt,
                  pl.BlockSpec((1, 9, D_MODEL), lambda i: (row_of_tile(i), 0, 0)),
                  pl.BlockSpec((1, D_MODEL), const), pl.BlockSpec(w.shape, const),
                  pl.BlockSpec(wut.shape, const), pl.BlockSpec(gains.shape, const),
                  pl.BlockSpec((tm, 128), rope_blk), pl.BlockSpec((tm, 128), rope_blk),
                  pl.BlockSpec(bdq.shape, const), pl.BlockSpec(cwb.shape, const),
                  pl.BlockSpec((tm, tm), const)],
        out_specs=(pl.BlockSpec((tm, AH_Q), tile),) * 3 + (pl.BlockSpec((tm, AH_KV), tile),) * 2
        + (pl.BlockSpec((AH_U3, tm), lambda i: (0, i)), pl.BlockSpec((HY_C, tm), lambda i: (0, i))),
        compiler_params=pltpu.CompilerParams(dimension_semantics=("arbitrary",),
                                             vmem_limit_bytes=VMEM_LIMIT),
        name="ah_inproj",
    )(x, x, x, mod, g.reshape(1, D_MODEL), w, wut, gains, rope[0], rope[1], bdq, cwb, _split_perm(tm))


def _attn_chains(chains, sink_ref, valid):
    tq = chains[0][0].shape[0]
    lane = lax.broadcasted_iota(jnp.int32, (tq, G_A * HD_A), 1)
    valid4 = None if valid is None else jnp.concatenate([valid] * G_A, axis=0)
    scores, sinks = [], []
    for q, kt, _, g in chains:
        qs = jnp.concatenate([jnp.where(lane // HD_A == j, q, jnp.zeros_like(q)) for j in range(G_A)], axis=0)
        s = lax.dot_general(qs, kt, (((1,), (1,)), ((), ())), preferred_element_type=F32)
        scores.append(s if valid4 is None else jnp.where(valid4, s, NEG_INF))
        sinks.append(jnp.concatenate([jnp.broadcast_to(sink_ref[g * G_A + j:g * G_A + j + 1, 0:1], (tq, 1))
                                      for j in range(G_A)], axis=0))
    maxes = [jnp.maximum(jnp.max(s, axis=-1, keepdims=True), sk) for s, sk in zip(scores, sinks)]
    probs = [jnp.exp(s - m) for s, m in zip(scores, maxes)]
    dens = [jnp.sum(p, axis=-1, keepdims=True) + jnp.exp(sk - m) for p, sk, m in zip(probs, sinks, maxes)]
    pvs = [jnp.dot(p.astype(BF16), c[2], preferred_element_type=F32) / d
           for p, c, d in zip(probs, chains, dens)]
    outs = []
    for pv in pvs:
        out = pv[:tq]
        for j in range(1, G_A):
            out = jnp.where(lane // HD_A == j, pv[j * tq:(j + 1) * tq], out)
        outs.append(out)
    return outs


def _attn_ctx_kernel(q_ref, kt_ref, vt_ref, sink_ref, o_ref, *, lp):
    w = G_A * HD_A
    where = [(slice(s0, s0 + lp), slice(g * w, (g + 1) * w), g)
             for s0 in range(0, q_ref.shape[0], lp) for g in range(KVH_A)]
    outs = _attn_chains([(q_ref[r, c], kt_ref[r, c], vt_ref[r, c], g) for r, c, g in where], sink_ref, None)
    for (r, c, _), o in zip(where, outs):
        o_ref[r, c] = o.astype(BF16)


def _attn_win_kernel(q_ref, kp_ref, km_ref, kn_ref, vp_ref, vm_ref, vn_ref, ck_ref, cv_ref, sink_ref,
                     prev_ref, o_ref, *, tq, ls):
    del prev_ref
    i = pl.program_id(1)
    n_ctx = ck_ref.shape[1]
    w = G_A * HD_A
    q_pos = i * tq + lax.broadcasted_iota(jnp.int32, (tq, n_ctx + tq + 2 * WINDOW), 0)
    col = lax.broadcasted_iota(jnp.int32, (tq, n_ctx + tq + 2 * WINDOW), 1)
    k_pos = i * tq - WINDOW + (col - n_ctx)
    valid = (col < n_ctx) | ((jnp.abs(q_pos - k_pos) <= WINDOW) & (k_pos >= 0) & (k_pos < ls))
    chains = []
    for g in range(KVH_A):
        cols = slice(g * w, (g + 1) * w)
        kt = jnp.concatenate([ck_ref[0, :, cols], kp_ref[:, cols], km_ref[:, cols], kn_ref[:, cols]], axis=0)
        vt = jnp.concatenate([cv_ref[0, :, cols], vp_ref[:, cols], vm_ref[:, cols], vn_ref[:, cols]], axis=0)
        chains.append((q_ref[:, cols], kt, vt, g))
    for g, o in enumerate(_attn_chains(chains, sink_ref, valid)):
        o_ref[:, g * w:(g + 1) * w] = o.astype(BF16)


def _attention(q, kt, vt, ck, cv, sink, *, tq, tp, lp, ls):
    t = q.shape[0]
    bp, bs = tp // lp, (t - tp) // ls
    sink_rows = jnp.broadcast_to(sink.astype(F32)[:, None], (H_A, 128))
    params = pltpu.CompilerParams(dimension_semantics=("arbitrary",), vmem_limit_bytes=VMEM_LIMIT)
    seq = lambda b: (b, 0)
    nsq = math.gcd(CTX_SEQS, bp)
    a = pl.pallas_call(
        functools.partial(_attn_ctx_kernel, lp=lp),
        out_shape=jax.ShapeDtypeStruct((t, AH_Q), BF16), grid=(bp // nsq,),
        in_specs=[pl.BlockSpec((nsq * lp, AH_Q), seq)] * 3 + [pl.BlockSpec((H_A, 128), lambda b: (0, 0))],
        out_specs=pl.BlockSpec((nsq * lp, AH_Q), seq), compiler_params=params, name="attn_ctx",
    )(q, kt, vt, sink_rows)
    nq = ls // tq
    wb = tq // WINDOW
    n128 = t // WINDOW
    main = lambda b, i: ((tp + b * ls) // tq + i, 0)
    prev = lambda b, i: (jnp.maximum((tp + b * ls) // WINDOW + i * wb - 1, 0), 0)
    nxt = lambda b, i: (jnp.minimum((tp + b * ls) // WINDOW + (i + 1) * wb, n128 - 1), 0)
    ctx = lambda b, i: (b, 0, 0)
    kv_specs = [pl.BlockSpec((WINDOW, AH_Q), prev), pl.BlockSpec((tq, AH_Q), main),
                pl.BlockSpec((WINDOW, AH_Q), nxt)]
    n_ctx = ck.shape[1]
    return pl.pallas_call(
        functools.partial(_attn_win_kernel, tq=tq, ls=ls),
        out_shape=jax.ShapeDtypeStruct((t, AH_Q), BF16), grid=(bs, nq),
        in_specs=[pl.BlockSpec((tq, AH_Q), main)] + kv_specs + kv_specs
        + [pl.BlockSpec((1, n_ctx, AH_Q), ctx)] * 2
        + [pl.BlockSpec((H_A, 128), lambda b, i: (0, 0)), pl.BlockSpec(memory_space=pl.ANY)],
        out_specs=pl.BlockSpec((tq, AH_Q), main),
        input_output_aliases={10: 0},
        compiler_params=pltpu.CompilerParams(dimension_semantics=("arbitrary", "arbitrary"),
                                             vmem_limit_bytes=VMEM_LIMIT),
        name="attn_win",
    )(q, kt, kt, kt, vt, vt, vt, ck, cv, sink_rows, a)


DFT_SUB = 64
DFT_HALF = 256


def _dft_gen_kernel(o_ref, cl_ref, sl_ref, *, length):
    a = pl.program_id(0)
    n2 = 4 * length

    def angles(n_vec, shape):
        k = lax.broadcasted_iota(jnp.int32, shape, 1)
        ph = lax.rem(n_vec * (2 * k + 1), n2)
        return ph.astype(F32) * (2.0 * math.pi / n2)

    @pl.when(a == 0)
    def _():
        th = angles(lax.broadcasted_iota(jnp.int32, (DFT_SUB, length), 0), (DFT_SUB, length))
        cl_ref[...] = jnp.cos(th)
        sl_ref[...] = jnp.sin(th)

    th = angles(jnp.full((8, length), a * DFT_SUB, jnp.int32), (8, length))[0:1]
    ch, sh = jnp.cos(th), jnp.sin(th)
    c = ch * cl_ref[...] - sh * sl_ref[...]
    ns = -(sh * cl_ref[...] + ch * sl_ref[...])
    for jt in range(length // DFT_HALF):
        src = slice(jt * DFT_HALF, (jt + 1) * DFT_HALF)
        o_ref[:, 2 * jt * DFT_HALF:(2 * jt + 1) * DFT_HALF] = c[:, src].astype(BF16)
        o_ref[:, (2 * jt + 1) * DFT_HALF:(2 * jt + 2) * DFT_HALF] = ns[:, src].astype(BF16)


def _dft_matrix(length):
    return pl.pallas_call(
        functools.partial(_dft_gen_kernel, length=length),
        out_shape=jax.ShapeDtypeStruct((length, 2 * length), BF16), grid=(length // DFT_SUB,),
        out_specs=pl.BlockSpec((DFT_SUB, 2 * length), lambda a: (a, 0)),
        scratch_shapes=[pltpu.VMEM((DFT_SUB, length), F32)] * 2,
        compiler_params=pltpu.CompilerParams(dimension_semantics=("arbitrary",),
                                             vmem_limit_bytes=VMEM_LIMIT),
        name="dft_gen",
    )()


HYF_RB = 256


def _hy_filter_kernel(w1_ref, w2_ref, w3f_ref, w3b_ref, cols_ref, f_ref, o_ref, a_ref, b_ref, *, length):
    rc, jt = pl.program_id(0), pl.program_id(1)

    @pl.when(jt == 0)
    def _():
        m = lax.broadcasted_iota(jnp.int32, (8, length), 1).astype(F32)[0:1]
        t = m * (1.0 / (length - 1))
        w = m * (2.0 * math.pi / length)
        band = lax.broadcasted_iota(jnp.int32, (HY_BANDS, 1), 0).astype(F32)
        fb = 1e-4 + band * ((HY_BANDS - 1 - 1e-4) / (HY_BANDS - 1))
        z = jnp.concatenate([t, jnp.cos(fb * w), -jnp.sin(fb * w),
                             jnp.zeros((40 - HY_EMB, length), F32)], axis=0)
        cols = cols_ref[...]
        h = jnp.sin(cols[:, 1:2] * (_bdot(w1_ref[...], z) + cols[:, 0:1]))
        h = jnp.sin(cols[:, 3:4] * (_bdot(w2_ref[...], h) + cols[:, 2:3]))
        n_out = 2 * HY_ORDER * HY_C
        row = (rc * HYF_RB + lax.broadcasted_iota(jnp.int32, (HYF_RB, 1), 0)).astype(F32)
        step = (HY_MAX_DECAY - HY_MIN_DECAY) / (n_out - 1)
        d_f = jnp.abs(HY_MIN_DECAY + row * step)
        d_b = jnp.abs(HY_MIN_DECAY + (row + HY_ORDER * HY_C) * step)
        hf = _bdot(w3f_ref[...], h) * jnp.exp(-t * d_f)
        hb = jnp.where(m == 0.0, 0.0, _bdot(w3b_ref[...], h) * jnp.exp(-t * d_b))
        ssq = jnp.sum(hf * hf, axis=-1, keepdims=True) + jnp.sum(hb * hb, axis=-1, keepdims=True)
        scale = lax.rsqrt(ssq + NORM_EPS) * (1.0 / length)
        a_ref[...] = ((hf + hb) * scale).astype(BF16)
        b_ref[...] = ((hf - hb) * scale).astype(BF16)

    o_ref[:, :DFT_HALF] = jnp.dot(a_ref[...], f_ref[:, :DFT_HALF], preferred_element_type=F32)
    o_ref[:, DFT_HALF:] = jnp.dot(b_ref[...], f_ref[:, DFT_HALF:], preferred_element_type=F32)


def _hy_filter(fmat, w1, b1, f1, w2, b2, f2, w3, *, length):
    w1t = jnp.pad(w1.astype(F32).T, ((0, 0), (0, 40 - HY_EMB)))
    w3t = w3.astype(F32).T
    cols = jnp.stack([b1, f1, b2, f2] + [jnp.zeros_like(b1)] * 4, axis=1).astype(F32)
    n_rows = HY_ORDER * HY_C
    const = lambda rc, jt: (0, 0)
    return pl.pallas_call(
        functools.partial(_hy_filter_kernel, length=length),
        out_shape=jax.ShapeDtypeStruct((n_rows, 2 * length), F32),
        grid=(n_rows // HYF_RB, length // DFT_HALF),
        in_specs=[pl.BlockSpec(w1t.shape, const), pl.BlockSpec((w2.shape[1], w2.shape[0]), const),
                  pl.BlockSpec((HYF_RB, w3t.shape[1]), lambda rc, jt: (rc, 0)),
                  pl.BlockSpec((HYF_RB, w3t.shape[1]), lambda rc, jt: (n_rows // HYF_RB + rc, 0)),
                  pl.BlockSpec(cols.shape, const),
                  pl.BlockSpec((length, 2 * DFT_HALF), lambda rc, jt: (0, jt))],
        out_specs=pl.BlockSpec((HYF_RB, 2 * DFT_HALF), lambda rc, jt: (rc, jt)),
        scratch_shapes=[pltpu.VMEM((HYF_RB, length), BF16)] * 2,
        compiler_params=pltpu.CompilerParams(dimension_semantics=("arbitrary", "arbitrary"),
                                             vmem_limit_bytes=VMEM_LIMIT),
        name="hy_filter",
    )(w1t, w2.astype(F32).T, w3t, w3t, cols, fmat)


def _hy_fwd_kernel(z_ref, f_ref, k_ref, y_ref):
    zt = jnp.dot(z_ref[...], f_ref[...], preferred_element_type=F32)
    zr, zi = zt[:, :DFT_HALF], zt[:, DFT_HALF:]
    kr, ki = k_ref[:, :DFT_HALF], k_ref[:, DFT_HALF:]
    y_ref[:, :DFT_HALF] = (zr * kr - zi * ki).astype(BF16)
    y_ref[:, DFT_HALF:] = (zr * ki + zi * kr).astype(BF16)


def _hy_fwd(zb, fmat, kspec, order, *, lane0, n_seq, length):
    sb0 = lane0 // length
    return pl.pallas_call(
        _hy_fwd_kernel,
        out_shape=jax.ShapeDtypeStruct((n_seq * HY_C, 2 * length), BF16),
        grid=(n_seq, length // DFT_HALF),
        in_specs=[pl.BlockSpec((HY_C, length), lambda s, jt: (0, sb0 + s)),
                  pl.BlockSpec((length, 2 * DFT_HALF), lambda s, jt: (0, jt)),
                  pl.BlockSpec((HY_C, 2 * DFT_HALF), lambda s, jt: (order, jt))],
        out_specs=pl.BlockSpec((HY_C, 2 * DFT_HALF), lambda s, jt: (s, jt)),
        compiler_params=pltpu.CompilerParams(dimension_semantics=("arbitrary", "arbitrary"),
                                             vmem_limit_bytes=VMEM_LIMIT),
        name="hy_fwd",
    )(zb, fmat, kspec)


def _hy_inv_kernel(y_ref, f_ref, gate_ref, z_ref, bias_ref, *rest, with_prev, with_bf16):
    rest = list(rest)
    if with_prev:
        rest.pop(0)
    o_ref = rest.pop(0)
    y = lax.dot_general(y_ref[...], f_ref[...], (((1,), (1,)), ((), ())), preferred_element_type=F32)
    out = gate_ref[...] * (y + z_ref[...] * bias_ref[...])
    o_ref[...] = out
    if with_bf16:
        rest.pop(0)[...] = out.astype(BF16)


def _hy_inv(yspec, fmat, gate_arr, gate_rb, z_arr, z_rb, bias, prev, *, lane0, n_seq, length, tt, t_total,
            with_bf16):
    nt = length // tt
    lane_blk = lambda s, j: lane0 // tt + s * nt + j
    in_specs = [pl.BlockSpec((HY_C, 2 * length), lambda s, j: (s, 0)),
                pl.BlockSpec((tt, 2 * length), lambda s, j: (j, 0)),
                pl.BlockSpec((HY_C, tt), lambda s, j: (gate_rb, lane_blk(s, j))),
                pl.BlockSpec((HY_C, tt), lambda s, j: (z_rb, lane_blk(s, j))),
                pl.BlockSpec((HY_C, 1), lambda s, j: (0, 0))]
    args = [yspec, fmat, gate_arr, z_arr, bias.astype(F32).reshape(HY_C, 1)]
    aliases = {}
    if prev is not None:
        in_specs.append(pl.BlockSpec(memory_space=pl.ANY))
        aliases = {len(args): 0}
        args.append(prev)
    out_spec = pl.BlockSpec((HY_C, tt), lambda s, j: (0, lane_blk(s, j)))
    out_shape = [jax.ShapeDtypeStruct((HY_C, t_total), F32)]
    out_specs = [out_spec]
    if with_bf16:
        out_shape.append(jax.ShapeDtypeStruct((HY_C, t_total), BF16))
        out_specs.append(out_spec)
    res = pl.pallas_call(
        functools.partial(_hy_inv_kernel, with_prev=prev is not None, with_bf16=with_bf16),
        out_shape=tuple(out_shape), grid=(n_seq, nt), in_specs=in_specs, out_specs=tuple(out_specs),
        input_output_aliases=aliases,
        compiler_params=pltpu.CompilerParams(dimension_semantics=("arbitrary", "arbitrary"),
                                             vmem_limit_bytes=VMEM_LIMIT),
        name="hy_inv",
    )(*args)
    return res


def _hyena_group(ut, vb, fmat, kspec, hy_bias, y_prev, *, lane0, n_seq, length, t_total):
    tt = min(512, length)
    geo = dict(lane0=lane0, n_seq=n_seq, length=length)
    y1 = _hy_fwd(vb, fmat, kspec, 0, **geo)
    z2, z2b = _hy_inv(y1, fmat, ut, 0, ut, 2, hy_bias[0], None, tt=tt, t_total=t_total, with_bf16=True, **geo)
    y2 = _hy_fwd(z2b, fmat, kspec, 1, **geo)
    return _hy_inv(y2, fmat, ut, 1, z2, 0, hy_bias[1], y_prev, tt=tt, t_total=t_total, with_bf16=False,
                   **geo)[0]


HY_TILE = TOKEN_TILE


def _twiddle(jt, n):
    k0 = (jt * DFT_HALF + lax.broadcasted_iota(jnp.int32, (8, DFT_HALF), 1)).astype(F32)[0:1]
    ang = (k0 + 0.5) * (2.0 * math.pi / n)
    return jnp.cos(ang), -jnp.sin(ang)


def _cmul(ar, ai, br, bi):
    return ar * br - ai * bi, ar * bi + ai * br


def _hy_filter2_kernel(w1_ref, w2_ref, w3f_ref, w3b_ref, cols_ref, f_ref, ka_ref, kb_ref, *scr, length):
    rc, jt = pl.program_id(0), pl.program_id(1)
    half = length // 2

    @pl.when(jt == 0)
    def _():
        cols = cols_ref[...]
        n_out = 2 * HY_ORDER * HY_C
        row = (rc * HYF_RB + lax.broadcasted_iota(jnp.int32, (HYF_RB, 1), 0)).astype(F32)
        step = (HY_MAX_DECAY - HY_MIN_DECAY) / (n_out - 1)
        d_f = jnp.abs(HY_MIN_DECAY + row * step)
        d_b = jnp.abs(HY_MIN_DECAY + (row + HY_ORDER * HY_C) * step)
        band = lax.broadcasted_iota(jnp.int32, (HY_BANDS, 1), 0).astype(F32)
        fb = 1e-4 + band * ((HY_BANDS - 1 - 1e-4) / (HY_BANDS - 1))
        taps = []
        for parity in range(2):
            m = 2.0 * lax.broadcasted_iota(jnp.int32, (8, half), 1).astype(F32)[0:1] + parity
            t = m * (1.0 / (length - 1))
            w = m * (2.0 * math.pi / length)
            z = jnp.concatenate([t, jnp.cos(fb * w), -jnp.sin(fb * w),
                                 jnp.zeros((40 - HY_EMB, half), F32)], axis=0)
            h = jnp.sin(cols[:, 1:2] * (_bdot(w1_ref[...], z) + cols[:, 0:1]))
            h = jnp.sin(cols[:, 3:4] * (_bdot(w2_ref[...], h) + cols[:, 2:3]))
            hf = _bdot(w3f_ref[...], h) * jnp.exp(-t * d_f)
            hb = jnp.where(m == 0.0, 0.0, _bdot(w3b_ref[...], h) * jnp.exp(-t * d_b))
            taps.append((hf, hb))
        ssq = sum(jnp.sum(hf * hf, axis=-1, keepdims=True) + jnp.sum(hb * hb, axis=-1, keepdims=True)
                  for hf, hb in taps)
        scale = lax.rsqrt(ssq + NORM_EPS) * (1.0 / length)
        (hf, hb), (hfo, hbo) = taps
        scr[0][...] = ((hf + hb) * scale).astype(BF16)
        scr[1][...] = ((hf - hb) * scale).astype(BF16)
        scr[2][...] = (hfo * scale).astype(BF16)
        scr[3][...] = (hbo * scale).astype(BF16)

    dot = functools.partial(jnp.dot, preferred_element_type=F32)
    fc, fs = f_ref[:, :DFT_HALF], f_ref[:, DFT_HALF:]
    er, ei = dot(scr[0][...], fc), dot(scr[1][...], fs)
    tr, ti = _twiddle(jt, 2 * length)
    fr, fi = _cmul(tr, ti, dot(scr[2][...], fc), dot(scr[2][...], fs))
    gr, gi = _cmul(tr, -ti, dot(scr[3][...], fc), -dot(scr[3][...], fs))
    ka_ref[:, :DFT_HALF] = er + (fr + gr)
    ka_ref[:, DFT_HALF:] = ei + (fi + gi)
    kb_ref[:, :DFT_HALF] = er - (fr + gr)
    kb_ref[:, DFT_HALF:] = ei - (fi + gi)


def _hy_filter2(fhalf, w1, b1, f1, w2, b2, f2, w3, *, length):
    w1t = jnp.pad(w1.astype(F32).T, ((0, 0), (0, 40 - HY_EMB)))
    w3t = w3.astype(F32).T
    cols = jnp.stack([b1, f1, b2, f2] + [jnp.zeros_like(b1)] * 4, axis=1).astype(F32)
    n_rows = HY_ORDER * HY_C
    half = length // 2
    const = lambda rc, jt: (0, 0)
    out = jax.ShapeDtypeStruct((n_rows, length), F32)
    return pl.pallas_call(
        functools.partial(_hy_filter2_kernel, length=length),
        out_shape=(out, out), grid=(n_rows // HYF_RB, half // DFT_HALF),
        in_specs=[pl.BlockSpec(w1t.shape, const), pl.BlockSpec((w2.shape[1], w2.shape[0]), const),
                  pl.BlockSpec((HYF_RB, w3t.shape[1]), lambda rc, jt: (rc, 0)),
                  pl.BlockSpec((HYF_RB, w3t.shape[1]), lambda rc, jt: (n_rows // HYF_RB + rc, 0)),
                  pl.BlockSpec(cols.shape, const),
                  pl.BlockSpec((half, 2 * DFT_HALF), lambda rc, jt: (0, jt))],
        out_specs=(pl.BlockSpec((HYF_RB, 2 * DFT_HALF), lambda rc, jt: (rc, jt)),) * 2,
        scratch_shapes=[pltpu.VMEM((HYF_RB, half), BF16)] * 4,
        compiler_params=pltpu.CompilerParams(dimension_semantics=("arbitrary", "arbitrary"),
                                             vmem_limit_bytes=VMEM_LIMIT),
        name="hy_filter2",
    )(w1t, w2.astype(F32).T, w3t, w3t, cols, fhalf)


def _hy_fwd2_kernel(z_ref, f_ref, ka_ref, kb_ref, p_ref, q_ref, eo_ref, *, length):
    jt = pl.program_id(1)
    half_tile = HY_TILE // 2

    @pl.when(jt == 0)
    def _():
        for j in range(length // HY_TILE):
            dst = slice(j * half_tile, (j + 1) * half_tile)
            eo_ref[:HY_C, dst] = z_ref[:, j * HY_TILE:j * HY_TILE + half_tile]
            eo_ref[HY_C:, dst] = z_ref[:, j * HY_TILE + half_tile:(j + 1) * HY_TILE]

    r = jnp.dot(eo_ref[...], f_ref[...], preferred_element_type=F32)
    er, ei = r[:HY_C, :DFT_HALF], r[:HY_C, DFT_HALF:]
    tr, ti = _twiddle(jt, 2 * length)
    pr, pi = _cmul(tr, ti, r[HY_C:, :DFT_HALF], r[HY_C:, DFT_HALF:])
    yar, yai = _cmul(er + pr, ei + pi, ka_ref[:, :DFT_HALF], ka_ref[:, DFT_HALF:])
    ybr, ybi = _cmul(er - pr, ei - pi, kb_ref[:, :DFT_HALF], kb_ref[:, DFT_HALF:])
    p_ref[:, :DFT_HALF] = (yar + ybr).astype(BF16)
    p_ref[:, DFT_HALF:] = (yai + ybi).astype(BF16)
    qr, qi = _cmul(yar - ybr, yai - ybi, tr, -ti)
    q_ref[:, :DFT_HALF] = qr.astype(BF16)
    q_ref[:, DFT_HALF:] = qi.astype(BF16)


def _hy_fwd2(zb, fhalf, kspec, order, *, lane0, n_seq, length):
    sb0 = lane0 // length
    half = length // 2
    out = jax.ShapeDtypeStruct((n_seq * HY_C, length), BF16)
    tile = pl.BlockSpec((HY_C, 2 * DFT_HALF), lambda s, jt: (s, jt))
    kblk = pl.BlockSpec((HY_C, 2 * DFT_HALF), lambda s, jt: (order, jt))
    return pl.pallas_call(
        functools.partial(_hy_fwd2_kernel, length=length),
        out_shape=(out, out), grid=(n_seq, half // DFT_HALF),
        in_specs=[pl.BlockSpec((HY_C, length), lambda s, jt: (0, sb0 + s)),
                  pl.BlockSpec((half, 2 * DFT_HALF), lambda s, jt: (0, jt)), kblk, kblk],
        out_specs=(tile, tile),
        scratch_shapes=[pltpu.VMEM((2 * HY_C, half), BF16)],
        compiler_params=pltpu.CompilerParams(dimension_semantics=("arbitrary", "arbitrary"),
                                             vmem_limit_bytes=VMEM_LIMIT),
        name="hy_fwd2",
    )(zb, fhalf, kspec[0], kspec[1])


def _hy_inv2_kernel(p_ref, q_ref, f_ref, gate_ref, z_ref, bias_ref, *rest, with_prev, with_bf16):
    rest = list(rest)
    if with_prev:
        rest.pop(0)
    o_ref = rest.pop(0)
    pq = jnp.concatenate([p_ref[...], q_ref[...]], axis=0)
    y = lax.dot_general(pq, f_ref[...], (((1,), (1,)), ((), ())), preferred_element_type=F32)
    y = jnp.concatenate([y[:HY_C], y[HY_C:]], axis=1)
    out = gate_ref[...] * (y + z_ref[...] * bias_ref[...])
    o_ref[...] = out
    if with_bf16:
        rest.pop(0)[...] = out.astype(BF16)


def _hy_inv2(pq, fhalf, gate_arr, gate_rb, z_arr, z_rb, bias, prev, *, lane0, n_seq, length, t_total,
             with_bf16):
    nt = length // HY_TILE
    lane_blk = lambda s, j: lane0 // HY_TILE + s * nt + j
    spec = pl.BlockSpec((HY_C, length), lambda s, j: (s, 0))
    in_specs = [spec, spec, pl.BlockSpec((HY_TILE // 2, length), lambda s, j: (j, 0)),
                pl.BlockSpec((HY_C, HY_TILE), lambda s, j: (gate_rb, lane_blk(s, j))),
                pl.BlockSpec((HY_C, HY_TILE), lambda s, j: (z_rb, lane_blk(s, j))),
                pl.BlockSpec((HY_C, 1), lambda s, j: (0, 0))]
    args = [pq[0], pq[1], fhalf, gate_arr, z_arr, bias.astype(F32).reshape(HY_C, 1)]
    aliases = {}
    if prev is not None:
        in_specs.append(pl.BlockSpec(memory_space=pl.ANY))
        aliases = {len(args): 0}
        args.append(prev)
    out_spec = pl.BlockSpec((HY_C, HY_TILE), lambda s, j: (0, lane_blk(s, j)))
    out_shape = [jax.ShapeDtypeStruct((HY_C, t_total), F32)]
    out_specs = [out_spec]
    if with_bf16:
        out_shape.append(jax.ShapeDtypeStruct((HY_C, t_total), BF16))
        out_specs.append(out_spec)
    return pl.pallas_call(
        functools.partial(_hy_inv2_kernel, with_prev=prev is not None, with_bf16=with_bf16),
        out_shape=tuple(out_shape), grid=(n_seq, nt), in_specs=in_specs, out_specs=tuple(out_specs),
        input_output_aliases=aliases,
        compiler_params=pltpu.CompilerParams(dimension_semantics=("arbitrary", "arbitrary"),
                                             vmem_limit_bytes=VMEM_LIMIT),
        name="hy_inv2",
    )(*args)


def _hyena_group2(ut, vb, fhalf, kspec, hy_bias, y_prev, *, lane0, n_seq, length, t_total):
    geo = dict(lane0=lane0, n_seq=n_seq, length=length)
    pq = _hy_fwd2(vb, fhalf, kspec, 0, **geo)
    z2, z2b = _hy_inv2(pq, fhalf, ut, 0, ut, 2, hy_bias[0], None, t_total=t_total, with_bf16=True, **geo)
    pq = _hy_fwd2(z2b, fhalf, kspec, 1, **geo)
    return _hy_inv2(pq, fhalf, ut, 1, z2, 0, hy_bias[1], y_prev, t_total=t_total, with_bf16=False, **geo)[0]


def _ah_outproj_kernel(x_ref, a_ref, yt_ref, mod_ref, wa_ref, wy_ref, perm_ref, o_ref, *, n_tiles_p):
    o = jnp.dot(a_ref[...], wa_ref[...], preferred_element_type=F32)
    yb = yt_ref[...].astype(BF16)
    nat = jnp.dot(yb, perm_ref[...], preferred_element_type=F32).astype(BF16)
    yb = jnp.where(pl.program_id(0) < n_tiles_p, yb, nat)
    o = o + lax.dot_general(yb, wy_ref[...], (((0,), (0,)), ((), ())), preferred_element_type=F32)
    o_ref[...] = x_ref[...] + mod_ref[0][5:6] * o


def _ah_outproj(x, a, yt, mod, row_of_tile, w_out, *, tm, tp):
    t = x.shape[0]
    tile = lambda i: (i, 0)
    const = lambda i: (0, 0)
    wa, wy = w_out[:AH_Q].astype(BF16), w_out[AH_Q:].astype(BF16)
    perm = _split_perm(tm)
    return pl.pallas_call(
        functools.partial(_ah_outproj_kernel, n_tiles_p=tp // tm),
        out_shape=jax.ShapeDtypeStruct((t, D_MODEL), F32), grid=(t // tm,),
        in_specs=[pl.BlockSpec((tm, D_MODEL), tile), pl.BlockSpec((tm, AH_Q), tile),
                  pl.BlockSpec((HY_C, tm), lambda i: (0, i)),
                  pl.BlockSpec((1, 9, D_MODEL), lambda i: (row_of_tile(i), 0, 0)),
                  pl.BlockSpec(wa.shape, const), pl.BlockSpec(wy.shape, const),
                  pl.BlockSpec((tm, tm), const)],
        out_specs=pl.BlockSpec((tm, D_MODEL), tile),
        compiler_params=pltpu.CompilerParams(dimension_semantics=("arbitrary",),
                                             vmem_limit_bytes=VMEM_LIMIT),
        name="ah_outproj",
    )(x, a, yt, mod, wa, wy, perm)


def _ah_layer(x, mod, row_of_tile, g, w_in, w_out, q_norm, k_norm, sink, conv_w, conv_b, hy_bias,
              ck, cv, rope, fmats, kspecs, *, tm, tq, tp, lp, ls):
    t = x.shape[0]
    q, kt, vt, kn, vn, ut, vb = _ah_inproj(x, mod, row_of_tile, g, w_in, q_norm, k_norm, conv_w, conv_b, rope,
                                           tm=tm, tp=tp, lp=lp, ls=ls)
    a = _attention(q, kt, vt, ck, cv, sink, tq=tq, tp=tp, lp=lp, ls=ls)
    yt = _hyena_group(ut, vb, fmats[0], kspecs[0], hy_bias, None, lane0=0, n_seq=tp // lp, length=lp, t_total=t)
    yt = _hyena_group2(ut, vb, fmats[1], kspecs[1], hy_bias, yt, lane0=tp, n_seq=(t - tp) // ls, length=ls,
                       t_total=t)
    return _ah_outproj(x, a, yt, mod, row_of_tile, w_out, tm=tm, tp=tp), kn[:tp], vn[:tp]


def _rep4_ctx(c):
    b, s = c.shape[:2]
    return jnp.broadcast_to(c[:, :, :, None, :], (b, s, KVH_A, G_A, HD_A)).reshape(b, s, AH_Q).astype(BF16)


def _ada_kernel(c_ref, w_ref, b_ref, o_ref):
    s = jax.nn.silu(c_ref[...]).astype(BF16)
    o_ref[0] = jnp.dot(s, w_ref[0].astype(BF16), preferred_element_type=F32) + b_ref[0]


def _ada_mod(cond, ada_w, ada_b):
    depth, _, n = ada_w.shape
    rows = 16
    cp = jnp.pad(cond.astype(F32), ((0, rows - cond.shape[0]), (0, 0)))
    out = pl.pallas_call(
        _ada_kernel, out_shape=jax.ShapeDtypeStruct((depth, rows, n), F32),
        grid=(depth, n // D_MODEL),
        in_specs=[pl.BlockSpec((rows, D_MODEL), lambda l, j: (0, 0)),
                  pl.BlockSpec((1, D_MODEL, D_MODEL), lambda l, j: (l, 0, j)),
                  pl.BlockSpec((1, 1, D_MODEL), lambda l, j: (l, 0, j))],
        out_specs=pl.BlockSpec((1, rows, D_MODEL), lambda l, j: (l, 0, j)),
        compiler_params=pltpu.CompilerParams(dimension_semantics=("arbitrary", "arbitrary"),
                                             vmem_limit_bytes=VMEM_LIMIT),
        name="ada_mod",
    )(cp, ada_w, ada_b.reshape(depth, 1, n))
    return out.reshape(depth, rows, n // D_MODEL, D_MODEL)


def kernel(x_prompt, x_sample, cache_k, cache_v, state_fwd, state_bwd, c, c_ctx, norm_g, ada_w, ada_b, ffn_w13, ffn_w2, mx_w_in, mx_w_out, q_norm, k_norm, attn_sink, hy_conv_w, hy_conv_b, hy_w1, hy_b1, hy_freq1, hy_w2, hy_b2, hy_freq2, hy_w3, hy_bias, dn_w_in, dn_w_out, dn_conv_w, dn_a_log, dn_dt_bias, dn_norm_g):
    bp, lp, _ = x_prompt.shape
    bs, ls, _ = x_sample.shape
    tp, ts = bp * lp, bs * ls
    assert tp % ls == 0
    tm = math.gcd(TOKEN_TILE, math.gcd(tp, ls))
    ct = math.gcd(DN_BLOCK, math.gcd(lp, ls))
    tq = math.gcd(ATTN_TQ, ls)
    x = jnp.concatenate([x_prompt.reshape(tp, D_MODEL), x_sample.reshape(ts, D_MODEL)], axis=0)
    cond = jnp.concatenate([c, c_ctx[None, :]], axis=0)
    mods = _ada_mod(cond, ada_w, ada_b)
    tiles_p, tiles_per_s = tp // tm, ls // tm

    def row_of_tile(i):
        return jnp.where(i < tiles_p, bs, (i - tiles_p) // tiles_per_s)

    rope = _rope_tables(ls)
    assert tm == HY_TILE and ls % tm == 0 and (ls // 2) % DFT_HALF == 0
    fmats = (_dft_matrix(lp), _dft_matrix(ls // 2))
    new_k, new_v, new_sf, new_sb = [], [], [], []
    for layer in range(DEPTH):
        mod = mods[layer]
        i = layer // 2
        x = _ffn(x, mod, row_of_tile, norm_g[layer, 0], ffn_w13[layer, 0], ffn_w2[layer, 0], 0, tm=tm)
        if layer % 2 == 0:
            kspecs = tuple(fn(f, hy_w1[i], hy_b1[i], hy_freq1[i], hy_w2[i], hy_b2[i], hy_freq2[i], hy_w3[i],
                              length=n) for fn, f, n in zip((_hy_filter, _hy_filter2), fmats, (lp, ls)))
            x, k_p, v_p = _ah_layer(x, mod, row_of_tile, norm_g[layer, 1], mx_w_in[i], mx_w_out[i], q_norm[i],
                                    k_norm[i], attn_sink[i], hy_conv_w[i], hy_conv_b[i], hy_bias[i],
                                    _rep4_ctx(cache_k[:, i]), _rep4_ctx(cache_v[:, i]), rope, fmats, kspecs,
                                    tm=tm, tq=tq, tp=tp, lp=lp, ls=ls)
            new_k.append(k_p.reshape(bp, lp, KVH_A, HD_A))
            new_v.append(v_p.reshape(bp, lp, KVH_A, HD_A))
        else:
            x, s_f, s_b = _dn_layer(x, mod, row_of_tile, norm_g[layer, 1], dn_w_in[i], dn_w_out[i],
                                    dn_conv_w[i], dn_a_log[i], dn_dt_bias[i], dn_norm_g[i],
                                    state_fwd[:, i], state_bwd[:, i], tm=tm, ct=ct, tp=tp, lp=lp, ls=ls)
            new_sf.append(s_f)
            new_sb.append(s_b)
        x = _ffn(x, mod, row_of_tile, norm_g[layer, 2], ffn_w13[layer, 1], ffn_w2[layer, 1], 2, tm=tm)

    return (x[:tp].reshape(bp, lp, D_MODEL), x[tp:].reshape(bs, ls, D_MODEL),
            jnp.stack(new_k, axis=1), jnp.stack(new_v, axis=1),
            jnp.stack(new_sf, axis=1), jnp.stack(new_sb, axis=1))
```

```python
import functools
import math

import jax
import jax.numpy as jnp
from jax import lax
from jax.experimental import pallas as pl
from jax.experimental.pallas import tpu as pltpu

D_MODEL = 1024
DEPTH = 4
GRID_W = 64
H_A = 8
KVH_A = 2
G_A = H_A // KVH_A
HD_A = 64
WINDOW = 128
ATTN_BLOCK = 128
ROPE_THETA = 10000.0
ROPE_AXIS_DIM = HD_A // 2
HY_C = 512
HY_ORDER = 2
HY_EMB = 33
HY_BANDS = (HY_EMB - 1) // 2
HY_MIN_DECAY = math.log(1e-2) / 1.5
HY_MAX_DECAY = math.log(1e-2) / 0.3
H_C = 8
DK_C = 128
DV_C = 128
DN_CHUNK = 64
DN_OUT = H_C * DV_C
D_FF = 2816
NORM_EPS = 1e-6
NEG_INF = -1e30

F32 = jnp.float32
BF16 = jnp.bfloat16

TOKEN_TILE = 512
FFN_CHUNK = 256
DN_BLOCK = 128
ATTN_TQ = 256
CTX_SEQS = 4
VMEM_LIMIT = 56 * 1024 * 1024


def _ffn_kernel(x_ref, mod_ref, g_ref, w1_ref, w3_ref, w2_ref, o_ref, acc_ref, *, j, n_chunks):
    x = x_ref[...]
    y = x * lax.rsqrt(jnp.mean(x * x, axis=-1, keepdims=True) + NORM_EPS) * g_ref[...]
    m = mod_ref[0]
    shift, scale, gate = m[3 * j:3 * j + 1], m[3 * j + 1:3 * j + 2], m[3 * j + 2:3 * j + 3]
    h = (y * (1.0 + scale) + shift).astype(BF16)
    acc_ref[...] = jnp.zeros_like(acc_ref)

    def body(c, carry):
        gt = jnp.dot(h, w1_ref[c], preferred_element_type=F32)
        up = jnp.dot(h, w3_ref[c], preferred_element_type=F32)
        a = (jax.nn.silu(gt) * up).astype(BF16)
        acc_ref[...] += jnp.dot(a, w2_ref[c], preferred_element_type=F32)
        return carry

    lax.fori_loop(0, n_chunks, body, 0, unroll=True)
    o_ref[...] = x + 0.5 * gate * acc_ref[...]


def _ffn(x, mod, row_of_tile, g, w13, w2, j, *, tm):
    t = x.shape[0]
    n_chunks = D_FF // FFN_CHUNK
    w1 = w13[:, :D_FF].astype(BF16).reshape(D_MODEL, n_chunks, FFN_CHUNK).transpose(1, 0, 2)
    w3 = w13[:, D_FF:].astype(BF16).reshape(D_MODEL, n_chunks, FFN_CHUNK).transpose(1, 0, 2)
    w2c = w2.astype(BF16).reshape(n_chunks, FFN_CHUNK, D_MODEL)
    const3 = lambda i: (0, 0, 0)
    return pl.pallas_call(
        functools.partial(_ffn_kernel, j=j, n_chunks=n_chunks),
        out_shape=jax.ShapeDtypeStruct((t, D_MODEL), F32),
        grid=(t // tm,),
        in_specs=[
            pl.BlockSpec((tm, D_MODEL), lambda i: (i, 0)),
            pl.BlockSpec((1, 9, D_MODEL), lambda i: (row_of_tile(i), 0, 0)),
            pl.BlockSpec((1, D_MODEL), lambda i: (0, 0)),
            pl.BlockSpec((n_chunks, D_MODEL, FFN_CHUNK), const3),
            pl.BlockSpec((n_chunks, D_MODEL, FFN_CHUNK), const3),
            pl.BlockSpec((n_chunks, FFN_CHUNK, D_MODEL), const3),
        ],
        out_specs=pl.BlockSpec((tm, D_MODEL), lambda i: (i, 0)),
        scratch_shapes=[pltpu.VMEM((tm, D_MODEL), F32)],
        compiler_params=pltpu.CompilerParams(dimension_semantics=("arbitrary",),
                                             vmem_limit_bytes=VMEM_LIMIT),
        name=f"ffn{j}",
    )(x, mod, g.reshape(1, D_MODEL), w1, w3, w2c)


def _bdot(a, b):
    return jnp.dot(a.astype(BF16), b.astype(BF16), preferred_element_type=F32)


def _bdot_nt(a, b):
    return lax.dot_general(a.astype(BF16), b.astype(BF16), (((1,), (1,)), ((), ())),
                           preferred_element_type=F32)


def _bdot_tn(a, b):
    return lax.dot_general(a.astype(BF16), b.astype(BF16), (((0,), (0,)), ((), ())),
                           preferred_element_type=F32)


def _split3(x):
    hi = x.astype(BF16)
    r1 = x - hi.astype(F32)
    mid = r1.astype(BF16)
    lo = (r1 - mid.astype(F32)).astype(BF16)
    return hi, mid, lo


def _ada_h(x, g_row, m, j):
    y = x * lax.rsqrt(jnp.mean(x * x, axis=-1, keepdims=True) + NORM_EPS) * g_row
    return y * (1.0 + m[3 * j + 1:3 * j + 2]) + m[3 * j:3 * j + 1]


def _seq_edges(i, tm, n_tiles_p, tp, lp, ls, shape, axis):
    is_p = i < n_tiles_p
    seq_len = jnp.where(is_p, lp, ls)
    t0 = i * tm - jnp.where(is_p, 0, tp)
    base = lax.rem(t0, seq_len)
    pos = (base + lax.broadcasted_iota(jnp.int32, shape, axis)).astype(F32)
    lf = seq_len.astype(F32)
    rem = pos - jnp.floor((pos + 0.5) / lf) * lf
    return rem == 0.0, rem == lf - 1.0


def _halo_specs(tm, t):
    nb = t // 8
    prev = pl.BlockSpec((8, D_MODEL), lambda i: (jnp.maximum(i * (tm // 8) - 1, 0), 0))
    nxt = pl.BlockSpec((8, D_MODEL), lambda i: (jnp.minimum((i + 1) * (tm // 8), nb - 1), 0))
    return prev, nxt


DN_QKV = 2 * H_C * DK_C + H_C * DV_C
DN_CW = 512


def _dn_inproj_kernel(xp_ref, x_ref, xn_ref, mod_ref, g_ref, w_ref, cw_ref, ab_ref,
                      q_ref, k_ref, v_ref, z_ref, bg_ref, pext_ref, *, tm, n_tiles_p, tp, lp, ls):
    i = pl.program_id(0)
    xe = jnp.concatenate([xp_ref[...], x_ref[...], xn_ref[...]], axis=0)
    h = _ada_h(xe, g_ref[...], mod_ref[0], 1).astype(BF16)
    first, last = _seq_edges(i, tm, n_tiles_p, tp, lp, ls, (tm, 1), 0)
    nq = H_C * DK_C
    for c0 in range(0, DN_QKV, DN_CW):
        pext_ref[...] = jnp.dot(h, w_ref[:, c0:c0 + DN_CW], preferred_element_type=F32)
        cw = cw_ref[:, c0:c0 + DN_CW]
        c = (jnp.where(first, 0.0, pext_ref[7:tm + 7, :]) * cw[0:1]
             + pext_ref[8:tm + 8, :] * cw[1:2]
             + jnp.where(last, 0.0, pext_ref[9:tm + 9, :]) * cw[2:3])
        a = jax.nn.silu(c)
        for b0 in range(0, DN_CW, DK_C):
            col = c0 + b0
            blk = a[:, b0:b0 + DK_C]
            if col < 2 * nq:
                blk = blk * lax.rsqrt(jnp.sum(blk * blk, axis=-1, keepdims=True) + NORM_EPS)
            if col < nq:
                q_ref[:, col:col + DK_C] = (blk * (DK_C ** -0.5)).astype(BF16)
            elif col < 2 * nq:
                k_ref[:, col - nq:col - nq + DK_C] = blk.astype(BF16)
            else:
                v_ref[:, col - 2 * nq:col - 2 * nq + DK_C] = blk.astype(BF16)
    hm = h[8:tm + 8]
    for c0 in range(0, DN_OUT, DN_CW):
        z_ref[:, c0:c0 + DN_CW] = jnp.dot(hm, w_ref[:, DN_QKV + c0:DN_QKV + c0 + DN_CW],
                                           preferred_element_type=F32).astype(BF16)
    r = jnp.dot(hm, w_ref[:, DN_QKV + DN_OUT:], preferred_element_type=F32)
    lane = lax.broadcasted_iota(jnp.int32, r.shape, 1)
    xs = r + ab_ref[1:2]
    softplus = jnp.maximum(xs, 0.0) + jnp.log1p(jnp.exp(-jnp.abs(xs)))
    bg_ref[...] = jnp.where(lane < 2 * H_C, jax.nn.sigmoid(r),
                            jnp.where(lane < 4 * H_C, -ab_ref[0:1] * softplus, 0.0))


def _dn_inproj(x, mod, row_of_tile, g, w_in, conv_w, a_log, dt_bias, *, tm, tp, lp, ls):
    t = x.shape[0]
    pad = 128 - 4 * H_C
    w = jnp.pad(w_in, ((0, 0), (0, pad))).astype(BF16)
    ab = jnp.zeros((2, 128), F32)
    ab = ab.at[0, 2 * H_C:4 * H_C].set(jnp.exp(a_log.astype(F32)).reshape(-1))
    ab = ab.at[1, 2 * H_C:4 * H_C].set(dt_bias.astype(F32).reshape(-1))
    prev, nxt = _halo_specs(tm, t)
    tile = lambda i: (i, 0)
    const = lambda i: (0, 0)
    wide = jax.ShapeDtypeStruct((t, DN_OUT), BF16)
    return pl.pallas_call(
        functools.partial(_dn_inproj_kernel, tm=tm, n_tiles_p=tp // tm, tp=tp, lp=lp, ls=ls),
        out_shape=(wide, wide, wide, wide, jax.ShapeDtypeStruct((t, 128), F32)),
        grid=(t // tm,),
        in_specs=[prev, pl.BlockSpec((tm, D_MODEL), tile), nxt,
                  pl.BlockSpec((1, 9, D_MODEL), lambda i: (row_of_tile(i), 0, 0)),
                  pl.BlockSpec((1, D_MODEL), const),
                  pl.BlockSpec(w.shape, const),
                  pl.BlockSpec(conv_w.shape, const),
                  pl.BlockSpec((2, 128), const)],
        out_specs=(pl.BlockSpec((tm, DN_OUT), tile),) * 4 + (pl.BlockSpec((tm, 128), tile),),
        scratch_shapes=[pltpu.VMEM((tm + 16, DN_CW), F32)],
        compiler_params=pltpu.CompilerParams(dimension_semantics=("arbitrary",),
                                             vmem_limit_bytes=VMEM_LIMIT),
        name="dn_inproj",
    )(x, x, x, mod, g.reshape(1, D_MODEL), w, conv_w.astype(F32), ab)


DN_GROUP = 4


def _dn_masks(rev):
    n = DN_CHUNK
    r_i = lax.broadcasted_iota(jnp.int32, (n, n), 0)
    c_i = lax.broadcasted_iota(jnp.int32, (n, n), 1)
    tri = jnp.where((r_i <= c_i) if rev else (r_i >= c_i), 1.0, 0.0).astype(BF16)
    rows = DN_GROUP * n
    rr = lax.broadcasted_iota(jnp.int32, (rows, rows), 0)
    cc = lax.broadcasted_iota(jnp.int32, (rows, rows), 1)
    same = (rr // n) == (cc // n)
    incl = same & ((rr <= cc) if rev else (rr >= cc))
    strict = same & ((rr < cc) if rev else (rr > cc))
    eye = jnp.where(rr == cc, 1.0, 0.0)
    return tri, incl, strict, eye


def _dn_prepare(qc, kc, vc, bgc, masks, rev):
    n = DN_CHUNK
    tri, incl, strict, eye = masks
    gct = sum(jnp.dot(tri, p, preferred_element_type=F32) for p in _split3(bgc))
    rows = DN_GROUP * n
    lane = lax.broadcasted_iota(jnp.int32, (rows, 128), 1)
    last = 0 if rev else n - 1
    prep = []
    for gi in range(H_C // DN_GROUP):
        heads = range(gi * DN_GROUP, (gi + 1) * DN_GROUP)
        sb = [(H_C if rev else 0) + h for h in heads]
        sg = [2 * H_C + (H_C if rev else 0) + h for h in heads]
        beta = jnp.concatenate([bgc[:, s:s + 1] for s in sb], axis=0)
        gc = jnp.concatenate([gct[:, s:s + 1] for s in sg], axis=0)
        gl = jnp.concatenate([jnp.broadcast_to(gct[last:last + 1, s:s + 1], (n, 1)) for s in sg], axis=0)
        q4 = jnp.concatenate([qc[:, h * DK_C:(h + 1) * DK_C] for h in heads], axis=0)
        k4 = jnp.concatenate([kc[:, h * DK_C:(h + 1) * DK_C] for h in heads], axis=0)
        v4 = jnp.concatenate([vc[:, h * DV_C:(h + 1) * DV_C] for h in heads], axis=0)
        hi, mid, lo = (p.astype(F32) for p in _split3(gc))
        one = jnp.where(lane < 6, 1.0, 0.0)
        u_m = jnp.where(lane == 0, hi, jnp.where(lane == 1, mid, jnp.where(lane == 2, lo, one)))
        v_m = jnp.where(lane == 3, -hi, jnp.where(lane == 4, -mid, jnp.where(lane == 5, -lo, one)))
        gd = _bdot_nt(jnp.where(lane < 6, u_m, 0.0), jnp.where(lane < 6, v_m, 0.0))
        decay = jnp.where(incl, jnp.exp(jnp.where(incl, gd, 0.0)), 0.0)
        k4f = k4.astype(F32)
        kb = k4f * beta
        lm = jnp.where(strict, _bdot_nt(kb, k4) * decay, 0.0)
        x = jnp.concatenate([v4.astype(F32) * beta, kb * jnp.exp(gc)], axis=1)
        qe = q4.astype(F32) * jnp.exp(gc)
        ke = (k4f * jnp.exp(gl - gc)).astype(BF16)
        aqk = jnp.where(incl, _bdot_nt(q4, k4) * decay, 0.0).astype(BF16)
        prep.append(dict(lm=lm, x=x, qe=qe, ke=ke, aqk=aqk, egl=jnp.exp(gl)))
    return prep


def _dn_solve(groups, eye):
    n = DN_CHUNK
    dot = functools.partial(jnp.dot, preferred_element_type=F32)
    for g in groups:
        g["lh"] = g["lm"].astype(BF16)
        g["p"] = -g["lh"]
        g["t"] = eye - g["lm"]
    for _ in range(5):
        for g in groups:
            g["p"] = dot(g["p"], g["p"]).astype(BF16)
        for g in groups:
            g["t"] = g["t"] + dot(g["t"].astype(BF16), g["p"])
    for g in groups:
        g["tb"] = g["t"].astype(BF16)
        g["x0"] = dot(g["tb"], g["x"].astype(BF16))
    for g in groups:
        g["res"] = g["x"] - g["x0"] - dot(g["lh"], g["x0"].astype(BF16))
    out = []
    for g in groups:
        x = g["x0"] + dot(g["tb"], g["res"].astype(BF16))
        u4, w4, qe = x[:, :DV_C], x[:, DV_C:], g["qe"]
        wqe = [jnp.concatenate([w4[j * n:(j + 1) * n], qe[j * n:(j + 1) * n]], axis=0).astype(BF16)
               for j in range(DN_GROUP)]
        out.append((u4, wqe, g["ke"], g["aqk"], g["egl"]))
    return out


def _dn_advance(prep, s_ref):
    n = DN_CHUNK
    outs = []
    for gi, (u4, wqe, ke, aqk, egl) in enumerate(prep):
        vn, qs = [], []
        for j in range(DN_GROUP):
            wq = jnp.dot(wqe[j], s_ref[gi * DN_GROUP + j].astype(BF16), preferred_element_type=F32)
            vn.append(u4[j * n:(j + 1) * n] - wq[:n])
            qs.append(wq[n:])
        o4 = jnp.concatenate(qs, axis=0) + jnp.dot(aqk, jnp.concatenate(vn, axis=0).astype(BF16),
                                                   preferred_element_type=F32)
        for j in range(DN_GROUP):
            h = gi * DN_GROUP + j
            sl = slice(j * n, (j + 1) * n)
            s_ref[h] = s_ref[h] * egl[j * n:j * n + 1] + lax.dot_general(
                ke[sl], vn[j].astype(BF16), (((0,), (0,)), ((), ())), preferred_element_type=F32)
            outs.append(o4[sl])
    return outs


def _dn_scan_kernel(*refs, n_chunks, blocks_per_seq, zero_init, has_prev, write_state):
    refs = list(refs)
    fwd_in, bwd_in = refs[:4], refs[4:8]
    del refs[:8]
    s0f_ref, s0b_ref = (None, None) if zero_init else (refs.pop(0), refs.pop(0))
    if has_prev:
        del refs[:2]
    of_ref, ob_ref = refs.pop(0), refs.pop(0)
    sfo_ref, sbo_ref = (refs.pop(0), refs.pop(0)) if write_state else (None, None)
    sf_ref, sb_ref = refs
    i = pl.program_id(0)
    f_in_seq = lax.rem(i, blocks_per_seq)
    r_in_seq = lax.rem(pl.num_programs(0) - 1 - i, blocks_per_seq)

    @pl.when(f_in_seq == 0)
    def _():
        sf_ref[...] = jnp.zeros_like(sf_ref) if zero_init else s0f_ref[0]

    @pl.when(r_in_seq == blocks_per_seq - 1)
    def _():
        sb_ref[...] = jnp.zeros_like(sb_ref) if zero_init else s0b_ref[0]

    def chunk(io, c):
        rows = slice(c * DN_CHUNK, (c + 1) * DN_CHUNK)
        return [r[rows, :] for r in io]

    mf, mb = _dn_masks(False), _dn_masks(True)
    raw = [_dn_prepare(*chunk(fwd_in, c), mf, False) for c in range(n_chunks)]
    raw += [_dn_prepare(*chunk(bwd_in, c), mb, True) for c in range(n_chunks)]
    n_groups = H_C // DN_GROUP
    solved = _dn_solve([g for r in raw for g in r], mf[3])
    per_chunk = [solved[j * n_groups:(j + 1) * n_groups] for j in range(2 * n_chunks)]
    prep_f, prep_b = per_chunk[:n_chunks], per_chunk[n_chunks:]
    for step in range(n_chunks):
        for o_ref, prep, s_ref, c in ((of_ref, prep_f, sf_ref, step),
                                      (ob_ref, prep_b, sb_ref, n_chunks - 1 - step)):
            for h, o in enumerate(_dn_advance(prep[c], s_ref)):
                o_ref[c * DN_CHUNK:(c + 1) * DN_CHUNK, h * DV_C:(h + 1) * DV_C] = o.astype(BF16)

    if write_state:
        @pl.when(f_in_seq == blocks_per_seq - 1)
        def _():
            sfo_ref[0] = sf_ref[...]

        @pl.when(r_in_seq == 0)
        def _():
            sbo_ref[0] = sb_ref[...]


def _dn_scan(q, k, v, bg, s0_f, s0_b, prev, *, row0, n_seq, seq_len, ct, write_state):
    t = q.shape[0]
    bps = seq_len // ct
    nblk = n_seq * bps
    b0 = row0 // ct
    zero_init = s0_f is None
    fwd = lambda i: (b0 + i, 0)
    bwd = lambda i: (b0 + nblk - 1 - i, 0)
    seq_f = lambda i: (i // bps, 0, 0, 0)
    seq_b = lambda i: ((nblk - 1 - i) // bps, 0, 0, 0)
    in_specs, args = [], []
    for rows in (fwd, bwd):
        in_specs += [pl.BlockSpec((ct, DN_OUT), rows)] * 3 + [pl.BlockSpec((ct, 128), rows)]
        args += [q, k, v, bg]
    state_blk = (1, H_C, DK_C, DV_C)
    if not zero_init:
        in_specs += [pl.BlockSpec(state_blk, seq_f), pl.BlockSpec(state_blk, seq_b)]
        args += [s0_f, s0_b]
    aliases = {}
    if prev is not None:
        aliases = {len(args): 0, len(args) + 1: 1}
        in_specs += [pl.BlockSpec(memory_space=pl.ANY)] * 2
        args += list(prev)
    out_shape = [jax.ShapeDtypeStruct((t, DN_OUT), BF16)] * 2
    out_specs = [pl.BlockSpec((ct, DN_OUT), fwd), pl.BlockSpec((ct, DN_OUT), bwd)]
    if write_state:
        out_shape += [jax.ShapeDtypeStruct((n_seq,) + state_blk[1:], F32)] * 2
        out_specs += [pl.BlockSpec(state_blk, seq_f), pl.BlockSpec(state_blk, seq_b)]
    kern = functools.partial(_dn_scan_kernel, n_chunks=ct // DN_CHUNK, blocks_per_seq=bps,
                             zero_init=zero_init, has_prev=prev is not None, write_state=write_state)
    return pl.pallas_call(
        kern, out_shape=tuple(out_shape), grid=(nblk,), in_specs=in_specs, out_specs=tuple(out_specs),
        scratch_shapes=[pltpu.VMEM((H_C, DK_C, DV_C), F32)] * 2,
        input_output_aliases=aliases,
        compiler_params=pltpu.CompilerParams(dimension_semantics=("arbitrary",),
                                             vmem_limit_bytes=VMEM_LIMIT),
        name="dn_scan",
    )(*args)


def _dn_outproj_kernel(x_ref, of_ref, ob_ref, z_ref, mod_ref, ng_ref, w_ref, y_ref):
    o = of_ref[...].astype(F32) + ob_ref[...].astype(F32)
    parts = []
    for h in range(H_C):
        blk = o[:, h * DV_C:(h + 1) * DV_C]
        parts.append(blk * lax.rsqrt(jnp.mean(blk * blk, axis=-1, keepdims=True) + NORM_EPS))
    y = jnp.concatenate(parts, axis=1) * ng_ref[...] * jax.nn.silu(z_ref[...].astype(F32))
    gate = mod_ref[0][5:6]
    y_ref[...] = x_ref[...] + gate * jnp.dot(y.astype(BF16), w_ref[...], preferred_element_type=F32)


def _dn_outproj(x, o_f, o_b, z, mod, row_of_tile, norm_g, w_out, *, tm):
    t = x.shape[0]
    tile = lambda i: (i, 0)
    const = lambda i: (0, 0)
    return pl.pallas_call(
        _dn_outproj_kernel,
        out_shape=jax.ShapeDtypeStruct((t, D_MODEL), F32),
        grid=(t // tm,),
        in_specs=[pl.BlockSpec((tm, D_MODEL), tile), pl.BlockSpec((tm, DN_OUT), tile),
                  pl.BlockSpec((tm, DN_OUT), tile), pl.BlockSpec((tm, DN_OUT), tile),
                  pl.BlockSpec((1, 9, D_MODEL), lambda i: (row_of_tile(i), 0, 0)),
                  pl.BlockSpec((1, DN_OUT), const), pl.BlockSpec((DN_OUT, D_MODEL), const)],
        out_specs=pl.BlockSpec((tm, D_MODEL), tile),
        compiler_params=pltpu.CompilerParams(dimension_semantics=("arbitrary",),
                                             vmem_limit_bytes=VMEM_LIMIT),
        name="dn_outproj",
    )(x, o_f, o_b, z, mod, jnp.tile(norm_g.astype(F32), H_C).reshape(1, DN_OUT), w_out.astype(BF16))


def _dn_layer(x, mod, row_of_tile, g, w_in, w_out, conv_w, a_log, dt_bias, norm_g, s0_f, s0_b,
              *, tm, ct, tp, lp, ls):
    t = x.shape[0]
    q, k, v, z, bg = _dn_inproj(x, mod, row_of_tile, g, w_in, conv_w, a_log, dt_bias, tm=tm, tp=tp, lp=lp, ls=ls)
    bp, bs = tp // lp, (t - tp) // ls
    o_f, o_b, sf, sb = _dn_scan(q, k, v, bg, None, None, None, row0=0, n_seq=bp, seq_len=lp, ct=ct,
                                write_state=True)
    o_f, o_b = _dn_scan(q, k, v, bg, s0_f.astype(F32), s0_b.astype(F32), (o_f, o_b), row0=tp, n_seq=bs,
                        seq_len=ls, ct=ct, write_state=False)
    return _dn_outproj(x, o_f, o_b, z, mod, row_of_tile, norm_g, w_out, tm=tm), sf, sb


AH_Q = H_A * HD_A
AH_KV = KVH_A * HD_A
AH_U3 = 3 * HY_C


def _group_mean_sq(x, bd):
    sq = x * x
    hi = sq.astype(BF16)
    lo = (sq - hi.astype(F32)).astype(BF16)
    return (jnp.dot(hi, bd, preferred_element_type=F32) + jnp.dot(lo, bd, preferred_element_type=F32)) * (1.0 / HD_A)


def _rope_lanes(x, cos_t, sin_t):
    w = x.shape[1]
    lane = lax.broadcasted_iota(jnp.int32, x.shape, 1)
    low = lax.rem(lane, HD_A) < HD_A // 2
    partner = jnp.where(low, pltpu.roll(x, w - HD_A // 2, 1), pltpu.roll(x, HD_A // 2, 1))
    reps = w // cos_t.shape[1]
    return x * jnp.concatenate([cos_t] * reps, axis=1) + partner * jnp.concatenate([sin_t] * reps, axis=1)


def _rep4(x):
    lane = lax.broadcasted_iota(jnp.int32, x.shape, 1)
    sw = pltpu.roll(x, HD_A, 1)
    a = jnp.where(lane < HD_A, x, sw)
    b = jnp.where(lane < HD_A, sw, x)
    return jnp.concatenate([a, a, b, b], axis=1)


def _ah_inproj_kernel(xp_ref, x_ref, xn_ref, mod_ref, g_ref, w_ref, wut_ref, gains_ref, cos_ref, sin_ref,
                      bdq_ref, cwb_ref, perm_ref, q_ref, kt_ref, vt_ref, kn_ref, vn_ref, u_ref, vb_ref,
                      *, tm, n_tiles_p, tp, lp, ls):
    i = pl.program_id(0)
    is_p = i < n_tiles_p
    xe = jnp.concatenate([xp_ref[...], x_ref[...], xn_ref[...]], axis=0)
    he = _ada_h(xe, g_ref[...], mod_ref[0], 1).astype(BF16)
    hm = he[8:tm + 8]
    p = jnp.dot(hm, w_ref[...], preferred_element_type=F32)
    cos_t, sin_t = cos_ref[...], sin_ref[...]
    q = p[:, :AH_Q]
    q = q * lax.rsqrt(_group_mean_sq(q, bdq_ref[...]) + NORM_EPS) * gains_ref[0:1, :]
    q = jnp.where(is_p, q, _rope_lanes(q, cos_t, sin_t))
    q_ref[...] = (q * (HD_A ** -0.5)).astype(BF16)
    k = p[:, AH_Q:AH_Q + AH_KV]
    k = k * lax.rsqrt(_group_mean_sq(k, bdq_ref[:AH_KV, :AH_KV]) + NORM_EPS) * gains_ref[1:2, :AH_KV]
    kn_ref[...] = k
    k = jnp.where(is_p, k, _rope_lanes(k, cos_t, sin_t))
    kt_ref[...] = _rep4(k).astype(BF16)
    v = p[:, AH_Q + AH_KV:]
    vn_ref[...] = v
    vt_ref[...] = _rep4(v).astype(BF16)
    halo = jnp.concatenate([he[0:8], he[tm + 8:tm + 16], jnp.zeros((112, D_MODEL), BF16)], axis=0)
    nt = (((1,), (1,)), ((), ()))

    def store(r0, c):
        u_ref[r0:r0 + HY_C, :] = c
        if r0 == 2 * HY_C:
            vb_ref[...] = c.astype(BF16)

    @pl.when(is_p)
    def _():
        first, last = _seq_edges(i, tm, n_tiles_p, tp, lp, ls, (1, tm), 1)
        lane = lax.broadcasted_iota(jnp.int32, (1, tm), 1)
        for j in range(AH_U3 // HY_C):
            wu = wut_ref[j * HY_C:(j + 1) * HY_C, :]
            u = lax.dot_general(wu, hm, nt, preferred_element_type=F32)
            uh = lax.dot_general(wu, halo, nt, preferred_element_type=F32)
            left = jnp.where(lane == 0, uh[:, 7:8], pltpu.roll(u, 1, 1))
            right = jnp.where(lane == tm - 1, uh[:, 8:9], pltpu.roll(u, tm - 1, 1))
            cwb = cwb_ref[j * HY_C:(j + 1) * HY_C, :]
            store(j * HY_C, jnp.where(first, 0.0, left) * cwb[:, 0:1] + u * cwb[:, 1:2]
                  + jnp.where(last, 0.0, right) * cwb[:, 2:3] + cwb[:, 3:4])

    @pl.when(jnp.logical_not(is_p))
    def _():
        half = tm // 2
        hs = jnp.dot(perm_ref[...], hm, preferred_element_type=F32).astype(BF16)
        t0 = i * tm - tp
        seq_start = lax.rem(t0, ls) == 0
        seq_end = lax.rem(t0 + tm, ls) == 0
        lane = lax.broadcasted_iota(jnp.int32, (1, half), 1)
        for j in range(AH_U3 // HY_C):
            wu = wut_ref[j * HY_C:(j + 1) * HY_C, :]
            u = lax.dot_general(wu, hs, nt, preferred_element_type=F32)
            uh = lax.dot_general(wu, halo, nt, preferred_element_type=F32)
            ev, od = u[:, :half], u[:, half:]
            prev_tok = jnp.where(seq_start, 0.0, uh[:, 7:8])
            next_tok = jnp.where(seq_end, 0.0, uh[:, 8:9])
            od_before = jnp.where(lane == 0, prev_tok, pltpu.roll(od, 1, 1))
            ev_after = jnp.where(lane == half - 1, next_tok, pltpu.roll(ev, half - 1, 1))
            cwb = cwb_ref[j * HY_C:(j + 1) * HY_C, :]
            w0, w1, w2, b = cwb[:, 0:1], cwb[:, 1:2], cwb[:, 2:3], cwb[:, 3:4]
            store(j * HY_C, jnp.concatenate([od_before * w0 + ev * w1 + od * w2 + b,
                                             ev * w0 + od * w1 + ev_after * w2 + b], axis=1))


def _split_perm(tm):
    tok = jnp.arange(tm)
    pos = jnp.where(tok % 2 == 0, tok // 2, tm // 2 + tok // 2)
    return (jnp.arange(tm)[:, None] == pos[None, :]).astype(BF16)


def _rope_tables(ls):
    rows = ls // GRID_W
    r, col = jnp.meshgrid(jnp.arange(rows), jnp.arange(GRID_W), indexing='ij')
    inv = ROPE_THETA ** (-jnp.arange(0, ROPE_AXIS_DIM, 2, dtype=F32) / ROPE_AXIS_DIM)
    ang = jnp.concatenate([r.reshape(-1, 1).astype(F32) * inv, col.reshape(-1, 1).astype(F32) * inv], axis=-1)
    cos, sin = jnp.cos(ang), jnp.sin(ang)
    return jnp.concatenate([cos, cos] * 2, axis=1), jnp.concatenate([-sin, sin] * 2, axis=1)


def _ah_inproj(x, mod, row_of_tile, g, w_in, q_norm, k_norm, conv_w, conv_b, rope, *, tm, tp, lp, ls):
    t = x.shape[0]
    w = w_in[:, :AH_Q + 2 * AH_KV].astype(BF16)
    wut = w_in[:, AH_Q + 2 * AH_KV:].T.astype(BF16)
    gains = jnp.stack([jnp.tile(q_norm.astype(F32), H_A), jnp.tile(k_norm.astype(F32), H_A)])
    gid = jnp.arange(AH_Q) // HD_A
    bdq = (gid[:, None] == gid[None, :]).astype(BF16)
    cwb = jnp.concatenate([conv_w.astype(F32).T, conv_b.astype(F32)[:, None],
                           jnp.zeros((AH_U3, 4), F32)], axis=1)
    n_tiles_p = tp // tm
    prev, nxt = _halo_specs(tm, t)
    tile = lambda i: (i, 0)
    const = lambda i: (0, 0)
    rope_blk = lambda i: (jnp.where(i < n_tiles_p, 0, lax.rem(i * tm - tp, ls) // tm), 0)
    bf = lambda n: jax.ShapeDtypeStruct((t, n), BF16)
    return pl.pallas_call(
        functools.partial(_ah_inproj_kernel, tm=tm, n_tiles_p=n_tiles_p, tp=tp, lp=lp, ls=ls),
        out_shape=(bf(AH_Q), bf(AH_Q), bf(AH_Q), jax.ShapeDtypeStruct((t, AH_KV), F32),
                   jax.ShapeDtypeStruct((t, AH_KV), F32), jax.ShapeDtypeStruct((AH_U3, t), F32),
                   jax.ShapeDtypeStruct((HY_C, t), BF16)),
        grid=(t // tm,),
        in_specs=[prev, pl.BlockSpec((tm, D_MODEL), tile), nxt,
                  pl.BlockSpec((1, 9, D_MODEL), lambda i: (row_of_tile(i), 0, 0)),
                  pl.BlockSpec((1, D_MODEL), const), pl.BlockSpec(w.shape, const),
                  pl.BlockSpec(wut.shape, const), pl.BlockSpec(gains.shape, const),
                  pl.BlockSpec((tm, 128), rope_blk), pl.BlockSpec((tm, 128), rope_blk),
                  pl.BlockSpec(bdq.shape, const), pl.BlockSpec(cwb.shape, const),
                  pl.BlockSpec((tm, tm), const)],
        out_specs=(pl.BlockSpec((tm, AH_Q), tile),) * 3 + (pl.BlockSpec((tm, AH_KV), tile),) * 2
        + (pl.BlockSpec((AH_U3, tm), lambda i: (0, i)), pl.BlockSpec((HY_C, tm), lambda i: (0, i))),
        compiler_params=pltpu.CompilerParams(dimension_semantics=("arbitrary",),
                                             vmem_limit_bytes=VMEM_LIMIT),
        name="ah_inproj",
    )(x, x, x, mod, g.reshape(1, D_MODEL), w, wut, gains, rope[0], rope[1], bdq, cwb, _split_perm(tm))


def _attn_chains(chains, sink_ref, valid):
    tq = chains[0][0].shape[0]
    lane = lax.broadcasted_iota(jnp.int32, (tq, G_A * HD_A), 1)
    valid4 = None if valid is None else jnp.concatenate([valid] * G_A, axis=0)
    scores, sinks = [], []
    for q, kt, _, g in chains:
        qs = jnp.concatenate([jnp.where(lane // HD_A == j, q, jnp.zeros_like(q)) for j in range(G_A)], axis=0)
        s = lax.dot_general(qs, kt, (((1,), (1,)), ((), ())), preferred_element_type=F32)
        scores.append(s if valid4 is None else jnp.where(valid4, s, NEG_INF))
        sinks.append(jnp.concatenate([jnp.broadcast_to(sink_ref[g * G_A + j:g * G_A + j + 1, 0:1], (tq, 1))
                                      for j in range(G_A)], axis=0))
    maxes = [jnp.maximum(jnp.max(s, axis=-1, keepdims=True), sk) for s, sk in zip(scores, sinks)]
    probs = [jnp.exp(s - m) for s, m in zip(scores, maxes)]
    dens = [jnp.sum(p, axis=-1, keepdims=True) + jnp.exp(sk - m) for p, sk, m in zip(probs, sinks, maxes)]
    pvs = [jnp.dot(p.astype(BF16), c[2], preferred_element_type=F32) / d
           for p, c, d in zip(probs, chains, dens)]
    outs = []
    for pv in pvs:
        out = pv[:tq]
        for j in range(1, G_A):
            out = jnp.where(lane // HD_A == j, pv[j * tq:(j + 1) * tq], out)
        outs.append(out)
    return outs


def _attn_ctx_kernel(q_ref, kt_ref, vt_ref, sink_ref, o_ref, *, lp):
    w = G_A * HD_A
    where = [(slice(s0, s0 + lp), slice(g * w, (g + 1) * w), g)
             for s0 in range(0, q_ref.shape[0], lp) for g in range(KVH_A)]
    outs = _attn_chains([(q_ref[r, c], kt_ref[r, c], vt_ref[r, c], g) for r, c, g in where], sink_ref, None)
    for (r, c, _), o in zip(where, outs):
        o_ref[r, c] = o.astype(BF16)


def _attn_win_kernel(q_ref, kp_ref, km_ref, kn_ref, vp_ref, vm_ref, vn_ref, ck_ref, cv_ref, sink_ref,
                     prev_ref, o_ref, *, tq, ls):
    del prev_ref
    i = pl.program_id(1)
    n_ctx = ck_ref.shape[1]
    w = G_A * HD_A
    q_pos = i * tq + lax.broadcasted_iota(jnp.int32, (tq, n_ctx + tq + 2 * WINDOW), 0)
    col = lax.broadcasted_iota(jnp.int32, (tq, n_ctx + tq + 2 * WINDOW), 1)
    k_pos = i * tq - WINDOW + (col - n_ctx)
    valid = (col < n_ctx) | ((jnp.abs(q_pos - k_pos) <= WINDOW) & (k_pos >= 0) & (k_pos < ls))
    chains = []
    for g in range(KVH_A):
        cols = slice(g * w, (g + 1) * w)
        kt = jnp.concatenate([ck_ref[0, :, cols], kp_ref[:, cols], km_ref[:, cols], kn_ref[:, cols]], axis=0)
        vt = jnp.concatenate([cv_ref[0, :, cols], vp_ref[:, cols], vm_ref[:, cols], vn_ref[:, cols]], axis=0)
        chains.append((q_ref[:, cols], kt, vt, g))
    for g, o in enumerate(_attn_chains(chains, sink_ref, valid)):
        o_ref[:, g * w:(g + 1) * w] = o.astype(BF16)


def _attention(q, kt, vt, ck, cv, sink, *, tq, tp, lp, ls):
    t = q.shape[0]
    bp, bs = tp // lp, (t - tp) // ls
    sink_rows = jnp.broadcast_to(sink.astype(F32)[:, None], (H_A, 128))
    params = pltpu.CompilerParams(dimension_semantics=("arbitrary",), vmem_limit_bytes=VMEM_LIMIT)
    seq = lambda b: (b, 0)
    nsq = math.gcd(CTX_SEQS, bp)
    a = pl.pallas_call(
        functools.partial(_attn_ctx_kernel, lp=lp),
        out_shape=jax.ShapeDtypeStruct((t, AH_Q), BF16), grid=(bp // nsq,),
        in_specs=[pl.BlockSpec((nsq * lp, AH_Q), seq)] * 3 + [pl.BlockSpec((H_A, 128), lambda b: (0, 0))],
        out_specs=pl.BlockSpec((nsq * lp, AH_Q), seq), compiler_params=params, name="attn_ctx",
    )(q, kt, vt, sink_rows)
    nq = ls // tq
    wb = tq // WINDOW
    n128 = t // WINDOW
    main = lambda b, i: ((tp + b * ls) // tq + i, 0)
    prev = lambda b, i: (jnp.maximum((tp + b * ls) // WINDOW + i * wb - 1, 0), 0)
    nxt = lambda b, i: (jnp.minimum((tp + b * ls) // WINDOW + (i + 1) * wb, n128 - 1), 0)
    ctx = lambda b, i: (b, 0, 0)
    kv_specs = [pl.BlockSpec((WINDOW, AH_Q), prev), pl.BlockSpec((tq, AH_Q), main),
                pl.BlockSpec((WINDOW, AH_Q), nxt)]
    n_ctx = ck.shape[1]
    return pl.pallas_call(
        functools.partial(_attn_win_kernel, tq=tq, ls=ls),
        out_shape=jax.ShapeDtypeStruct((t, AH_Q), BF16), grid=(bs, nq),
        in_specs=[pl.BlockSpec((tq, AH_Q), main)] + kv_specs + kv_specs
        + [pl.BlockSpec((1, n_ctx, AH_Q), ctx)] * 2
        + [pl.BlockSpec((H_A, 128), lambda b, i: (0, 0)), pl.BlockSpec(memory_space=pl.ANY)],
        out_specs=pl.BlockSpec((tq, AH_Q), main),
        input_output_aliases={10: 0},
        compiler_params=pltpu.CompilerParams(dimension_semantics=("arbitrary", "arbitrary"),
                                             vmem_limit_bytes=VMEM_LIMIT),
        name="attn_win",
    )(q, kt, kt, kt, vt, vt, vt, ck, cv, sink_rows, a)


DFT_SUB = 64
DFT_HALF = 256


def _dft_gen_kernel(o_ref, cl_ref, sl_ref, *, length):
    a = pl.program_id(0)
    n2 = 4 * length

    def angles(n_vec, shape):
        k = lax.broadcasted_iota(jnp.int32, shape, 1)
        ph = lax.rem(n_vec * (2 * k + 1), n2)
        return ph.astype(F32) * (2.0 * math.pi / n2)

    @pl.when(a == 0)
    def _():
        th = angles(lax.broadcasted_iota(jnp.int32, (DFT_SUB, length), 0), (DFT_SUB, length))
        cl_ref[...] = jnp.cos(th)
        sl_ref[...] = jnp.sin(th)

    th = angles(jnp.full((8, length), a * DFT_SUB, jnp.int32), (8, length))[0:1]
    ch, sh = jnp.cos(th), jnp.sin(th)
    c = ch * cl_ref[...] - sh * sl_ref[...]
    ns = -(sh * cl_ref[...] + ch * sl_ref[...])
    for jt in range(length // DFT_HALF):
        src = slice(jt * DFT_HALF, (jt + 1) * DFT_HALF)
        o_ref[:, 2 * jt * DFT_HALF:(2 * jt + 1) * DFT_HALF] = c[:, src].astype(BF16)
        o_ref[:, (2 * jt + 1) * DFT_HALF:(2 * jt + 2) * DFT_HALF] = ns[:, src].astype(BF16)


def _dft_matrix(length):
    return pl.pallas_call(
        functools.partial(_dft_gen_kernel, length=length),
        out_shape=jax.ShapeDtypeStruct((length, 2 * length), BF16), grid=(length // DFT_SUB,),
        out_specs=pl.BlockSpec((DFT_SUB, 2 * length), lambda a: (a, 0)),
        scratch_shapes=[pltpu.VMEM((DFT_SUB, length), F32)] * 2,
        compiler_params=pltpu.CompilerParams(dimension_semantics=("arbitrary",),
                                             vmem_limit_bytes=VMEM_LIMIT),
        name="dft_gen",
    )()


HYF_RB = 256


def _hy_filter_kernel(w1_ref, w2_ref, w3f_ref, w3b_ref, cols_ref, f_ref, o_ref, a_ref, b_ref, *, length):
    rc, jt = pl.program_id(0), pl.program_id(1)

    @pl.when(jt == 0)
    def _():
        m = lax.broadcasted_iota(jnp.int32, (8, length), 1).astype(F32)[0:1]
        t = m * (1.0 / (length - 1))
        w = m * (2.0 * math.pi / length)
        band = lax.broadcasted_iota(jnp.int32, (HY_BANDS, 1), 0).astype(F32)
        fb = 1e-4 + band * ((HY_BANDS - 1 - 1e-4) / (HY_BANDS - 1))
        z = jnp.concatenate([t, jnp.cos(fb * w), -jnp.sin(fb * w),
                             jnp.zeros((40 - HY_EMB, length), F32)], axis=0)
        cols = cols_ref[...]
        h = jnp.sin(cols[:, 1:2] * (_bdot(w1_ref[...], z) + cols[:, 0:1]))
        h = jnp.sin(cols[:, 3:4] * (_bdot(w2_ref[...], h) + cols[:, 2:3]))
        n_out = 2 * HY_ORDER * HY_C
        row = (rc * HYF_RB + lax.broadcasted_iota(jnp.int32, (HYF_RB, 1), 0)).astype(F32)
        step = (HY_MAX_DECAY - HY_MIN_DECAY) / (n_out - 1)
        d_f = jnp.abs(HY_MIN_DECAY + row * step)
        d_b = jnp.abs(HY_MIN_DECAY + (row + HY_ORDER * HY_C) * step)
        hf = _bdot(w3f_ref[...], h) * jnp.exp(-t * d_f)
        hb = jnp.where(m == 0.0, 0.0, _bdot(w3b_ref[...], h) * jnp.exp(-t * d_b))
        ssq = jnp.sum(hf * hf, axis=-1, keepdims=True) + jnp.sum(hb * hb, axis=-1, keepdims=True)
        scale = lax.rsqrt(ssq + NORM_EPS) * (1.0 / length)
        a_ref[...] = ((hf + hb) * scale).astype(BF16)
        b_ref[...] = ((hf - hb) * scale).astype(BF16)

    o_ref[:, :DFT_HALF] = jnp.dot(a_ref[...], f_ref[:, :DFT_HALF], preferred_element_type=F32)
    o_ref[:, DFT_HALF:] = jnp.dot(b_ref[...], f_ref[:, DFT_HALF:], preferred_element_type=F32)


def _hy_filter(fmat, w1, b1, f1, w2, b2, f2, w3, *, length):
    w1t = jnp.pad(w1.astype(F32).T, ((0, 0), (0, 40 - HY_EMB)))
    w3t = w3.astype(F32).T
    cols = jnp.stack([b1, f1, b2, f2] + [jnp.zeros_like(b1)] * 4, axis=1).astype(F32)
    n_rows = HY_ORDER * HY_C
    const = lambda rc, jt: (0, 0)
    return pl.pallas_call(
        functools.partial(_hy_filter_kernel, length=length),
        out_shape=jax.ShapeDtypeStruct((n_rows, 2 * length), F32),
        grid=(n_rows // HYF_RB, length // DFT_HALF),
        in_specs=[pl.BlockSpec(w1t.shape, const), pl.BlockSpec((w2.shape[1], w2.shape[0]), const),
                  pl.BlockSpec((HYF_RB, w3t.shape[1]), lambda rc, jt: (rc, 0)),
                  pl.BlockSpec((HYF_RB, w3t.shape[1]), lambda rc, jt: (n_rows // HYF_RB + rc, 0)),
                  pl.BlockSpec(cols.shape, const),
                  pl.BlockSpec((length, 2 * DFT_HALF), lambda rc, jt: (0, jt))],
        out_specs=pl.BlockSpec((HYF_RB, 2 * DFT_HALF), lambda rc, jt: (rc, jt)),
        scratch_shapes=[pltpu.VMEM((HYF_RB, length), BF16)] * 2,
        compiler_params=pltpu.CompilerParams(dimension_semantics=("arbitrary", "arbitrary"),
                                             vmem_limit_bytes=VMEM_LIMIT),
        name="hy_filter",
    )(w1t, w2.astype(F32).T, w3t, w3t, cols, fmat)


HY_SEQS = 8


def _hy_fwd_kernel(z_ref, f_ref, k_ref, y_ref, *, nsq, length):
    kr, ki = k_ref[:, :DFT_HALF], k_ref[:, DFT_HALF:]
    for s in range(nsq):
        zt = jnp.dot(z_ref[:, s * length:(s + 1) * length], f_ref[...], preferred_element_type=F32)
        zr, zi = zt[:, :DFT_HALF], zt[:, DFT_HALF:]
        y_ref[s * HY_C:(s + 1) * HY_C, :DFT_HALF] = (zr * kr - zi * ki).astype(BF16)
        y_ref[s * HY_C:(s + 1) * HY_C, DFT_HALF:] = (zr * ki + zi * kr).astype(BF16)


def _hy_fwd(zb, fmat, kspec, order, *, lane0, n_seq, length):
    nsq = math.gcd(HY_SEQS, n_seq) if lane0 == 0 else 1
    sb0 = lane0 // length
    return pl.pallas_call(
        functools.partial(_hy_fwd_kernel, nsq=nsq, length=length),
        out_shape=jax.ShapeDtypeStruct((n_seq * HY_C, 2 * length), BF16),
        grid=(n_seq // nsq, length // DFT_HALF),
        in_specs=[pl.BlockSpec((HY_C, nsq * length), lambda s, jt: (0, sb0 + s)),
                  pl.BlockSpec((length, 2 * DFT_HALF), lambda s, jt: (0, jt)),
                  pl.BlockSpec((HY_C, 2 * DFT_HALF), lambda s, jt: (order, jt))],
        out_specs=pl.BlockSpec((nsq * HY_C, 2 * DFT_HALF), lambda s, jt: (s, jt)),
        compiler_params=pltpu.CompilerParams(dimension_semantics=("arbitrary", "arbitrary"),
                                             vmem_limit_bytes=VMEM_LIMIT),
        name="hy_fwd",
    )(zb, fmat, kspec)


def _hy_inv_kernel(y_ref, f_ref, gate_ref, z_ref, bias_ref, *rest, with_prev, with_bf16, nsq, tt):
    rest = list(rest)
    if with_prev:
        rest.pop(0)
    o_ref = rest.pop(0)
    ob_ref = rest.pop(0) if with_bf16 else None
    for s in range(nsq):
        lanes = slice(s * tt, (s + 1) * tt)
        y = lax.dot_general(y_ref[s * HY_C:(s + 1) * HY_C, :], f_ref[...], (((1,), (1,)), ((), ())),
                            preferred_element_type=F32)
        out = gate_ref[:, lanes] * (y + z_ref[:, lanes] * bias_ref[...])
        o_ref[:, lanes] = out
        if with_bf16:
            ob_ref[:, lanes] = out.astype(BF16)


def _hy_inv(yspec, fmat, gate_arr, gate_rb, z_arr, z_rb, bias, prev, *, lane0, n_seq, length, tt, t_total,
            with_bf16):
    nt = length // tt
    nsq = math.gcd(HY_SEQS, n_seq) if (nt == 1 and lane0 == 0) else 1
    lane_blk = lambda s, j: lane0 // tt + s * nt + j
    in_specs = [pl.BlockSpec((nsq * HY_C, 2 * length), lambda s, j: (s, 0)),
                pl.BlockSpec((tt, 2 * length), lambda s, j: (j, 0)),
                pl.BlockSpec((HY_C, nsq * tt), lambda s, j: (gate_rb, lane_blk(s, j))),
                pl.BlockSpec((HY_C, nsq * tt), lambda s, j: (z_rb, lane_blk(s, j))),
                pl.BlockSpec((HY_C, 1), lambda s, j: (0, 0))]
    args = [yspec, fmat, gate_arr, z_arr, bias.astype(F32).reshape(HY_C, 1)]
    aliases = {}
    if prev is not None:
        in_specs.append(pl.BlockSpec(memory_space=pl.ANY))
        aliases = {len(args): 0}
        args.append(prev)
    out_spec = pl.BlockSpec((HY_C, nsq * tt), lambda s, j: (0, lane_blk(s, j)))
    out_shape = [jax.ShapeDtypeStruct((HY_C, t_total), F32)]
    out_specs = [out_spec]
    if with_bf16:
        out_shape.append(jax.ShapeDtypeStruct((HY_C, t_total), BF16))
        out_specs.append(out_spec)
    res = pl.pallas_call(
        functools.partial(_hy_inv_kernel, with_prev=prev is not None, with_bf16=with_bf16, nsq=nsq, tt=tt),
        out_shape=tuple(out_shape), grid=(n_seq // nsq, nt), in_specs=in_specs, out_specs=tuple(out_specs),
        input_output_aliases=aliases,
        compiler_params=pltpu.CompilerParams(dimension_semantics=("arbitrary", "arbitrary"),
                                             vmem_limit_bytes=VMEM_LIMIT),
        name="hy_inv",
    )(*args)
    return res


def _hyena_group(ut, vb, fmat, kspec, hy_bias, y_prev, *, lane0, n_seq, length, t_total):
    tt = min(512, length)
    geo = dict(lane0=lane0, n_seq=n_seq, length=length)
    y1 = _hy_fwd(vb, fmat, kspec, 0, **geo)
    z2, z2b = _hy_inv(y1, fmat, ut, 0, ut, 2, hy_bias[0], None, tt=tt, t_total=t_total, with_bf16=True, **geo)
    y2 = _hy_fwd(z2b, fmat, kspec, 1, **geo)
    return _hy_inv(y2, fmat, ut, 1, z2, 0, hy_bias[1], y_prev, tt=tt, t_total=t_total, with_bf16=False,
                   **geo)[0]


HY_TILE = TOKEN_TILE


def _twiddle(jt, n):
    k0 = (jt * DFT_HALF + lax.broadcasted_iota(jnp.int32, (8, DFT_HALF), 1)).astype(F32)[0:1]
    ang = (k0 + 0.5) * (2.0 * math.pi / n)
    return jnp.cos(ang), -jnp.sin(ang)


def _cmul(ar, ai, br, bi):
    return ar * br - ai * bi, ar * bi + ai * br


def _hy_filter2_kernel(w1_ref, w2_ref, w3f_ref, w3b_ref, cols_ref, f_ref, ka_ref, kb_ref, *scr, length):
    rc, jt = pl.program_id(0), pl.program_id(1)
    half = length // 2

    @pl.when(jt == 0)
    def _():
        cols = cols_ref[...]
        n_out = 2 * HY_ORDER * HY_C
        row = (rc * HYF_RB + lax.broadcasted_iota(jnp.int32, (HYF_RB, 1), 0)).astype(F32)
        step = (HY_MAX_DECAY - HY_MIN_DECAY) / (n_out - 1)
        d_f = jnp.abs(HY_MIN_DECAY + row * step)
        d_b = jnp.abs(HY_MIN_DECAY + (row + HY_ORDER * HY_C) * step)
        band = lax.broadcasted_iota(jnp.int32, (HY_BANDS, 1), 0).astype(F32)
        fb = 1e-4 + band * ((HY_BANDS - 1 - 1e-4) / (HY_BANDS - 1))
        taps = []
        for parity in range(2):
            m = 2.0 * lax.broadcasted_iota(jnp.int32, (8, half), 1).astype(F32)[0:1] + parity
            t = m * (1.0 / (length - 1))
            w = m * (2.0 * math.pi / length)
            z = jnp.concatenate([t, jnp.cos(fb * w), -jnp.sin(fb * w),
                                 jnp.zeros((40 - HY_EMB, half), F32)], axis=0)
            h = jnp.sin(cols[:, 1:2] * (_bdot(w1_ref[...], z) + cols[:, 0:1]))
            h = jnp.sin(cols[:, 3:4] * (_bdot(w2_ref[...], h) + cols[:, 2:3]))
            hf = _bdot(w3f_ref[...], h) * jnp.exp(-t * d_f)
            hb = jnp.where(m == 0.0, 0.0, _bdot(w3b_ref[...], h) * jnp.exp(-t * d_b))
            taps.append((hf, hb))
        ssq = sum(jnp.sum(hf * hf, axis=-1, keepdims=True) + jnp.sum(hb * hb, axis=-1, keepdims=True)
                  for hf, hb in taps)
        scale = lax.rsqrt(ssq + NORM_EPS) * (1.0 / length)
        (hf, hb), (hfo, hbo) = taps
        scr[0][...] = ((hf + hb) * scale).astype(BF16)
        scr[1][...] = ((hf - hb) * scale).astype(BF16)
        scr[2][...] = (hfo * scale).astype(BF16)
        scr[3][...] = (hbo * scale).astype(BF16)

    dot = functools.partial(jnp.dot, preferred_element_type=F32)
    fc, fs = f_ref[:, :DFT_HALF], f_ref[:, DFT_HALF:]
    er, ei = dot(scr[0][...], fc), dot(scr[1][...], fs)
    tr, ti = _twiddle(jt, 2 * length)
    fr, fi = _cmul(tr, ti, dot(scr[2][...], fc), dot(scr[2][...], fs))
    gr, gi = _cmul(tr, -ti, dot(scr[3][...], fc), -dot(scr[3][...], fs))
    ka_ref[:, :DFT_HALF] = er + (fr + gr)
    ka_ref[:, DFT_HALF:] = ei + (fi + gi)
    kb_ref[:, :DFT_HALF] = er - (fr + gr)
    kb_ref[:, DFT_HALF:] = ei - (fi + gi)


def _hy_filter2(fhalf, w1, b1, f1, w2, b2, f2, w3, *, length):
    w1t = jnp.pad(w1.astype(F32).T, ((0, 0), (0, 40 - HY_EMB)))
    w3t = w3.astype(F32).T
    cols = jnp.stack([b1, f1, b2, f2] + [jnp.zeros_like(b1)] * 4, axis=1).astype(F32)
    n_rows = HY_ORDER * HY_C
    half = length // 2
    const = lambda rc, jt: (0, 0)
    out = jax.ShapeDtypeStruct((n_rows, length), F32)
    return pl.pallas_call(
        functools.partial(_hy_filter2_kernel, length=length),
        out_shape=(out, out), grid=(n_rows // HYF_RB, half // DFT_HALF),
        in_specs=[pl.BlockSpec(w1t.shape, const), pl.BlockSpec((w2.shape[1], w2.shape[0]), const),
                  pl.BlockSpec((HYF_RB, w3t.shape[1]), lambda rc, jt: (rc, 0)),
                  pl.BlockSpec((HYF_RB, w3t.shape[1]), lambda rc, jt: (n_rows // HYF_RB + rc, 0)),
                  pl.BlockSpec(cols.shape, const),
                  pl.BlockSpec((half, 2 * DFT_HALF), lambda rc, jt: (0, jt))],
        out_specs=(pl.BlockSpec((HYF_RB, 2 * DFT_HALF), lambda rc, jt: (rc, jt)),) * 2,
        scratch_shapes=[pltpu.VMEM((HYF_RB, half), BF16)] * 4,
        compiler_params=pltpu.CompilerParams(dimension_semantics=("arbitrary", "arbitrary"),
                                             vmem_limit_bytes=VMEM_LIMIT),
        name="hy_filter2",
    )(w1t, w2.astype(F32).T, w3t, w3t, cols, fhalf)


def _hy_fwd2_kernel(z_ref, f_ref, ka_ref, kb_ref, p_ref, q_ref, eo_ref, *, length):
    jt = pl.program_id(1)
    half_tile = HY_TILE // 2

    @pl.when(jt == 0)
    def _():
        for j in range(length // HY_TILE):
            dst = slice(j * half_tile, (j + 1) * half_tile)
            eo_ref[:HY_C, dst] = z_ref[:, j * HY_TILE:j * HY_TILE + half_tile]
            eo_ref[HY_C:, dst] = z_ref[:, j * HY_TILE + half_tile:(j + 1) * HY_TILE]

    r = jnp.dot(eo_ref[...], f_ref[...], preferred_element_type=F32)
    er, ei = r[:HY_C, :DFT_HALF], r[:HY_C, DFT_HALF:]
    tr, ti = _twiddle(jt, 2 * length)
    pr, pi = _cmul(tr, ti, r[HY_C:, :DFT_HALF], r[HY_C:, DFT_HALF:])
    yar, yai = _cmul(er + pr, ei + pi, ka_ref[:, :DFT_HALF], ka_ref[:, DFT_HALF:])
    ybr, ybi = _cmul(er - pr, ei - pi, kb_ref[:, :DFT_HALF], kb_ref[:, DFT_HALF:])
    p_ref[:, :DFT_HALF] = (yar + ybr).astype(BF16)
    p_ref[:, DFT_HALF:] = (yai + ybi).astype(BF16)
    qr, qi = _cmul(yar - ybr, yai - ybi, tr, -ti)
    q_ref[:, :DFT_HALF] = qr.astype(BF16)
    q_ref[:, DFT_HALF:] = qi.astype(BF16)


def _hy_fwd2(zb, fhalf, kspec, order, *, lane0, n_seq, length):
    sb0 = lane0 // length
    half = length // 2
    out = jax.ShapeDtypeStruct((n_seq * HY_C, length), BF16)
    tile = pl.BlockSpec((HY_C, 2 * DFT_HALF), lambda s, jt: (s, jt))
    kblk = pl.BlockSpec((HY_C, 2 * DFT_HALF), lambda s, jt: (order, jt))
    return pl.pallas_call(
        functools.partial(_hy_fwd2_kernel, length=length),
        out_shape=(out, out), grid=(n_seq, half // DFT_HALF),
        in_specs=[pl.BlockSpec((HY_C, length), lambda s, jt: (0, sb0 + s)),
                  pl.BlockSpec((half, 2 * DFT_HALF), lambda s, jt: (0, jt)), kblk, kblk],
        out_specs=(tile, tile),
        scratch_shapes=[pltpu.VMEM((2 * HY_C, half), BF16)],
        compiler_params=pltpu.CompilerParams(dimension_semantics=("arbitrary", "arbitrary"),
                                             vmem_limit_bytes=VMEM_LIMIT),
        name="hy_fwd2",
    )(zb, fhalf, kspec[0], kspec[1])


def _hy_inv2_kernel(p_ref, q_ref, f_ref, gate_ref, z_ref, bias_ref, *rest, with_prev, with_bf16):
    rest = list(rest)
    if with_prev:
        rest.pop(0)
    o_ref = rest.pop(0)
    pq = jnp.concatenate([p_ref[...], q_ref[...]], axis=0)
    y = lax.dot_general(pq, f_ref[...], (((1,), (1,)), ((), ())), preferred_element_type=F32)
    y = jnp.concatenate([y[:HY_C], y[HY_C:]], axis=1)
    out = gate_ref[...] * (y + z_ref[...] * bias_ref[...])
    o_ref[...] = out
    if with_bf16:
        rest.pop(0)[...] = out.astype(BF16)


def _hy_inv2(pq, fhalf, gate_arr, gate_rb, z_arr, z_rb, bias, prev, *, lane0, n_seq, length, t_total,
             with_bf16):
    nt = length // HY_TILE
    lane_blk = lambda s, j: lane0 // HY_TILE + s * nt + j
    spec = pl.BlockSpec((HY_C, length), lambda s, j: (s, 0))
    in_specs = [spec, spec, pl.BlockSpec((HY_TILE // 2, length), lambda s, j: (j, 0)),
                pl.BlockSpec((HY_C, HY_TILE), lambda s, j: (gate_rb, lane_blk(s, j))),
                pl.BlockSpec((HY_C, HY_TILE), lambda s, j: (z_rb, lane_blk(s, j))),
                pl.BlockSpec((HY_C, 1), lambda s, j: (0, 0))]
    args = [pq[0], pq[1], fhalf, gate_arr, z_arr, bias.astype(F32).reshape(HY_C, 1)]
    aliases = {}
    if prev is not None:
        in_specs.append(pl.BlockSpec(memory_space=pl.ANY))
        aliases = {len(args): 0}
        args.append(prev)
    out_spec = pl.BlockSpec((HY_C, HY_TILE), lambda s, j: (0, lane_blk(s, j)))
    out_shape = [jax.ShapeDtypeStruct((HY_C, t_total), F32)]
    out_specs = [out_spec]
    if with_bf16:
        out_shape.append(jax.ShapeDtypeStruct((HY_C, t_total), BF16))
        out_specs.append(out_spec)
    return pl.pallas_call(
        functools.partial(_hy_inv2_kernel, with_prev=prev is not None, with_bf16=with_bf16),
        out_shape=tuple(out_shape), grid=(n_seq, nt), in_specs=in_specs, out_specs=tuple(out_specs),
        input_output_aliases=aliases,
        compiler_params=pltpu.CompilerParams(dimension_semantics=("arbitrary", "arbitrary"),
                                             vmem_limit_bytes=VMEM_LIMIT),
        name="hy_inv2",
    )(*args)


def _hyena_group2(ut, vb, fhalf, kspec, hy_bias, y_prev, *, lane0, n_seq, length, t_total):
    geo = dict(lane0=lane0, n_seq=n_seq, length=length)
    pq = _hy_fwd2(vb, fhalf, kspec, 0, **geo)
    z2, z2b = _hy_inv2(pq, fhalf, ut, 0, ut, 2, hy_bias[0], None, t_total=t_total, with_bf16=True, **geo)
    pq = _hy_fwd2(z2b, fhalf, kspec, 1, **geo)
    return _hy_inv2(pq, fhalf, ut, 1, z2, 0, hy_bias[1], y_prev, t_total=t_total, with_bf16=False, **geo)[0]


def _ah_outproj_kernel(x_ref, a_ref, yt_ref, mod_ref, wa_ref, wy_ref, perm_ref, o_ref, *, n_tiles_p):
    o = jnp.dot(a_ref[...], wa_ref[...], preferred_element_type=F32)
    yb = yt_ref[...].astype(BF16)
    nat = jnp.dot(yb, perm_ref[...], preferred_element_type=F32).astype(BF16)
    yb = jnp.where(pl.program_id(0) < n_tiles_p, yb, nat)
    o = o + lax.dot_general(yb, wy_ref[...], (((0,), (0,)), ((), ())), preferred_element_type=F32)
    o_ref[...] = x_ref[...] + mod_ref[0][5:6] * o


def _ah_outproj(x, a, yt, mod, row_of_tile, w_out, *, tm, tp):
    t = x.shape[0]
    tile = lambda i: (i, 0)
    const = lambda i: (0, 0)
    wa, wy = w_out[:AH_Q].astype(BF16), w_out[AH_Q:].astype(BF16)
    perm = _split_perm(tm)
    return pl.pallas_call(
        functools.partial(_ah_outproj_kernel, n_tiles_p=tp // tm),
        out_shape=jax.ShapeDtypeStruct((t, D_MODEL), F32), grid=(t // tm,),
        in_specs=[pl.BlockSpec((tm, D_MODEL), tile), pl.BlockSpec((tm, AH_Q), tile),
                  pl.BlockSpec((HY_C, tm), lambda i: (0, i)),
                  pl.BlockSpec((1, 9, D_MODEL), lambda i: (row_of_tile(i), 0, 0)),
                  pl.BlockSpec(wa.shape, const), pl.BlockSpec(wy.shape, const),
                  pl.BlockSpec((tm, tm), const)],
        out_specs=pl.BlockSpec((tm, D_MODEL), tile),
        compiler_params=pltpu.CompilerParams(dimension_semantics=("arbitrary",),
                                             vmem_limit_bytes=VMEM_LIMIT),
        name="ah_outproj",
    )(x, a, yt, mod, wa, wy, perm)


def _ah_layer(x, mod, row_of_tile, g, w_in, w_out, q_norm, k_norm, sink, conv_w, conv_b, hy_bias,
              ck, cv, rope, fmats, kspecs, *, tm, tq, tp, lp, ls):
    t = x.shape[0]
    q, kt, vt, kn, vn, ut, vb = _ah_inproj(x, mod, row_of_tile, g, w_in, q_norm, k_norm, conv_w, conv_b, rope,
                                           tm=tm, tp=tp, lp=lp, ls=ls)
    a = _attention(q, kt, vt, ck, cv, sink, tq=tq, tp=tp, lp=lp, ls=ls)
    yt = _hyena_group(ut, vb, fmats[0], kspecs[0], hy_bias, None, lane0=0, n_seq=tp // lp, length=lp, t_total=t)
    yt = _hyena_group2(ut, vb, fmats[1], kspecs[1], hy_bias, yt, lane0=tp, n_seq=(t - tp) // ls, length=ls,
                       t_total=t)
    return _ah_outproj(x, a, yt, mod, row_of_tile, w_out, tm=tm, tp=tp), kn[:tp], vn[:tp]


def _rep4_ctx(c):
    b, s = c.shape[:2]
    return jnp.broadcast_to(c[:, :, :, None, :], (b, s, KVH_A, G_A, HD_A)).reshape(b, s, AH_Q).astype(BF16)


def _ada_kernel(c_ref, w_ref, b_ref, o_ref):
    s = jax.nn.silu(c_ref[...]).astype(BF16)
    o_ref[0] = jnp.dot(s, w_ref[0].astype(BF16), preferred_element_type=F32) + b_ref[0]


def _ada_mod(cond, ada_w, ada_b):
    depth, _, n = ada_w.shape
    rows = 16
    cp = jnp.pad(cond.astype(F32), ((0, rows - cond.shape[0]), (0, 0)))
    out = pl.pallas_call(
        _ada_kernel, out_shape=jax.ShapeDtypeStruct((depth, rows, n), F32),
        grid=(depth, n // D_MODEL),
        in_specs=[pl.BlockSpec((rows, D_MODEL), lambda l, j: (0, 0)),
                  pl.BlockSpec((1, D_MODEL, D_MODEL), lambda l, j: (l, 0, j)),
                  pl.BlockSpec((1, 1, D_MODEL), lambda l, j: (l, 0, j))],
        out_specs=pl.BlockSpec((1, rows, D_MODEL), lambda l, j: (l, 0, j)),
        compiler_params=pltpu.CompilerParams(dimension_semantics=("arbitrary", "arbitrary"),
                                             vmem_limit_bytes=VMEM_LIMIT),
        name="ada_mod",
    )(cp, ada_w, ada_b.reshape(depth, 1, n))
    return out.reshape(depth, rows, n // D_MODEL, D_MODEL)


def kernel(x_prompt, x_sample, cache_k, cache_v, state_fwd, state_bwd, c, c_ctx, norm_g, ada_w, ada_b, ffn_w13, ffn_w2, mx_w_in, mx_w_out, q_norm, k_norm, attn_sink, hy_conv_w, hy_conv_b, hy_w1, hy_b1, hy_freq1, hy_w2, hy_b2, hy_freq2, hy_w3, hy_bias, dn_w_in, dn_w_out, dn_conv_w, dn_a_log, dn_dt_bias, dn_norm_g):
    bp, lp, _ = x_prompt.shape
    bs, ls, _ = x_sample.shape
    tp, ts = bp * lp, bs * ls
    assert tp % ls == 0
    tm = math.gcd(TOKEN_TILE, math.gcd(tp, ls))
    ct = math.gcd(DN_BLOCK, math.gcd(lp, ls))
    tq = math.gcd(ATTN_TQ, ls)
    x = jnp.concatenate([x_prompt.reshape(tp, D_MODEL), x_sample.reshape(ts, D_MODEL)], axis=0)
    cond = jnp.concatenate([c, c_ctx[None, :]], axis=0)
    mods = _ada_mod(cond, ada_w, ada_b)
    tiles_p, tiles_per_s = tp // tm, ls // tm

    def row_of_tile(i):
        return jnp.where(i < tiles_p, bs, (i - tiles_p) // tiles_per_s)

    rope = _rope_tables(ls)
    assert tm == HY_TILE and ls % tm == 0 and (ls // 2) % DFT_HALF == 0
    fmats = (_dft_matrix(lp), _dft_matrix(ls // 2))
    new_k, new_v, new_sf, new_sb = [], [], [], []
    for layer in range(DEPTH):
        mod = mods[layer]
        i = layer // 2
        x = _ffn(x, mod, row_of_tile, norm_g[layer, 0], ffn_w13[layer, 0], ffn_w2[layer, 0], 0, tm=tm)
        if layer % 2 == 0:
            kspecs = tuple(fn(f, hy_w1[i], hy_b1[i], hy_freq1[i], hy_w2[i], hy_b2[i], hy_freq2[i], hy_w3[i],
                              length=n) for fn, f, n in zip((_hy_filter, _hy_filter2), fmats, (lp, ls)))
            x, k_p, v_p = _ah_layer(x, mod, row_of_tile, norm_g[layer, 1], mx_w_in[i], mx_w_out[i], q_norm[i],
                                    k_norm[i], attn_sink[i], hy_conv_w[i], hy_conv_b[i], hy_bias[i],
                                    _rep4_ctx(cache_k[:, i]), _rep4_ctx(cache_v[:, i]), rope, fmats, kspecs,
                                    tm=tm, tq=tq, tp=tp, lp=lp, ls=ls)
            new_k.append(k_p.reshape(bp, lp, KVH_A, HD_A))
            new_v.append(v_p.reshape(bp, lp, KVH_A, HD_A))
        else:
            x, s_f, s_b = _dn_layer(x, mod, row_of_tile, norm_g[layer, 1], dn_w_in[i], dn_w_out[i],
                                    dn_conv_w[i], dn_a_log[i], dn_dt_bias[i], dn_norm_g[i],
                                    state_fwd[:, i], state_bwd[:, i], tm=tm, ct=ct, tp=tp, lp=lp, ls=ls)
            new_sf.append(s_f)
            new_sb.append(s_b)
        x = _ffn(x, mod, row_of_tile, norm_g[layer, 2], ffn_w13[layer, 1], ffn_w2[layer, 1], 2, tm=tm)

    return (x[:tp].reshape(bp, lp, D_MODEL), x[tp:].reshape(bs, ls, D_MODEL),
            jnp.stack(new_k, axis=1), jnp.stack(new_v, axis=1),
            jnp.stack(new_sf, axis=1), jnp.stack(new_sb, axis=1))
```

```python
import functools
import math

import jax
import jax.numpy as jnp
from jax import lax
from jax.experimental import pallas as pl
from jax.experimental.pallas import tpu as pltpu

D_MODEL = 1024
DEPTH = 4
GRID_W = 64
H_A = 8
KVH_A = 2
G_A = H_A // KVH_A
HD_A = 64
WINDOW = 128
ATTN_BLOCK = 128
ROPE_THETA = 10000.0
ROPE_AXIS_DIM = HD_A // 2
HY_C = 512
HY_ORDER = 2
HY_EMB = 33
HY_BANDS = (HY_EMB - 1) // 2
HY_MIN_DECAY = math.log(1e-2) / 1.5
HY_MAX_DECAY = math.log(1e-2) / 0.3
H_C = 8
DK_C = 128
DV_C = 128
DN_CHUNK = 64
DN_OUT = H_C * DV_C
D_FF = 2816
NORM_EPS = 1e-6
NEG_INF = -1e30

F32 = jnp.float32
BF16 = jnp.bfloat16

TOKEN_TILE = 512
FFN_CHUNK = 256
DN_BLOCK = 128
ATTN_TQ = 256
CTX_SEQS = 4
VMEM_LIMIT = 56 * 1024 * 1024


def _ffn_kernel(x_ref, mod_ref, g_ref, w1_ref, w3_ref, w2_ref, *rest, j, n_chunks, has_prev):
    o_ref, acc_ref = rest[1:] if has_prev else rest
    x = x_ref[...]
    y = x * lax.rsqrt(jnp.mean(x * x, axis=-1, keepdims=True) + NORM_EPS) * g_ref[...]
    m = mod_ref[0]
    shift, scale, gate = m[3 * j:3 * j + 1], m[3 * j + 1:3 * j + 2], m[3 * j + 2:3 * j + 3]
    h = (y * (1.0 + scale) + shift).astype(BF16)
    acc_ref[...] = jnp.zeros_like(acc_ref)

    def body(c, carry):
        gt = jnp.dot(h, w1_ref[c], preferred_element_type=F32)
        up = jnp.dot(h, w3_ref[c], preferred_element_type=F32)
        a = (jax.nn.silu(gt) * up).astype(BF16)
        acc_ref[...] += jnp.dot(a, w2_ref[c], preferred_element_type=F32)
        return carry

    lax.fori_loop(0, n_chunks, body, 0, unroll=True)
    o_ref[...] = x + 0.5 * gate * acc_ref[...]


def _ffn_weights(w13, w2):
    n_chunks = D_FF // FFN_CHUNK
    w1 = w13[:, :D_FF].astype(BF16).reshape(D_MODEL, n_chunks, FFN_CHUNK).transpose(1, 0, 2)
    w3 = w13[:, D_FF:].astype(BF16).reshape(D_MODEL, n_chunks, FFN_CHUNK).transpose(1, 0, 2)
    return w1, w3, w2.astype(BF16).reshape(n_chunks, FFN_CHUNK, D_MODEL)


def _ffn(x, mod, row_of_tile, g, weights, j, *, tm, in_tile0=0, n_tiles=None, out_rows=None, out_tile0=0,
         prev=None):
    n_tiles = x.shape[0] // tm if n_tiles is None else n_tiles
    out_rows = x.shape[0] if out_rows is None else out_rows
    n_chunks = D_FF // FFN_CHUNK
    joint0 = max(in_tile0, out_tile0)
    const3 = lambda i: (0, 0, 0)
    in_specs = [
        pl.BlockSpec((tm, D_MODEL), lambda i: (in_tile0 + i, 0)),
        pl.BlockSpec((1, 9, D_MODEL), lambda i: (row_of_tile(joint0 + i), 0, 0)),
        pl.BlockSpec((1, D_MODEL), lambda i: (0, 0)),
        pl.BlockSpec((n_chunks, D_MODEL, FFN_CHUNK), const3),
        pl.BlockSpec((n_chunks, D_MODEL, FFN_CHUNK), const3),
        pl.BlockSpec((n_chunks, FFN_CHUNK, D_MODEL), const3),
    ]
    args = [x, mod, g.reshape(1, D_MODEL), *weights]
    aliases = {}
    if prev is not None:
        in_specs.append(pl.BlockSpec(memory_space=pl.ANY))
        aliases = {len(args): 0}
        args.append(prev)
    return pl.pallas_call(
        functools.partial(_ffn_kernel, j=j, n_chunks=n_chunks, has_prev=prev is not None),
        out_shape=jax.ShapeDtypeStruct((out_rows, D_MODEL), F32),
        grid=(n_tiles,), in_specs=in_specs,
        out_specs=pl.BlockSpec((tm, D_MODEL), lambda i: (out_tile0 + i, 0)),
        scratch_shapes=[pltpu.VMEM((tm, D_MODEL), F32)],
        input_output_aliases=aliases,
        compiler_params=pltpu.CompilerParams(dimension_semantics=("arbitrary",),
                                             vmem_limit_bytes=VMEM_LIMIT),
        name=f"ffn{j}",
    )(*args)


def _bdot(a, b):
    return jnp.dot(a.astype(BF16), b.astype(BF16), preferred_element_type=F32)


def _bdot_nt(a, b):
    return lax.dot_general(a.astype(BF16), b.astype(BF16), (((1,), (1,)), ((), ())),
                           preferred_element_type=F32)


def _bdot_tn(a, b):
    return lax.dot_general(a.astype(BF16), b.astype(BF16), (((0,), (0,)), ((), ())),
                           preferred_element_type=F32)


def _split3(x):
    hi = x.astype(BF16)
    r1 = x - hi.astype(F32)
    mid = r1.astype(BF16)
    lo = (r1 - mid.astype(F32)).astype(BF16)
    return hi, mid, lo


def _ada_h(x, g_row, m, j):
    y = x * lax.rsqrt(jnp.mean(x * x, axis=-1, keepdims=True) + NORM_EPS) * g_row
    return y * (1.0 + m[3 * j + 1:3 * j + 2]) + m[3 * j:3 * j + 1]


def _seq_edges(i, tm, n_tiles_p, tp, lp, ls, shape, axis):
    is_p = i < n_tiles_p
    seq_len = jnp.where(is_p, lp, ls)
    t0 = i * tm - jnp.where(is_p, 0, tp)
    base = lax.rem(t0, seq_len)
    pos = (base + lax.broadcasted_iota(jnp.int32, shape, axis)).astype(F32)
    lf = seq_len.astype(F32)
    rem = pos - jnp.floor((pos + 0.5) / lf) * lf
    return rem == 0.0, rem == lf - 1.0


def _halo_specs(tm, t):
    nb = t // 8
    prev = pl.BlockSpec((8, D_MODEL), lambda i: (jnp.maximum(i * (tm // 8) - 1, 0), 0))
    nxt = pl.BlockSpec((8, D_MODEL), lambda i: (jnp.minimum((i + 1) * (tm // 8), nb - 1), 0))
    return prev, nxt


DN_QKV = 2 * H_C * DK_C + H_C * DV_C
DN_CW = 512


def _dn_inproj_kernel(xp_ref, x_ref, xn_ref, mod_ref, g_ref, w_ref, cw_ref, ab_ref,
                      q_ref, k_ref, v_ref, z_ref, bg_ref, pext_ref, *, tm, n_tiles_p, tp, lp, ls):
    i = pl.program_id(0)
    xe = jnp.concatenate([xp_ref[...], x_ref[...], xn_ref[...]], axis=0)
    h = _ada_h(xe, g_ref[...], mod_ref[0], 1).astype(BF16)
    first, last = _seq_edges(i, tm, n_tiles_p, tp, lp, ls, (tm, 1), 0)
    nq = H_C * DK_C
    for c0 in range(0, DN_QKV, DN_CW):
        pext_ref[...] = jnp.dot(h, w_ref[:, c0:c0 + DN_CW], preferred_element_type=F32)
        cw = cw_ref[:, c0:c0 + DN_CW]
        c = (jnp.where(first, 0.0, pext_ref[7:tm + 7, :]) * cw[0:1]
             + pext_ref[8:tm + 8, :] * cw[1:2]
             + jnp.where(last, 0.0, pext_ref[9:tm + 9, :]) * cw[2:3])
        a = jax.nn.silu(c)
        for b0 in range(0, DN_CW, DK_C):
            col = c0 + b0
            blk = a[:, b0:b0 + DK_C]
            if col < 2 * nq:
                blk = blk * lax.rsqrt(jnp.sum(blk * blk, axis=-1, keepdims=True) + NORM_EPS)
            if col < nq:
                q_ref[:, col:col + DK_C] = (blk * (DK_C ** -0.5)).astype(BF16)
            elif col < 2 * nq:
                k_ref[:, col - nq:col - nq + DK_C] = blk.astype(BF16)
            else:
                v_ref[:, col - 2 * nq:col - 2 * nq + DK_C] = blk.astype(BF16)
    hm = h[8:tm + 8]
    for c0 in range(0, DN_OUT, DN_CW):
        z_ref[:, c0:c0 + DN_CW] = jnp.dot(hm, w_ref[:, DN_QKV + c0:DN_QKV + c0 + DN_CW],
                                           preferred_element_type=F32).astype(BF16)
    r = jnp.dot(hm, w_ref[:, DN_QKV + DN_OUT:], preferred_element_type=F32)
    lane = lax.broadcasted_iota(jnp.int32, r.shape, 1)
    xs = r + ab_ref[1:2]
    softplus = jnp.maximum(xs, 0.0) + jnp.log1p(jnp.exp(-jnp.abs(xs)))
    bg_ref[...] = jnp.where(lane < 2 * H_C, jax.nn.sigmoid(r),
                            jnp.where(lane < 4 * H_C, -ab_ref[0:1] * softplus, 0.0))


def _dn_inproj(x, mod, row_of_tile, g, w_in, conv_w, a_log, dt_bias, *, tm, tp, lp, ls):
    t = x.shape[0]
    pad = 128 - 4 * H_C
    w = jnp.pad(w_in, ((0, 0), (0, pad))).astype(BF16)
    ab = jnp.zeros((2, 128), F32)
    ab = ab.at[0, 2 * H_C:4 * H_C].set(jnp.exp(a_log.astype(F32)).reshape(-1))
    ab = ab.at[1, 2 * H_C:4 * H_C].set(dt_bias.astype(F32).reshape(-1))
    prev, nxt = _halo_specs(tm, t)
    tile = lambda i: (i, 0)
    const = lambda i: (0, 0)
    wide = jax.ShapeDtypeStruct((t, DN_OUT), BF16)
    return pl.pallas_call(
        functools.partial(_dn_inproj_kernel, tm=tm, n_tiles_p=tp // tm, tp=tp, lp=lp, ls=ls),
        out_shape=(wide, wide, wide, wide, jax.ShapeDtypeStruct((t, 128), F32)),
        grid=(t // tm,),
        in_specs=[prev, pl.BlockSpec((tm, D_MODEL), tile), nxt,
                  pl.BlockSpec((1, 9, D_MODEL), lambda i: (row_of_tile(i), 0, 0)),
                  pl.BlockSpec((1, D_MODEL), const),
                  pl.BlockSpec(w.shape, const),
                  pl.BlockSpec(conv_w.shape, const),
                  pl.BlockSpec((2, 128), const)],
        out_specs=(pl.BlockSpec((tm, DN_OUT), tile),) * 4 + (pl.BlockSpec((tm, 128), tile),),
        scratch_shapes=[pltpu.VMEM((tm + 16, DN_CW), F32)],
        compiler_params=pltpu.CompilerParams(dimension_semantics=("arbitrary",),
                                             vmem_limit_bytes=VMEM_LIMIT),
        name="dn_inproj",
    )(x, x, x, mod, g.reshape(1, D_MODEL), w, conv_w.astype(F32), ab)


DN_GROUP = 4


def _dn_masks(rev):
    n = DN_CHUNK
    r_i = lax.broadcasted_iota(jnp.int32, (n, n), 0)
    c_i = lax.broadcasted_iota(jnp.int32, (n, n), 1)
    tri = jnp.where((r_i <= c_i) if rev else (r_i >= c_i), 1.0, 0.0).astype(BF16)
    rows = DN_GROUP * n
    rr = lax.broadcasted_iota(jnp.int32, (rows, rows), 0)
    cc = lax.broadcasted_iota(jnp.int32, (rows, rows), 1)
    same = (rr // n) == (cc // n)
    incl = same & ((rr <= cc) if rev else (rr >= cc))
    strict = same & ((rr < cc) if rev else (rr > cc))
    eye = jnp.where(rr == cc, 1.0, 0.0)
    return tri, incl, strict, eye


def _dn_prepare(qc, kc, vc, bgc, masks, rev):
    n = DN_CHUNK
    tri, incl, strict, eye = masks
    gct = sum(jnp.dot(tri, p, preferred_element_type=F32) for p in _split3(bgc))
    rows = DN_GROUP * n
    lane = lax.broadcasted_iota(jnp.int32, (rows, 128), 1)
    last = 0 if rev else n - 1
    prep = []
    for gi in range(H_C // DN_GROUP):
        heads = range(gi * DN_GROUP, (gi + 1) * DN_GROUP)
        sb = [(H_C if rev else 0) + h for h in heads]
        sg = [2 * H_C + (H_C if rev else 0) + h for h in heads]
        beta = jnp.concatenate([bgc[:, s:s + 1] for s in sb], axis=0)
        gc = jnp.concatenate([gct[:, s:s + 1] for s in sg], axis=0)
        gl = jnp.concatenate([jnp.broadcast_to(gct[last:last + 1, s:s + 1], (n, 1)) for s in sg], axis=0)
        q4 = jnp.concatenate([qc[:, h * DK_C:(h + 1) * DK_C] for h in heads], axis=0)
        k4 = jnp.concatenate([kc[:, h * DK_C:(h + 1) * DK_C] for h in heads], axis=0)
        v4 = jnp.concatenate([vc[:, h * DV_C:(h + 1) * DV_C] for h in heads], axis=0)
        hi, mid, lo = (p.astype(F32) for p in _split3(gc))
        one = jnp.where(lane < 6, 1.0, 0.0)
        u_m = jnp.where(lane == 0, hi, jnp.where(lane == 1, mid, jnp.where(lane == 2, lo, one)))
        v_m = jnp.where(lane == 3, -hi, jnp.where(lane == 4, -mid, jnp.where(lane == 5, -lo, one)))
        gd = _bdot_nt(jnp.where(lane < 6, u_m, 0.0), jnp.where(lane < 6, v_m, 0.0))
        decay = jnp.where(incl, jnp.exp(jnp.where(incl, gd, 0.0)), 0.0)
        k4f = k4.astype(F32)
        kb = k4f * beta
        lm = jnp.where(strict, _bdot_nt(kb, k4) * decay, 0.0)
        x = jnp.concatenate([v4.astype(F32) * beta, kb * jnp.exp(gc)], axis=1)
        qe = q4.astype(F32) * jnp.exp(gc)
        ke = (k4f * jnp.exp(gl - gc)).astype(BF16)
        aqk = jnp.where(incl, _bdot_nt(q4, k4) * decay, 0.0).astype(BF16)
        prep.append(dict(lm=lm, x=x, qe=qe, ke=ke, aqk=aqk, egl=jnp.exp(gl), rev=rev))
    return prep


def _dn_solve(groups, eye):
    n = DN_CHUNK
    dot = functools.partial(jnp.dot, preferred_element_type=F32)
    for g in groups:
        g["lh"] = g["lm"].astype(BF16)
        g["p"] = -g["lh"]
        g["t"] = eye - g["lm"]
    def rows(m, lo, hi):
        return jnp.concatenate([m[b * n + lo:b * n + hi] for b in range(DN_GROUP)], axis=0)

    def put(upd, lo, hi):
        k = hi - lo
        parts = []
        for b in range(DN_GROUP):
            parts += [jnp.zeros((lo, upd.shape[1]), upd.dtype), upd[b * k:(b + 1) * k],
                      jnp.zeros((n - hi, upd.shape[1]), upd.dtype)]
        return jnp.concatenate([p for p in parts if p.shape[0]], axis=0)

    for s in range(1, 6):
        w = 2 ** s
        for g in groups:
            g["win"] = ((0, n - w) if g["rev"] else (w, n)) if w >= 16 else None
        for g in groups:
            if g["win"]:
                g["p"] = put(dot(rows(g["p"], *g["win"]), g["p"]).astype(BF16), *g["win"])
            else:
                g["p"] = dot(g["p"], g["p"]).astype(BF16)
        for g in groups:
            tb = g["t"].astype(BF16)
            if g["win"]:
                g["t"] = g["t"] + put(dot(rows(tb, *g["win"]), g["p"]), *g["win"])
            else:
                g["t"] = g["t"] + dot(tb, g["p"])
    for g in groups:
        g["tb"] = g["t"].astype(BF16)
        g["x0"] = dot(g["tb"], g["x"].astype(BF16))
    for g in groups:
        g["res"] = g["x"] - g["x0"] - dot(g["lh"], g["x0"].astype(BF16))
    out = []
    for g in groups:
        x = g["x0"] + dot(g["tb"], g["res"].astype(BF16))
        u4, w4, qe = x[:, :DV_C], x[:, DV_C:], g["qe"]
        wqe = [jnp.concatenate([w4[j * n:(j + 1) * n], qe[j * n:(j + 1) * n]], axis=0).astype(BF16)
               for j in range(DN_GROUP)]
        out.append((u4, wqe, g["ke"], g["aqk"], g["egl"]))
    return out


def _dn_advance(prep, s_ref):
    n = DN_CHUNK
    outs = []
    for gi, (u4, wqe, ke, aqk, egl) in enumerate(prep):
        vn, qs = [], []
        for j in range(DN_GROUP):
            wq = jnp.dot(wqe[j], s_ref[gi * DN_GROUP + j].astype(BF16), preferred_element_type=F32)
            vn.append(u4[j * n:(j + 1) * n] - wq[:n])
            qs.append(wq[n:])
        o4 = jnp.concatenate(qs, axis=0) + jnp.dot(aqk, jnp.concatenate(vn, axis=0).astype(BF16),
                                                   preferred_element_type=F32)
        for j in range(DN_GROUP):
            h = gi * DN_GROUP + j
            sl = slice(j * n, (j + 1) * n)
            s_ref[h] = s_ref[h] * egl[j * n:j * n + 1] + lax.dot_general(
                ke[sl], vn[j].astype(BF16), (((0,), (0,)), ((), ())), preferred_element_type=F32)
            outs.append(o4[sl])
    return outs


def _dn_scan_kernel(*refs, n_chunks, blocks_per_seq, zero_init, has_prev, write_state):
    refs = list(refs)
    fwd_in, bwd_in = refs[:4], refs[4:8]
    del refs[:8]
    s0f_ref, s0b_ref = (None, None) if zero_init else (refs.pop(0), refs.pop(0))
    if has_prev:
        del refs[:2]
    of_ref, ob_ref = refs.pop(0), refs.pop(0)
    sfo_ref, sbo_ref = (refs.pop(0), refs.pop(0)) if write_state else (None, None)
    sf_ref, sb_ref = refs
    i = pl.program_id(0)
    f_in_seq = lax.rem(i, blocks_per_seq)
    r_in_seq = lax.rem(pl.num_programs(0) - 1 - i, blocks_per_seq)

    @pl.when(f_in_seq == 0)
    def _():
        sf_ref[...] = jnp.zeros_like(sf_ref) if zero_init else s0f_ref[0]

    @pl.when(r_in_seq == blocks_per_seq - 1)
    def _():
        sb_ref[...] = jnp.zeros_like(sb_ref) if zero_init else s0b_ref[0]

    def chunk(io, c):
        rows = slice(c * DN_CHUNK, (c + 1) * DN_CHUNK)
        return [r[rows, :] for r in io]

    mf, mb = _dn_masks(False), _dn_masks(True)
    raw = [_dn_prepare(*chunk(fwd_in, c), mf, False) for c in range(n_chunks)]
    raw += [_dn_prepare(*chunk(bwd_in, c), mb, True) for c in range(n_chunks)]
    n_groups = H_C // DN_GROUP
    solved = _dn_solve([g for r in raw for g in r], mf[3])
    per_chunk = [solved[j * n_groups:(j + 1) * n_groups] for j in range(2 * n_chunks)]
    prep_f, prep_b = per_chunk[:n_chunks], per_chunk[n_chunks:]
    for step in range(n_chunks):
        for o_ref, prep, s_ref, c in ((of_ref, prep_f, sf_ref, step),
                                      (ob_ref, prep_b, sb_ref, n_chunks - 1 - step)):
            for h, o in enumerate(_dn_advance(prep[c], s_ref)):
                o_ref[c * DN_CHUNK:(c + 1) * DN_CHUNK, h * DV_C:(h + 1) * DV_C] = o.astype(BF16)

    if write_state:
        @pl.when(f_in_seq == blocks_per_seq - 1)
        def _():
            sfo_ref[0] = sf_ref[...]

        @pl.when(r_in_seq == 0)
        def _():
            sbo_ref[0] = sb_ref[...]


def _dn_scan(q, k, v, bg, s0_f, s0_b, prev, *, row0, n_seq, seq_len, ct, write_state):
    t = q.shape[0]
    bps = seq_len // ct
    nblk = n_seq * bps
    b0 = row0 // ct
    zero_init = s0_f is None
    fwd = lambda i: (b0 + i, 0)
    bwd = lambda i: (b0 + nblk - 1 - i, 0)
    seq_f = lambda i: (i // bps, 0, 0, 0)
    seq_b = lambda i: ((nblk - 1 - i) // bps, 0, 0, 0)
    in_specs, args = [], []
    for rows in (fwd, bwd):
        in_specs += [pl.BlockSpec((ct, DN_OUT), rows)] * 3 + [pl.BlockSpec((ct, 128), rows)]
        args += [q, k, v, bg]
    state_blk = (1, H_C, DK_C, DV_C)
    if not zero_init:
        in_specs += [pl.BlockSpec(state_blk, seq_f), pl.BlockSpec(state_blk, seq_b)]
        args += [s0_f, s0_b]
    aliases = {}
    if prev is not None:
        aliases = {len(args): 0, len(args) + 1: 1}
        in_specs += [pl.BlockSpec(memory_space=pl.ANY)] * 2
        args += list(prev)
    out_shape = [jax.ShapeDtypeStruct((t, DN_OUT), BF16)] * 2
    out_specs = [pl.BlockSpec((ct, DN_OUT), fwd), pl.BlockSpec((ct, DN_OUT), bwd)]
    if write_state:
        out_shape += [jax.ShapeDtypeStruct((n_seq,) + state_blk[1:], F32)] * 2
        out_specs += [pl.BlockSpec(state_blk, seq_f), pl.BlockSpec(state_blk, seq_b)]
    kern = functools.partial(_dn_scan_kernel, n_chunks=ct // DN_CHUNK, blocks_per_seq=bps,
                             zero_init=zero_init, has_prev=prev is not None, write_state=write_state)
    return pl.pallas_call(
        kern, out_shape=tuple(out_shape), grid=(nblk,), in_specs=in_specs, out_specs=tuple(out_specs),
        scratch_shapes=[pltpu.VMEM((H_C, DK_C, DV_C), F32)] * 2,
        input_output_aliases=aliases,
        compiler_params=pltpu.CompilerParams(dimension_semantics=("arbitrary",),
                                             vmem_limit_bytes=VMEM_LIMIT),
        name="dn_scan",
    )(*args)


def _dn_outproj_kernel(x_ref, of_ref, ob_ref, z_ref, mod_ref, ng_ref, w_ref, y_ref):
    o = of_ref[...].astype(F32) + ob_ref[...].astype(F32)
    parts = []
    for h in range(H_C):
        blk = o[:, h * DV_C:(h + 1) * DV_C]
        parts.append(blk * lax.rsqrt(jnp.mean(blk * blk, axis=-1, keepdims=True) + NORM_EPS))
    y = jnp.concatenate(parts, axis=1) * ng_ref[...] * jax.nn.silu(z_ref[...].astype(F32))
    gate = mod_ref[0][5:6]
    y_ref[...] = x_ref[...] + gate * jnp.dot(y.astype(BF16), w_ref[...], preferred_element_type=F32)


def _dn_outproj(x, o_f, o_b, z, mod, row_of_tile, norm_g, w_out, *, tm):
    t = x.shape[0]
    tile = lambda i: (i, 0)
    const = lambda i: (0, 0)
    return pl.pallas_call(
        _dn_outproj_kernel,
        out_shape=jax.ShapeDtypeStruct((t, D_MODEL), F32),
        grid=(t // tm,),
        in_specs=[pl.BlockSpec((tm, D_MODEL), tile), pl.BlockSpec((tm, DN_OUT), tile),
                  pl.BlockSpec((tm, DN_OUT), tile), pl.BlockSpec((tm, DN_OUT), tile),
                  pl.BlockSpec((1, 9, D_MODEL), lambda i: (row_of_tile(i), 0, 0)),
                  pl.BlockSpec((1, DN_OUT), const), pl.BlockSpec((DN_OUT, D_MODEL), const)],
        out_specs=pl.BlockSpec((tm, D_MODEL), tile),
        compiler_params=pltpu.CompilerParams(dimension_semantics=("arbitrary",),
                                             vmem_limit_bytes=VMEM_LIMIT),
        name="dn_outproj",
    )(x, o_f, o_b, z, mod, jnp.tile(norm_g.astype(F32), H_C).reshape(1, DN_OUT), w_out.astype(BF16))


def _dn_layer(x, mod, row_of_tile, g, w_in, w_out, conv_w, a_log, dt_bias, norm_g, s0_f, s0_b,
              *, tm, ct, tp, lp, ls):
    t = x.shape[0]
    q, k, v, z, bg = _dn_inproj(x, mod, row_of_tile, g, w_in, conv_w, a_log, dt_bias, tm=tm, tp=tp, lp=lp, ls=ls)
    bp, bs = tp // lp, (t - tp) // ls
    o_f, o_b, sf, sb = _dn_scan(q, k, v, bg, None, None, None, row0=0, n_seq=bp, seq_len=lp, ct=ct,
                                write_state=True)
    o_f, o_b = _dn_scan(q, k, v, bg, s0_f.astype(F32), s0_b.astype(F32), (o_f, o_b), row0=tp, n_seq=bs,
                        seq_len=ls, ct=ct, write_state=False)
    return _dn_outproj(x, o_f, o_b, z, mod, row_of_tile, norm_g, w_out, tm=tm), sf, sb


AH_Q = H_A * HD_A
AH_KV = KVH_A * HD_A
AH_U3 = 3 * HY_C


def _group_mean_sq(x, bd):
    sq = x * x
    hi = sq.astype(BF16)
    lo = (sq - hi.astype(F32)).astype(BF16)
    return (jnp.dot(hi, bd, preferred_element_type=F32) + jnp.dot(lo, bd, preferred_element_type=F32)) * (1.0 / HD_A)


def _rope_lanes(x, cos_t, sin_t):
    w = x.shape[1]
    lane = lax.broadcasted_iota(jnp.int32, x.shape, 1)
    low = lax.rem(lane, HD_A) < HD_A // 2
    partner = jnp.where(low, pltpu.roll(x, w - HD_A // 2, 1), pltpu.roll(x, HD_A // 2, 1))
    reps = w // cos_t.shape[1]
    return x * jnp.concatenate([cos_t] * reps, axis=1) + partner * jnp.concatenate([sin_t] * reps, axis=1)


def _rep4(x):
    lane = lax.broadcasted_iota(jnp.int32, x.shape, 1)
    sw = pltpu.roll(x, HD_A, 1)
    a = jnp.where(lane < HD_A, x, sw)
    b = jnp.where(lane < HD_A, sw, x)
    return jnp.concatenate([a, a, b, b], axis=1)


def _ah_inproj_kernel(xp_ref, x_ref, xn_ref, mod_ref, g_ref, w_ref, wut_ref, gains_ref, cos_ref, sin_ref,
                      bdq_ref, cwb_ref, perm_ref, q_ref, kt_ref, vt_ref, kn_ref, vn_ref, u_ref, vb_ref,
                      *, tm, n_tiles_p, tp, lp, ls):
    i = pl.program_id(0)
    is_p = i < n_tiles_p
    xe = jnp.concatenate([xp_ref[...], x_ref[...], xn_ref[...]], axis=0)
    he = _ada_h(xe, g_ref[...], mod_ref[0], 1).astype(BF16)
    hm = he[8:tm + 8]
    p = jnp.dot(hm, w_ref[...], preferred_element_type=F32)
    cos_t, sin_t = cos_ref[...], sin_ref[...]
    q = p[:, :AH_Q]
    q = q * lax.rsqrt(_group_mean_sq(q, bdq_ref[...]) + NORM_EPS) * gains_ref[0:1, :]
    q = jnp.where(is_p, q, _rope_lanes(q, cos_t, sin_t))
    q_ref[...] = (q * (HD_A ** -0.5)).astype(BF16)
    k = p[:, AH_Q:AH_Q + AH_KV]
    k = k * lax.rsqrt(_group_mean_sq(k, bdq_ref[:AH_KV, :AH_KV]) + NORM_EPS) * gains_ref[1:2, :AH_KV]
    kn_ref[...] = k
    k = jnp.where(is_p, k, _rope_lanes(k, cos_t, sin_t))
    kt_ref[...] = _rep4(k).astype(BF16)
    v = p[:, AH_Q + AH_KV:]
    vn_ref[...] = v
    vt_ref[...] = _rep4(v).astype(BF16)
    halo = jnp.concatenate([he[0:8], he[tm + 8:tm + 16], jnp.zeros((112, D_MODEL), BF16)], axis=0)
    nt = (((1,), (1,)), ((), ()))

    def store(r0, c):
        u_ref[r0:r0 + HY_C, :] = c
        if r0 == 2 * HY_C:
            vb_ref[...] = c.astype(BF16)

    @pl.when(is_p)
    def _():
        first, last = _seq_edges(i, tm, n_tiles_p, tp, lp, ls, (1, tm), 1)
        lane = lax.broadcasted_iota(jnp.int32, (1, tm), 1)
        for j in range(AH_U3 // HY_C):
            wu = wut_ref[j * HY_C:(j + 1) * HY_C, :]
            u = lax.dot_general(wu, hm, nt, preferred_element_type=F32)
            uh = lax.dot_general(wu, halo, nt, preferred_element_type=F32)
            left = jnp.where(lane == 0, uh[:, 7:8], pltpu.roll(u, 1, 1))
            right = jnp.where(lane == tm - 1, uh[:, 8:9], pltpu.roll(u, tm - 1, 1))
            cwb = cwb_ref[j * HY_C:(j + 1) * HY_C, :]
            store(j * HY_C, jnp.where(first, 0.0, left) * cwb[:, 0:1] + u * cwb[:, 1:2]
                  + jnp.where(last, 0.0, right) * cwb[:, 2:3] + cwb[:, 3:4])

    @pl.when(jnp.logical_not(is_p))
    def _():
        half = tm // 2
        hs = jnp.dot(perm_ref[...], hm, preferred_element_type=F32).astype(BF16)
        t0 = i * tm - tp
        seq_start = lax.rem(t0, ls) == 0
        seq_end = lax.rem(t0 + tm, ls) == 0
        lane = lax.broadcasted_iota(jnp.int32, (1, half), 1)
        for j in range(AH_U3 // HY_C):
            wu = wut_ref[j * HY_C:(j + 1) * HY_C, :]
            u = lax.dot_general(wu, hs, nt, preferred_element_type=F32)
            uh = lax.dot_general(wu, halo, nt, preferred_element_type=F32)
            ev, od = u[:, :half], u[:, half:]
            prev_tok = jnp.where(seq_start, 0.0, uh[:, 7:8])
            next_tok = jnp.where(seq_end, 0.0, uh[:, 8:9])
            od_before = jnp.where(lane == 0, prev_tok, pltpu.roll(od, 1, 1))
            ev_after = jnp.where(lane == half - 1, next_tok, pltpu.roll(ev, half - 1, 1))
            cwb = cwb_ref[j * HY_C:(j + 1) * HY_C, :]
            w0, w1, w2, b = cwb[:, 0:1], cwb[:, 1:2], cwb[:, 2:3], cwb[:, 3:4]
            store(j * HY_C, jnp.concatenate([od_before * w0 + ev * w1 + od * w2 + b,
                                             ev * w0 + od * w1 + ev_after * w2 + b], axis=1))


def _split_perm(tm):
    tok = jnp.arange(tm)
    pos = jnp.where(tok % 2 == 0, tok // 2, tm // 2 + tok // 2)
    return (jnp.arange(tm)[:, None] == pos[None, :]).astype(BF16)


def _rope_tables(ls):
    rows = ls // GRID_W
    r, col = jnp.meshgrid(jnp.arange(rows), jnp.arange(GRID_W), indexing='ij')
    inv = ROPE_THETA ** (-jnp.arange(0, ROPE_AXIS_DIM, 2, dtype=F32) / ROPE_AXIS_DIM)
    ang = jnp.concatenate([r.reshape(-1, 1).astype(F32) * inv, col.reshape(-1, 1).astype(F32) * inv], axis=-1)
    cos, sin = jnp.cos(ang), jnp.sin(ang)
    return jnp.concatenate([cos, cos] * 2, axis=1), jnp.concatenate([-sin, sin] * 2, axis=1)


def _ah_inproj(x, mod, row_of_tile, g, w_in, q_norm, k_norm, conv_w, conv_b, rope, *, tm, tp, lp, ls):
    t = x.shape[0]
    w = w_in[:, :AH_Q + 2 * AH_KV].astype(BF16)
    wut = w_in[:, AH_Q + 2 * AH_KV:].T.astype(BF16)
    gains = jnp.stack([jnp.tile(q_norm.astype(F32), H_A), jnp.tile(k_norm.astype(F32), H_A)])
    gid = jnp.arange(AH_Q) // HD_A
    bdq = (gid[:, None] == gid[None, :]).astype(BF16)
    cwb = jnp.concatenate([conv_w.astype(F32).T, conv_b.astype(F32)[:, None],
                           jnp.zeros((AH_U3, 4), F32)], axis=1)
    n_tiles_p = tp // tm
    prev, nxt = _halo_specs(tm, t)
    tile = lambda i: (i, 0)
    const = lambda i: (0, 0)
    rope_blk = lambda i: (jnp.where(i < n_tiles_p, 0, lax.rem(i * tm - tp, ls) // tm), 0)
    bf = lambda n: jax.ShapeDtypeStruct((t, n), BF16)
    return pl.pallas_call(
        functools.partial(_ah_inproj_kernel, tm=tm, n_tiles_p=n_tiles_p, tp=tp, lp=lp, ls=ls),
        out_shape=(bf(AH_Q), bf(AH_Q), bf(AH_Q), jax.ShapeDtypeStruct((t, AH_KV), F32),
                   jax.ShapeDtypeStruct((t, AH_KV), F32), jax.ShapeDtypeStruct((AH_U3, t), F32),
                   jax.ShapeDtypeStruct((HY_C, t), BF16)),
        grid=(t // tm,),
        in_specs=[prev, pl.BlockSpec((tm, D_MODEL), tile), nxt,
                  pl.BlockSpec((1, 9, D_MODEL), lambda i: (row_of_tile(i), 0, 0)),
                  pl.BlockSpec((1, D_MODEL), const), pl.BlockSpec(w.shape, const),
                  pl.BlockSpec(wut.shape, const), pl.BlockSpec(gains.shape, const),
                  pl.BlockSpec((tm, 128), rope_blk), pl.BlockSpec((tm, 128), rope_blk),
                  pl.BlockSpec(bdq.shape, const), pl.BlockSpec(cwb.shape, const),
                  pl.BlockSpec((tm, tm), const)],
        out_specs=(pl.BlockSpec((tm, AH_Q), tile),) * 3 + (pl.BlockSpec((tm, AH_KV), tile),) * 2
        + (pl.BlockSpec((AH_U3, tm), lambda i: (0, i)), pl.BlockSpec((HY_C, tm), lambda i: (0, i))),
        compiler_params=pltpu.CompilerParams(dimension_semantics=("arbitrary",),
                                             vmem_limit_bytes=VMEM_LIMIT),
        name="ah_inproj",
    )(x, x, x, mod, g.reshape(1, D_MODEL), w, wut, gains, rope[0], rope[1], bdq, cwb, _split_perm(tm))


def _attn_chains(chains, sink_ref, valid):
    tq = chains[0][0].shape[0]
    lane = lax.broadcasted_iota(jnp.int32, (tq, G_A * HD_A), 1)
    valid4 = None if valid is None else jnp.concatenate([valid] * G_A, axis=0)
    scores, sinks = [], []
    for q, kt, _, g in chains:
        qs = jnp.concatenate([jnp.where(lane // HD_A == j, q, jnp.zeros_like(q)) for j in range(G_A)], axis=0)
        s = lax.dot_general(qs, kt, (((1,), (1,)), ((), ())), preferred_element_type=F32)
        scores.append(s if valid4 is None else jnp.where(valid4, s, NEG_INF))
        sinks.append(jnp.concatenate([jnp.broadcast_to(sink_ref[g * G_A + j:g * G_A + j + 1, 0:1], (tq, 1))
                                      for j in range(G_A)], axis=0))
    maxes = [jnp.maximum(jnp.max(s, axis=-1, keepdims=True), sk) for s, sk in zip(scores, sinks)]
    probs = [jnp.exp(s - m) for s, m in zip(scores, maxes)]
    dens = [jnp.sum(p, axis=-1, keepdims=True) + jnp.exp(sk - m) for p, sk, m in zip(probs, sinks, maxes)]
    pvs = [jnp.dot(p.astype(BF16), c[2], preferred_element_type=F32) / d
           for p, c, d in zip(probs, chains, dens)]
    outs = []
    for pv in pvs:
        out = pv[:tq]
        for j in range(1, G_A):
            out = jnp.where(lane // HD_A == j, pv[j * tq:(j + 1) * tq], out)
        outs.append(out)
    return outs


def _attn_ctx_kernel(q_ref, kt_ref, vt_ref, sink_ref, o_ref, *, lp):
    w = G_A * HD_A
    where = [(slice(s0, s0 + lp), slice(g * w, (g + 1) * w), g)
             for s0 in range(0, q_ref.shape[0], lp) for g in range(KVH_A)]
    outs = _attn_chains([(q_ref[r, c], kt_ref[r, c], vt_ref[r, c], g) for r, c, g in where], sink_ref, None)
    for (r, c, _), o in zip(where, outs):
        o_ref[r, c] = o.astype(BF16)


def _attn_win_kernel(q_ref, kp_ref, km_ref, kn_ref, vp_ref, vm_ref, vn_ref, ck_ref, cv_ref, sink_ref,
                     prev_ref, o_ref, *, tq, ls):
    del prev_ref
    i = pl.program_id(1)
    n_ctx = ck_ref.shape[1]
    w = G_A * HD_A
    q_pos = i * tq + lax.broadcasted_iota(jnp.int32, (tq, n_ctx + tq + 2 * WINDOW), 0)
    col = lax.broadcasted_iota(jnp.int32, (tq, n_ctx + tq + 2 * WINDOW), 1)
    k_pos = i * tq - WINDOW + (col - n_ctx)
    valid = (col < n_ctx) | ((jnp.abs(q_pos - k_pos) <= WINDOW) & (k_pos >= 0) & (k_pos < ls))
    chains = []
    for g in range(KVH_A):
        cols = slice(g * w, (g + 1) * w)
        kt = jnp.concatenate([ck_ref[0, :, cols], kp_ref[:, cols], km_ref[:, cols], kn_ref[:, cols]], axis=0)
        vt = jnp.concatenate([cv_ref[0, :, cols], vp_ref[:, cols], vm_ref[:, cols], vn_ref[:, cols]], axis=0)
        chains.append((q_ref[:, cols], kt, vt, g))
    for g, o in enumerate(_attn_chains(chains, sink_ref, valid)):
        o_ref[:, g * w:(g + 1) * w] = o.astype(BF16)


def _attention(q, kt, vt, ck, cv, sink, *, tq, tp, lp, ls):
    t = q.shape[0]
    bp, bs = tp // lp, (t - tp) // ls
    sink_rows = jnp.broadcast_to(sink.astype(F32)[:, None], (H_A, 128))
    params = pltpu.CompilerParams(dimension_semantics=("arbitrary",), vmem_limit_bytes=VMEM_LIMIT)
    seq = lambda b: (b, 0)
    nsq = math.gcd(CTX_SEQS, bp)
    a = pl.pallas_call(
        functools.partial(_attn_ctx_kernel, lp=lp),
        out_shape=jax.ShapeDtypeStruct((t, AH_Q), BF16), grid=(bp // nsq,),
        in_specs=[pl.BlockSpec((nsq * lp, AH_Q), seq)] * 3 + [pl.BlockSpec((H_A, 128), lambda b: (0, 0))],
        out_specs=pl.BlockSpec((nsq * lp, AH_Q), seq), compiler_params=params, name="attn_ctx",
    )(q, kt, vt, sink_rows)
    nq = ls // tq
    wb = tq // WINDOW
    n128 = t // WINDOW
    main = lambda b, i: ((tp + b * ls) // tq + i, 0)
    prev = lambda b, i: (jnp.maximum((tp + b * ls) // WINDOW + i * wb - 1, 0), 0)
    nxt = lambda b, i: (jnp.minimum((tp + b * ls) // WINDOW + (i + 1) * wb, n128 - 1), 0)
    ctx = lambda b, i: (b, 0, 0)
    kv_specs = [pl.BlockSpec((WINDOW, AH_Q), prev), pl.BlockSpec((tq, AH_Q), main),
                pl.BlockSpec((WINDOW, AH_Q), nxt)]
    n_ctx = ck.shape[1]
    return pl.pallas_call(
        functools.partial(_attn_win_kernel, tq=tq, ls=ls),
        out_shape=jax.ShapeDtypeStruct((t, AH_Q), BF16), grid=(bs, nq),
        in_specs=[pl.BlockSpec((tq, AH_Q), main)] + kv_specs + kv_specs
        + [pl.BlockSpec((1, n_ctx, AH_Q), ctx)] * 2
        + [pl.BlockSpec((H_A, 128), lambda b, i: (0, 0)), pl.BlockSpec(memory_space=pl.ANY)],
        out_specs=pl.BlockSpec((tq, AH_Q), main),
        input_output_aliases={10: 0},
        compiler_params=pltpu.CompilerParams(dimension_semantics=("arbitrary", "arbitrary"),
                                             vmem_limit_bytes=VMEM_LIMIT),
        name="attn_win",
    )(q, kt, kt, kt, vt, vt, vt, ck, cv, sink_rows, a)


DFT_SUB = 64
DFT_HALF = 256


def _dft_gen_kernel(o_ref, cl_ref, sl_ref, *, length):
    a = pl.program_id(0)
    n2 = 4 * length

    def angles(n_vec, shape):
        k = lax.broadcasted_iota(jnp.int32, shape, 1)
        ph = lax.rem(n_vec * (2 * k + 1), n2)
        return ph.astype(F32) * (2.0 * math.pi / n2)

    @pl.when(a == 0)
    def _():
        th = angles(lax.broadcasted_iota(jnp.int32, (DFT_SUB, length), 0), (DFT_SUB, length))
        cl_ref[...] = jnp.cos(th)
        sl_ref[...] = jnp.sin(th)

    th = angles(jnp.full((8, length), a * DFT_SUB, jnp.int32), (8, length))[0:1]
    ch, sh = jnp.cos(th), jnp.sin(th)
    c = ch * cl_ref[...] - sh * sl_ref[...]
    ns = -(sh * cl_ref[...] + ch * sl_ref[...])
    for jt in range(length // DFT_HALF):
        src = slice(jt * DFT_HALF, (jt + 1) * DFT_HALF)
        o_ref[:, 2 * jt * DFT_HALF:(2 * jt + 1) * DFT_HALF] = c[:, src].astype(BF16)
        o_ref[:, (2 * jt + 1) * DFT_HALF:(2 * jt + 2) * DFT_HALF] = ns[:, src].astype(BF16)


def _dft_matrix(length):
    return pl.pallas_call(
        functools.partial(_dft_gen_kernel, length=length),
        out_shape=jax.ShapeDtypeStruct((length, 2 * length), BF16), grid=(length // DFT_SUB,),
        out_specs=pl.BlockSpec((DFT_SUB, 2 * length), lambda a: (a, 0)),
        scratch_shapes=[pltpu.VMEM((DFT_SUB, length), F32)] * 2,
        compiler_params=pltpu.CompilerParams(dimension_semantics=("arbitrary",),
                                             vmem_limit_bytes=VMEM_LIMIT),
        name="dft_gen",
    )()


HYF_RB = 256


def _hy_filter_kernel(w1_ref, w2_ref, w3f_ref, w3b_ref, cols_ref, f_ref, o_ref, a_ref, b_ref, *, length):
    rc, jt = pl.program_id(0), pl.program_id(1)

    @pl.when(jt == 0)
    def _():
        m = lax.broadcasted_iota(jnp.int32, (8, length), 1).astype(F32)[0:1]
        t = m * (1.0 / (length - 1))
        w = m * (2.0 * math.pi / length)
        band = lax.broadcasted_iota(jnp.int32, (HY_BANDS, 1), 0).astype(F32)
        fb = 1e-4 + band * ((HY_BANDS - 1 - 1e-4) / (HY_BANDS - 1))
        z = jnp.concatenate([t, jnp.cos(fb * w), -jnp.sin(fb * w),
                             jnp.zeros((40 - HY_EMB, length), F32)], axis=0)
        cols = cols_ref[...]
        h = jnp.sin(cols[:, 1:2] * (_bdot(w1_ref[...], z) + cols[:, 0:1]))
        h = jnp.sin(cols[:, 3:4] * (_bdot(w2_ref[...], h) + cols[:, 2:3]))
        n_out = 2 * HY_ORDER * HY_C
        row = (rc * HYF_RB + lax.broadcasted_iota(jnp.int32, (HYF_RB, 1), 0)).astype(F32)
        step = (HY_MAX_DECAY - HY_MIN_DECAY) / (n_out - 1)
        d_f = jnp.abs(HY_MIN_DECAY + row * step)
        d_b = jnp.abs(HY_MIN_DECAY + (row + HY_ORDER * HY_C) * step)
        hf = _bdot(w3f_ref[...], h) * jnp.exp(-t * d_f)
        hb = jnp.where(m == 0.0, 0.0, _bdot(w3b_ref[...], h) * jnp.exp(-t * d_b))
        ssq = jnp.sum(hf * hf, axis=-1, keepdims=True) + jnp.sum(hb * hb, axis=-1, keepdims=True)
        scale = lax.rsqrt(ssq + NORM_EPS) * (1.0 / length)
        a_ref[...] = ((hf + hb) * scale).astype(BF16)
        b_ref[...] = ((hf - hb) * scale).astype(BF16)

    o_ref[:, :DFT_HALF] = jnp.dot(a_ref[...], f_ref[:, :DFT_HALF], preferred_element_type=F32)
    o_ref[:, DFT_HALF:] = jnp.dot(b_ref[...], f_ref[:, DFT_HALF:], preferred_element_type=F32)


def _hy_filter(fmat, w1, b1, f1, w2, b2, f2, w3, *, length):
    w1t = jnp.pad(w1.astype(F32).T, ((0, 0), (0, 40 - HY_EMB)))
    w3t = w3.astype(F32).T
    cols = jnp.stack([b1, f1, b2, f2] + [jnp.zeros_like(b1)] * 4, axis=1).astype(F32)
    n_rows = HY_ORDER * HY_C
    const = lambda rc, jt: (0, 0)
    return pl.pallas_call(
        functools.partial(_hy_filter_kernel, length=length),
        out_shape=jax.ShapeDtypeStruct((n_rows, 2 * length), F32),
        grid=(n_rows // HYF_RB, length // DFT_HALF),
        in_specs=[pl.BlockSpec(w1t.shape, const), pl.BlockSpec((w2.shape[1], w2.shape[0]), const),
                  pl.BlockSpec((HYF_RB, w3t.shape[1]), lambda rc, jt: (rc, 0)),
                  pl.BlockSpec((HYF_RB, w3t.shape[1]), lambda rc, jt: (n_rows // HYF_RB + rc, 0)),
                  pl.BlockSpec(cols.shape, const),
                  pl.BlockSpec((length, 2 * DFT_HALF), lambda rc, jt: (0, jt))],
        out_specs=pl.BlockSpec((HYF_RB, 2 * DFT_HALF), lambda rc, jt: (rc, jt)),
        scratch_shapes=[pltpu.VMEM((HYF_RB, length), BF16)] * 2,
        compiler_params=pltpu.CompilerParams(dimension_semantics=("arbitrary", "arbitrary"),
                                             vmem_limit_bytes=VMEM_LIMIT),
        name="hy_filter",
    )(w1t, w2.astype(F32).T, w3t, w3t, cols, fmat)


HY_SEQS = 8


def _hy_fwd_kernel(z_ref, f_ref, k_ref, y_ref, *, nsq, length):
    kr, ki = k_ref[:, :DFT_HALF], k_ref[:, DFT_HALF:]
    for s in range(nsq):
        zt = jnp.dot(z_ref[:, s * length:(s + 1) * length], f_ref[...], preferred_element_type=F32)
        zr, zi = zt[:, :DFT_HALF], zt[:, DFT_HALF:]
        y_ref[s * HY_C:(s + 1) * HY_C, :DFT_HALF] = (zr * kr - zi * ki).astype(BF16)
        y_ref[s * HY_C:(s + 1) * HY_C, DFT_HALF:] = (zr * ki + zi * kr).astype(BF16)


def _hy_fwd(zb, fmat, kspec, order, *, lane0, n_seq, length):
    nsq = math.gcd(HY_SEQS, n_seq) if lane0 == 0 else 1
    sb0 = lane0 // length
    return pl.pallas_call(
        functools.partial(_hy_fwd_kernel, nsq=nsq, length=length),
        out_shape=jax.ShapeDtypeStruct((n_seq * HY_C, 2 * length), BF16),
        grid=(n_seq // nsq, length // DFT_HALF),
        in_specs=[pl.BlockSpec((HY_C, nsq * length), lambda s, jt: (0, sb0 + s)),
                  pl.BlockSpec((length, 2 * DFT_HALF), lambda s, jt: (0, jt)),
                  pl.BlockSpec((HY_C, 2 * DFT_HALF), lambda s, jt: (order, jt))],
        out_specs=pl.BlockSpec((nsq * HY_C, 2 * DFT_HALF), lambda s, jt: (s, jt)),
        compiler_params=pltpu.CompilerParams(dimension_semantics=("arbitrary", "arbitrary"),
                                             vmem_limit_bytes=VMEM_LIMIT),
        name="hy_fwd",
    )(zb, fmat, kspec)


def _hy_inv_kernel(y_ref, f_ref, gate_ref, z_ref, bias_ref, *rest, with_prev, with_bf16, nsq, tt):
    rest = list(rest)
    if with_prev:
        rest.pop(0)
    o_ref = rest.pop(0)
    ob_ref = rest.pop(0) if with_bf16 else None
    for s in range(nsq):
        lanes = slice(s * tt, (s + 1) * tt)
        y = lax.dot_general(y_ref[s * HY_C:(s + 1) * HY_C, :], f_ref[...], (((1,), (1,)), ((), ())),
                            preferred_element_type=F32)
        out = gate_ref[:, lanes] * (y + z_ref[:, lanes] * bias_ref[...])
        o_ref[:, lanes] = out
        if with_bf16:
            ob_ref[:, lanes] = out.astype(BF16)


def _hy_inv(yspec, fmat, gate_arr, gate_rb, z_arr, z_rb, bias, prev, *, lane0, n_seq, length, tt, t_total,
            with_bf16):
    nt = length // tt
    nsq = math.gcd(HY_SEQS, n_seq) if (nt == 1 and lane0 == 0) else 1
    lane_blk = lambda s, j: lane0 // tt + s * nt + j
    in_specs = [pl.BlockSpec((nsq * HY_C, 2 * length), lambda s, j: (s, 0)),
                pl.BlockSpec((tt, 2 * length), lambda s, j: (j, 0)),
                pl.BlockSpec((HY_C, nsq * tt), lambda s, j: (gate_rb, lane_blk(s, j))),
                pl.BlockSpec((HY_C, nsq * tt), lambda s, j: (z_rb, lane_blk(s, j))),
                pl.BlockSpec((HY_C, 1), lambda s, j: (0, 0))]
    args = [yspec, fmat, gate_arr, z_arr, bias.astype(F32).reshape(HY_C, 1)]
    aliases = {}
    if prev is not None:
        in_specs.append(pl.BlockSpec(memory_space=pl.ANY))
        aliases = {len(args): 0}
        args.append(prev)
    out_spec = pl.BlockSpec((HY_C, nsq * tt), lambda s, j: (0, lane_blk(s, j)))
    out_shape = [jax.ShapeDtypeStruct((HY_C, t_total), F32)]
    out_specs = [out_spec]
    if with_bf16:
        out_shape.append(jax.ShapeDtypeStruct((HY_C, t_total), BF16))
        out_specs.append(out_spec)
    res = pl.pallas_call(
        functools.partial(_hy_inv_kernel, with_prev=prev is not None, with_bf16=with_bf16, nsq=nsq, tt=tt),
        out_shape=tuple(out_shape), grid=(n_seq // nsq, nt), in_specs=in_specs, out_specs=tuple(out_specs),
        input_output_aliases=aliases,
        compiler_params=pltpu.CompilerParams(dimension_semantics=("arbitrary", "arbitrary"),
                                             vmem_limit_bytes=VMEM_LIMIT),
        name="hy_inv",
    )(*args)
    return res


def _hyena_group(ut, vb, fmat, kspec, hy_bias, y_prev, *, lane0, n_seq, length, t_total):
    tt = min(512, length)
    geo = dict(lane0=lane0, n_seq=n_seq, length=length)
    y1 = _hy_fwd(vb, fmat, kspec, 0, **geo)
    z2, z2b = _hy_inv(y1, fmat, ut, 0, ut, 2, hy_bias[0], None, tt=tt, t_total=t_total, with_bf16=True, **geo)
    y2 = _hy_fwd(z2b, fmat, kspec, 1, **geo)
    return _hy_inv(y2, fmat, ut, 1, z2, 0, hy_bias[1], y_prev, tt=tt, t_total=t_total, with_bf16=False,
                   **geo)[0]


HY_TILE = TOKEN_TILE


def _twiddle(jt, n):
    k0 = (jt * DFT_HALF + lax.broadcasted_iota(jnp.int32, (8, DFT_HALF), 1)).astype(F32)[0:1]
    ang = (k0 + 0.5) * (2.0 * math.pi / n)
    return jnp.cos(ang), -jnp.sin(ang)


def _cmul(ar, ai, br, bi):
    return ar * br - ai * bi, ar * bi + ai * br


def _hy_filter2_kernel(w1_ref, w2_ref, w3f_ref, w3b_ref, cols_ref, f_ref, ka_ref, kb_ref, *scr, length):
    rc, jt = pl.program_id(0), pl.program_id(1)
    half = length // 2

    @pl.when(jt == 0)
    def _():
        cols = cols_ref[...]
        n_out = 2 * HY_ORDER * HY_C
        row = (rc * HYF_RB + lax.broadcasted_iota(jnp.int32, (HYF_RB, 1), 0)).astype(F32)
        step = (HY_MAX_DECAY - HY_MIN_DECAY) / (n_out - 1)
        d_f = jnp.abs(HY_MIN_DECAY + row * step)
        d_b = jnp.abs(HY_MIN_DECAY + (row + HY_ORDER * HY_C) * step)
        band = lax.broadcasted_iota(jnp.int32, (HY_BANDS, 1), 0).astype(F32)
        fb = 1e-4 + band * ((HY_BANDS - 1 - 1e-4) / (HY_BANDS - 1))
        taps = []
        for parity in range(2):
            m = 2.0 * lax.broadcasted_iota(jnp.int32, (8, half), 1).astype(F32)[0:1] + parity
            t = m * (1.0 / (length - 1))
            w = m * (2.0 * math.pi / length)
            z = jnp.concatenate([t, jnp.cos(fb * w), -jnp.sin(fb * w),
                                 jnp.zeros((40 - HY_EMB, half), F32)], axis=0)
            h = jnp.sin(cols[:, 1:2] * (_bdot(w1_ref[...], z) + cols[:, 0:1]))
            h = jnp.sin(cols[:, 3:4] * (_bdot(w2_ref[...], h) + cols[:, 2:3]))
            hf = _bdot(w3f_ref[...], h) * jnp.exp(-t * d_f)
            hb = jnp.where(m == 0.0, 0.0, _bdot(w3b_ref[...], h) * jnp.exp(-t * d_b))
            taps.append((hf, hb))
        ssq = sum(jnp.sum(hf * hf, axis=-1, keepdims=True) + jnp.sum(hb * hb, axis=-1, keepdims=True)
                  for hf, hb in taps)
        scale = lax.rsqrt(ssq + NORM_EPS) * (1.0 / length)
        (hf, hb), (hfo, hbo) = taps
        scr[0][...] = ((hf + hb) * scale).astype(BF16)
        scr[1][...] = ((hf - hb) * scale).astype(BF16)
        scr[2][...] = (hfo * scale).astype(BF16)
        scr[3][...] = (hbo * scale).astype(BF16)

    dot = functools.partial(jnp.dot, preferred_element_type=F32)
    fc, fs = f_ref[:, :DFT_HALF], f_ref[:, DFT_HALF:]
    er, ei = dot(scr[0][...], fc), dot(scr[1][...], fs)
    tr, ti = _twiddle(jt, 2 * length)
    fr, fi = _cmul(tr, ti, dot(scr[2][...], fc), dot(scr[2][...], fs))
    gr, gi = _cmul(tr, -ti, dot(scr[3][...], fc), -dot(scr[3][...], fs))
    ka_ref[:, :DFT_HALF] = er + (fr + gr)
    ka_ref[:, DFT_HALF:] = ei + (fi + gi)
    kb_ref[:, :DFT_HALF] = er - (fr + gr)
    kb_ref[:, DFT_HALF:] = ei - (fi + gi)


def _hy_filter2(fhalf, w1, b1, f1, w2, b2, f2, w3, *, length):
    w1t = jnp.pad(w1.astype(F32).T, ((0, 0), (0, 40 - HY_EMB)))
    w3t = w3.astype(F32).T
    cols = jnp.stack([b1, f1, b2, f2] + [jnp.zeros_like(b1)] * 4, axis=1).astype(F32)
    n_rows = HY_ORDER * HY_C
    half = length // 2
    const = lambda rc, jt: (0, 0)
    out = jax.ShapeDtypeStruct((n_rows, length), F32)
    return pl.pallas_call(
        functools.partial(_hy_filter2_kernel, length=length),
        out_shape=(out, out), grid=(n_rows // HYF_RB, half // DFT_HALF),
        in_specs=[pl.BlockSpec(w1t.shape, const), pl.BlockSpec((w2.shape[1], w2.shape[0]), const),
                  pl.BlockSpec((HYF_RB, w3t.shape[1]), lambda rc, jt: (rc, 0)),
                  pl.BlockSpec((HYF_RB, w3t.shape[1]), lambda rc, jt: (n_rows // HYF_RB + rc, 0)),
                  pl.BlockSpec(cols.shape, const),
                  pl.BlockSpec((half, 2 * DFT_HALF), lambda rc, jt: (0, jt))],
        out_specs=(pl.BlockSpec((HYF_RB, 2 * DFT_HALF), lambda rc, jt: (rc, jt)),) * 2,
        scratch_shapes=[pltpu.VMEM((HYF_RB, half), BF16)] * 4,
        compiler_params=pltpu.CompilerParams(dimension_semantics=("arbitrary", "arbitrary"),
                                             vmem_limit_bytes=VMEM_LIMIT),
        name="hy_filter2",
    )(w1t, w2.astype(F32).T, w3t, w3t, cols, fhalf)


def _hy_fwd2_kernel(z_ref, f_ref, ka_ref, kb_ref, p_ref, q_ref, eo_ref, *, length, ft):
    jt = pl.program_id(1)
    half_tile = HY_TILE // 2

    @pl.when(jt == 0)
    def _():
        for j in range(length // HY_TILE):
            dst = slice(j * half_tile, (j + 1) * half_tile)
            eo_ref[:HY_C, dst] = z_ref[:, j * HY_TILE:j * HY_TILE + half_tile]
            eo_ref[HY_C:, dst] = z_ref[:, j * HY_TILE + half_tile:(j + 1) * HY_TILE]

    for u in range(ft):
        re = slice(2 * u * DFT_HALF, (2 * u + 1) * DFT_HALF)
        im = slice((2 * u + 1) * DFT_HALF, (2 * u + 2) * DFT_HALF)
        r = jnp.dot(eo_ref[...], f_ref[:, 2 * u * DFT_HALF:(2 * u + 2) * DFT_HALF],
                    preferred_element_type=F32)
        er, ei = r[:HY_C, :DFT_HALF], r[:HY_C, DFT_HALF:]
        tr, ti = _twiddle(jt * ft + u, 2 * length)
        pr, pi = _cmul(tr, ti, r[HY_C:, :DFT_HALF], r[HY_C:, DFT_HALF:])
        yar, yai = _cmul(er + pr, ei + pi, ka_ref[:, re], ka_ref[:, im])
        ybr, ybi = _cmul(er - pr, ei - pi, kb_ref[:, re], kb_ref[:, im])
        p_ref[:, re] = (yar + ybr).astype(BF16)
        p_ref[:, im] = (yai + ybi).astype(BF16)
        qr, qi = _cmul(yar - ybr, yai - ybi, tr, -ti)
        q_ref[:, re] = qr.astype(BF16)
        q_ref[:, im] = qi.astype(BF16)


HY_FT = 2


def _hy_fwd2(zb, fhalf, kspec, order, *, lane0, n_seq, length):
    sb0 = lane0 // length
    half = length // 2
    ft = math.gcd(HY_FT, half // DFT_HALF)
    width = 2 * DFT_HALF * ft
    out = jax.ShapeDtypeStruct((n_seq * HY_C, length), BF16)
    tile = pl.BlockSpec((HY_C, width), lambda s, jt: (s, jt))
    kblk = pl.BlockSpec((HY_C, width), lambda s, jt: (order, jt))
    return pl.pallas_call(
        functools.partial(_hy_fwd2_kernel, length=length, ft=ft),
        out_shape=(out, out), grid=(n_seq, 2 * half // width),
        in_specs=[pl.BlockSpec((HY_C, length), lambda s, jt: (0, sb0 + s)),
                  pl.BlockSpec((half, width), lambda s, jt: (0, jt)), kblk, kblk],
        out_specs=(tile, tile),
        scratch_shapes=[pltpu.VMEM((2 * HY_C, half), BF16)],
        compiler_params=pltpu.CompilerParams(dimension_semantics=("arbitrary", "arbitrary"),
                                             vmem_limit_bytes=VMEM_LIMIT),
        name="hy_fwd2",
    )(zb, fhalf, kspec[0], kspec[1])


def _hy_inv2_kernel(p_ref, q_ref, f_ref, gate_ref, z_ref, bias_ref, *rest, with_prev, with_bf16):
    rest = list(rest)
    if with_prev:
        rest.pop(0)
    o_ref = rest.pop(0)
    pq = jnp.concatenate([p_ref[...], q_ref[...]], axis=0)
    y = lax.dot_general(pq, f_ref[...], (((1,), (1,)), ((), ())), preferred_element_type=F32)
    y = jnp.concatenate([y[:HY_C], y[HY_C:]], axis=1)
    out = gate_ref[...] * (y + z_ref[...] * bias_ref[...])
    o_ref[...] = out
    if with_bf16:
        rest.pop(0)[...] = out.astype(BF16)


def _hy_inv2(pq, fhalf, gate_arr, gate_rb, z_arr, z_rb, bias, prev, *, lane0, n_seq, length, t_total,
             with_bf16):
    nt = length // HY_TILE
    lane_blk = lambda s, j: lane0 // HY_TILE + s * nt + j
    spec = pl.BlockSpec((HY_C, length), lambda s, j: (s, 0))
    in_specs = [spec, spec, pl.BlockSpec((HY_TILE // 2, length), lambda s, j: (j, 0)),
                pl.BlockSpec((HY_C, HY_TILE), lambda s, j: (gate_rb, lane_blk(s, j))),
                pl.BlockSpec((HY_C, HY_TILE), lambda s, j: (z_rb, lane_blk(s, j))),
                pl.BlockSpec((HY_C, 1), lambda s, j: (0, 0))]
    args = [pq[0], pq[1], fhalf, gate_arr, z_arr, bias.astype(F32).reshape(HY_C, 1)]
    aliases = {}
    if prev is not None:
        in_specs.append(pl.BlockSpec(memory_space=pl.ANY))
        aliases = {len(args): 0}
        args.append(prev)
    out_spec = pl.BlockSpec((HY_C, HY_TILE), lambda s, j: (0, lane_blk(s, j)))
    out_shape = [jax.ShapeDtypeStruct((HY_C, t_total), F32)]
    out_specs = [out_spec]
    if with_bf16:
        out_shape.append(jax.ShapeDtypeStruct((HY_C, t_total), BF16))
        out_specs.append(out_spec)
    return pl.pallas_call(
        functools.partial(_hy_inv2_kernel, with_prev=prev is not None, with_bf16=with_bf16),
        out_shape=tuple(out_shape), grid=(n_seq, nt), in_specs=in_specs, out_specs=tuple(out_specs),
        input_output_aliases=aliases,
        compiler_params=pltpu.CompilerParams(dimension_semantics=("arbitrary", "arbitrary"),
                                             vmem_limit_bytes=VMEM_LIMIT),
        name="hy_inv2",
    )(*args)


def _hyena_group2(ut, vb, fhalf, kspec, hy_bias, y_prev, *, lane0, n_seq, length, t_total):
    geo = dict(lane0=lane0, n_seq=n_seq, length=length)
    pq = _hy_fwd2(vb, fhalf, kspec, 0, **geo)
    z2, z2b = _hy_inv2(pq, fhalf, ut, 0, ut, 2, hy_bias[0], None, t_total=t_total, with_bf16=True, **geo)
    pq = _hy_fwd2(z2b, fhalf, kspec, 1, **geo)
    return _hy_inv2(pq, fhalf, ut, 1, z2, 0, hy_bias[1], y_prev, t_total=t_total, with_bf16=False, **geo)[0]


def _ah_outproj_kernel(x_ref, a_ref, yt_ref, mod_ref, wa_ref, wy_ref, perm_ref, o_ref, *, n_tiles_p):
    o = jnp.dot(a_ref[...], wa_ref[...], preferred_element_type=F32)
    yb = yt_ref[...].astype(BF16)
    nat = jnp.dot(yb, perm_ref[...], preferred_element_type=F32).astype(BF16)
    yb = jnp.where(pl.program_id(0) < n_tiles_p, yb, nat)
    o = o + lax.dot_general(yb, wy_ref[...], (((0,), (0,)), ((), ())), preferred_element_type=F32)
    o_ref[...] = x_ref[...] + mod_ref[0][5:6] * o


def _ah_outproj(x, a, yt, mod, row_of_tile, w_out, *, tm, tp):
    t = x.shape[0]
    tile = lambda i: (i, 0)
    const = lambda i: (0, 0)
    wa, wy = w_out[:AH_Q].astype(BF16), w_out[AH_Q:].astype(BF16)
    perm = _split_perm(tm)
    return pl.pallas_call(
        functools.partial(_ah_outproj_kernel, n_tiles_p=tp // tm),
        out_shape=jax.ShapeDtypeStruct((t, D_MODEL), F32), grid=(t // tm,),
        in_specs=[pl.BlockSpec((tm, D_MODEL), tile), pl.BlockSpec((tm, AH_Q), tile),
                  pl.BlockSpec((HY_C, tm), lambda i: (0, i)),
                  pl.BlockSpec((1, 9, D_MODEL), lambda i: (row_of_tile(i), 0, 0)),
                  pl.BlockSpec(wa.shape, const), pl.BlockSpec(wy.shape, const),
                  pl.BlockSpec((tm, tm), const)],
        out_specs=pl.BlockSpec((tm, D_MODEL), tile),
        compiler_params=pltpu.CompilerParams(dimension_semantics=("arbitrary",),
                                             vmem_limit_bytes=VMEM_LIMIT),
        name="ah_outproj",
    )(x, a, yt, mod, wa, wy, perm)


def _ah_layer(x, mod, row_of_tile, g, w_in, w_out, q_norm, k_norm, sink, conv_w, conv_b, hy_bias,
              ck, cv, rope, fmats, kspecs, *, tm, tq, tp, lp, ls):
    t = x.shape[0]
    q, kt, vt, kn, vn, ut, vb = _ah_inproj(x, mod, row_of_tile, g, w_in, q_norm, k_norm, conv_w, conv_b, rope,
                                           tm=tm, tp=tp, lp=lp, ls=ls)
    a = _attention(q, kt, vt, ck, cv, sink, tq=tq, tp=tp, lp=lp, ls=ls)
    yt = _hyena_group(ut, vb, fmats[0], kspecs[0], hy_bias, None, lane0=0, n_seq=tp // lp, length=lp, t_total=t)
    yt = _hyena_group2(ut, vb, fmats[1], kspecs[1], hy_bias, yt, lane0=tp, n_seq=(t - tp) // ls, length=ls,
                       t_total=t)
    return _ah_outproj(x, a, yt, mod, row_of_tile, w_out, tm=tm, tp=tp), kn[:tp], vn[:tp]


def _rep4_ctx(c):
    b, s = c.shape[:2]
    return jnp.broadcast_to(c[:, :, :, None, :], (b, s, KVH_A, G_A, HD_A)).reshape(b, s, AH_Q).astype(BF16)


def _ada_kernel(c_ref, w_ref, b_ref, o_ref):
    s = jax.nn.silu(c_ref[...]).astype(BF16)
    o_ref[0] = jnp.dot(s, w_ref[0].astype(BF16), preferred_element_type=F32) + b_ref[0]


def _ada_mod(cond, ada_w, ada_b):
    depth, _, n = ada_w.shape
    rows = 16
    cp = jnp.pad(cond.astype(F32), ((0, rows - cond.shape[0]), (0, 0)))
    out = pl.pallas_call(
        _ada_kernel, out_shape=jax.ShapeDtypeStruct((depth, rows, n), F32),
        grid=(depth, n // D_MODEL),
        in_specs=[pl.BlockSpec((rows, D_MODEL), lambda l, j: (0, 0)),
                  pl.BlockSpec((1, D_MODEL, D_MODEL), lambda l, j: (l, 0, j)),
                  pl.BlockSpec((1, 1, D_MODEL), lambda l, j: (l, 0, j))],
        out_specs=pl.BlockSpec((1, rows, D_MODEL), lambda l, j: (l, 0, j)),
        compiler_params=pltpu.CompilerParams(dimension_semantics=("arbitrary", "arbitrary"),
                                             vmem_limit_bytes=VMEM_LIMIT),
        name="ada_mod",
    )(cp, ada_w, ada_b.reshape(depth, 1, n))
    return out.reshape(depth, rows, n // D_MODEL, D_MODEL)


def kernel(x_prompt, x_sample, cache_k, cache_v, state_fwd, state_bwd, c, c_ctx, norm_g, ada_w, ada_b, ffn_w13, ffn_w2, mx_w_in, mx_w_out, q_norm, k_norm, attn_sink, hy_conv_w, hy_conv_b, hy_w1, hy_b1, hy_freq1, hy_w2, hy_b2, hy_freq2, hy_w3, hy_bias, dn_w_in, dn_w_out, dn_conv_w, dn_a_log, dn_dt_bias, dn_norm_g):
    bp, lp, _ = x_prompt.shape
    bs, ls, _ = x_sample.shape
    tp, ts = bp * lp, bs * ls
    assert tp % ls == 0
    tm = math.gcd(TOKEN_TILE, math.gcd(tp, ls))
    ct = math.gcd(DN_BLOCK, math.gcd(lp, ls))
    tq = math.gcd(ATTN_TQ, ls)
    cond = jnp.concatenate([c, c_ctx[None, :]], axis=0)
    mods = _ada_mod(cond, ada_w, ada_b)
    tiles_p, tiles_per_s = tp // tm, ls // tm

    def row_of_tile(i):
        return jnp.where(i < tiles_p, bs, (i - tiles_p) // tiles_per_s)

    rope = _rope_tables(ls)
    assert tm == HY_TILE and ls % tm == 0 and (ls // 2) % DFT_HALF == 0
    fmats = (_dft_matrix(lp), _dft_matrix(ls // 2))
    new_k, new_v, new_sf, new_sb = [], [], [], []
    for layer in range(DEPTH):
        mod = mods[layer]
        i = layer // 2
        w_a = _ffn_weights(ffn_w13[layer, 0], ffn_w2[layer, 0])
        if layer == 0:
            x = _ffn(x_prompt.reshape(tp, D_MODEL), mod, row_of_tile, norm_g[layer, 0], w_a, 0, tm=tm,
                     out_rows=tp + ts)
            x = _ffn(x_sample.reshape(ts, D_MODEL), mod, row_of_tile, norm_g[layer, 0], w_a, 0, tm=tm,
                     out_rows=tp + ts, out_tile0=tiles_p, prev=x)
        else:
            x = _ffn(x, mod, row_of_tile, norm_g[layer, 0], w_a, 0, tm=tm)
        if layer % 2 == 0:
            kspecs = tuple(fn(f, hy_w1[i], hy_b1[i], hy_freq1[i], hy_w2[i], hy_b2[i], hy_freq2[i], hy_w3[i],
                              length=n) for fn, f, n in zip((_hy_filter, _hy_filter2), fmats, (lp, ls)))
            x, k_p, v_p = _ah_layer(x, mod, row_of_tile, norm_g[layer, 1], mx_w_in[i], mx_w_out[i], q_norm[i],
                                    k_norm[i], attn_sink[i], hy_conv_w[i], hy_conv_b[i], hy_bias[i],
                                    _rep4_ctx(cache_k[:, i]), _rep4_ctx(cache_v[:, i]), rope, fmats, kspecs,
                                    tm=tm, tq=tq, tp=tp, lp=lp, ls=ls)
            new_k.append(k_p.reshape(bp, lp, KVH_A, HD_A))
            new_v.append(v_p.reshape(bp, lp, KVH_A, HD_A))
        else:
            x, s_f, s_b = _dn_layer(x, mod, row_of_tile, norm_g[layer, 1], dn_w_in[i], dn_w_out[i],
                                    dn_conv_w[i], dn_a_log[i], dn_dt_bias[i], dn_norm_g[i],
                                    state_fwd[:, i], state_bwd[:, i], tm=tm, ct=ct, tp=tp, lp=lp, ls=ls)
            new_sf.append(s_f)
            new_sb.append(s_b)
        w_b = _ffn_weights(ffn_w13[layer, 1], ffn_w2[layer, 1])
        if layer < DEPTH - 1:
            x = _ffn(x, mod, row_of_tile, norm_g[layer, 2], w_b, 2, tm=tm)
        else:
            y_p = _ffn(x, mod, row_of_tile, norm_g[layer, 2], w_b, 2, tm=tm, n_tiles=tiles_p, out_rows=tp)
            y_s = _ffn(x, mod, row_of_tile, norm_g[layer, 2], w_b, 2, tm=tm, in_tile0=tiles_p,
                       n_tiles=ts // tm, out_rows=ts)

    return (y_p.reshape(bp, lp, D_MODEL), y_s.reshape(bs, ls, D_MODEL),
            jnp.stack(new_k, axis=1), jnp.stack(new_v, axis=1),
            jnp.stack(new_sf, axis=1), jnp.stack(new_sb, axis=1))
```

```python
import functools
import math

import jax
import jax.numpy as jnp
from jax import lax
from jax.experimental import pallas as pl
from jax.experimental.pallas import tpu as pltpu

D_MODEL = 1024
DEPTH = 4
GRID_W = 64
H_A = 8
KVH_A = 2
G_A = H_A // KVH_A
HD_A = 64
WINDOW = 128
ATTN_BLOCK = 128
ROPE_THETA = 10000.0
ROPE_AXIS_DIM = HD_A // 2
HY_C = 512
HY_ORDER = 2
HY_EMB = 33
HY_BANDS = (HY_EMB - 1) // 2
HY_MIN_DECAY = math.log(1e-2) / 1.5
HY_MAX_DECAY = math.log(1e-2) / 0.3
H_C = 8
DK_C = 128
DV_C = 128
DN_CHUNK = 64
DN_OUT = H_C * DV_C
D_FF = 2816
NORM_EPS = 1e-6
NEG_INF = -1e30

F32 = jnp.float32
BF16 = jnp.bfloat16

TOKEN_TILE = 512
FFN_TILE = 512
FFN_CHUNK = 256
DN_BLOCK = 128
ATTN_TQ = 256
CTX_SEQS = 4
VMEM_LIMIT = 56 * 1024 * 1024


def _ffn_kernel(x_ref, mod_ref, g_ref, w1_ref, w3_ref, w2_ref, *rest, j, n_chunks, has_prev):
    o_ref, acc_ref = rest[1:] if has_prev else rest
    x = x_ref[...]
    y = x * lax.rsqrt(jnp.mean(x * x, axis=-1, keepdims=True) + NORM_EPS) * g_ref[...]
    m = mod_ref[0]
    shift, scale, gate = m[3 * j:3 * j + 1], m[3 * j + 1:3 * j + 2], m[3 * j + 2:3 * j + 3]
    h = (y * (1.0 + scale) + shift).astype(BF16)
    acc_ref[...] = jnp.zeros_like(acc_ref)

    def body(c, carry):
        gt = jnp.dot(h, w1_ref[c], preferred_element_type=F32)
        up = jnp.dot(h, w3_ref[c], preferred_element_type=F32)
        a = (jax.nn.silu(gt) * up).astype(BF16)
        acc_ref[...] += jnp.dot(a, w2_ref[c], preferred_element_type=F32)
        return carry

    lax.fori_loop(0, n_chunks, body, 0, unroll=True)
    o_ref[...] = x + 0.5 * gate * acc_ref[...]


def _ffn_weights(w13, w2):
    n_chunks = D_FF // FFN_CHUNK
    w1 = w13[:, :D_FF].astype(BF16).reshape(D_MODEL, n_chunks, FFN_CHUNK).transpose(1, 0, 2)
    w3 = w13[:, D_FF:].astype(BF16).reshape(D_MODEL, n_chunks, FFN_CHUNK).transpose(1, 0, 2)
    return w1, w3, w2.astype(BF16).reshape(n_chunks, FFN_CHUNK, D_MODEL)


def _ffn(x, mod, row_of_tile, g, weights, j, *, tm, in_tile0=0, n_tiles=None, out_rows=None, out_tile0=0,
         prev=None):
    n_tiles = x.shape[0] // tm if n_tiles is None else n_tiles
    out_rows = x.shape[0] if out_rows is None else out_rows
    n_chunks = D_FF // FFN_CHUNK
    joint0 = max(in_tile0, out_tile0)
    const3 = lambda i: (0, 0, 0)
    in_specs = [
        pl.BlockSpec((tm, D_MODEL), lambda i: (in_tile0 + i, 0)),
        pl.BlockSpec((1, 9, D_MODEL), lambda i: (row_of_tile(joint0 + i), 0, 0)),
        pl.BlockSpec((1, D_MODEL), lambda i: (0, 0)),
        pl.BlockSpec((n_chunks, D_MODEL, FFN_CHUNK), const3),
        pl.BlockSpec((n_chunks, D_MODEL, FFN_CHUNK), const3),
        pl.BlockSpec((n_chunks, FFN_CHUNK, D_MODEL), const3),
    ]
    args = [x, mod, g.reshape(1, D_MODEL), *weights]
    aliases = {}
    if prev is not None:
        in_specs.append(pl.BlockSpec(memory_space=pl.ANY))
        aliases = {len(args): 0}
        args.append(prev)
    return pl.pallas_call(
        functools.partial(_ffn_kernel, j=j, n_chunks=n_chunks, has_prev=prev is not None),
        out_shape=jax.ShapeDtypeStruct((out_rows, D_MODEL), F32),
        grid=(n_tiles,), in_specs=in_specs,
        out_specs=pl.BlockSpec((tm, D_MODEL), lambda i: (out_tile0 + i, 0)),
        scratch_shapes=[pltpu.VMEM((tm, D_MODEL), F32)],
        input_output_aliases=aliases,
        compiler_params=pltpu.CompilerParams(dimension_semantics=("arbitrary",),
                                             vmem_limit_bytes=VMEM_LIMIT),
        name=f"ffn{j}",
    )(*args)


def _bdot(a, b):
    return jnp.dot(a.astype(BF16), b.astype(BF16), preferred_element_type=F32)


def _bdot_nt(a, b):
    return lax.dot_general(a.astype(BF16), b.astype(BF16), (((1,), (1,)), ((), ())),
                           preferred_element_type=F32)


def _bdot_tn(a, b):
    return lax.dot_general(a.astype(BF16), b.astype(BF16), (((0,), (0,)), ((), ())),
                           preferred_element_type=F32)


def _split3(x):
    hi = x.astype(BF16)
    r1 = x - hi.astype(F32)
    mid = r1.astype(BF16)
    lo = (r1 - mid.astype(F32)).astype(BF16)
    return hi, mid, lo


def _ada_h(x, g_row, m, j):
    y = x * lax.rsqrt(jnp.mean(x * x, axis=-1, keepdims=True) + NORM_EPS) * g_row
    return y * (1.0 + m[3 * j + 1:3 * j + 2]) + m[3 * j:3 * j + 1]


def _seq_edges(i, tm, n_tiles_p, tp, lp, ls, shape, axis):
    is_p = i < n_tiles_p
    seq_len = jnp.where(is_p, lp, ls)
    t0 = i * tm - jnp.where(is_p, 0, tp)
    base = lax.rem(t0, seq_len)
    pos = (base + lax.broadcasted_iota(jnp.int32, shape, axis)).astype(F32)
    lf = seq_len.astype(F32)
    rem = pos - jnp.floor((pos + 0.5) / lf) * lf
    return rem == 0.0, rem == lf - 1.0


def _halo_specs(tm, t, tile0=0):
    nb = t // 8
    prev = pl.BlockSpec((8, D_MODEL), lambda i: (jnp.maximum((tile0 + i) * (tm // 8) - 1, 0), 0))
    nxt = pl.BlockSpec((8, D_MODEL), lambda i: (jnp.minimum((tile0 + i + 1) * (tm // 8), nb - 1), 0))
    return prev, nxt


DN_QKV = 2 * H_C * DK_C + H_C * DV_C
DN_CW = 512


def _dn_inproj_kernel(xp_ref, x_ref, xn_ref, mod_ref, g_ref, w_ref, cw_ref, ab_ref,
                      q_ref, k_ref, v_ref, z_ref, bg_ref, pext_ref, *, tm, n_tiles_p, tp, lp, ls):
    i = pl.program_id(0)
    xe = jnp.concatenate([xp_ref[...], x_ref[...], xn_ref[...]], axis=0)
    h = _ada_h(xe, g_ref[...], mod_ref[0], 1).astype(BF16)
    first, last = _seq_edges(i, tm, n_tiles_p, tp, lp, ls, (tm, 1), 0)
    nq = H_C * DK_C
    for c0 in range(0, DN_QKV, DN_CW):
        pext_ref[...] = jnp.dot(h, w_ref[:, c0:c0 + DN_CW], preferred_element_type=F32)
        cw = cw_ref[:, c0:c0 + DN_CW]
        c = (jnp.where(first, 0.0, pext_ref[7:tm + 7, :]) * cw[0:1]
             + pext_ref[8:tm + 8, :] * cw[1:2]
             + jnp.where(last, 0.0, pext_ref[9:tm + 9, :]) * cw[2:3])
        a = jax.nn.silu(c)
        for b0 in range(0, DN_CW, DK_C):
            col = c0 + b0
            blk = a[:, b0:b0 + DK_C]
            if col < 2 * nq:
                blk = blk * lax.rsqrt(jnp.sum(blk * blk, axis=-1, keepdims=True) + NORM_EPS)
            if col < nq:
                q_ref[:, col:col + DK_C] = (blk * (DK_C ** -0.5)).astype(BF16)
            elif col < 2 * nq:
                k_ref[:, col - nq:col - nq + DK_C] = blk.astype(BF16)
            else:
                v_ref[:, col - 2 * nq:col - 2 * nq + DK_C] = blk.astype(BF16)
    hm = h[8:tm + 8]
    for c0 in range(0, DN_OUT, DN_CW):
        z_ref[:, c0:c0 + DN_CW] = jnp.dot(hm, w_ref[:, DN_QKV + c0:DN_QKV + c0 + DN_CW],
                                           preferred_element_type=F32).astype(BF16)
    r = jnp.dot(hm, w_ref[:, DN_QKV + DN_OUT:], preferred_element_type=F32)
    lane = lax.broadcasted_iota(jnp.int32, r.shape, 1)
    xs = r + ab_ref[1:2]
    softplus = jnp.maximum(xs, 0.0) + jnp.log1p(jnp.exp(-jnp.abs(xs)))
    bg_ref[...] = jnp.where(lane < 2 * H_C, jax.nn.sigmoid(r),
                            jnp.where(lane < 4 * H_C, -ab_ref[0:1] * softplus, 0.0))


def _dn_inproj(x, mod, row_of_tile, g, w_in, conv_w, a_log, dt_bias, *, tm, tp, lp, ls):
    t = x.shape[0]
    pad = 128 - 4 * H_C
    w = jnp.pad(w_in, ((0, 0), (0, pad))).astype(BF16)
    ab = jnp.zeros((2, 128), F32)
    ab = ab.at[0, 2 * H_C:4 * H_C].set(jnp.exp(a_log.astype(F32)).reshape(-1))
    ab = ab.at[1, 2 * H_C:4 * H_C].set(dt_bias.astype(F32).reshape(-1))
    prev, nxt = _halo_specs(tm, t)
    tile = lambda i: (i, 0)
    const = lambda i: (0, 0)
    wide = jax.ShapeDtypeStruct((t, DN_OUT), BF16)
    return pl.pallas_call(
        functools.partial(_dn_inproj_kernel, tm=tm, n_tiles_p=tp // tm, tp=tp, lp=lp, ls=ls),
        out_shape=(wide, wide, wide, wide, jax.ShapeDtypeStruct((t, 128), F32)),
        grid=(t // tm,),
        in_specs=[prev, pl.BlockSpec((tm, D_MODEL), tile), nxt,
                  pl.BlockSpec((1, 9, D_MODEL), lambda i: (row_of_tile(i), 0, 0)),
                  pl.BlockSpec((1, D_MODEL), const),
                  pl.BlockSpec(w.shape, const),
                  pl.BlockSpec(conv_w.shape, const),
                  pl.BlockSpec((2, 128), const)],
        out_specs=(pl.BlockSpec((tm, DN_OUT), tile),) * 4 + (pl.BlockSpec((tm, 128), tile),),
        scratch_shapes=[pltpu.VMEM((tm + 16, DN_CW), F32)],
        compiler_params=pltpu.CompilerParams(dimension_semantics=("arbitrary",),
                                             vmem_limit_bytes=VMEM_LIMIT),
        name="dn_inproj",
    )(x, x, x, mod, g.reshape(1, D_MODEL), w, conv_w.astype(F32), ab)


DN_GROUP = 4


def _dn_masks(rev):
    n = DN_CHUNK
    r_i = lax.broadcasted_iota(jnp.int32, (n, n), 0)
    c_i = lax.broadcasted_iota(jnp.int32, (n, n), 1)
    tri = jnp.where((r_i <= c_i) if rev else (r_i >= c_i), 1.0, 0.0).astype(BF16)
    rows = DN_GROUP * n
    rr = lax.broadcasted_iota(jnp.int32, (rows, rows), 0)
    cc = lax.broadcasted_iota(jnp.int32, (rows, rows), 1)
    same = (rr // n) == (cc // n)
    incl = same & ((rr <= cc) if rev else (rr >= cc))
    strict = same & ((rr < cc) if rev else (rr > cc))
    eye = jnp.where(rr == cc, 1.0, 0.0)
    return tri, incl, strict, eye


def _dn_prepare(qc, kc, vc, bgc, masks, rev):
    n = DN_CHUNK
    tri, incl, strict, eye = masks
    gct = sum(jnp.dot(tri, p, preferred_element_type=F32) for p in _split3(bgc))
    rows = DN_GROUP * n
    lane = lax.broadcasted_iota(jnp.int32, (rows, 128), 1)
    last = 0 if rev else n - 1
    prep = []
    for gi in range(H_C // DN_GROUP):
        heads = range(gi * DN_GROUP, (gi + 1) * DN_GROUP)
        sb = [(H_C if rev else 0) + h for h in heads]
        sg = [2 * H_C + (H_C if rev else 0) + h for h in heads]
        beta = jnp.concatenate([bgc[:, s:s + 1] for s in sb], axis=0)
        gc = jnp.concatenate([gct[:, s:s + 1] for s in sg], axis=0)
        gl = jnp.concatenate([jnp.broadcast_to(gct[last:last + 1, s:s + 1], (n, 1)) for s in sg], axis=0)
        q4 = jnp.concatenate([qc[:, h * DK_C:(h + 1) * DK_C] for h in heads], axis=0)
        k4 = jnp.concatenate([kc[:, h * DK_C:(h + 1) * DK_C] for h in heads], axis=0)
        v4 = jnp.concatenate([vc[:, h * DV_C:(h + 1) * DV_C] for h in heads], axis=0)
        hi, mid, lo = (p.astype(F32) for p in _split3(gc))
        one = jnp.where(lane < 6, 1.0, 0.0)
        u_m = jnp.where(lane == 0, hi, jnp.where(lane == 1, mid, jnp.where(lane == 2, lo, one)))
        v_m = jnp.where(lane == 3, -hi, jnp.where(lane == 4, -mid, jnp.where(lane == 5, -lo, one)))
        gd = _bdot_nt(jnp.where(lane < 6, u_m, 0.0), jnp.where(lane < 6, v_m, 0.0))
        decay = jnp.where(incl, jnp.exp(jnp.where(incl, gd, 0.0)), 0.0)
        k4f = k4.astype(F32)
        kb = k4f * beta
        lm = jnp.where(strict, _bdot_nt(kb, k4) * decay, 0.0)
        x = jnp.concatenate([v4.astype(F32) * beta, kb * jnp.exp(gc)], axis=1)
        qe = q4.astype(F32) * jnp.exp(gc)
        ke = (k4f * jnp.exp(gl - gc)).astype(BF16)
        aqk = jnp.where(incl, _bdot_nt(q4, k4) * decay, 0.0).astype(BF16)
        prep.append(dict(lm=lm, x=x, qe=qe, ke=ke, aqk=aqk, egl=jnp.exp(gl), rev=rev))
    return prep


def _dn_solve(groups, eye):
    n = DN_CHUNK
    dot = functools.partial(jnp.dot, preferred_element_type=F32)
    for g in groups:
        g["lh"] = g["lm"].astype(BF16)
        g["p"] = -g["lh"]
        g["t"] = eye - g["lm"]
    def rows(m, lo, hi):
        return jnp.concatenate([m[b * n + lo:b * n + hi] for b in range(DN_GROUP)], axis=0)

    def put(upd, lo, hi):
        k = hi - lo
        parts = []
        for b in range(DN_GROUP):
            parts += [jnp.zeros((lo, upd.shape[1]), upd.dtype), upd[b * k:(b + 1) * k],
                      jnp.zeros((n - hi, upd.shape[1]), upd.dtype)]
        return jnp.concatenate([p for p in parts if p.shape[0]], axis=0)

    for s in range(1, 6):
        w = 2 ** s
        for g in groups:
            g["win"] = ((0, n - w) if g["rev"] else (w, n)) if w >= 16 else None
        for g in groups:
            if g["win"]:
                g["p"] = put(dot(rows(g["p"], *g["win"]), g["p"]).astype(BF16), *g["win"])
            else:
                g["p"] = dot(g["p"], g["p"]).astype(BF16)
        for g in groups:
            tb = g["t"].astype(BF16)
            if g["win"]:
                g["t"] = g["t"] + put(dot(rows(tb, *g["win"]), g["p"]), *g["win"])
            else:
                g["t"] = g["t"] + dot(tb, g["p"])
    for g in groups:
        g["tb"] = g["t"].astype(BF16)
        g["x0"] = dot(g["tb"], g["x"].astype(BF16))
    for g in groups:
        g["res"] = g["x"] - g["x0"] - dot(g["lh"], g["x0"].astype(BF16))
    out = []
    for g in groups:
        x = g["x0"] + dot(g["tb"], g["res"].astype(BF16))
        u4, w4, qe = x[:, :DV_C], x[:, DV_C:], g["qe"]
        wqe = [jnp.concatenate([w4[j * n:(j + 1) * n], qe[j * n:(j + 1) * n]], axis=0).astype(BF16)
               for j in range(DN_GROUP)]
        out.append((u4, wqe, g["ke"], g["aqk"], g["egl"]))
    return out


def _dn_advance(prep, s_ref):
    n = DN_CHUNK
    outs = []
    for gi, (u4, wqe, ke, aqk, egl) in enumerate(prep):
        vn, qs = [], []
        for j in range(DN_GROUP):
            wq = jnp.dot(wqe[j], s_ref[gi * DN_GROUP + j].astype(BF16), preferred_element_type=F32)
            vn.append(u4[j * n:(j + 1) * n] - wq[:n])
            qs.append(wq[n:])
        o4 = jnp.concatenate(qs, axis=0) + jnp.dot(aqk, jnp.concatenate(vn, axis=0).astype(BF16),
                                                   preferred_element_type=F32)
        for j in range(DN_GROUP):
            h = gi * DN_GROUP + j
            sl = slice(j * n, (j + 1) * n)
            s_ref[h] = s_ref[h] * egl[j * n:j * n + 1] + lax.dot_general(
                ke[sl], vn[j].astype(BF16), (((0,), (0,)), ((), ())), preferred_element_type=F32)
            outs.append(o4[sl])
    return outs


def _dn_scan_kernel(*refs, n_chunks, blocks_per_seq, zero_init, has_prev, write_state):
    refs = list(refs)
    fwd_in, bwd_in = refs[:4], refs[4:8]
    del refs[:8]
    s0f_ref, s0b_ref = (None, None) if zero_init else (refs.pop(0), refs.pop(0))
    if has_prev:
        del refs[:2]
    of_ref, ob_ref = refs.pop(0), refs.pop(0)
    sfo_ref, sbo_ref = (refs.pop(0), refs.pop(0)) if write_state else (None, None)
    sf_ref, sb_ref = refs
    i = pl.program_id(0)
    f_in_seq = lax.rem(i, blocks_per_seq)
    r_in_seq = lax.rem(pl.num_programs(0) - 1 - i, blocks_per_seq)

    @pl.when(f_in_seq == 0)
    def _():
        sf_ref[...] = jnp.zeros_like(sf_ref) if zero_init else s0f_ref[0]

    @pl.when(r_in_seq == blocks_per_seq - 1)
    def _():
        sb_ref[...] = jnp.zeros_like(sb_ref) if zero_init else s0b_ref[0]

    def chunk(io, c):
        rows = slice(c * DN_CHUNK, (c + 1) * DN_CHUNK)
        return [r[rows, :] for r in io]

    mf, mb = _dn_masks(False), _dn_masks(True)
    raw = [_dn_prepare(*chunk(fwd_in, c), mf, False) for c in range(n_chunks)]
    raw += [_dn_prepare(*chunk(bwd_in, c), mb, True) for c in range(n_chunks)]
    n_groups = H_C // DN_GROUP
    solved = _dn_solve([g for r in raw for g in r], mf[3])
    per_chunk = [solved[j * n_groups:(j + 1) * n_groups] for j in range(2 * n_chunks)]
    prep_f, prep_b = per_chunk[:n_chunks], per_chunk[n_chunks:]
    for step in range(n_chunks):
        for o_ref, prep, s_ref, c in ((of_ref, prep_f, sf_ref, step),
                                      (ob_ref, prep_b, sb_ref, n_chunks - 1 - step)):
            for h, o in enumerate(_dn_advance(prep[c], s_ref)):
                o_ref[c * DN_CHUNK:(c + 1) * DN_CHUNK, h * DV_C:(h + 1) * DV_C] = o.astype(BF16)

    if write_state:
        @pl.when(f_in_seq == blocks_per_seq - 1)
        def _():
            sfo_ref[0] = sf_ref[...]

        @pl.when(r_in_seq == 0)
        def _():
            sbo_ref[0] = sb_ref[...]


def _dn_scan(q, k, v, bg, s0_f, s0_b, prev, *, row0, n_seq, seq_len, ct, write_state):
    t = q.shape[0]
    bps = seq_len // ct
    nblk = n_seq * bps
    b0 = row0 // ct
    zero_init = s0_f is None
    fwd = lambda i: (b0 + i, 0)
    bwd = lambda i: (b0 + nblk - 1 - i, 0)
    seq_f = lambda i: (i // bps, 0, 0, 0)
    seq_b = lambda i: ((nblk - 1 - i) // bps, 0, 0, 0)
    in_specs, args = [], []
    for rows in (fwd, bwd):
        in_specs += [pl.BlockSpec((ct, DN_OUT), rows)] * 3 + [pl.BlockSpec((ct, 128), rows)]
        args += [q, k, v, bg]
    state_blk = (1, H_C, DK_C, DV_C)
    if not zero_init:
        in_specs += [pl.BlockSpec(state_blk, seq_f), pl.BlockSpec(state_blk, seq_b)]
        args += [s0_f, s0_b]
    aliases = {}
    if prev is not None:
        aliases = {len(args): 0, len(args) + 1: 1}
        in_specs += [pl.BlockSpec(memory_space=pl.ANY)] * 2
        args += list(prev)
    out_shape = [jax.ShapeDtypeStruct((t, DN_OUT), BF16)] * 2
    out_specs = [pl.BlockSpec((ct, DN_OUT), fwd), pl.BlockSpec((ct, DN_OUT), bwd)]
    if write_state:
        out_shape += [jax.ShapeDtypeStruct((n_seq,) + state_blk[1:], F32)] * 2
        out_specs += [pl.BlockSpec(state_blk, seq_f), pl.BlockSpec(state_blk, seq_b)]
    kern = functools.partial(_dn_scan_kernel, n_chunks=ct // DN_CHUNK, blocks_per_seq=bps,
                             zero_init=zero_init, has_prev=prev is not None, write_state=write_state)
    return pl.pallas_call(
        kern, out_shape=tuple(out_shape), grid=(nblk,), in_specs=in_specs, out_specs=tuple(out_specs),
        scratch_shapes=[pltpu.VMEM((H_C, DK_C, DV_C), F32)] * 2,
        input_output_aliases=aliases,
        compiler_params=pltpu.CompilerParams(dimension_semantics=("arbitrary",),
                                             vmem_limit_bytes=VMEM_LIMIT),
        name="dn_scan",
    )(*args)


def _dn_outproj_kernel(x_ref, of_ref, ob_ref, z_ref, mod_ref, ng_ref, w_ref, y_ref):
    o = of_ref[...].astype(F32) + ob_ref[...].astype(F32)
    parts = []
    for h in range(H_C):
        blk = o[:, h * DV_C:(h + 1) * DV_C]
        parts.append(blk * lax.rsqrt(jnp.mean(blk * blk, axis=-1, keepdims=True) + NORM_EPS))
    y = jnp.concatenate(parts, axis=1) * ng_ref[...] * jax.nn.silu(z_ref[...].astype(F32))
    gate = mod_ref[0][5:6]
    y_ref[...] = x_ref[...] + gate * jnp.dot(y.astype(BF16), w_ref[...], preferred_element_type=F32)


def _dn_outproj(x, o_f, o_b, z, mod, row_of_tile, norm_g, w_out, *, tm):
    t = x.shape[0]
    tile = lambda i: (i, 0)
    const = lambda i: (0, 0)
    return pl.pallas_call(
        _dn_outproj_kernel,
        out_shape=jax.ShapeDtypeStruct((t, D_MODEL), F32),
        grid=(t // tm,),
        in_specs=[pl.BlockSpec((tm, D_MODEL), tile), pl.BlockSpec((tm, DN_OUT), tile),
                  pl.BlockSpec((tm, DN_OUT), tile), pl.BlockSpec((tm, DN_OUT), tile),
                  pl.BlockSpec((1, 9, D_MODEL), lambda i: (row_of_tile(i), 0, 0)),
                  pl.BlockSpec((1, DN_OUT), const), pl.BlockSpec((DN_OUT, D_MODEL), const)],
        out_specs=pl.BlockSpec((tm, D_MODEL), tile),
        compiler_params=pltpu.CompilerParams(dimension_semantics=("arbitrary",),
                                             vmem_limit_bytes=VMEM_LIMIT),
        name="dn_outproj",
    )(x, o_f, o_b, z, mod, jnp.tile(norm_g.astype(F32), H_C).reshape(1, DN_OUT), w_out.astype(BF16))


def _dn_layer(x, mod, row_of_tile, g, w_in, w_out, conv_w, a_log, dt_bias, norm_g, s0_f, s0_b,
              *, tm, ct, tp, lp, ls):
    t = x.shape[0]
    q, k, v, z, bg = _dn_inproj(x, mod, row_of_tile, g, w_in, conv_w, a_log, dt_bias, tm=tm, tp=tp, lp=lp, ls=ls)
    bp, bs = tp // lp, (t - tp) // ls
    o_f, o_b, sf, sb = _dn_scan(q, k, v, bg, None, None, None, row0=0, n_seq=bp, seq_len=lp, ct=ct,
                                write_state=True)
    o_f, o_b = _dn_scan(q, k, v, bg, s0_f.astype(F32), s0_b.astype(F32), (o_f, o_b), row0=tp, n_seq=bs,
                        seq_len=ls, ct=ct, write_state=False)
    return _dn_outproj(x, o_f, o_b, z, mod, row_of_tile, norm_g, w_out, tm=tm), sf, sb


AH_Q = H_A * HD_A
AH_KV = KVH_A * HD_A
AH_U3 = 3 * HY_C


def _group_mean_sq(x, bd):
    sq = x * x
    hi = sq.astype(BF16)
    lo = (sq - hi.astype(F32)).astype(BF16)
    return (jnp.dot(hi, bd, preferred_element_type=F32) + jnp.dot(lo, bd, preferred_element_type=F32)) * (1.0 / HD_A)


def _rope_lanes(x, cos_t, sin_t):
    w = x.shape[1]
    lane = lax.broadcasted_iota(jnp.int32, x.shape, 1)
    low = lax.rem(lane, HD_A) < HD_A // 2
    partner = jnp.where(low, pltpu.roll(x, w - HD_A // 2, 1), pltpu.roll(x, HD_A // 2, 1))
    reps = w // cos_t.shape[1]
    return x * jnp.concatenate([cos_t] * reps, axis=1) + partner * jnp.concatenate([sin_t] * reps, axis=1)


def _rep4(x):
    lane = lax.broadcasted_iota(jnp.int32, x.shape, 1)
    sw = pltpu.roll(x, HD_A, 1)
    a = jnp.where(lane < HD_A, x, sw)
    b = jnp.where(lane < HD_A, sw, x)
    return jnp.concatenate([a, a, b, b], axis=1)


def _ah_inproj_kernel(xp_ref, x_ref, xn_ref, mod_ref, g_ref, w_ref, wut_ref, gains_ref, bdq_ref, cwb_ref,
                      *rest, tm, tile0, latent, tp, lp, ls):
    if latent:
        cos_ref, sin_ref, perm_ref = rest[:3]
        q_ref, kt_ref, vt_ref, u_ref, vb_ref = rest[8:]
    else:
        q_ref, kt_ref, vt_ref, u_ref, vb_ref, kn_ref, vn_ref = rest
    i = tile0 + pl.program_id(0)
    n_tiles_p = tp // tm
    xe = jnp.concatenate([xp_ref[...], x_ref[...], xn_ref[...]], axis=0)
    he = _ada_h(xe, g_ref[...], mod_ref[0], 1).astype(BF16)
    hm = he[8:tm + 8]
    p = jnp.dot(hm, w_ref[...], preferred_element_type=F32)
    q = p[:, :AH_Q]
    q = q * lax.rsqrt(_group_mean_sq(q, bdq_ref[...]) + NORM_EPS) * gains_ref[0:1, :]
    k = p[:, AH_Q:AH_Q + AH_KV]
    k = k * lax.rsqrt(_group_mean_sq(k, bdq_ref[:AH_KV, :AH_KV]) + NORM_EPS) * gains_ref[1:2, :AH_KV]
    v = p[:, AH_Q + AH_KV:]
    if latent:
        q = _rope_lanes(q, cos_ref[...], sin_ref[...])
        k = _rope_lanes(k, cos_ref[...], sin_ref[...])
    else:
        kn_ref[...] = k
        vn_ref[...] = v
    q_ref[...] = (q * (HD_A ** -0.5)).astype(BF16)
    kt_ref[...] = _rep4(k).astype(BF16)
    vt_ref[...] = _rep4(v).astype(BF16)
    halo = jnp.concatenate([he[0:8], he[tm + 8:tm + 16], jnp.zeros((112, D_MODEL), BF16)], axis=0)
    nt = (((1,), (1,)), ((), ()))

    def store(r0, c):
        u_ref[r0:r0 + HY_C, :] = c
        if r0 == 2 * HY_C:
            vb_ref[...] = c.astype(BF16)

    if not latent:
        first, last = _seq_edges(i, tm, n_tiles_p, tp, lp, ls, (1, tm), 1)
        lane = lax.broadcasted_iota(jnp.int32, (1, tm), 1)
        for j in range(AH_U3 // HY_C):
            wu = wut_ref[j * HY_C:(j + 1) * HY_C, :]
            u = lax.dot_general(wu, hm, nt, preferred_element_type=F32)
            uh = lax.dot_general(wu, halo, nt, preferred_element_type=F32)
            left = jnp.where(lane == 0, uh[:, 7:8], pltpu.roll(u, 1, 1))
            right = jnp.where(lane == tm - 1, uh[:, 8:9], pltpu.roll(u, tm - 1, 1))
            cwb = cwb_ref[j * HY_C:(j + 1) * HY_C, :]
            store(j * HY_C, jnp.where(first, 0.0, left) * cwb[:, 0:1] + u * cwb[:, 1:2]
                  + jnp.where(last, 0.0, right) * cwb[:, 2:3] + cwb[:, 3:4])

    else:
        half = tm // 2
        hs = jnp.dot(perm_ref[...], hm, preferred_element_type=F32).astype(BF16)
        t0 = i * tm - tp
        seq_start = lax.rem(t0, ls) == 0
        seq_end = lax.rem(t0 + tm, ls) == 0
        lane = lax.broadcasted_iota(jnp.int32, (1, half), 1)
        for j in range(AH_U3 // HY_C):
            wu = wut_ref[j * HY_C:(j + 1) * HY_C, :]
            u = lax.dot_general(wu, hs, nt, preferred_element_type=F32)
            uh = lax.dot_general(wu, halo, nt, preferred_element_type=F32)
            ev, od = u[:, :half], u[:, half:]
            prev_tok = jnp.where(seq_start, 0.0, uh[:, 7:8])
            next_tok = jnp.where(seq_end, 0.0, uh[:, 8:9])
            od_before = jnp.where(lane == 0, prev_tok, pltpu.roll(od, 1, 1))
            ev_after = jnp.where(lane == half - 1, next_tok, pltpu.roll(ev, half - 1, 1))
            cwb = cwb_ref[j * HY_C:(j + 1) * HY_C, :]
            w0, w1, w2, b = cwb[:, 0:1], cwb[:, 1:2], cwb[:, 2:3], cwb[:, 3:4]
            store(j * HY_C, jnp.concatenate([od_before * w0 + ev * w1 + od * w2 + b,
                                             ev * w0 + od * w1 + ev_after * w2 + b], axis=1))


def _split_perm(tm):
    tok = jnp.arange(tm)
    pos = jnp.where(tok % 2 == 0, tok // 2, tm // 2 + tok // 2)
    return (jnp.arange(tm)[:, None] == pos[None, :]).astype(BF16)


def _rope_tables(ls):
    rows = ls // GRID_W
    r, col = jnp.meshgrid(jnp.arange(rows), jnp.arange(GRID_W), indexing='ij')
    inv = ROPE_THETA ** (-jnp.arange(0, ROPE_AXIS_DIM, 2, dtype=F32) / ROPE_AXIS_DIM)
    ang = jnp.concatenate([r.reshape(-1, 1).astype(F32) * inv, col.reshape(-1, 1).astype(F32) * inv], axis=-1)
    cos, sin = jnp.cos(ang), jnp.sin(ang)
    return jnp.concatenate([cos, cos] * 2, axis=1), jnp.concatenate([-sin, sin] * 2, axis=1)


def _ah_inproj(x, mod, row_of_tile, g, w_in, q_norm, k_norm, conv_w, conv_b, rope, *, tm, tp, lp, ls):
    t = x.shape[0]
    w = w_in[:, :AH_Q + 2 * AH_KV].astype(BF16)
    wut = w_in[:, AH_Q + 2 * AH_KV:].T.astype(BF16)
    gains = jnp.stack([jnp.tile(q_norm.astype(F32), H_A), jnp.tile(k_norm.astype(F32), H_A)])
    gid = jnp.arange(AH_Q) // HD_A
    bdq = (gid[:, None] == gid[None, :]).astype(BF16)
    cwb = jnp.concatenate([conv_w.astype(F32).T, conv_b.astype(F32)[:, None],
                           jnp.zeros((AH_U3, 4), F32)], axis=1)
    n_tiles_p = tp // tm
    const = lambda i: (0, 0)
    bf = lambda n: jax.ShapeDtypeStruct((t, n), BF16)
    joint_shapes = (bf(AH_Q), bf(AH_Q), bf(AH_Q), jax.ShapeDtypeStruct((AH_U3, t), F32),
                    jax.ShapeDtypeStruct((HY_C, t), BF16))
    params = pltpu.CompilerParams(dimension_semantics=("arbitrary",), vmem_limit_bytes=VMEM_LIMIT)

    def call(tile0, n_tiles, latent, extra_in, extra_specs, extra_out, extra_out_specs, aliases):
        prev, nxt = _halo_specs(tm, t, tile0)
        tile = lambda i: (tile0 + i, 0)
        lanes = lambda i: (0, tile0 + i)
        joint_specs = (pl.BlockSpec((tm, AH_Q), tile),) * 3 + (pl.BlockSpec((AH_U3, tm), lanes),
                                                               pl.BlockSpec((HY_C, tm), lanes))
        return pl.pallas_call(
            functools.partial(_ah_inproj_kernel, tm=tm, tile0=tile0, latent=latent, tp=tp, lp=lp, ls=ls),
            out_shape=joint_shapes + extra_out, grid=(n_tiles,),
            in_specs=[prev, pl.BlockSpec((tm, D_MODEL), tile), nxt,
                      pl.BlockSpec((1, 9, D_MODEL), lambda i: (row_of_tile(tile0 + i), 0, 0)),
                      pl.BlockSpec((1, D_MODEL), const), pl.BlockSpec(w.shape, const),
                      pl.BlockSpec(wut.shape, const), pl.BlockSpec(gains.shape, const),
                      pl.BlockSpec(bdq.shape, const), pl.BlockSpec(cwb.shape, const)] + extra_specs,
            out_specs=joint_specs + extra_out_specs, input_output_aliases=aliases, compiler_params=params,
            name="ah_inproj_latent" if latent else "ah_inproj_ctx",
        )(x, x, x, mod, g.reshape(1, D_MODEL), w, wut, gains, bdq, cwb, *extra_in)

    kv_new = jax.ShapeDtypeStruct((tp, AH_KV), F32)
    kv_spec = pl.BlockSpec((tm, AH_KV), lambda i: (i, 0))
    q, kt, vt, ut, vb, kn, vn = call(0, n_tiles_p, False, [], [], (kv_new, kv_new), (kv_spec, kv_spec), {})
    rope_blk = lambda i: (lax.rem(i * tm, ls) // tm, 0)
    q, kt, vt, ut, vb = call(
        n_tiles_p, t // tm - n_tiles_p, True, [rope[0], rope[1], _split_perm(tm), q, kt, vt, ut, vb],
        [pl.BlockSpec((tm, 128), rope_blk)] * 2 + [pl.BlockSpec((tm, tm), const)]
        + [pl.BlockSpec(memory_space=pl.ANY)] * 5, (), (), {13 + j: j for j in range(5)})
    return q, kt, vt, kn, vn, ut, vb


def _attn_chains(chains, sink_ref, valid):
    tq = chains[0][0].shape[0]
    lane = lax.broadcasted_iota(jnp.int32, (tq, G_A * HD_A), 1)
    valid4 = None if valid is None else jnp.concatenate([valid] * G_A, axis=0)
    scores, sinks = [], []
    for q, kt, _, g in chains:
        qs = jnp.concatenate([jnp.where(lane // HD_A == j, q, jnp.zeros_like(q)) for j in range(G_A)], axis=0)
        s = lax.dot_general(qs, kt, (((1,), (1,)), ((), ())), preferred_element_type=F32)
        scores.append(s if valid4 is None else jnp.where(valid4, s, NEG_INF))
        sinks.append(jnp.concatenate([jnp.broadcast_to(sink_ref[g * G_A + j:g * G_A + j + 1, 0:1], (tq, 1))
                                      for j in range(G_A)], axis=0))
    maxes = [jnp.maximum(jnp.max(s, axis=-1, keepdims=True), sk) for s, sk in zip(scores, sinks)]
    probs = [jnp.exp(s - m) for s, m in zip(scores, maxes)]
    dens = [jnp.sum(p, axis=-1, keepdims=True) + jnp.exp(sk - m) for p, sk, m in zip(probs, sinks, maxes)]
    pvs = [jnp.dot(p.astype(BF16), c[2], preferred_element_type=F32) / d
           for p, c, d in zip(probs, chains, dens)]
    outs = []
    for pv in pvs:
        out = pv[:tq]
        for j in range(1, G_A):
            out = jnp.where(lane // HD_A == j, pv[j * tq:(j + 1) * tq], out)
        outs.append(out)
    return outs


def _attn_ctx_kernel(q_ref, kt_ref, vt_ref, sink_ref, o_ref, *, lp):
    w = G_A * HD_A
    where = [(slice(s0, s0 + lp), slice(g * w, (g + 1) * w), g)
             for s0 in range(0, q_ref.shape[0], lp) for g in range(KVH_A)]
    outs = _attn_chains([(q_ref[r, c], kt_ref[r, c], vt_ref[r, c], g) for r, c, g in where], sink_ref, None)
    for (r, c, _), o in zip(where, outs):
        o_ref[r, c] = o.astype(BF16)


def _attn_win_kernel(q_ref, kp_ref, km_ref, kn_ref, vp_ref, vm_ref, vn_ref, ck_ref, cv_ref, sink_ref,
                     prev_ref, o_ref, *, tq, ls):
    del prev_ref
    i = pl.program_id(1)
    n_ctx = ck_ref.shape[1]
    w = G_A * HD_A
    q_pos = i * tq + lax.broadcasted_iota(jnp.int32, (tq, n_ctx + tq + 2 * WINDOW), 0)
    col = lax.broadcasted_iota(jnp.int32, (tq, n_ctx + tq + 2 * WINDOW), 1)
    k_pos = i * tq - WINDOW + (col - n_ctx)
    valid = (col < n_ctx) | ((jnp.abs(q_pos - k_pos) <= WINDOW) & (k_pos >= 0) & (k_pos < ls))
    chains = []
    for g in range(KVH_A):
        cols = slice(g * w, (g + 1) * w)
        kt = jnp.concatenate([ck_ref[0, :, cols], kp_ref[:, cols], km_ref[:, cols], kn_ref[:, cols]], axis=0)
        vt = jnp.concatenate([cv_ref[0, :, cols], vp_ref[:, cols], vm_ref[:, cols], vn_ref[:, cols]], axis=0)
        chains.append((q_ref[:, cols], kt, vt, g))
    for g, o in enumerate(_attn_chains(chains, sink_ref, valid)):
        o_ref[:, g * w:(g + 1) * w] = o.astype(BF16)


def _attention(q, kt, vt, ck, cv, sink, *, tq, tp, lp, ls):
    t = q.shape[0]
    bp, bs = tp // lp, (t - tp) // ls
    sink_rows = jnp.broadcast_to(sink.astype(F32)[:, None], (H_A, 128))
    params = pltpu.CompilerParams(dimension_semantics=("arbitrary",), vmem_limit_bytes=VMEM_LIMIT)
    seq = lambda b: (b, 0)
    nsq = math.gcd(CTX_SEQS, bp)
    a = pl.pallas_call(
        functools.partial(_attn_ctx_kernel, lp=lp),
        out_shape=jax.ShapeDtypeStruct((t, AH_Q), BF16), grid=(bp // nsq,),
        in_specs=[pl.BlockSpec((nsq * lp, AH_Q), seq)] * 3 + [pl.BlockSpec((H_A, 128), lambda b: (0, 0))],
        out_specs=pl.BlockSpec((nsq * lp, AH_Q), seq), compiler_params=params, name="attn_ctx",
    )(q, kt, vt, sink_rows)
    nq = ls // tq
    wb = tq // WINDOW
    n128 = t // WINDOW
    main = lambda b, i: ((tp + b * ls) // tq + i, 0)
    prev = lambda b, i: (jnp.maximum((tp + b * ls) // WINDOW + i * wb - 1, 0), 0)
    nxt = lambda b, i: (jnp.minimum((tp + b * ls) // WINDOW + (i + 1) * wb, n128 - 1), 0)
    ctx = lambda b, i: (b, 0, 0)
    kv_specs = [pl.BlockSpec((WINDOW, AH_Q), prev), pl.BlockSpec((tq, AH_Q), main),
                pl.BlockSpec((WINDOW, AH_Q), nxt)]
    n_ctx = ck.shape[1]
    return pl.pallas_call(
        functools.partial(_attn_win_kernel, tq=tq, ls=ls),
        out_shape=jax.ShapeDtypeStruct((t, AH_Q), BF16), grid=(bs, nq),
        in_specs=[pl.BlockSpec((tq, AH_Q), main)] + kv_specs + kv_specs
        + [pl.BlockSpec((1, n_ctx, AH_Q), ctx)] * 2
        + [pl.BlockSpec((H_A, 128), lambda b, i: (0, 0)), pl.BlockSpec(memory_space=pl.ANY)],
        out_specs=pl.BlockSpec((tq, AH_Q), main),
        input_output_aliases={10: 0},
        compiler_params=pltpu.CompilerParams(dimension_semantics=("arbitrary", "arbitrary"),
                                             vmem_limit_bytes=VMEM_LIMIT),
        name="attn_win",
    )(q, kt, kt, kt, vt, vt, vt, ck, cv, sink_rows, a)


DFT_SUB = 64
DFT_HALF = 256


def _dft_gen_kernel(o_ref, cl_ref, sl_ref, *, length):
    a = pl.program_id(0)
    n2 = 4 * length

    def angles(n_vec, shape):
        k = lax.broadcasted_iota(jnp.int32, shape, 1)
        ph = lax.rem(n_vec * (2 * k + 1), n2)
        return ph.astype(F32) * (2.0 * math.pi / n2)

    @pl.when(a == 0)
    def _():
        th = angles(lax.broadcasted_iota(jnp.int32, (DFT_SUB, length), 0), (DFT_SUB, length))
        cl_ref[...] = jnp.cos(th)
        sl_ref[...] = jnp.sin(th)

    th = angles(jnp.full((8, length), a * DFT_SUB, jnp.int32), (8, length))[0:1]
    ch, sh = jnp.cos(th), jnp.sin(th)
    c = ch * cl_ref[...] - sh * sl_ref[...]
    ns = -(sh * cl_ref[...] + ch * sl_ref[...])
    for jt in range(length // DFT_HALF):
        src = slice(jt * DFT_HALF, (jt + 1) * DFT_HALF)
        o_ref[:, 2 * jt * DFT_HALF:(2 * jt + 1) * DFT_HALF] = c[:, src].astype(BF16)
        o_ref[:, (2 * jt + 1) * DFT_HALF:(2 * jt + 2) * DFT_HALF] = ns[:, src].astype(BF16)


def _dft_matrix(length):
    return pl.pallas_call(
        functools.partial(_dft_gen_kernel, length=length),
        out_shape=jax.ShapeDtypeStruct((length, 2 * length), BF16), grid=(length // DFT_SUB,),
        out_specs=pl.BlockSpec((DFT_SUB, 2 * length), lambda a: (a, 0)),
        scratch_shapes=[pltpu.VMEM((DFT_SUB, length), F32)] * 2,
        compiler_params=pltpu.CompilerParams(dimension_semantics=("arbitrary",),
                                             vmem_limit_bytes=VMEM_LIMIT),
        name="dft_gen",
    )()


HYF_RB = 256


def _hy_filter_kernel(w1_ref, w2_ref, w3f_ref, w3b_ref, cols_ref, f_ref, o_ref, a_ref, b_ref, *, length):
    rc, jt = pl.program_id(0), pl.program_id(1)

    @pl.when(jt == 0)
    def _():
        m = lax.broadcasted_iota(jnp.int32, (8, length), 1).astype(F32)[0:1]
        t = m * (1.0 / (length - 1))
        w = m * (2.0 * math.pi / length)
        band = lax.broadcasted_iota(jnp.int32, (HY_BANDS, 1), 0).astype(F32)
        fb = 1e-4 + band * ((HY_BANDS - 1 - 1e-4) / (HY_BANDS - 1))
        z = jnp.concatenate([t, jnp.cos(fb * w), -jnp.sin(fb * w),
                             jnp.zeros((40 - HY_EMB, length), F32)], axis=0)
        cols = cols_ref[...]
        h = jnp.sin(cols[:, 1:2] * (_bdot(w1_ref[...], z) + cols[:, 0:1]))
        h = jnp.sin(cols[:, 3:4] * (_bdot(w2_ref[...], h) + cols[:, 2:3]))
        n_out = 2 * HY_ORDER * HY_C
        row = (rc * HYF_RB + lax.broadcasted_iota(jnp.int32, (HYF_RB, 1), 0)).astype(F32)
        step = (HY_MAX_DECAY - HY_MIN_DECAY) / (n_out - 1)
        d_f = jnp.abs(HY_MIN_DECAY + row * step)
        d_b = jnp.abs(HY_MIN_DECAY + (row + HY_ORDER * HY_C) * step)
        hf = _bdot(w3f_ref[...], h) * jnp.exp(-t * d_f)
        hb = jnp.where(m == 0.0, 0.0, _bdot(w3b_ref[...], h) * jnp.exp(-t * d_b))
        ssq = jnp.sum(hf * hf, axis=-1, keepdims=True) + jnp.sum(hb * hb, axis=-1, keepdims=True)
        scale = lax.rsqrt(ssq + NORM_EPS) * (1.0 / length)
        a_ref[...] = ((hf + hb) * scale).astype(BF16)
        b_ref[...] = ((hf - hb) * scale).astype(BF16)

    o_ref[:, :DFT_HALF] = jnp.dot(a_ref[...], f_ref[:, :DFT_HALF], preferred_element_type=F32)
    o_ref[:, DFT_HALF:] = jnp.dot(b_ref[...], f_ref[:, DFT_HALF:], preferred_element_type=F32)


def _hy_filter(fmat, w1, b1, f1, w2, b2, f2, w3, *, length):
    w1t = jnp.pad(w1.astype(F32).T, ((0, 0), (0, 40 - HY_EMB)))
    w3t = w3.astype(F32).T
    cols = jnp.stack([b1, f1, b2, f2] + [jnp.zeros_like(b1)] * 4, axis=1).astype(F32)
    n_rows = HY_ORDER * HY_C
    const = lambda rc, jt: (0, 0)
    return pl.pallas_call(
        functools.partial(_hy_filter_kernel, length=length),
        out_shape=jax.ShapeDtypeStruct((n_rows, 2 * length), F32),
        grid=(n_rows // HYF_RB, length // DFT_HALF),
        in_specs=[pl.BlockSpec(w1t.shape, const), pl.BlockSpec((w2.shape[1], w2.shape[0]), const),
                  pl.BlockSpec((HYF_RB, w3t.shape[1]), lambda rc, jt: (rc, 0)),
                  pl.BlockSpec((HYF_RB, w3t.shape[1]), lambda rc, jt: (n_rows // HYF_RB + rc, 0)),
                  pl.BlockSpec(cols.shape, const),
                  pl.BlockSpec((length, 2 * DFT_HALF), lambda rc, jt: (0, jt))],
        out_specs=pl.BlockSpec((HYF_RB, 2 * DFT_HALF), lambda rc, jt: (rc, jt)),
        scratch_shapes=[pltpu.VMEM((HYF_RB, length), BF16)] * 2,
        compiler_params=pltpu.CompilerParams(dimension_semantics=("arbitrary", "arbitrary"),
                                             vmem_limit_bytes=VMEM_LIMIT),
        name="hy_filter",
    )(w1t, w2.astype(F32).T, w3t, w3t, cols, fmat)


HY_SEQS = 8


def _hy_fwd_kernel(z_ref, f_ref, k_ref, y_ref, *, nsq, length):
    kr, ki = k_ref[:, :DFT_HALF], k_ref[:, DFT_HALF:]
    for s in range(nsq):
        zt = jnp.dot(z_ref[:, s * length:(s + 1) * length], f_ref[...], preferred_element_type=F32)
        zr, zi = zt[:, :DFT_HALF], zt[:, DFT_HALF:]
        y_ref[s * HY_C:(s + 1) * HY_C, :DFT_HALF] = (zr * kr - zi * ki).astype(BF16)
        y_ref[s * HY_C:(s + 1) * HY_C, DFT_HALF:] = (zr * ki + zi * kr).astype(BF16)


def _hy_fwd(zb, fmat, kspec, order, *, lane0, n_seq, length):
    nsq = math.gcd(HY_SEQS, n_seq) if lane0 == 0 else 1
    sb0 = lane0 // length
    return pl.pallas_call(
        functools.partial(_hy_fwd_kernel, nsq=nsq, length=length),
        out_shape=jax.ShapeDtypeStruct((n_seq * HY_C, 2 * length), BF16),
        grid=(n_seq // nsq, length // DFT_HALF),
        in_specs=[pl.BlockSpec((HY_C, nsq * length), lambda s, jt: (0, sb0 + s)),
                  pl.BlockSpec((length, 2 * DFT_HALF), lambda s, jt: (0, jt)),
                  pl.BlockSpec((HY_C, 2 * DFT_HALF), lambda s, jt: (order, jt))],
        out_specs=pl.BlockSpec((nsq * HY_C, 2 * DFT_HALF), lambda s, jt: (s, jt)),
        compiler_params=pltpu.CompilerParams(dimension_semantics=("arbitrary", "arbitrary"),
                                             vmem_limit_bytes=VMEM_LIMIT),
        name="hy_fwd",
    )(zb, fmat, kspec)


def _hy_inv_kernel(y_ref, f_ref, gate_ref, z_ref, bias_ref, *rest, with_prev, with_bf16, nsq, tt):
    rest = list(rest)
    if with_prev:
        rest.pop(0)
    o_ref = rest.pop(0)
    ob_ref = rest.pop(0) if with_bf16 else None
    for s in range(nsq):
        lanes = slice(s * tt, (s + 1) * tt)
        y = lax.dot_general(y_ref[s * HY_C:(s + 1) * HY_C, :], f_ref[...], (((1,), (1,)), ((), ())),
                            preferred_element_type=F32)
        out = gate_ref[:, lanes] * (y + z_ref[:, lanes] * bias_ref[...])
        o_ref[:, lanes] = out
        if with_bf16:
            ob_ref[:, lanes] = out.astype(BF16)


def _hy_inv(yspec, fmat, gate_arr, gate_rb, z_arr, z_rb, bias, prev, *, lane0, n_seq, length, tt, t_total,
            with_bf16):
    nt = length // tt
    nsq = math.gcd(HY_SEQS, n_seq) if (nt == 1 and lane0 == 0) else 1
    lane_blk = lambda s, j: lane0 // tt + s * nt + j
    in_specs = [pl.BlockSpec((nsq * HY_C, 2 * length), lambda s, j: (s, 0)),
                pl.BlockSpec((tt, 2 * length), lambda s, j: (j, 0)),
                pl.BlockSpec((HY_C, nsq * tt), lambda s, j: (gate_rb, lane_blk(s, j))),
                pl.BlockSpec((HY_C, nsq * tt), lambda s, j: (z_rb, lane_blk(s, j))),
                pl.BlockSpec((HY_C, 1), lambda s, j: (0, 0))]
    args = [yspec, fmat, gate_arr, z_arr, bias.astype(F32).reshape(HY_C, 1)]
    aliases = {}
    if prev is not None:
        in_specs.append(pl.BlockSpec(memory_space=pl.ANY))
        aliases = {len(args): 0}
        args.append(prev)
    out_spec = pl.BlockSpec((HY_C, nsq * tt), lambda s, j: (0, lane_blk(s, j)))
    out_shape = [jax.ShapeDtypeStruct((HY_C, t_total), F32)]
    out_specs = [out_spec]
    if with_bf16:
        out_shape.append(jax.ShapeDtypeStruct((HY_C, t_total), BF16))
        out_specs.append(out_spec)
    res = pl.pallas_call(
        functools.partial(_hy_inv_kernel, with_prev=prev is not None, with_bf16=with_bf16, nsq=nsq, tt=tt),
        out_shape=tuple(out_shape), grid=(n_seq // nsq, nt), in_specs=in_specs, out_specs=tuple(out_specs),
        input_output_aliases=aliases,
        compiler_params=pltpu.CompilerParams(dimension_semantics=("arbitrary", "arbitrary"),
                                             vmem_limit_bytes=VMEM_LIMIT),
        name="hy_inv",
    )(*args)
    return res


def _hyena_group(ut, vb, fmat, kspec, hy_bias, y_prev, *, lane0, n_seq, length, t_total):
    tt = min(512, length)
    geo = dict(lane0=lane0, n_seq=n_seq, length=length)
    y1 = _hy_fwd(vb, fmat, kspec, 0, **geo)
    z2, z2b = _hy_inv(y1, fmat, ut, 0, ut, 2, hy_bias[0], None, tt=tt, t_total=t_total, with_bf16=True, **geo)
    y2 = _hy_fwd(z2b, fmat, kspec, 1, **geo)
    return _hy_inv(y2, fmat, ut, 1, z2, 0, hy_bias[1], y_prev, tt=tt, t_total=t_total, with_bf16=False,
                   **geo)[0]


HY_TILE = TOKEN_TILE


def _twiddle(jt, n):
    k0 = (jt * DFT_HALF + lax.broadcasted_iota(jnp.int32, (8, DFT_HALF), 1)).astype(F32)[0:1]
    ang = (k0 + 0.5) * (2.0 * math.pi / n)
    return jnp.cos(ang), -jnp.sin(ang)


def _cmul(ar, ai, br, bi):
    return ar * br - ai * bi, ar * bi + ai * br


def _hy_filter2_kernel(w1_ref, w2_ref, w3f_ref, w3b_ref, cols_ref, f_ref, ka_ref, kb_ref, *scr, length):
    rc, jt = pl.program_id(0), pl.program_id(1)
    half = length // 2

    @pl.when(jt == 0)
    def _():
        cols = cols_ref[...]
        n_out = 2 * HY_ORDER * HY_C
        row = (rc * HYF_RB + lax.broadcasted_iota(jnp.int32, (HYF_RB, 1), 0)).astype(F32)
        step = (HY_MAX_DECAY - HY_MIN_DECAY) / (n_out - 1)
        d_f = jnp.abs(HY_MIN_DECAY + row * step)
        d_b = jnp.abs(HY_MIN_DECAY + (row + HY_ORDER * HY_C) * step)
        band = lax.broadcasted_iota(jnp.int32, (HY_BANDS, 1), 0).astype(F32)
        fb = 1e-4 + band * ((HY_BANDS - 1 - 1e-4) / (HY_BANDS - 1))
        taps = []
        for parity in range(2):
            m = 2.0 * lax.broadcasted_iota(jnp.int32, (8, half), 1).astype(F32)[0:1] + parity
            t = m * (1.0 / (length - 1))
            w = m * (2.0 * math.pi / length)
            z = jnp.concatenate([t, jnp.cos(fb * w), -jnp.sin(fb * w),
                                 jnp.zeros((40 - HY_EMB, half), F32)], axis=0)
            h = jnp.sin(cols[:, 1:2] * (_bdot(w1_ref[...], z) + cols[:, 0:1]))
            h = jnp.sin(cols[:, 3:4] * (_bdot(w2_ref[...], h) + cols[:, 2:3]))
            hf = _bdot(w3f_ref[...], h) * jnp.exp(-t * d_f)
            hb = jnp.where(m == 0.0, 0.0, _bdot(w3b_ref[...], h) * jnp.exp(-t * d_b))
            taps.append((hf, hb))
        ssq = sum(jnp.sum(hf * hf, axis=-1, keepdims=True) + jnp.sum(hb * hb, axis=-1, keepdims=True)
                  for hf, hb in taps)
        scale = lax.rsqrt(ssq + NORM_EPS) * (1.0 / length)
        (hf, hb), (hfo, hbo) = taps
        scr[0][...] = ((hf + hb) * scale).astype(BF16)
        scr[1][...] = ((hf - hb) * scale).astype(BF16)
        scr[2][...] = (hfo * scale).astype(BF16)
        scr[3][...] = (hbo * scale).astype(BF16)

    dot = functools.partial(jnp.dot, preferred_element_type=F32)
    fc, fs = f_ref[:, :DFT_HALF], f_ref[:, DFT_HALF:]
    er, ei = dot(scr[0][...], fc), dot(scr[1][...], fs)
    tr, ti = _twiddle(jt, 2 * length)
    fr, fi = _cmul(tr, ti, dot(scr[2][...], fc), dot(scr[2][...], fs))
    gr, gi = _cmul(tr, -ti, dot(scr[3][...], fc), -dot(scr[3][...], fs))
    ka_ref[:, :DFT_HALF] = er + (fr + gr)
    ka_ref[:, DFT_HALF:] = ei + (fi + gi)
    kb_ref[:, :DFT_HALF] = er - (fr + gr)
    kb_ref[:, DFT_HALF:] = ei - (fi + gi)


def _hy_filter2(fhalf, w1, b1, f1, w2, b2, f2, w3, *, length):
    w1t = jnp.pad(w1.astype(F32).T, ((0, 0), (0, 40 - HY_EMB)))
    w3t = w3.astype(F32).T
    cols = jnp.stack([b1, f1, b2, f2] + [jnp.zeros_like(b1)] * 4, axis=1).astype(F32)
    n_rows = HY_ORDER * HY_C
    half = length // 2
    const = lambda rc, jt: (0, 0)
    out = jax.ShapeDtypeStruct((n_rows, length), F32)
    return pl.pallas_call(
        functools.partial(_hy_filter2_kernel, length=length),
        out_shape=(out, out), grid=(n_rows // HYF_RB, half // DFT_HALF),
        in_specs=[pl.BlockSpec(w1t.shape, const), pl.BlockSpec((w2.shape[1], w2.shape[0]), const),
                  pl.BlockSpec((HYF_RB, w3t.shape[1]), lambda rc, jt: (rc, 0)),
                  pl.BlockSpec((HYF_RB, w3t.shape[1]), lambda rc, jt: (n_rows // HYF_RB + rc, 0)),
                  pl.BlockSpec(cols.shape, const),
                  pl.BlockSpec((half, 2 * DFT_HALF), lambda rc, jt: (0, jt))],
        out_specs=(pl.BlockSpec((HYF_RB, 2 * DFT_HALF), lambda rc, jt: (rc, jt)),) * 2,
        scratch_shapes=[pltpu.VMEM((HYF_RB, half), BF16)] * 4,
        compiler_params=pltpu.CompilerParams(dimension_semantics=("arbitrary", "arbitrary"),
                                             vmem_limit_bytes=VMEM_LIMIT),
        name="hy_filter2",
    )(w1t, w2.astype(F32).T, w3t, w3t, cols, fhalf)


def _hy_fwd2_kernel(z_ref, f_ref, ka_ref, kb_ref, p_ref, q_ref, eo_ref, *, length, ft):
    jt = pl.program_id(1)
    half_tile = HY_TILE // 2

    @pl.when(jt == 0)
    def _():
        for j in range(length // HY_TILE):
            dst = slice(j * half_tile, (j + 1) * half_tile)
            eo_ref[:HY_C, dst] = z_ref[:, j * HY_TILE:j * HY_TILE + half_tile]
            eo_ref[HY_C:, dst] = z_ref[:, j * HY_TILE + half_tile:(j + 1) * HY_TILE]

    for u in range(ft):
        re = slice(2 * u * DFT_HALF, (2 * u + 1) * DFT_HALF)
        im = slice((2 * u + 1) * DFT_HALF, (2 * u + 2) * DFT_HALF)
        r = jnp.dot(eo_ref[...], f_ref[:, 2 * u * DFT_HALF:(2 * u + 2) * DFT_HALF],
                    preferred_element_type=F32)
        er, ei = r[:HY_C, :DFT_HALF], r[:HY_C, DFT_HALF:]
        tr, ti = _twiddle(jt * ft + u, 2 * length)
        pr, pi = _cmul(tr, ti, r[HY_C:, :DFT_HALF], r[HY_C:, DFT_HALF:])
        yar, yai = _cmul(er + pr, ei + pi, ka_ref[:, re], ka_ref[:, im])
        ybr, ybi = _cmul(er - pr, ei - pi, kb_ref[:, re], kb_ref[:, im])
        p_ref[:, re] = (yar + ybr).astype(BF16)
        p_ref[:, im] = (yai + ybi).astype(BF16)
        qr, qi = _cmul(yar - ybr, yai - ybi, tr, -ti)
        q_ref[:, re] = qr.astype(BF16)
        q_ref[:, im] = qi.astype(BF16)


HY_FT = 2


def _hy_fwd2(zb, fhalf, kspec, order, *, lane0, n_seq, length):
    sb0 = lane0 // length
    half = length // 2
    ft = math.gcd(HY_FT, half // DFT_HALF)
    width = 2 * DFT_HALF * ft
    out = jax.ShapeDtypeStruct((n_seq * HY_C, length), BF16)
    tile = pl.BlockSpec((HY_C, width), lambda s, jt: (s, jt))
    kblk = pl.BlockSpec((HY_C, width), lambda s, jt: (order, jt))
    return pl.pallas_call(
        functools.partial(_hy_fwd2_kernel, length=length, ft=ft),
        out_shape=(out, out), grid=(n_seq, 2 * half // width),
        in_specs=[pl.BlockSpec((HY_C, length), lambda s, jt: (0, sb0 + s)),
                  pl.BlockSpec((half, width), lambda s, jt: (0, jt)), kblk, kblk],
        out_specs=(tile, tile),
        scratch_shapes=[pltpu.VMEM((2 * HY_C, half), BF16)],
        compiler_params=pltpu.CompilerParams(dimension_semantics=("arbitrary", "arbitrary"),
                                             vmem_limit_bytes=VMEM_LIMIT),
        name="hy_fwd2",
    )(zb, fhalf, kspec[0], kspec[1])


def _hy_inv2_kernel(p_ref, q_ref, f_ref, gate_ref, z_ref, bias_ref, *rest, with_prev, with_bf16):
    rest = list(rest)
    if with_prev:
        rest.pop(0)
    o_ref = rest.pop(0)
    pq = jnp.concatenate([p_ref[...], q_ref[...]], axis=0)
    y = lax.dot_general(pq, f_ref[...], (((1,), (1,)), ((), ())), preferred_element_type=F32)
    y = jnp.concatenate([y[:HY_C], y[HY_C:]], axis=1)
    out = gate_ref[...] * (y + z_ref[...] * bias_ref[...])
    o_ref[...] = out
    if with_bf16:
        rest.pop(0)[...] = out.astype(BF16)


def _hy_inv2(pq, fhalf, gate_arr, gate_rb, z_arr, z_rb, bias, prev, *, lane0, n_seq, length, t_total,
             with_bf16):
    nt = length // HY_TILE
    lane_blk = lambda s, j: lane0 // HY_TILE + s * nt + j
    spec = pl.BlockSpec((HY_C, length), lambda s, j: (s, 0))
    in_specs = [spec, spec, pl.BlockSpec((HY_TILE // 2, length), lambda s, j: (j, 0)),
                pl.BlockSpec((HY_C, HY_TILE), lambda s, j: (gate_rb, lane_blk(s, j))),
                pl.BlockSpec((HY_C, HY_TILE), lambda s, j: (z_rb, lane_blk(s, j))),
                pl.BlockSpec((HY_C, 1), lambda s, j: (0, 0))]
    args = [pq[0], pq[1], fhalf, gate_arr, z_arr, bias.astype(F32).reshape(HY_C, 1)]
    aliases = {}
    if prev is not None:
        in_specs.append(pl.BlockSpec(memory_space=pl.ANY))
        aliases = {len(args): 0}
        args.append(prev)
    out_spec = pl.BlockSpec((HY_C, HY_TILE), lambda s, j: (0, lane_blk(s, j)))
    out_shape = [jax.ShapeDtypeStruct((HY_C, t_total), F32)]
    out_specs = [out_spec]
    if with_bf16:
        out_shape.append(jax.ShapeDtypeStruct((HY_C, t_total), BF16))
        out_specs.append(out_spec)
    return pl.pallas_call(
        functools.partial(_hy_inv2_kernel, with_prev=prev is not None, with_bf16=with_bf16),
        out_shape=tuple(out_shape), grid=(n_seq, nt), in_specs=in_specs, out_specs=tuple(out_specs),
        input_output_aliases=aliases,
        compiler_params=pltpu.CompilerParams(dimension_semantics=("arbitrary", "arbitrary"),
                                             vmem_limit_bytes=VMEM_LIMIT),
        name="hy_inv2",
    )(*args)


def _hyena_group2(ut, vb, fhalf, kspec, hy_bias, y_prev, *, lane0, n_seq, length, t_total):
    geo = dict(lane0=lane0, n_seq=n_seq, length=length)
    pq = _hy_fwd2(vb, fhalf, kspec, 0, **geo)
    z2, z2b = _hy_inv2(pq, fhalf, ut, 0, ut, 2, hy_bias[0], None, t_total=t_total, with_bf16=True, **geo)
    pq = _hy_fwd2(z2b, fhalf, kspec, 1, **geo)
    return _hy_inv2(pq, fhalf, ut, 1, z2, 0, hy_bias[1], y_prev, t_total=t_total, with_bf16=False, **geo)[0]


def _ah_outproj_kernel(x_ref, a_ref, yt_ref, mod_ref, wa_ref, wy_ref, perm_ref, o_ref, *, n_tiles_p):
    o = jnp.dot(a_ref[...], wa_ref[...], preferred_element_type=F32)
    yb = yt_ref[...].astype(BF16)
    nat = jnp.dot(yb, perm_ref[...], preferred_element_type=F32).astype(BF16)
    yb = jnp.where(pl.program_id(0) < n_tiles_p, yb, nat)
    o = o + lax.dot_general(yb, wy_ref[...], (((0,), (0,)), ((), ())), preferred_element_type=F32)
    o_ref[...] = x_ref[...] + mod_ref[0][5:6] * o


def _ah_outproj(x, a, yt, mod, row_of_tile, w_out, *, tm, tp):
    t = x.shape[0]
    tile = lambda i: (i, 0)
    const = lambda i: (0, 0)
    wa, wy = w_out[:AH_Q].astype(BF16), w_out[AH_Q:].astype(BF16)
    perm = _split_perm(tm)
    return pl.pallas_call(
        functools.partial(_ah_outproj_kernel, n_tiles_p=tp // tm),
        out_shape=jax.ShapeDtypeStruct((t, D_MODEL), F32), grid=(t // tm,),
        in_specs=[pl.BlockSpec((tm, D_MODEL), tile), pl.BlockSpec((tm, AH_Q), tile),
                  pl.BlockSpec((HY_C, tm), lambda i: (0, i)),
                  pl.BlockSpec((1, 9, D_MODEL), lambda i: (row_of_tile(i), 0, 0)),
                  pl.BlockSpec(wa.shape, const), pl.BlockSpec(wy.shape, const),
                  pl.BlockSpec((tm, tm), const)],
        out_specs=pl.BlockSpec((tm, D_MODEL), tile),
        compiler_params=pltpu.CompilerParams(dimension_semantics=("arbitrary",),
                                             vmem_limit_bytes=VMEM_LIMIT),
        name="ah_outproj",
    )(x, a, yt, mod, wa, wy, perm)


def _ah_layer(x, mod, row_of_tile, g, w_in, w_out, q_norm, k_norm, sink, conv_w, conv_b, hy_bias,
              ck, cv, rope, fmats, kspecs, *, tm, tq, tp, lp, ls):
    t = x.shape[0]
    q, kt, vt, kn, vn, ut, vb = _ah_inproj(x, mod, row_of_tile, g, w_in, q_norm, k_norm, conv_w, conv_b, rope,
                                           tm=tm, tp=tp, lp=lp, ls=ls)
    a = _attention(q, kt, vt, ck, cv, sink, tq=tq, tp=tp, lp=lp, ls=ls)
    yt = _hyena_group(ut, vb, fmats[0], kspecs[0], hy_bias, None, lane0=0, n_seq=tp // lp, length=lp, t_total=t)
    yt = _hyena_group2(ut, vb, fmats[1], kspecs[1], hy_bias, yt, lane0=tp, n_seq=(t - tp) // ls, length=ls,
                       t_total=t)
    return _ah_outproj(x, a, yt, mod, row_of_tile, w_out, tm=tm, tp=tp), kn[:tp], vn[:tp]


def _rep4_ctx(c):
    b, s = c.shape[:2]
    return jnp.broadcast_to(c[:, :, :, None, :], (b, s, KVH_A, G_A, HD_A)).reshape(b, s, AH_Q).astype(BF16)


def _ada_kernel(c_ref, w_ref, b_ref, o_ref):
    s = jax.nn.silu(c_ref[...]).astype(BF16)
    o_ref[0] = jnp.dot(s, w_ref[0].astype(BF16), preferred_element_type=F32) + b_ref[0]


def _ada_mod(cond, ada_w, ada_b):
    depth, _, n = ada_w.shape
    rows = 16
    cp = jnp.pad(cond.astype(F32), ((0, rows - cond.shape[0]), (0, 0)))
    out = pl.pallas_call(
        _ada_kernel, out_shape=jax.ShapeDtypeStruct((depth, rows, n), F32),
        grid=(depth, n // D_MODEL),
        in_specs=[pl.BlockSpec((rows, D_MODEL), lambda l, j: (0, 0)),
                  pl.BlockSpec((1, D_MODEL, D_MODEL), lambda l, j: (l, 0, j)),
                  pl.BlockSpec((1, 1, D_MODEL), lambda l, j: (l, 0, j))],
        out_specs=pl.BlockSpec((1, rows, D_MODEL), lambda l, j: (l, 0, j)),
        compiler_params=pltpu.CompilerParams(dimension_semantics=("arbitrary", "arbitrary"),
                                             vmem_limit_bytes=VMEM_LIMIT),
        name="ada_mod",
    )(cp, ada_w, ada_b.reshape(depth, 1, n))
    return out.reshape(depth, rows, n // D_MODEL, D_MODEL)


def kernel(x_prompt, x_sample, cache_k, cache_v, state_fwd, state_bwd, c, c_ctx, norm_g, ada_w, ada_b, ffn_w13, ffn_w2, mx_w_in, mx_w_out, q_norm, k_norm, attn_sink, hy_conv_w, hy_conv_b, hy_w1, hy_b1, hy_freq1, hy_w2, hy_b2, hy_freq2, hy_w3, hy_bias, dn_w_in, dn_w_out, dn_conv_w, dn_a_log, dn_dt_bias, dn_norm_g):
    bp, lp, _ = x_prompt.shape
    bs, ls, _ = x_sample.shape
    tp, ts = bp * lp, bs * ls
    assert tp % ls == 0
    tm = math.gcd(TOKEN_TILE, math.gcd(tp, ls))
    ct = math.gcd(DN_BLOCK, math.gcd(lp, ls))
    tq = math.gcd(ATTN_TQ, ls)
    cond = jnp.concatenate([c, c_ctx[None, :]], axis=0)
    mods = _ada_mod(cond, ada_w, ada_b)
    tf = math.gcd(FFN_TILE, math.gcd(tp, ls))
    tiles_p, tiles_pf = tp // tm, tp // tf

    def mod_row(tile):
        return lambda i: jnp.where(i < tp // tile, bs, (i - tp // tile) // (ls // tile))

    row_of_tile, row_f = mod_row(tm), mod_row(tf)

    rope = _rope_tables(ls)
    assert tm == HY_TILE and ls % tm == 0 and (ls // 2) % DFT_HALF == 0
    fmats = (_dft_matrix(lp), _dft_matrix(ls // 2))
    new_k, new_v, new_sf, new_sb = [], [], [], []
    for layer in range(DEPTH):
        mod = mods[layer]
        i = layer // 2
        w_a = _ffn_weights(ffn_w13[layer, 0], ffn_w2[layer, 0])
        if layer == 0:
            x = _ffn(x_prompt.reshape(tp, D_MODEL), mod, row_f, norm_g[layer, 0], w_a, 0, tm=tf,
                     out_rows=tp + ts)
            x = _ffn(x_sample.reshape(ts, D_MODEL), mod, row_f, norm_g[layer, 0], w_a, 0, tm=tf,
                     out_rows=tp + ts, out_tile0=tiles_pf, prev=x)
        else:
            x = _ffn(x, mod, row_f, norm_g[layer, 0], w_a, 0, tm=tf)
        if layer % 2 == 0:
            kspecs = tuple(fn(f, hy_w1[i], hy_b1[i], hy_freq1[i], hy_w2[i], hy_b2[i], hy_freq2[i], hy_w3[i],
                              length=n) for fn, f, n in zip((_hy_filter, _hy_filter2), fmats, (lp, ls)))
            x, k_p, v_p = _ah_layer(x, mod, row_of_tile, norm_g[layer, 1], mx_w_in[i], mx_w_out[i], q_norm[i],
                                    k_norm[i], attn_sink[i], hy_conv_w[i], hy_conv_b[i], hy_bias[i],
                                    _rep4_ctx(cache_k[:, i]), _rep4_ctx(cache_v[:, i]), rope, fmats, kspecs,
                                    tm=tm, tq=tq, tp=tp, lp=lp, ls=ls)
            new_k.append(k_p.reshape(bp, lp, KVH_A, HD_A))
            new_v.append(v_p.reshape(bp, lp, KVH_A, HD_A))
        else:
            x, s_f, s_b = _dn_layer(x, mod, row_of_tile, norm_g[layer, 1], dn_w_in[i], dn_w_out[i],
                                    dn_conv_w[i], dn_a_log[i], dn_dt_bias[i], dn_norm_g[i],
                                    state_fwd[:, i], state_bwd[:, i], tm=tm, ct=ct, tp=tp, lp=lp, ls=ls)
            new_sf.append(s_f)
            new_sb.append(s_b)
        w_b = _ffn_weights(ffn_w13[layer, 1], ffn_w2[layer, 1])
        if layer < DEPTH - 1:
            x = _ffn(x, mod, row_f, norm_g[layer, 2], w_b, 2, tm=tf)
        else:
            y_p = _ffn(x, mod, row_f, norm_g[layer, 2], w_b, 2, tm=tf, n_tiles=tiles_pf, out_rows=tp)
            y_s = _ffn(x, mod, row_f, norm_g[layer, 2], w_b, 2, tm=tf, in_tile0=tiles_pf,
                       n_tiles=ts // tf, out_rows=ts)

    return (y_p.reshape(bp, lp, D_MODEL), y_s.reshape(bs, ls, D_MODEL),
            jnp.stack(new_k, axis=1), jnp.stack(new_v, axis=1),
            jnp.stack(new_sf, axis=1), jnp.stack(new_sb, axis=1))
```

```python
import functools
import math

import jax
import jax.numpy as jnp
from jax import lax
from jax.experimental import pallas as pl
from jax.experimental.pallas import tpu as pltpu

D_MODEL = 1024
DEPTH = 4
GRID_W = 64
H_A = 8
KVH_A = 2
G_A = H_A // KVH_A
HD_A = 64
WINDOW = 128
ATTN_BLOCK = 128
ROPE_THETA = 10000.0
ROPE_AXIS_DIM = HD_A // 2
HY_C = 512
HY_ORDER = 2
HY_EMB = 33
HY_BANDS = (HY_EMB - 1) // 2
HY_MIN_DECAY = math.log(1e-2) / 1.5
HY_MAX_DECAY = math.log(1e-2) / 0.3
H_C = 8
DK_C = 128
DV_C = 128
DN_CHUNK = 64
DN_OUT = H_C * DV_C
D_FF = 2816
NORM_EPS = 1e-6
NEG_INF = -1e30

F32 = jnp.float32
BF16 = jnp.bfloat16

TOKEN_TILE = 512
FFN_TILE = 512
FFN_CHUNK = 256
DN_BLOCK = 128
ATTN_TQ = 256
CTX_SEQS = 4
VMEM_LIMIT = 56 * 1024 * 1024


def _ffn_kernel(x_ref, mod_ref, g_ref, w13_ref, w2_ref, *rest, j, n_chunks, has_prev):
    o_ref, acc_ref = rest[1:] if has_prev else rest
    x = x_ref[...]
    y = x * lax.rsqrt(jnp.mean(x * x, axis=-1, keepdims=True) + NORM_EPS) * g_ref[...]
    m = mod_ref[0]
    shift, scale, gate = m[3 * j:3 * j + 1], m[3 * j + 1:3 * j + 2], m[3 * j + 2:3 * j + 3]
    h = (y * (1.0 + scale) + shift).astype(BF16)
    acc_ref[...] = jnp.zeros_like(acc_ref)

    for c in range(n_chunks):
        cols = slice(c * FFN_CHUNK, (c + 1) * FFN_CHUNK)
        gt = jnp.dot(h, w13_ref[:, cols].astype(BF16), preferred_element_type=F32)
        up = jnp.dot(h, w13_ref[:, D_FF + c * FFN_CHUNK:D_FF + (c + 1) * FFN_CHUNK].astype(BF16),
                     preferred_element_type=F32)
        a = (jax.nn.silu(gt) * up).astype(BF16)
        acc_ref[...] += jnp.dot(a, w2_ref[cols, :].astype(BF16), preferred_element_type=F32)
    o_ref[...] = x + 0.5 * gate * acc_ref[...]


def _ffn(x, mod, row_of_tile, g, weights, j, *, tm, in_tile0=0, n_tiles=None, out_rows=None, out_tile0=0,
         prev=None):
    n_tiles = x.shape[0] // tm if n_tiles is None else n_tiles
    out_rows = x.shape[0] if out_rows is None else out_rows
    n_chunks = D_FF // FFN_CHUNK
    joint0 = max(in_tile0, out_tile0)
    w13_all, w2_all, layer, which = weights
    pick = lambda i: (layer, which, 0, 0)
    in_specs = [
        pl.BlockSpec((tm, D_MODEL), lambda i: (in_tile0 + i, 0)),
        pl.BlockSpec((1, 9, D_MODEL), lambda i: (row_of_tile(joint0 + i), 0, 0)),
        pl.BlockSpec((1, D_MODEL), lambda i: (0, 0)),
        pl.BlockSpec((None, None, D_MODEL, 2 * D_FF), pick, pipeline_mode=pl.Buffered(1)),
        pl.BlockSpec((None, None, D_FF, D_MODEL), pick, pipeline_mode=pl.Buffered(1)),
    ]
    args = [x, mod, g.reshape(1, D_MODEL), w13_all, w2_all]
    aliases = {}
    if prev is not None:
        in_specs.append(pl.BlockSpec(memory_space=pl.ANY))
        aliases = {len(args): 0}
        args.append(prev)
    return pl.pallas_call(
        functools.partial(_ffn_kernel, j=j, n_chunks=n_chunks, has_prev=prev is not None),
        out_shape=jax.ShapeDtypeStruct((out_rows, D_MODEL), F32),
        grid=(n_tiles,), in_specs=in_specs,
        out_specs=pl.BlockSpec((tm, D_MODEL), lambda i: (out_tile0 + i, 0)),
        scratch_shapes=[pltpu.VMEM((tm, D_MODEL), F32)],
        input_output_aliases=aliases,
        compiler_params=pltpu.CompilerParams(dimension_semantics=("arbitrary",),
                                             vmem_limit_bytes=VMEM_LIMIT),
        name=f"ffn{j}",
    )(*args)


def _bdot(a, b):
    return jnp.dot(a.astype(BF16), b.astype(BF16), preferred_element_type=F32)


def _bdot_nt(a, b):
    return lax.dot_general(a.astype(BF16), b.astype(BF16), (((1,), (1,)), ((), ())),
                           preferred_element_type=F32)


def _bdot_tn(a, b):
    return lax.dot_general(a.astype(BF16), b.astype(BF16), (((0,), (0,)), ((), ())),
                           preferred_element_type=F32)


def _split3(x):
    hi = x.astype(BF16)
    r1 = x - hi.astype(F32)
    mid = r1.astype(BF16)
    lo = (r1 - mid.astype(F32)).astype(BF16)
    return hi, mid, lo


def _ada_h(x, g_row, m, j):
    y = x * lax.rsqrt(jnp.mean(x * x, axis=-1, keepdims=True) + NORM_EPS) * g_row
    return y * (1.0 + m[3 * j + 1:3 * j + 2]) + m[3 * j:3 * j + 1]


def _seq_edges(i, tm, n_tiles_p, tp, lp, ls, shape, axis):
    is_p = i < n_tiles_p
    seq_len = jnp.where(is_p, lp, ls)
    t0 = i * tm - jnp.where(is_p, 0, tp)
    base = lax.rem(t0, seq_len)
    pos = (base + lax.broadcasted_iota(jnp.int32, shape, axis)).astype(F32)
    lf = seq_len.astype(F32)
    rem = pos - jnp.floor((pos + 0.5) / lf) * lf
    return rem == 0.0, rem == lf - 1.0


def _halo_specs(tm, t, tile0=0):
    nb = t // 8
    prev = pl.BlockSpec((8, D_MODEL), lambda i: (jnp.maximum((tile0 + i) * (tm // 8) - 1, 0), 0))
    nxt = pl.BlockSpec((8, D_MODEL), lambda i: (jnp.minimum((tile0 + i + 1) * (tm // 8), nb - 1), 0))
    return prev, nxt


DN_QKV = 2 * H_C * DK_C + H_C * DV_C
DN_CW = 512


def _dn_inproj_kernel(xp_ref, x_ref, xn_ref, mod_ref, g_ref, w_ref, cw_ref, ab_ref,
                      q_ref, k_ref, v_ref, z_ref, bg_ref, pext_ref, *, tm, n_tiles_p, tp, lp, ls):
    i = pl.program_id(0)
    xe = jnp.concatenate([xp_ref[...], x_ref[...], xn_ref[...]], axis=0)
    h = _ada_h(xe, g_ref[...], mod_ref[0], 1).astype(BF16)
    first, last = _seq_edges(i, tm, n_tiles_p, tp, lp, ls, (tm, 1), 0)
    nq = H_C * DK_C
    for c0 in range(0, DN_QKV, DN_CW):
        pext_ref[...] = jnp.dot(h, w_ref[:, c0:c0 + DN_CW], preferred_element_type=F32)
        cw = cw_ref[:, c0:c0 + DN_CW]
        c = (jnp.where(first, 0.0, pext_ref[7:tm + 7, :]) * cw[0:1]
             + pext_ref[8:tm + 8, :] * cw[1:2]
             + jnp.where(last, 0.0, pext_ref[9:tm + 9, :]) * cw[2:3])
        a = jax.nn.silu(c)
        for b0 in range(0, DN_CW, DK_C):
            col = c0 + b0
            blk = a[:, b0:b0 + DK_C]
            if col < 2 * nq:
                blk = blk * lax.rsqrt(jnp.sum(blk * blk, axis=-1, keepdims=True) + NORM_EPS)
            if col < nq:
                q_ref[:, col:col + DK_C] = (blk * (DK_C ** -0.5)).astype(BF16)
            elif col < 2 * nq:
                k_ref[:, col - nq:col - nq + DK_C] = blk.astype(BF16)
            else:
                v_ref[:, col - 2 * nq:col - 2 * nq + DK_C] = blk.astype(BF16)
    hm = h[8:tm + 8]
    for c0 in range(0, DN_OUT, DN_CW):
        z_ref[:, c0:c0 + DN_CW] = jnp.dot(hm, w_ref[:, DN_QKV + c0:DN_QKV + c0 + DN_CW],
                                           preferred_element_type=F32).astype(BF16)
    r = jnp.dot(hm, w_ref[:, DN_QKV + DN_OUT:], preferred_element_type=F32)
    lane = lax.broadcasted_iota(jnp.int32, r.shape, 1)
    xs = r + ab_ref[1:2]
    softplus = jnp.maximum(xs, 0.0) + jnp.log1p(jnp.exp(-jnp.abs(xs)))
    bg_ref[...] = jnp.where(lane < 2 * H_C, jax.nn.sigmoid(r),
                            jnp.where(lane < 4 * H_C, -ab_ref[0:1] * softplus, 0.0))


def _dn_inproj(x, mod, row_of_tile, g, w_in, conv_w, a_log, dt_bias, *, tm, tp, lp, ls):
    t = x.shape[0]
    pad = 128 - 4 * H_C
    w = jnp.pad(w_in, ((0, 0), (0, pad))).astype(BF16)
    ab = jnp.zeros((2, 128), F32)
    ab = ab.at[0, 2 * H_C:4 * H_C].set(jnp.exp(a_log.astype(F32)).reshape(-1))
    ab = ab.at[1, 2 * H_C:4 * H_C].set(dt_bias.astype(F32).reshape(-1))
    prev, nxt = _halo_specs(tm, t)
    tile = lambda i: (i, 0)
    const = lambda i: (0, 0)
    wide = jax.ShapeDtypeStruct((t, DN_OUT), BF16)
    return pl.pallas_call(
        functools.partial(_dn_inproj_kernel, tm=tm, n_tiles_p=tp // tm, tp=tp, lp=lp, ls=ls),
        out_shape=(wide, wide, wide, wide, jax.ShapeDtypeStruct((t, 128), F32)),
        grid=(t // tm,),
        in_specs=[prev, pl.BlockSpec((tm, D_MODEL), tile), nxt,
                  pl.BlockSpec((1, 9, D_MODEL), lambda i: (row_of_tile(i), 0, 0)),
                  pl.BlockSpec((1, D_MODEL), const),
                  pl.BlockSpec(w.shape, const),
                  pl.BlockSpec(conv_w.shape, const),
                  pl.BlockSpec((2, 128), const)],
        out_specs=(pl.BlockSpec((tm, DN_OUT), tile),) * 4 + (pl.BlockSpec((tm, 128), tile),),
        scratch_shapes=[pltpu.VMEM((tm + 16, DN_CW), F32)],
        compiler_params=pltpu.CompilerParams(dimension_semantics=("arbitrary",),
                                             vmem_limit_bytes=VMEM_LIMIT),
        name="dn_inproj",
    )(x, x, x, mod, g.reshape(1, D_MODEL), w, conv_w.astype(F32), ab)


DN_GROUP = 4


def _dn_masks(rev):
    n = DN_CHUNK
    r_i = lax.broadcasted_iota(jnp.int32, (n, n), 0)
    c_i = lax.broadcasted_iota(jnp.int32, (n, n), 1)
    tri = jnp.where((r_i <= c_i) if rev else (r_i >= c_i), 1.0, 0.0).astype(BF16)
    rows = DN_GROUP * n
    rr = lax.broadcasted_iota(jnp.int32, (rows, rows), 0)
    cc = lax.broadcasted_iota(jnp.int32, (rows, rows), 1)
    same = (rr // n) == (cc // n)
    incl = same & ((rr <= cc) if rev else (rr >= cc))
    strict = same & ((rr < cc) if rev else (rr > cc))
    eye = jnp.where(rr == cc, 1.0, 0.0)
    return tri, incl, strict, eye


def _dn_prepare(qc, kc, vc, bgc, masks, rev):
    n = DN_CHUNK
    tri, incl, strict, eye = masks
    gct = sum(jnp.dot(tri, p, preferred_element_type=F32) for p in _split3(bgc))
    rows = DN_GROUP * n
    lane = lax.broadcasted_iota(jnp.int32, (rows, 128), 1)
    last = 0 if rev else n - 1
    prep = []
    for gi in range(H_C // DN_GROUP):
        heads = range(gi * DN_GROUP, (gi + 1) * DN_GROUP)
        sb = [(H_C if rev else 0) + h for h in heads]
        sg = [2 * H_C + (H_C if rev else 0) + h for h in heads]
        beta = jnp.concatenate([bgc[:, s:s + 1] for s in sb], axis=0)
        gc = jnp.concatenate([gct[:, s:s + 1] for s in sg], axis=0)
        gl = jnp.concatenate([jnp.broadcast_to(gct[last:last + 1, s:s + 1], (n, 1)) for s in sg], axis=0)
        q4 = jnp.concatenate([qc[:, h * DK_C:(h + 1) * DK_C] for h in heads], axis=0)
        k4 = jnp.concatenate([kc[:, h * DK_C:(h + 1) * DK_C] for h in heads], axis=0)
        v4 = jnp.concatenate([vc[:, h * DV_C:(h + 1) * DV_C] for h in heads], axis=0)
        hi, mid, lo = (p.astype(F32) for p in _split3(gc))
        one = jnp.where(lane < 6, 1.0, 0.0)
        u_m = jnp.where(lane == 0, hi, jnp.where(lane == 1, mid, jnp.where(lane == 2, lo, one)))
        v_m = jnp.where(lane == 3, -hi, jnp.where(lane == 4, -mid, jnp.where(lane == 5, -lo, one)))
        gd = _bdot_nt(jnp.where(lane < 6, u_m, 0.0), jnp.where(lane < 6, v_m, 0.0))
        decay = jnp.where(incl, jnp.exp(jnp.where(incl, gd, 0.0)), 0.0)
        k4f = k4.astype(F32)
        kb = k4f * beta
        lm = jnp.where(strict, _bdot_nt(kb, k4) * decay, 0.0)
        x = jnp.concatenate([v4.astype(F32) * beta, kb * jnp.exp(gc)], axis=1)
        qe = q4.astype(F32) * jnp.exp(gc)
        ke = (k4f * jnp.exp(gl - gc)).astype(BF16)
        aqk = jnp.where(incl, _bdot_nt(q4, k4) * decay, 0.0).astype(BF16)
        prep.append(dict(lm=lm, x=x, qe=qe, ke=ke, aqk=aqk, egl=jnp.exp(gl), rev=rev))
    return prep


def _dn_solve(groups, eye):
    n = DN_CHUNK
    dot = functools.partial(jnp.dot, preferred_element_type=F32)
    for g in groups:
        g["lh"] = g["lm"].astype(BF16)
        g["p"] = -g["lh"]
        g["t"] = eye - g["lm"]
    def rows(m, lo, hi):
        return jnp.concatenate([m[b * n + lo:b * n + hi] for b in range(DN_GROUP)], axis=0)

    def put(upd, lo, hi):
        k = hi - lo
        parts = []
        for b in range(DN_GROUP):
            parts += [jnp.zeros((lo, upd.shape[1]), upd.dtype), upd[b * k:(b + 1) * k],
                      jnp.zeros((n - hi, upd.shape[1]), upd.dtype)]
        return jnp.concatenate([p for p in parts if p.shape[0]], axis=0)

    for s in range(1, 6):
        w = 2 ** s
        for g in groups:
            g["win"] = ((0, n - w) if g["rev"] else (w, n)) if w >= 16 else None
        for g in groups:
            if g["win"]:
                g["p"] = put(dot(rows(g["p"], *g["win"]), g["p"]).astype(BF16), *g["win"])
            else:
                g["p"] = dot(g["p"], g["p"]).astype(BF16)
        for g in groups:
            tb = g["t"].astype(BF16)
            if g["win"]:
                g["t"] = g["t"] + put(dot(rows(tb, *g["win"]), g["p"]), *g["win"])
            else:
                g["t"] = g["t"] + dot(tb, g["p"])
    for g in groups:
        g["tb"] = g["t"].astype(BF16)
        g["x0"] = dot(g["tb"], g["x"].astype(BF16))
    for g in groups:
        g["res"] = g["x"] - g["x0"] - dot(g["lh"], g["x0"].astype(BF16))
    out = []
    for g in groups:
        x = g["x0"] + dot(g["tb"], g["res"].astype(BF16))
        u4, w4, qe = x[:, :DV_C], x[:, DV_C:], g["qe"]
        wqe = [jnp.concatenate([w4[j * n:(j + 1) * n], qe[j * n:(j + 1) * n]], axis=0).astype(BF16)
               for j in range(DN_GROUP)]
        out.append((u4, wqe, g["ke"], g["aqk"], g["egl"]))
    return out


def _dn_advance(prep, s_ref):
    n = DN_CHUNK
    outs = []
    for gi, (u4, wqe, ke, aqk, egl) in enumerate(prep):
        vn, qs = [], []
        for j in range(DN_GROUP):
            wq = jnp.dot(wqe[j], s_ref[gi * DN_GROUP + j].astype(BF16), preferred_element_type=F32)
            vn.append(u4[j * n:(j + 1) * n] - wq[:n])
            qs.append(wq[n:])
        o4 = jnp.concatenate(qs, axis=0) + jnp.dot(aqk, jnp.concatenate(vn, axis=0).astype(BF16),
                                                   preferred_element_type=F32)
        for j in range(DN_GROUP):
            h = gi * DN_GROUP + j
            sl = slice(j * n, (j + 1) * n)
            s_ref[h] = s_ref[h] * egl[j * n:j * n + 1] + lax.dot_general(
                ke[sl], vn[j].astype(BF16), (((0,), (0,)), ((), ())), preferred_element_type=F32)
            outs.append(o4[sl])
    return outs


def _dn_scan_kernel(*refs, n_chunks, blocks_per_seq, zero_init, has_prev, write_state):
    refs = list(refs)
    fwd_in, bwd_in = refs[:4], refs[4:8]
    del refs[:8]
    s0f_ref, s0b_ref = (None, None) if zero_init else (refs.pop(0), refs.pop(0))
    if has_prev:
        del refs[:2]
    of_ref, ob_ref = refs.pop(0), refs.pop(0)
    sfo_ref, sbo_ref = (refs.pop(0), refs.pop(0)) if write_state else (None, None)
    sf_ref, sb_ref = refs
    i = pl.program_id(0)
    f_in_seq = lax.rem(i, blocks_per_seq)
    r_in_seq = lax.rem(pl.num_programs(0) - 1 - i, blocks_per_seq)

    @pl.when(f_in_seq == 0)
    def _():
        sf_ref[...] = jnp.zeros_like(sf_ref) if zero_init else s0f_ref[0]

    @pl.when(r_in_seq == blocks_per_seq - 1)
    def _():
        sb_ref[...] = jnp.zeros_like(sb_ref) if zero_init else s0b_ref[0]

    def chunk(io, c):
        rows = slice(c * DN_CHUNK, (c + 1) * DN_CHUNK)
        return [r[rows, :] for r in io]

    mf, mb = _dn_masks(False), _dn_masks(True)
    raw = [_dn_prepare(*chunk(fwd_in, c), mf, False) for c in range(n_chunks)]
    raw += [_dn_prepare(*chunk(bwd_in, c), mb, True) for c in range(n_chunks)]
    n_groups = H_C // DN_GROUP
    solved = _dn_solve([g for r in raw for g in r], mf[3])
    per_chunk = [solved[j * n_groups:(j + 1) * n_groups] for j in range(2 * n_chunks)]
    prep_f, prep_b = per_chunk[:n_chunks], per_chunk[n_chunks:]
    for step in range(n_chunks):
        for o_ref, prep, s_ref, c in ((of_ref, prep_f, sf_ref, step),
                                      (ob_ref, prep_b, sb_ref, n_chunks - 1 - step)):
            for h, o in enumerate(_dn_advance(prep[c], s_ref)):
                o_ref[c * DN_CHUNK:(c + 1) * DN_CHUNK, h * DV_C:(h + 1) * DV_C] = o.astype(BF16)

    if write_state:
        @pl.when(f_in_seq == blocks_per_seq - 1)
        def _():
            sfo_ref[0] = sf_ref[...]

        @pl.when(r_in_seq == 0)
        def _():
            sbo_ref[0] = sb_ref[...]


def _dn_scan(q, k, v, bg, s0_f, s0_b, prev, *, row0, n_seq, seq_len, ct, write_state):
    t = q.shape[0]
    bps = seq_len // ct
    nblk = n_seq * bps
    b0 = row0 // ct
    zero_init = s0_f is None
    fwd = lambda i: (b0 + i, 0)
    bwd = lambda i: (b0 + nblk - 1 - i, 0)
    seq_f = lambda i: (i // bps, 0, 0, 0)
    seq_b = lambda i: ((nblk - 1 - i) // bps, 0, 0, 0)
    in_specs, args = [], []
    for rows in (fwd, bwd):
        in_specs += [pl.BlockSpec((ct, DN_OUT), rows)] * 3 + [pl.BlockSpec((ct, 128), rows)]
        args += [q, k, v, bg]
    state_blk = (1, H_C, DK_C, DV_C)
    if not zero_init:
        in_specs += [pl.BlockSpec(state_blk, seq_f), pl.BlockSpec(state_blk, seq_b)]
        args += [s0_f, s0_b]
    aliases = {}
    if prev is not None:
        aliases = {len(args): 0, len(args) + 1: 1}
        in_specs += [pl.BlockSpec(memory_space=pl.ANY)] * 2
        args += list(prev)
    out_shape = [jax.ShapeDtypeStruct((t, DN_OUT), BF16)] * 2
    out_specs = [pl.BlockSpec((ct, DN_OUT), fwd), pl.BlockSpec((ct, DN_OUT), bwd)]
    if write_state:
        out_shape += [jax.ShapeDtypeStruct((n_seq,) + state_blk[1:], F32)] * 2
        out_specs += [pl.BlockSpec(state_blk, seq_f), pl.BlockSpec(state_blk, seq_b)]
    kern = functools.partial(_dn_scan_kernel, n_chunks=ct // DN_CHUNK, blocks_per_seq=bps,
                             zero_init=zero_init, has_prev=prev is not None, write_state=write_state)
    return pl.pallas_call(
        kern, out_shape=tuple(out_shape), grid=(nblk,), in_specs=in_specs, out_specs=tuple(out_specs),
        scratch_shapes=[pltpu.VMEM((H_C, DK_C, DV_C), F32)] * 2,
        input_output_aliases=aliases,
        compiler_params=pltpu.CompilerParams(dimension_semantics=("arbitrary",),
                                             vmem_limit_bytes=VMEM_LIMIT),
        name="dn_scan",
    )(*args)


def _dn_outproj_kernel(x_ref, of_ref, ob_ref, z_ref, mod_ref, ng_ref, w_ref, y_ref):
    o = of_ref[...].astype(F32) + ob_ref[...].astype(F32)
    parts = []
    for h in range(H_C):
        blk = o[:, h * DV_C:(h + 1) * DV_C]
        parts.append(blk * lax.rsqrt(jnp.mean(blk * blk, axis=-1, keepdims=True) + NORM_EPS))
    y = jnp.concatenate(parts, axis=1) * ng_ref[...] * jax.nn.silu(z_ref[...].astype(F32))
    gate = mod_ref[0][5:6]
    y_ref[...] = x_ref[...] + gate * jnp.dot(y.astype(BF16), w_ref[...], preferred_element_type=F32)


def _dn_outproj(x, o_f, o_b, z, mod, row_of_tile, norm_g, w_out, *, tm):
    t = x.shape[0]
    tile = lambda i: (i, 0)
    const = lambda i: (0, 0)
    return pl.pallas_call(
        _dn_outproj_kernel,
        out_shape=jax.ShapeDtypeStruct((t, D_MODEL), F32),
        grid=(t // tm,),
        in_specs=[pl.BlockSpec((tm, D_MODEL), tile), pl.BlockSpec((tm, DN_OUT), tile),
                  pl.BlockSpec((tm, DN_OUT), tile), pl.BlockSpec((tm, DN_OUT), tile),
                  pl.BlockSpec((1, 9, D_MODEL), lambda i: (row_of_tile(i), 0, 0)),
                  pl.BlockSpec((1, DN_OUT), const), pl.BlockSpec((DN_OUT, D_MODEL), const)],
        out_specs=pl.BlockSpec((tm, D_MODEL), tile),
        compiler_params=pltpu.CompilerParams(dimension_semantics=("arbitrary",),
                                             vmem_limit_bytes=VMEM_LIMIT),
        name="dn_outproj",
    )(x, o_f, o_b, z, mod, jnp.tile(norm_g.astype(F32), H_C).reshape(1, DN_OUT), w_out.astype(BF16))


def _dn_layer(x, mod, row_of_tile, g, w_in, w_out, conv_w, a_log, dt_bias, norm_g, s0_f, s0_b,
              *, tm, ct, tp, lp, ls):
    t = x.shape[0]
    q, k, v, z, bg = _dn_inproj(x, mod, row_of_tile, g, w_in, conv_w, a_log, dt_bias, tm=tm, tp=tp, lp=lp, ls=ls)
    bp, bs = tp // lp, (t - tp) // ls
    o_f, o_b, sf, sb = _dn_scan(q, k, v, bg, None, None, None, row0=0, n_seq=bp, seq_len=lp, ct=ct,
                                write_state=True)
    o_f, o_b = _dn_scan(q, k, v, bg, s0_f.astype(F32), s0_b.astype(F32), (o_f, o_b), row0=tp, n_seq=bs,
                        seq_len=ls, ct=ct, write_state=False)
    return _dn_outproj(x, o_f, o_b, z, mod, row_of_tile, norm_g, w_out, tm=tm), sf, sb


AH_Q = H_A * HD_A
AH_KV = KVH_A * HD_A
AH_U3 = 3 * HY_C


def _group_mean_sq(x, bd):
    sq = x * x
    hi = sq.astype(BF16)
    lo = (sq - hi.astype(F32)).astype(BF16)
    return (jnp.dot(hi, bd, preferred_element_type=F32) + jnp.dot(lo, bd, preferred_element_type=F32)) * (1.0 / HD_A)


def _rope_lanes(x, cos_t, sin_t):
    w = x.shape[1]
    lane = lax.broadcasted_iota(jnp.int32, x.shape, 1)
    low = lax.rem(lane, HD_A) < HD_A // 2
    partner = jnp.where(low, pltpu.roll(x, w - HD_A // 2, 1), pltpu.roll(x, HD_A // 2, 1))
    reps = w // cos_t.shape[1]
    return x * jnp.concatenate([cos_t] * reps, axis=1) + partner * jnp.concatenate([sin_t] * reps, axis=1)


def _rep4(x):
    lane = lax.broadcasted_iota(jnp.int32, x.shape, 1)
    sw = pltpu.roll(x, HD_A, 1)
    a = jnp.where(lane < HD_A, x, sw)
    b = jnp.where(lane < HD_A, sw, x)
    return jnp.concatenate([a, a, b, b], axis=1)


def _ah_inproj_kernel(xp_ref, x_ref, xn_ref, mod_ref, g_ref, w_ref, wut_ref, gains_ref, bdq_ref, cwb_ref,
                      *rest, tm, tile0, latent, tp, lp, ls):
    if latent:
        cos_ref, sin_ref, perm_ref = rest[:3]
        q_ref, kt_ref, vt_ref, u_ref, vb_ref = rest[8:]
    else:
        q_ref, kt_ref, vt_ref, u_ref, vb_ref, kn_ref, vn_ref = rest
    i = tile0 + pl.program_id(0)
    n_tiles_p = tp // tm
    xe = jnp.concatenate([xp_ref[...], x_ref[...], xn_ref[...]], axis=0)
    he = _ada_h(xe, g_ref[...], mod_ref[0], 1).astype(BF16)
    hm = he[8:tm + 8]
    p = jnp.dot(hm, w_ref[...], preferred_element_type=F32)
    q = p[:, :AH_Q]
    q = q * lax.rsqrt(_group_mean_sq(q, bdq_ref[...]) + NORM_EPS) * gains_ref[0:1, :]
    k = p[:, AH_Q:AH_Q + AH_KV]
    k = k * lax.rsqrt(_group_mean_sq(k, bdq_ref[:AH_KV, :AH_KV]) + NORM_EPS) * gains_ref[1:2, :AH_KV]
    v = p[:, AH_Q + AH_KV:]
    if latent:
        q = _rope_lanes(q, cos_ref[...], sin_ref[...])
        k = _rope_lanes(k, cos_ref[...], sin_ref[...])
    else:
        kn_ref[...] = k
        vn_ref[...] = v
    q_ref[...] = (q * (HD_A ** -0.5)).astype(BF16)
    kt_ref[...] = _rep4(k).astype(BF16)
    vt_ref[...] = _rep4(v).astype(BF16)
    halo = jnp.concatenate([he[0:8], he[tm + 8:tm + 16], jnp.zeros((112, D_MODEL), BF16)], axis=0)
    nt = (((1,), (1,)), ((), ()))

    def store(r0, c):
        u_ref[r0:r0 + HY_C, :] = c
        if r0 == 2 * HY_C:
            vb_ref[...] = c.astype(BF16)

    if not latent:
        first, last = _seq_edges(i, tm, n_tiles_p, tp, lp, ls, (1, tm), 1)
        lane = lax.broadcasted_iota(jnp.int32, (1, tm), 1)
        for j in range(AH_U3 // HY_C):
            wu = wut_ref[j * HY_C:(j + 1) * HY_C, :]
            u = lax.dot_general(wu, hm, nt, preferred_element_type=F32)
            uh = lax.dot_general(wu, halo, nt, preferred_element_type=F32)
            left = jnp.where(lane == 0, uh[:, 7:8], pltpu.roll(u, 1, 1))
            right = jnp.where(lane == tm - 1, uh[:, 8:9], pltpu.roll(u, tm - 1, 1))
            cwb = cwb_ref[j * HY_C:(j + 1) * HY_C, :]
            store(j * HY_C, jnp.where(first, 0.0, left) * cwb[:, 0:1] + u * cwb[:, 1:2]
                  + jnp.where(last, 0.0, right) * cwb[:, 2:3] + cwb[:, 3:4])

    else:
        half = tm // 2
        hs = jnp.dot(perm_ref[...], hm, preferred_element_type=F32).astype(BF16)
        t0 = i * tm - tp
        seq_start = lax.rem(t0, ls) == 0
        seq_end = lax.rem(t0 + tm, ls) == 0
        lane = lax.broadcasted_iota(jnp.int32, (1, half), 1)
        for j in range(AH_U3 // HY_C):
            wu = wut_ref[j * HY_C:(j + 1) * HY_C, :]
            u = lax.dot_general(wu, hs, nt, preferred_element_type=F32)
            uh = lax.dot_general(wu, halo, nt, preferred_element_type=F32)
            ev, od = u[:, :half], u[:, half:]
            prev_tok = jnp.where(seq_start, 0.0, uh[:, 7:8])
            next_tok = jnp.where(seq_end, 0.0, uh[:, 8:9])
            od_before = jnp.where(lane == 0, prev_tok, pltpu.roll(od, 1, 1))
            ev_after = jnp.where(lane == half - 1, next_tok, pltpu.roll(ev, half - 1, 1))
            cwb = cwb_ref[j * HY_C:(j + 1) * HY_C, :]
            w0, w1, w2, b = cwb[:, 0:1], cwb[:, 1:2], cwb[:, 2:3], cwb[:, 3:4]
            store(j * HY_C, jnp.concatenate([od_before * w0 + ev * w1 + od * w2 + b,
                                             ev * w0 + od * w1 + ev_after * w2 + b], axis=1))


def _split_perm(tm):
    tok = jnp.arange(tm)
    pos = jnp.where(tok % 2 == 0, tok // 2, tm // 2 + tok // 2)
    return (jnp.arange(tm)[:, None] == pos[None, :]).astype(BF16)


def _rope_tables(ls):
    rows = ls // GRID_W
    r, col = jnp.meshgrid(jnp.arange(rows), jnp.arange(GRID_W), indexing='ij')
    inv = ROPE_THETA ** (-jnp.arange(0, ROPE_AXIS_DIM, 2, dtype=F32) / ROPE_AXIS_DIM)
    ang = jnp.concatenate([r.reshape(-1, 1).astype(F32) * inv, col.reshape(-1, 1).astype(F32) * inv], axis=-1)
    cos, sin = jnp.cos(ang), jnp.sin(ang)
    return jnp.concatenate([cos, cos] * 2, axis=1), jnp.concatenate([-sin, sin] * 2, axis=1)


def _ah_inproj(x, mod, row_of_tile, g, w_in, q_norm, k_norm, conv_w, conv_b, rope, *, tm, tp, lp, ls):
    t = x.shape[0]
    w = w_in[:, :AH_Q + 2 * AH_KV].astype(BF16)
    wut = w_in[:, AH_Q + 2 * AH_KV:].T.astype(BF16)
    gains = jnp.stack([jnp.tile(q_norm.astype(F32), H_A), jnp.tile(k_norm.astype(F32), H_A)])
    gid = jnp.arange(AH_Q) // HD_A
    bdq = (gid[:, None] == gid[None, :]).astype(BF16)
    cwb = jnp.concatenate([conv_w.astype(F32).T, conv_b.astype(F32)[:, None],
                           jnp.zeros((AH_U3, 4), F32)], axis=1)
    n_tiles_p = tp // tm
    const = lambda i: (0, 0)
    bf = lambda n: jax.ShapeDtypeStruct((t, n), BF16)
    joint_shapes = (bf(AH_Q), bf(AH_Q), bf(AH_Q), jax.ShapeDtypeStruct((AH_U3, t), F32),
                    jax.ShapeDtypeStruct((HY_C, t), BF16))
    params = pltpu.CompilerParams(dimension_semantics=("arbitrary",), vmem_limit_bytes=VMEM_LIMIT)

    def call(tile0, n_tiles, latent, extra_in, extra_specs, extra_out, extra_out_specs, aliases):
        prev, nxt = _halo_specs(tm, t, tile0)
        tile = lambda i: (tile0 + i, 0)
        lanes = lambda i: (0, tile0 + i)
        joint_specs = (pl.BlockSpec((tm, AH_Q), tile),) * 3 + (pl.BlockSpec((AH_U3, tm), lanes),
                                                               pl.BlockSpec((HY_C, tm), lanes))
        return pl.pallas_call(
            functools.partial(_ah_inproj_kernel, tm=tm, tile0=tile0, latent=latent, tp=tp, lp=lp, ls=ls),
            out_shape=joint_shapes + extra_out, grid=(n_tiles,),
            in_specs=[prev, pl.BlockSpec((tm, D_MODEL), tile), nxt,
                      pl.BlockSpec((1, 9, D_MODEL), lambda i: (row_of_tile(tile0 + i), 0, 0)),
                      pl.BlockSpec((1, D_MODEL), const), pl.BlockSpec(w.shape, const),
                      pl.BlockSpec(wut.shape, const), pl.BlockSpec(gains.shape, const),
                      pl.BlockSpec(bdq.shape, const), pl.BlockSpec(cwb.shape, const)] + extra_specs,
            out_specs=joint_specs + extra_out_specs, input_output_aliases=aliases, compiler_params=params,
            name="ah_inproj_latent" if latent else "ah_inproj_ctx",
        )(x, x, x, mod, g.reshape(1, D_MODEL), w, wut, gains, bdq, cwb, *extra_in)

    kv_new = jax.ShapeDtypeStruct((tp, AH_KV), F32)
    kv_spec = pl.BlockSpec((tm, AH_KV), lambda i: (i, 0))
    q, kt, vt, ut, vb, kn, vn = call(0, n_tiles_p, False, [], [], (kv_new, kv_new), (kv_spec, kv_spec), {})
    rope_blk = lambda i: (lax.rem(i * tm, ls) // tm, 0)
    q, kt, vt, ut, vb = call(
        n_tiles_p, t // tm - n_tiles_p, True, [rope[0], rope[1], _split_perm(tm), q, kt, vt, ut, vb],
        [pl.BlockSpec((tm, 128), rope_blk)] * 2 + [pl.BlockSpec((tm, tm), const)]
        + [pl.BlockSpec(memory_space=pl.ANY)] * 5, (), (), {13 + j: j for j in range(5)})
    return q, kt, vt, kn, vn, ut, vb


def _attn_chains(chains, sink_ref, valid):
    tq = chains[0][0].shape[0]
    lane = lax.broadcasted_iota(jnp.int32, (tq, G_A * HD_A), 1)
    valid4 = None if valid is None else jnp.concatenate([valid] * G_A, axis=0)
    scores, sinks = [], []
    for q, kt, _, g in chains:
        qs = jnp.concatenate([jnp.where(lane // HD_A == j, q, jnp.zeros_like(q)) for j in range(G_A)], axis=0)
        s = lax.dot_general(qs, kt, (((1,), (1,)), ((), ())), preferred_element_type=F32)
        scores.append(s if valid4 is None else jnp.where(valid4, s, NEG_INF))
        sinks.append(jnp.concatenate([jnp.broadcast_to(sink_ref[g * G_A + j:g * G_A + j + 1, 0:1], (tq, 1))
                                      for j in range(G_A)], axis=0))
    maxes = [jnp.maximum(jnp.max(s, axis=-1, keepdims=True), sk) for s, sk in zip(scores, sinks)]
    probs = [jnp.exp(s - m) for s, m in zip(scores, maxes)]
    dens = [jnp.sum(p, axis=-1, keepdims=True) + jnp.exp(sk - m) for p, sk, m in zip(probs, sinks, maxes)]
    pvs = [jnp.dot(p.astype(BF16), c[2], preferred_element_type=F32) / d
           for p, c, d in zip(probs, chains, dens)]
    outs = []
    for pv in pvs:
        out = pv[:tq]
        for j in range(1, G_A):
            out = jnp.where(lane // HD_A == j, pv[j * tq:(j + 1) * tq], out)
        outs.append(out)
    return outs


def _attn_ctx_kernel(q_ref, kt_ref, vt_ref, sink_ref, o_ref, *, lp):
    w = G_A * HD_A
    where = [(slice(s0, s0 + lp), slice(g * w, (g + 1) * w), g)
             for s0 in range(0, q_ref.shape[0], lp) for g in range(KVH_A)]
    outs = _attn_chains([(q_ref[r, c], kt_ref[r, c], vt_ref[r, c], g) for r, c, g in where], sink_ref, None)
    for (r, c, _), o in zip(where, outs):
        o_ref[r, c] = o.astype(BF16)


def _attn_win_kernel(q_ref, kp_ref, km_ref, kn_ref, vp_ref, vm_ref, vn_ref, ck_ref, cv_ref, sink_ref,
                     prev_ref, o_ref, *, tq, ls):
    del prev_ref
    i = pl.program_id(1)
    n_ctx = ck_ref.shape[1]
    w = G_A * HD_A
    q_pos = i * tq + lax.broadcasted_iota(jnp.int32, (tq, n_ctx + tq + 2 * WINDOW), 0)
    col = lax.broadcasted_iota(jnp.int32, (tq, n_ctx + tq + 2 * WINDOW), 1)
    k_pos = i * tq - WINDOW + (col - n_ctx)
    valid = (col < n_ctx) | ((jnp.abs(q_pos - k_pos) <= WINDOW) & (k_pos >= 0) & (k_pos < ls))
    chains = []
    for g in range(KVH_A):
        cols = slice(g * w, (g + 1) * w)
        kt = jnp.concatenate([ck_ref[0, :, cols], kp_ref[:, cols], km_ref[:, cols], kn_ref[:, cols]], axis=0)
        vt = jnp.concatenate([cv_ref[0, :, cols], vp_ref[:, cols], vm_ref[:, cols], vn_ref[:, cols]], axis=0)
        chains.append((q_ref[:, cols], kt, vt, g))
    for g, o in enumerate(_attn_chains(chains, sink_ref, valid)):
        o_ref[:, g * w:(g + 1) * w] = o.astype(BF16)


def _attention(q, kt, vt, ck, cv, sink, *, tq, tp, lp, ls):
    t = q.shape[0]
    bp, bs = tp // lp, (t - tp) // ls
    sink_rows = jnp.broadcast_to(sink.astype(F32)[:, None], (H_A, 128))
    params = pltpu.CompilerParams(dimension_semantics=("arbitrary",), vmem_limit_bytes=VMEM_LIMIT)
    seq = lambda b: (b, 0)
    nsq = math.gcd(CTX_SEQS, bp)
    a = pl.pallas_call(
        functools.partial(_attn_ctx_kernel, lp=lp),
        out_shape=jax.ShapeDtypeStruct((t, AH_Q), BF16), grid=(bp // nsq,),
        in_specs=[pl.BlockSpec((nsq * lp, AH_Q), seq)] * 3 + [pl.BlockSpec((H_A, 128), lambda b: (0, 0))],
        out_specs=pl.BlockSpec((nsq * lp, AH_Q), seq), compiler_params=params, name="attn_ctx",
    )(q, kt, vt, sink_rows)
    nq = ls // tq
    wb = tq // WINDOW
    n128 = t // WINDOW
    main = lambda b, i: ((tp + b * ls) // tq + i, 0)
    prev = lambda b, i: (jnp.maximum((tp + b * ls) // WINDOW + i * wb - 1, 0), 0)
    nxt = lambda b, i: (jnp.minimum((tp + b * ls) // WINDOW + (i + 1) * wb, n128 - 1), 0)
    ctx = lambda b, i: (b, 0, 0)
    kv_specs = [pl.BlockSpec((WINDOW, AH_Q), prev), pl.BlockSpec((tq, AH_Q), main),
                pl.BlockSpec((WINDOW, AH_Q), nxt)]
    n_ctx = ck.shape[1]
    return pl.pallas_call(
        functools.partial(_attn_win_kernel, tq=tq, ls=ls),
        out_shape=jax.ShapeDtypeStruct((t, AH_Q), BF16), grid=(bs, nq),
        in_specs=[pl.BlockSpec((tq, AH_Q), main)] + kv_specs + kv_specs
        + [pl.BlockSpec((1, n_ctx, AH_Q), ctx)] * 2
        + [pl.BlockSpec((H_A, 128), lambda b, i: (0, 0)), pl.BlockSpec(memory_space=pl.ANY)],
        out_specs=pl.BlockSpec((tq, AH_Q), main),
        input_output_aliases={10: 0},
        compiler_params=pltpu.CompilerParams(dimension_semantics=("arbitrary", "arbitrary"),
                                             vmem_limit_bytes=VMEM_LIMIT),
        name="attn_win",
    )(q, kt, kt, kt, vt, vt, vt, ck, cv, sink_rows, a)


DFT_SUB = 64
DFT_HALF = 256


def _dft_gen_kernel(o_ref, cl_ref, sl_ref, *, length):
    a = pl.program_id(0)
    n2 = 4 * length

    def angles(n_vec, shape):
        k = lax.broadcasted_iota(jnp.int32, shape, 1)
        ph = lax.rem(n_vec * (2 * k + 1), n2)
        return ph.astype(F32) * (2.0 * math.pi / n2)

    @pl.when(a == 0)
    def _():
        th = angles(lax.broadcasted_iota(jnp.int32, (DFT_SUB, length), 0), (DFT_SUB, length))
        cl_ref[...] = jnp.cos(th)
        sl_ref[...] = jnp.sin(th)

    th = angles(jnp.full((8, length), a * DFT_SUB, jnp.int32), (8, length))[0:1]
    ch, sh = jnp.cos(th), jnp.sin(th)
    c = ch * cl_ref[...] - sh * sl_ref[...]
    ns = -(sh * cl_ref[...] + ch * sl_ref[...])
    for jt in range(length // DFT_HALF):
        src = slice(jt * DFT_HALF, (jt + 1) * DFT_HALF)
        o_ref[:, 2 * jt * DFT_HALF:(2 * jt + 1) * DFT_HALF] = c[:, src].astype(BF16)
        o_ref[:, (2 * jt + 1) * DFT_HALF:(2 * jt + 2) * DFT_HALF] = ns[:, src].astype(BF16)


def _dft_matrix(length):
    return pl.pallas_call(
        functools.partial(_dft_gen_kernel, length=length),
        out_shape=jax.ShapeDtypeStruct((length, 2 * length), BF16), grid=(length // DFT_SUB,),
        out_specs=pl.BlockSpec((DFT_SUB, 2 * length), lambda a: (a, 0)),
        scratch_shapes=[pltpu.VMEM((DFT_SUB, length), F32)] * 2,
        compiler_params=pltpu.CompilerParams(dimension_semantics=("arbitrary",),
                                             vmem_limit_bytes=VMEM_LIMIT),
        name="dft_gen",
    )()


HYF_RB = 256


def _hy_filter_kernel(w1_ref, w2_ref, w3f_ref, w3b_ref, cols_ref, f_ref, o_ref, a_ref, b_ref, *, length):
    rc, jt = pl.program_id(0), pl.program_id(1)

    @pl.when(jt == 0)
    def _():
        m = lax.broadcasted_iota(jnp.int32, (8, length), 1).astype(F32)[0:1]
        t = m * (1.0 / (length - 1))
        w = m * (2.0 * math.pi / length)
        band = lax.broadcasted_iota(jnp.int32, (HY_BANDS, 1), 0).astype(F32)
        fb = 1e-4 + band * ((HY_BANDS - 1 - 1e-4) / (HY_BANDS - 1))
        z = jnp.concatenate([t, jnp.cos(fb * w), -jnp.sin(fb * w),
                             jnp.zeros((40 - HY_EMB, length), F32)], axis=0)
        cols = cols_ref[...]
        h = jnp.sin(cols[:, 1:2] * (_bdot(w1_ref[...], z) + cols[:, 0:1]))
        h = jnp.sin(cols[:, 3:4] * (_bdot(w2_ref[...], h) + cols[:, 2:3]))
        n_out = 2 * HY_ORDER * HY_C
        row = (rc * HYF_RB + lax.broadcasted_iota(jnp.int32, (HYF_RB, 1), 0)).astype(F32)
        step = (HY_MAX_DECAY - HY_MIN_DECAY) / (n_out - 1)
        d_f = jnp.abs(HY_MIN_DECAY + row * step)
        d_b = jnp.abs(HY_MIN_DECAY + (row + HY_ORDER * HY_C) * step)
        hf = _bdot(w3f_ref[...], h) * jnp.exp(-t * d_f)
        hb = jnp.where(m == 0.0, 0.0, _bdot(w3b_ref[...], h) * jnp.exp(-t * d_b))
        ssq = jnp.sum(hf * hf, axis=-1, keepdims=True) + jnp.sum(hb * hb, axis=-1, keepdims=True)
        scale = lax.rsqrt(ssq + NORM_EPS) * (1.0 / length)
        a_ref[...] = ((hf + hb) * scale).astype(BF16)
        b_ref[...] = ((hf - hb) * scale).astype(BF16)

    o_ref[:, :DFT_HALF] = jnp.dot(a_ref[...], f_ref[:, :DFT_HALF], preferred_element_type=F32)
    o_ref[:, DFT_HALF:] = jnp.dot(b_ref[...], f_ref[:, DFT_HALF:], preferred_element_type=F32)


def _hy_filter(fmat, w1, b1, f1, w2, b2, f2, w3, *, length):
    w1t = jnp.pad(w1.astype(F32).T, ((0, 0), (0, 40 - HY_EMB)))
    w3t = w3.astype(F32).T
    cols = jnp.stack([b1, f1, b2, f2] + [jnp.zeros_like(b1)] * 4, axis=1).astype(F32)
    n_rows = HY_ORDER * HY_C
    const = lambda rc, jt: (0, 0)
    return pl.pallas_call(
        functools.partial(_hy_filter_kernel, length=length),
        out_shape=jax.ShapeDtypeStruct((n_rows, 2 * length), F32),
        grid=(n_rows // HYF_RB, length // DFT_HALF),
        in_specs=[pl.BlockSpec(w1t.shape, const), pl.BlockSpec((w2.shape[1], w2.shape[0]), const),
                  pl.BlockSpec((HYF_RB, w3t.shape[1]), lambda rc, jt: (rc, 0)),
                  pl.BlockSpec((HYF_RB, w3t.shape[1]), lambda rc, jt: (n_rows // HYF_RB + rc, 0)),
                  pl.BlockSpec(cols.shape, const),
                  pl.BlockSpec((length, 2 * DFT_HALF), lambda rc, jt: (0, jt))],
        out_specs=pl.BlockSpec((HYF_RB, 2 * DFT_HALF), lambda rc, jt: (rc, jt)),
        scratch_shapes=[pltpu.VMEM((HYF_RB, length), BF16)] * 2,
        compiler_params=pltpu.CompilerParams(dimension_semantics=("arbitrary", "arbitrary"),
                                             vmem_limit_bytes=VMEM_LIMIT),
        name="hy_filter",
    )(w1t, w2.astype(F32).T, w3t, w3t, cols, fmat)


HY_SEQS = 8


def _hy_fwd_kernel(z_ref, f_ref, k_ref, y_ref, *, nsq, length):
    kr, ki = k_ref[:, :DFT_HALF], k_ref[:, DFT_HALF:]
    for s in range(nsq):
        zt = jnp.dot(z_ref[:, s * length:(s + 1) * length], f_ref[...], preferred_element_type=F32)
        zr, zi = zt[:, :DFT_HALF], zt[:, DFT_HALF:]
        y_ref[s * HY_C:(s + 1) * HY_C, :DFT_HALF] = (zr * kr - zi * ki).astype(BF16)
        y_ref[s * HY_C:(s + 1) * HY_C, DFT_HALF:] = (zr * ki + zi * kr).astype(BF16)


def _hy_fwd(zb, fmat, kspec, order, *, lane0, n_seq, length):
    nsq = math.gcd(HY_SEQS, n_seq) if lane0 == 0 else 1
    sb0 = lane0 // length
    return pl.pallas_call(
        functools.partial(_hy_fwd_kernel, nsq=nsq, length=length),
        out_shape=jax.ShapeDtypeStruct((n_seq * HY_C, 2 * length), BF16),
        grid=(n_seq // nsq, length // DFT_HALF),
        in_specs=[pl.BlockSpec((HY_C, nsq * length), lambda s, jt: (0, sb0 + s)),
                  pl.BlockSpec((length, 2 * DFT_HALF), lambda s, jt: (0, jt)),
                  pl.BlockSpec((HY_C, 2 * DFT_HALF), lambda s, jt: (order, jt))],
        out_specs=pl.BlockSpec((nsq * HY_C, 2 * DFT_HALF), lambda s, jt: (s, jt)),
        compiler_params=pltpu.CompilerParams(dimension_semantics=("arbitrary", "arbitrary"),
                                             vmem_limit_bytes=VMEM_LIMIT),
        name="hy_fwd",
    )(zb, fmat, kspec)


def _hy_inv_kernel(y_ref, f_ref, gate_ref, z_ref, bias_ref, *rest, with_prev, with_bf16, nsq, tt):
    rest = list(rest)
    if with_prev:
        rest.pop(0)
    o_ref = rest.pop(0)
    ob_ref = rest.pop(0) if with_bf16 else None
    for s in range(nsq):
        lanes = slice(s * tt, (s + 1) * tt)
        y = lax.dot_general(y_ref[s * HY_C:(s + 1) * HY_C, :], f_ref[...], (((1,), (1,)), ((), ())),
                            preferred_element_type=F32)
        out = gate_ref[:, lanes] * (y + z_ref[:, lanes] * bias_ref[...])
        o_ref[:, lanes] = out
        if with_bf16:
            ob_ref[:, lanes] = out.astype(BF16)


def _hy_inv(yspec, fmat, gate_arr, gate_rb, z_arr, z_rb, bias, prev, *, lane0, n_seq, length, tt, t_total,
            with_bf16):
    nt = length // tt
    nsq = math.gcd(HY_SEQS, n_seq) if (nt == 1 and lane0 == 0) else 1
    lane_blk = lambda s, j: lane0 // tt + s * nt + j
    in_specs = [pl.BlockSpec((nsq * HY_C, 2 * length), lambda s, j: (s, 0)),
                pl.BlockSpec((tt, 2 * length), lambda s, j: (j, 0)),
                pl.BlockSpec((HY_C, nsq * tt), lambda s, j: (gate_rb, lane_blk(s, j))),
                pl.BlockSpec((HY_C, nsq * tt), lambda s, j: (z_rb, lane_blk(s, j))),
                pl.BlockSpec((HY_C, 1), lambda s, j: (0, 0))]
    args = [yspec, fmat, gate_arr, z_arr, bias.astype(F32).reshape(HY_C, 1)]
    aliases = {}
    if prev is not None:
        in_specs.append(pl.BlockSpec(memory_space=pl.ANY))
        aliases = {len(args): 0}
        args.append(prev)
    out_spec = pl.BlockSpec((HY_C, nsq * tt), lambda s, j: (0, lane_blk(s, j)))
    out_shape = [jax.ShapeDtypeStruct((HY_C, t_total), F32)]
    out_specs = [out_spec]
    if with_bf16:
        out_shape.append(jax.ShapeDtypeStruct((HY_C, t_total), BF16))
        out_specs.append(out_spec)
    res = pl.pallas_call(
        functools.partial(_hy_inv_kernel, with_prev=prev is not None, with_bf16=with_bf16, nsq=nsq, tt=tt),
        out_shape=tuple(out_shape), grid=(n_seq // nsq, nt), in_specs=in_specs, out_specs=tuple(out_specs),
        input_output_aliases=aliases,
        compiler_params=pltpu.CompilerParams(dimension_semantics=("arbitrary", "arbitrary"),
                                             vmem_limit_bytes=VMEM_LIMIT),
        name="hy_inv",
    )(*args)
    return res


def _hyena_group(ut, vb, fmat, kspec, hy_bias, y_prev, *, lane0, n_seq, length, t_total):
    tt = min(512, length)
    geo = dict(lane0=lane0, n_seq=n_seq, length=length)
    y1 = _hy_fwd(vb, fmat, kspec, 0, **geo)
    z2, z2b = _hy_inv(y1, fmat, ut, 0, ut, 2, hy_bias[0], None, tt=tt, t_total=t_total, with_bf16=True, **geo)
    y2 = _hy_fwd(z2b, fmat, kspec, 1, **geo)
    return _hy_inv(y2, fmat, ut, 1, z2, 0, hy_bias[1], y_prev, tt=tt, t_total=t_total, with_bf16=False,
                   **geo)[0]


HY_TILE = TOKEN_TILE


def _twiddle(jt, n):
    k0 = (jt * DFT_HALF + lax.broadcasted_iota(jnp.int32, (8, DFT_HALF), 1)).astype(F32)[0:1]
    ang = (k0 + 0.5) * (2.0 * math.pi / n)
    return jnp.cos(ang), -jnp.sin(ang)


def _cmul(ar, ai, br, bi):
    return ar * br - ai * bi, ar * bi + ai * br


def _hy_filter2_kernel(w1_ref, w2_ref, w3f_ref, w3b_ref, cols_ref, f_ref, ka_ref, kb_ref, *scr, length):
    rc, jt = pl.program_id(0), pl.program_id(1)
    half = length // 2

    @pl.when(jt == 0)
    def _():
        cols = cols_ref[...]
        n_out = 2 * HY_ORDER * HY_C
        row = (rc * HYF_RB + lax.broadcasted_iota(jnp.int32, (HYF_RB, 1), 0)).astype(F32)
        step = (HY_MAX_DECAY - HY_MIN_DECAY) / (n_out - 1)
        d_f = jnp.abs(HY_MIN_DECAY + row * step)
        d_b = jnp.abs(HY_MIN_DECAY + (row + HY_ORDER * HY_C) * step)
        band = lax.broadcasted_iota(jnp.int32, (HY_BANDS, 1), 0).astype(F32)
        fb = 1e-4 + band * ((HY_BANDS - 1 - 1e-4) / (HY_BANDS - 1))
        taps = []
        for parity in range(2):
            m = 2.0 * lax.broadcasted_iota(jnp.int32, (8, half), 1).astype(F32)[0:1] + parity
            t = m * (1.0 / (length - 1))
            w = m * (2.0 * math.pi / length)
            z = jnp.concatenate([t, jnp.cos(fb * w), -jnp.sin(fb * w),
                                 jnp.zeros((40 - HY_EMB, half), F32)], axis=0)
            h = jnp.sin(cols[:, 1:2] * (_bdot(w1_ref[...], z) + cols[:, 0:1]))
            h = jnp.sin(cols[:, 3:4] * (_bdot(w2_ref[...], h) + cols[:, 2:3]))
            hf = _bdot(w3f_ref[...], h) * jnp.exp(-t * d_f)
            hb = jnp.where(m == 0.0, 0.0, _bdot(w3b_ref[...], h) * jnp.exp(-t * d_b))
            taps.append((hf, hb))
        ssq = sum(jnp.sum(hf * hf, axis=-1, keepdims=True) + jnp.sum(hb * hb, axis=-1, keepdims=True)
                  for hf, hb in taps)
        scale = lax.rsqrt(ssq + NORM_EPS) * (1.0 / length)
        (hf, hb), (hfo, hbo) = taps
        scr[0][...] = ((hf + hb) * scale).astype(BF16)
        scr[1][...] = ((hf - hb) * scale).astype(BF16)
        scr[2][...] = (hfo * scale).astype(BF16)
        scr[3][...] = (hbo * scale).astype(BF16)

    dot = functools.partial(jnp.dot, preferred_element_type=F32)
    fc, fs = f_ref[:, :DFT_HALF], f_ref[:, DFT_HALF:]
    er, ei = dot(scr[0][...], fc), dot(scr[1][...], fs)
    tr, ti = _twiddle(jt, 2 * length)
    fr, fi = _cmul(tr, ti, dot(scr[2][...], fc), dot(scr[2][...], fs))
    gr, gi = _cmul(tr, -ti, dot(scr[3][...], fc), -dot(scr[3][...], fs))
    ka_ref[:, :DFT_HALF] = er + (fr + gr)
    ka_ref[:, DFT_HALF:] = ei + (fi + gi)
    kb_ref[:, :DFT_HALF] = er - (fr + gr)
    kb_ref[:, DFT_HALF:] = ei - (fi + gi)


def _hy_filter2(fhalf, w1, b1, f1, w2, b2, f2, w3, *, length):
    w1t = jnp.pad(w1.astype(F32).T, ((0, 0), (0, 40 - HY_EMB)))
    w3t = w3.astype(F32).T
    cols = jnp.stack([b1, f1, b2, f2] + [jnp.zeros_like(b1)] * 4, axis=1).astype(F32)
    n_rows = HY_ORDER * HY_C
    half = length // 2
    const = lambda rc, jt: (0, 0)
    out = jax.ShapeDtypeStruct((n_rows, length), F32)
    return pl.pallas_call(
        functools.partial(_hy_filter2_kernel, length=length),
        out_shape=(out, out), grid=(n_rows // HYF_RB, half // DFT_HALF),
        in_specs=[pl.BlockSpec(w1t.shape, const), pl.BlockSpec((w2.shape[1], w2.shape[0]), const),
                  pl.BlockSpec((HYF_RB, w3t.shape[1]), lambda rc, jt: (rc, 0)),
                  pl.BlockSpec((HYF_RB, w3t.shape[1]), lambda rc, jt: (n_rows // HYF_RB + rc, 0)),
                  pl.BlockSpec(cols.shape, const),
                  pl.BlockSpec((half, 2 * DFT_HALF), lambda rc, jt: (0, jt))],
        out_specs=(pl.BlockSpec((HYF_RB, 2 * DFT_HALF), lambda rc, jt: (rc, jt)),) * 2,
        scratch_shapes=[pltpu.VMEM((HYF_RB, half), BF16)] * 4,
        compiler_params=pltpu.CompilerParams(dimension_semantics=("arbitrary", "arbitrary"),
                                             vmem_limit_bytes=VMEM_LIMIT),
        name="hy_filter2",
    )(w1t, w2.astype(F32).T, w3t, w3t, cols, fhalf)


def _hy_fwd2_kernel(z_ref, f_ref, ka_ref, kb_ref, p_ref, q_ref, eo_ref, *, length, ft):
    jt = pl.program_id(1)
    half_tile = HY_TILE // 2

    @pl.when(jt == 0)
    def _():
        for j in range(length // HY_TILE):
            dst = slice(j * half_tile, (j + 1) * half_tile)
            eo_ref[:HY_C, dst] = z_ref[:, j * HY_TILE:j * HY_TILE + half_tile]
            eo_ref[HY_C:, dst] = z_ref[:, j * HY_TILE + half_tile:(j + 1) * HY_TILE]

    for u in range(ft):
        re = slice(2 * u * DFT_HALF, (2 * u + 1) * DFT_HALF)
        im = slice((2 * u + 1) * DFT_HALF, (2 * u + 2) * DFT_HALF)
        r = jnp.dot(eo_ref[...], f_ref[:, 2 * u * DFT_HALF:(2 * u + 2) * DFT_HALF],
                    preferred_element_type=F32)
        er, ei = r[:HY_C, :DFT_HALF], r[:HY_C, DFT_HALF:]
        tr, ti = _twiddle(jt * ft + u, 2 * length)
        pr, pi = _cmul(tr, ti, r[HY_C:, :DFT_HALF], r[HY_C:, DFT_HALF:])
        yar, yai = _cmul(er + pr, ei + pi, ka_ref[:, re], ka_ref[:, im])
        ybr, ybi = _cmul(er - pr, ei - pi, kb_ref[:, re], kb_ref[:, im])
        p_ref[:, re] = (yar + ybr).astype(BF16)
        p_ref[:, im] = (yai + ybi).astype(BF16)
        qr, qi = _cmul(yar - ybr, yai - ybi, tr, -ti)
        q_ref[:, re] = qr.astype(BF16)
        q_ref[:, im] = qi.astype(BF16)


HY_FT = 2


def _hy_fwd2(zb, fhalf, kspec, order, *, lane0, n_seq, length):
    sb0 = lane0 // length
    half = length // 2
    ft = math.gcd(HY_FT, half // DFT_HALF)
    width = 2 * DFT_HALF * ft
    out = jax.ShapeDtypeStruct((n_seq * HY_C, length), BF16)
    tile = pl.BlockSpec((HY_C, width), lambda s, jt: (s, jt))
    kblk = pl.BlockSpec((HY_C, width), lambda s, jt: (order, jt))
    return pl.pallas_call(
        functools.partial(_hy_fwd2_kernel, length=length, ft=ft),
        out_shape=(out, out), grid=(n_seq, 2 * half // width),
        in_specs=[pl.BlockSpec((HY_C, length), lambda s, jt: (0, sb0 + s)),
                  pl.BlockSpec((half, width), lambda s, jt: (0, jt)), kblk, kblk],
        out_specs=(tile, tile),
        scratch_shapes=[pltpu.VMEM((2 * HY_C, half), BF16)],
        compiler_params=pltpu.CompilerParams(dimension_semantics=("arbitrary", "arbitrary"),
                                             vmem_limit_bytes=VMEM_LIMIT),
        name="hy_fwd2",
    )(zb, fhalf, kspec[0], kspec[1])


def _hy_inv2_kernel(p_ref, q_ref, f_ref, gate_ref, z_ref, bias_ref, *rest, with_prev, with_bf16):
    rest = list(rest)
    if with_prev:
        rest.pop(0)
    o_ref = rest.pop(0)
    pq = jnp.concatenate([p_ref[...], q_ref[...]], axis=0)
    y = lax.dot_general(pq, f_ref[...], (((1,), (1,)), ((), ())), preferred_element_type=F32)
    y = jnp.concatenate([y[:HY_C], y[HY_C:]], axis=1)
    out = gate_ref[...] * (y + z_ref[...] * bias_ref[...])
    o_ref[...] = out
    if with_bf16:
        rest.pop(0)[...] = out.astype(BF16)


def _hy_inv2(pq, fhalf, gate_arr, gate_rb, z_arr, z_rb, bias, prev, *, lane0, n_seq, length, t_total,
             with_bf16):
    nt = length // HY_TILE
    lane_blk = lambda s, j: lane0 // HY_TILE + s * nt + j
    spec = pl.BlockSpec((HY_C, length), lambda s, j: (s, 0))
    in_specs = [spec, spec, pl.BlockSpec((HY_TILE // 2, length), lambda s, j: (j, 0)),
                pl.BlockSpec((HY_C, HY_TILE), lambda s, j: (gate_rb, lane_blk(s, j))),
                pl.BlockSpec((HY_C, HY_TILE), lambda s, j: (z_rb, lane_blk(s, j))),
                pl.BlockSpec((HY_C, 1), lambda s, j: (0, 0))]
    args = [pq[0], pq[1], fhalf, gate_arr, z_arr, bias.astype(F32).reshape(HY_C, 1)]
    aliases = {}
    if prev is not None:
        in_specs.append(pl.BlockSpec(memory_space=pl.ANY))
        aliases = {len(args): 0}
        args.append(prev)
    out_spec = pl.BlockSpec((HY_C, HY_TILE), lambda s, j: (0, lane_blk(s, j)))
    out_shape = [jax.ShapeDtypeStruct((HY_C, t_total), F32)]
    out_specs = [out_spec]
    if with_bf16:
        out_shape.append(jax.ShapeDtypeStruct((HY_C, t_total), BF16))
        out_specs.append(out_spec)
    return pl.pallas_call(
        functools.partial(_hy_inv2_kernel, with_prev=prev is not None, with_bf16=with_bf16),
        out_shape=tuple(out_shape), grid=(n_seq, nt), in_specs=in_specs, out_specs=tuple(out_specs),
        input_output_aliases=aliases,
        compiler_params=pltpu.CompilerParams(dimension_semantics=("arbitrary", "arbitrary"),
                                             vmem_limit_bytes=VMEM_LIMIT),
        name="hy_inv2",
    )(*args)


def _hyena_group2(ut, vb, fhalf, kspec, hy_bias, y_prev, *, lane0, n_seq, length, t_total):
    geo = dict(lane0=lane0, n_seq=n_seq, length=length)
    pq = _hy_fwd2(vb, fhalf, kspec, 0, **geo)
    z2, z2b = _hy_inv2(pq, fhalf, ut, 0, ut, 2, hy_bias[0], None, t_total=t_total, with_bf16=True, **geo)
    pq = _hy_fwd2(z2b, fhalf, kspec, 1, **geo)
    return _hy_inv2(pq, fhalf, ut, 1, z2, 0, hy_bias[1], y_prev, t_total=t_total, with_bf16=False, **geo)[0]


def _ah_outproj_kernel(x_ref, a_ref, yt_ref, mod_ref, wa_ref, wy_ref, perm_ref, o_ref, *, n_tiles_p):
    o = jnp.dot(a_ref[...], wa_ref[...], preferred_element_type=F32)
    yb = yt_ref[...].astype(BF16)
    nat = jnp.dot(yb, perm_ref[...], preferred_element_type=F32).astype(BF16)
    yb = jnp.where(pl.program_id(0) < n_tiles_p, yb, nat)
    o = o + lax.dot_general(yb, wy_ref[...], (((0,), (0,)), ((), ())), preferred_element_type=F32)
    o_ref[...] = x_ref[...] + mod_ref[0][5:6] * o


def _ah_outproj(x, a, yt, mod, row_of_tile, w_out, *, tm, tp):
    t = x.shape[0]
    tile = lambda i: (i, 0)
    const = lambda i: (0, 0)
    wa, wy = w_out[:AH_Q].astype(BF16), w_out[AH_Q:].astype(BF16)
    perm = _split_perm(tm)
    return pl.pallas_call(
        functools.partial(_ah_outproj_kernel, n_tiles_p=tp // tm),
        out_shape=jax.ShapeDtypeStruct((t, D_MODEL), F32), grid=(t // tm,),
        in_specs=[pl.BlockSpec((tm, D_MODEL), tile), pl.BlockSpec((tm, AH_Q), tile),
                  pl.BlockSpec((HY_C, tm), lambda i: (0, i)),
                  pl.BlockSpec((1, 9, D_MODEL), lambda i: (row_of_tile(i), 0, 0)),
                  pl.BlockSpec(wa.shape, const), pl.BlockSpec(wy.shape, const),
                  pl.BlockSpec((tm, tm), const)],
        out_specs=pl.BlockSpec((tm, D_MODEL), tile),
        compiler_params=pltpu.CompilerParams(dimension_semantics=("arbitrary",),
                                             vmem_limit_bytes=VMEM_LIMIT),
        name="ah_outproj",
    )(x, a, yt, mod, wa, wy, perm)


def _ah_layer(x, mod, row_of_tile, g, w_in, w_out, q_norm, k_norm, sink, conv_w, conv_b, hy_bias,
              ck, cv, rope, fmats, kspecs, *, tm, tq, tp, lp, ls):
    t = x.shape[0]
    q, kt, vt, kn, vn, ut, vb = _ah_inproj(x, mod, row_of_tile, g, w_in, q_norm, k_norm, conv_w, conv_b, rope,
                                           tm=tm, tp=tp, lp=lp, ls=ls)
    a = _attention(q, kt, vt, ck, cv, sink, tq=tq, tp=tp, lp=lp, ls=ls)
    yt = _hyena_group(ut, vb, fmats[0], kspecs[0], hy_bias, None, lane0=0, n_seq=tp // lp, length=lp, t_total=t)
    yt = _hyena_group2(ut, vb, fmats[1], kspecs[1], hy_bias, yt, lane0=tp, n_seq=(t - tp) // ls, length=ls,
                       t_total=t)
    return _ah_outproj(x, a, yt, mod, row_of_tile, w_out, tm=tm, tp=tp), kn[:tp], vn[:tp]


def _rep4_ctx(c):
    b, s = c.shape[:2]
    return jnp.broadcast_to(c[:, :, :, None, :], (b, s, KVH_A, G_A, HD_A)).reshape(b, s, AH_Q).astype(BF16)


def _ada_kernel(c_ref, w_ref, b_ref, o_ref):
    s = jax.nn.silu(c_ref[...]).astype(BF16)
    o_ref[0] = jnp.dot(s, w_ref[0].astype(BF16), preferred_element_type=F32) + b_ref[0]


def _ada_mod(cond, ada_w, ada_b):
    depth, _, n = ada_w.shape
    rows = 16
    cp = jnp.pad(cond.astype(F32), ((0, rows - cond.shape[0]), (0, 0)))
    out = pl.pallas_call(
        _ada_kernel, out_shape=jax.ShapeDtypeStruct((depth, rows, n), F32),
        grid=(depth, n // D_MODEL),
        in_specs=[pl.BlockSpec((rows, D_MODEL), lambda l, j: (0, 0)),
                  pl.BlockSpec((1, D_MODEL, D_MODEL), lambda l, j: (l, 0, j)),
                  pl.BlockSpec((1, 1, D_MODEL), lambda l, j: (l, 0, j))],
        out_specs=pl.BlockSpec((1, rows, D_MODEL), lambda l, j: (l, 0, j)),
        compiler_params=pltpu.CompilerParams(dimension_semantics=("arbitrary", "arbitrary"),
                                             vmem_limit_bytes=VMEM_LIMIT),
        name="ada_mod",
    )(cp, ada_w, ada_b.reshape(depth, 1, n))
    return out.reshape(depth, rows, n // D_MODEL, D_MODEL)


def kernel(x_prompt, x_sample, cache_k, cache_v, state_fwd, state_bwd, c, c_ctx, norm_g, ada_w, ada_b, ffn_w13, ffn_w2, mx_w_in, mx_w_out, q_norm, k_norm, attn_sink, hy_conv_w, hy_conv_b, hy_w1, hy_b1, hy_freq1, hy_w2, hy_b2, hy_freq2, hy_w3, hy_bias, dn_w_in, dn_w_out, dn_conv_w, dn_a_log, dn_dt_bias, dn_norm_g):
    bp, lp, _ = x_prompt.shape
    bs, ls, _ = x_sample.shape
    tp, ts = bp * lp, bs * ls
    assert tp % ls == 0
    tm = math.gcd(TOKEN_TILE, math.gcd(tp, ls))
    ct = math.gcd(DN_BLOCK, math.gcd(lp, ls))
    tq = math.gcd(ATTN_TQ, ls)
    cond = jnp.concatenate([c, c_ctx[None, :]], axis=0)
    mods = _ada_mod(cond, ada_w, ada_b)
    tf = math.gcd(FFN_TILE, math.gcd(tp, ls))
    tiles_p, tiles_pf = tp // tm, tp // tf

    def mod_row(tile):
        return lambda i: jnp.where(i < tp // tile, bs, (i - tp // tile) // (ls // tile))

    row_of_tile, row_f = mod_row(tm), mod_row(tf)

    rope = _rope_tables(ls)
    assert tm == HY_TILE and ls % tm == 0 and (ls // 2) % DFT_HALF == 0
    fmats = (_dft_matrix(lp), _dft_matrix(ls // 2))
    new_k, new_v, new_sf, new_sb = [], [], [], []
    for layer in range(DEPTH):
        mod = mods[layer]
        i = layer // 2
        w_a = (ffn_w13, ffn_w2, layer, 0)
        if layer == 0:
            x = _ffn(x_prompt.reshape(tp, D_MODEL), mod, row_f, norm_g[layer, 0], w_a, 0, tm=tf,
                     out_rows=tp + ts)
            x = _ffn(x_sample.reshape(ts, D_MODEL), mod, row_f, norm_g[layer, 0], w_a, 0, tm=tf,
                     out_rows=tp + ts, out_tile0=tiles_pf, prev=x)
        else:
            x = _ffn(x, mod, row_f, norm_g[layer, 0], w_a, 0, tm=tf)
        if layer % 2 == 0:
            kspecs = tuple(fn(f, hy_w1[i], hy_b1[i], hy_freq1[i], hy_w2[i], hy_b2[i], hy_freq2[i], hy_w3[i],
                              length=n) for fn, f, n in zip((_hy_filter, _hy_filter2), fmats, (lp, ls)))
            x, k_p, v_p = _ah_layer(x, mod, row_of_tile, norm_g[layer, 1], mx_w_in[i], mx_w_out[i], q_norm[i],
                                    k_norm[i], attn_sink[i], hy_conv_w[i], hy_conv_b[i], hy_bias[i],
                                    _rep4_ctx(cache_k[:, i]), _rep4_ctx(cache_v[:, i]), rope, fmats, kspecs,
                                    tm=tm, tq=tq, tp=tp, lp=lp, ls=ls)
            new_k.append(k_p.reshape(bp, lp, KVH_A, HD_A))
            new_v.append(v_p.reshape(bp, lp, KVH_A, HD_A))
        else:
            x, s_f, s_b = _dn_layer(x, mod, row_of_tile, norm_g[layer, 1], dn_w_in[i], dn_w_out[i],
                                    dn_conv_w[i], dn_a_log[i], dn_dt_bias[i], dn_norm_g[i],
                                    state_fwd[:, i], state_bwd[:, i], tm=tm, ct=ct, tp=tp, lp=lp, ls=ls)
            new_sf.append(s_f)
            new_sb.append(s_b)
        w_b = (ffn_w13, ffn_w2, layer, 1)
        if layer < DEPTH - 1:
            x = _ffn(x, mod, row_f, norm_g[layer, 2], w_b, 2, tm=tf)
        else:
            y_p = _ffn(x, mod, row_f, norm_g[layer, 2], w_b, 2, tm=tf, n_tiles=tiles_pf, out_rows=tp)
            y_s = _ffn(x, mod, row_f, norm_g[layer, 2], w_b, 2, tm=tf, in_tile0=tiles_pf,
                       n_tiles=ts // tf, out_rows=ts)

    return (y_p.reshape(bp, lp, D_MODEL), y_s.reshape(bs, ls, D_MODEL),
            jnp.stack(new_k, axis=1), jnp.stack(new_v, axis=1),
            jnp.stack(new_sf, axis=1), jnp.stack(new_sb, axis=1))
```

```python
import functools
import math

import jax
import jax.numpy as jnp
from jax import lax
from jax.experimental import pallas as pl
from jax.experimental.pallas import tpu as pltpu

D_MODEL = 1024
DEPTH = 4
GRID_W = 64
H_A = 8
KVH_A = 2
G_A = H_A // KVH_A
HD_A = 64
WINDOW = 128
ROPE_THETA = 10000.0
ROPE_AXIS_DIM = HD_A // 2
HY_C = 512
HY_ORDER = 2
HY_EMB = 33
HY_BANDS = (HY_EMB - 1) // 2
HY_MIN_DECAY = math.log(1e-2) / 1.5
HY_MAX_DECAY = math.log(1e-2) / 0.3
H_C = 8
DK_C = 128
DV_C = 128
DN_CHUNK = 64
DN_OUT = H_C * DV_C
D_FF = 2816
NORM_EPS = 1e-6
NEG_INF = -1e30

F32 = jnp.float32
BF16 = jnp.bfloat16

TOKEN_TILE = 512
FFN_TILE = 512
FFN_CHUNK = 256
DN_BLOCK = 256
ATTN_TQ = 256
CTX_SEQS = 4
VMEM_LIMIT = 56 * 1024 * 1024


def _ffn_kernel(x_ref, mod_ref, g_ref, w13_ref, w2_ref, *rest, j, n_chunks, has_prev):
    o_ref, acc_ref = rest[1:] if has_prev else rest
    x = x_ref[...]
    y = x * lax.rsqrt(jnp.mean(x * x, axis=-1, keepdims=True) + NORM_EPS) * g_ref[...]
    m = mod_ref[0]
    shift, scale, gate = m[3 * j:3 * j + 1], m[3 * j + 1:3 * j + 2], m[3 * j + 2:3 * j + 3]
    h = (y * (1.0 + scale) + shift).astype(BF16)
    acc_ref[...] = jnp.zeros_like(acc_ref)

    for c in range(n_chunks):
        cols = slice(c * FFN_CHUNK, (c + 1) * FFN_CHUNK)
        gt = jnp.dot(h, w13_ref[:, cols].astype(BF16), preferred_element_type=F32)
        up = jnp.dot(h, w13_ref[:, D_FF + c * FFN_CHUNK:D_FF + (c + 1) * FFN_CHUNK].astype(BF16),
                     preferred_element_type=F32)
        a = (jax.nn.silu(gt) * up).astype(BF16)
        acc_ref[...] += jnp.dot(a, w2_ref[cols, :].astype(BF16), preferred_element_type=F32)
    o_ref[...] = x + 0.5 * gate * acc_ref[...]


def _ffn(x, mod, row_of_tile, g, weights, j, *, tm, in_tile0=0, n_tiles=None, out_rows=None, out_tile0=0,
         prev=None):
    n_tiles = x.shape[0] // tm if n_tiles is None else n_tiles
    out_rows = x.shape[0] if out_rows is None else out_rows
    n_chunks = D_FF // FFN_CHUNK
    joint0 = max(in_tile0, out_tile0)
    w13_all, w2_all, layer, which = weights
    pick = lambda i: (layer, which, 0, 0)
    in_specs = [
        pl.BlockSpec((tm, D_MODEL), lambda i: (in_tile0 + i, 0)),
        pl.BlockSpec((1, 9, D_MODEL), lambda i: (row_of_tile(joint0 + i), 0, 0)),
        pl.BlockSpec((1, D_MODEL), lambda i: (0, 0)),
        pl.BlockSpec((None, None, D_MODEL, 2 * D_FF), pick, pipeline_mode=pl.Buffered(1)),
        pl.BlockSpec((None, None, D_FF, D_MODEL), pick, pipeline_mode=pl.Buffered(1)),
    ]
    args = [x, mod, g.reshape(1, D_MODEL), w13_all, w2_all]
    aliases = {}
    if prev is not None:
        in_specs.append(pl.BlockSpec(memory_space=pl.ANY))
        aliases = {len(args): 0}
        args.append(prev)
    return pl.pallas_call(
        functools.partial(_ffn_kernel, j=j, n_chunks=n_chunks, has_prev=prev is not None),
        out_shape=jax.ShapeDtypeStruct((out_rows, D_MODEL), F32),
        grid=(n_tiles,), in_specs=in_specs,
        out_specs=pl.BlockSpec((tm, D_MODEL), lambda i: (out_tile0 + i, 0)),
        scratch_shapes=[pltpu.VMEM((tm, D_MODEL), F32)],
        input_output_aliases=aliases,
        compiler_params=pltpu.CompilerParams(dimension_semantics=("arbitrary",),
                                             vmem_limit_bytes=VMEM_LIMIT),
        name=f"ffn{j}",
    )(*args)


def _bdot(a, b):
    return jnp.dot(a.astype(BF16), b.astype(BF16), preferred_element_type=F32)


def _bdot_nt(a, b):
    return lax.dot_general(a.astype(BF16), b.astype(BF16), (((1,), (1,)), ((), ())),
                           preferred_element_type=F32)


def _split3(x):
    hi = x.astype(BF16)
    r1 = x - hi.astype(F32)
    mid = r1.astype(BF16)
    lo = (r1 - mid.astype(F32)).astype(BF16)
    return hi, mid, lo


def _ada_h(x, g_row, m, j):
    y = x * lax.rsqrt(jnp.mean(x * x, axis=-1, keepdims=True) + NORM_EPS) * g_row
    return y * (1.0 + m[3 * j + 1:3 * j + 2]) + m[3 * j:3 * j + 1]


def _seq_edges(i, tm, n_tiles_p, tp, lp, ls, shape, axis):
    is_p = i < n_tiles_p
    seq_len = jnp.where(is_p, lp, ls)
    t0 = i * tm - jnp.where(is_p, 0, tp)
    base = lax.rem(t0, seq_len)
    pos = (base + lax.broadcasted_iota(jnp.int32, shape, axis)).astype(F32)
    lf = seq_len.astype(F32)
    rem = pos - jnp.floor((pos + 0.5) / lf) * lf
    return rem == 0.0, rem == lf - 1.0


def _halo_specs(tm, t, tile0=0):
    nb = t // 8
    prev = pl.BlockSpec((8, D_MODEL), lambda i: (jnp.maximum((tile0 + i) * (tm // 8) - 1, 0), 0))
    nxt = pl.BlockSpec((8, D_MODEL), lambda i: (jnp.minimum((tile0 + i + 1) * (tm // 8), nb - 1), 0))
    return prev, nxt


DN_QKV = 2 * H_C * DK_C + H_C * DV_C
DN_CW = 512


def _dn_inproj_kernel(xp_ref, x_ref, xn_ref, mod_ref, g_ref, w_ref, cw_ref, ab_ref,
                      q_ref, k_ref, v_ref, z_ref, bg_ref, pext_ref, *, tm, n_tiles_p, tp, lp, ls):
    i = pl.program_id(0)
    xe = jnp.concatenate([xp_ref[...], x_ref[...], xn_ref[...]], axis=0)
    h = _ada_h(xe, g_ref[...], mod_ref[0], 1).astype(BF16)
    first, last = _seq_edges(i, tm, n_tiles_p, tp, lp, ls, (tm, 1), 0)
    nq = H_C * DK_C
    for c0 in range(0, DN_QKV, DN_CW):
        pext_ref[...] = jnp.dot(h, w_ref[:, c0:c0 + DN_CW], preferred_element_type=F32)
        cw = cw_ref[:, c0:c0 + DN_CW]
        c = (jnp.where(first, 0.0, pext_ref[7:tm + 7, :]) * cw[0:1]
             + pext_ref[8:tm + 8, :] * cw[1:2]
             + jnp.where(last, 0.0, pext_ref[9:tm + 9, :]) * cw[2:3])
        a = jax.nn.silu(c)
        for b0 in range(0, DN_CW, DK_C):
            col = c0 + b0
            blk = a[:, b0:b0 + DK_C]
            if col < 2 * nq:
                blk = blk * lax.rsqrt(jnp.sum(blk * blk, axis=-1, keepdims=True) + NORM_EPS)
            if col < nq:
                q_ref[:, col:col + DK_C] = (blk * (DK_C ** -0.5)).astype(BF16)
            elif col < 2 * nq:
                k_ref[:, col - nq:col - nq + DK_C] = blk.astype(BF16)
            else:
                v_ref[:, col - 2 * nq:col - 2 * nq + DK_C] = blk.astype(BF16)
    hm = h[8:tm + 8]
    for c0 in range(0, DN_OUT, DN_CW):
        z_ref[:, c0:c0 + DN_CW] = jnp.dot(hm, w_ref[:, DN_QKV + c0:DN_QKV + c0 + DN_CW],
                                           preferred_element_type=F32).astype(BF16)
    r = jnp.dot(hm, w_ref[:, DN_QKV + DN_OUT:], preferred_element_type=F32)
    lane = lax.broadcasted_iota(jnp.int32, r.shape, 1)
    xs = r + ab_ref[1:2]
    softplus = jnp.maximum(xs, 0.0) + jnp.log1p(jnp.exp(-jnp.abs(xs)))
    bg_ref[...] = jnp.where(lane < 2 * H_C, jax.nn.sigmoid(r),
                            jnp.where(lane < 4 * H_C, -ab_ref[0:1] * softplus, 0.0))


def _dn_inproj(x, mod, row_of_tile, g, w_in, conv_w, a_log, dt_bias, *, tm, tp, lp, ls):
    t = x.shape[0]
    pad = 128 - 4 * H_C
    w = jnp.pad(w_in, ((0, 0), (0, pad))).astype(BF16)
    ab = jnp.zeros((2, 128), F32)
    ab = ab.at[0, 2 * H_C:4 * H_C].set(jnp.exp(a_log.astype(F32)).reshape(-1))
    ab = ab.at[1, 2 * H_C:4 * H_C].set(dt_bias.astype(F32).reshape(-1))
    prev, nxt = _halo_specs(tm, t)
    tile = lambda i: (i, 0)
    const = lambda i: (0, 0)
    wide = jax.ShapeDtypeStruct((t, DN_OUT), BF16)
    return pl.pallas_call(
        functools.partial(_dn_inproj_kernel, tm=tm, n_tiles_p=tp // tm, tp=tp, lp=lp, ls=ls),
        out_shape=(wide, wide, wide, wide, jax.ShapeDtypeStruct((t, 128), F32)),
        grid=(t // tm,),
        in_specs=[prev, pl.BlockSpec((tm, D_MODEL), tile), nxt,
                  pl.BlockSpec((1, 9, D_MODEL), lambda i: (row_of_tile(i), 0, 0)),
                  pl.BlockSpec((1, D_MODEL), const),
                  pl.BlockSpec(w.shape, const),
                  pl.BlockSpec(conv_w.shape, const),
                  pl.BlockSpec((2, 128), const)],
        out_specs=(pl.BlockSpec((tm, DN_OUT), tile),) * 4 + (pl.BlockSpec((tm, 128), tile),),
        scratch_shapes=[pltpu.VMEM((tm + 16, DN_CW), F32)],
        compiler_params=pltpu.CompilerParams(dimension_semantics=("arbitrary",),
                                             vmem_limit_bytes=VMEM_LIMIT),
        name="dn_inproj",
    )(x, x, x, mod, g.reshape(1, D_MODEL), w, conv_w.astype(F32), ab)


DN_GROUP = 4


def _dn_masks(rev):
    n = DN_CHUNK
    r_i = lax.broadcasted_iota(jnp.int32, (n, n), 0)
    c_i = lax.broadcasted_iota(jnp.int32, (n, n), 1)
    tri = jnp.where((r_i <= c_i) if rev else (r_i >= c_i), 1.0, 0.0).astype(BF16)
    rows = DN_GROUP * n
    rr = lax.broadcasted_iota(jnp.int32, (rows, rows), 0)
    cc = lax.broadcasted_iota(jnp.int32, (rows, rows), 1)
    same = (rr // n) == (cc // n)
    incl = same & ((rr <= cc) if rev else (rr >= cc))
    strict = same & ((rr < cc) if rev else (rr > cc))
    eye = jnp.where(rr == cc, 1.0, 0.0)
    return tri, incl, strict, eye


def _dn_prepare(qc, kc, vc, bgc, masks, rev):
    n = DN_CHUNK
    tri, incl, strict, eye = masks
    gct = sum(jnp.dot(tri, p, preferred_element_type=F32) for p in _split3(bgc))
    rows = DN_GROUP * n
    lane = lax.broadcasted_iota(jnp.int32, (rows, 128), 1)
    last = 0 if rev else n - 1
    prep = []
    for gi in range(H_C // DN_GROUP):
        heads = range(gi * DN_GROUP, (gi + 1) * DN_GROUP)
        sb = [(H_C if rev else 0) + h for h in heads]
        sg = [2 * H_C + (H_C if rev else 0) + h for h in heads]
        beta = jnp.concatenate([bgc[:, s:s + 1] for s in sb], axis=0)
        gc = jnp.concatenate([gct[:, s:s + 1] for s in sg], axis=0)
        gl = jnp.concatenate([jnp.broadcast_to(gct[last:last + 1, s:s + 1], (n, 1)) for s in sg], axis=0)
        q4 = jnp.concatenate([qc[:, h * DK_C:(h + 1) * DK_C] for h in heads], axis=0)
        k4 = jnp.concatenate([kc[:, h * DK_C:(h + 1) * DK_C] for h in heads], axis=0)
        v4 = jnp.concatenate([vc[:, h * DV_C:(h + 1) * DV_C] for h in heads], axis=0)
        hi, mid, lo = (p.astype(F32) for p in _split3(gc))
        one = jnp.where(lane < 6, 1.0, 0.0)
        u_m = jnp.where(lane == 0, hi, jnp.where(lane == 1, mid, jnp.where(lane == 2, lo, one)))
        v_m = jnp.where(lane == 3, -hi, jnp.where(lane == 4, -mid, jnp.where(lane == 5, -lo, one)))
        gd = _bdot_nt(jnp.where(lane < 6, u_m, 0.0), jnp.where(lane < 6, v_m, 0.0))
        decay = jnp.where(incl, jnp.exp(jnp.where(incl, gd, 0.0)), 0.0)
        k4f = k4.astype(F32)
        kb = k4f * beta
        lm = jnp.where(strict, _bdot_nt(kb, k4) * decay, 0.0)
        x = jnp.concatenate([v4.astype(F32) * beta, kb * jnp.exp(gc)], axis=1)
        qe = q4.astype(F32) * jnp.exp(gc)
        ke = (k4f * jnp.exp(gl - gc)).astype(BF16)
        aqk = jnp.where(incl, _bdot_nt(q4, k4) * decay, 0.0).astype(BF16)
        prep.append(dict(lm=lm, x=x, qe=qe, ke=ke, aqk=aqk, egl=jnp.exp(gl), rev=rev))
    return prep


def _dn_solve(groups, eye):
    n = DN_CHUNK
    dot = functools.partial(jnp.dot, preferred_element_type=F32)
    for g in groups:
        g["lh"] = g["lm"].astype(BF16)
        g["p"] = -g["lh"]
        g["t"] = eye - g["lm"]
    def rows(m, lo, hi):
        return jnp.concatenate([m[b * n + lo:b * n + hi] for b in range(DN_GROUP)], axis=0)

    def put(upd, lo, hi):
        k = hi - lo
        parts = []
        for b in range(DN_GROUP):
            parts += [jnp.zeros((lo, upd.shape[1]), upd.dtype), upd[b * k:(b + 1) * k],
                      jnp.zeros((n - hi, upd.shape[1]), upd.dtype)]
        return jnp.concatenate([p for p in parts if p.shape[0]], axis=0)

    for s in range(1, 6):
        w = 2 ** s
        for g in groups:
            g["win"] = ((0, n - w) if g["rev"] else (w, n)) if w >= 16 else None
        for g in groups:
            if g["win"]:
                g["p"] = put(dot(rows(g["p"], *g["win"]), g["p"]).astype(BF16), *g["win"])
            else:
                g["p"] = dot(g["p"], g["p"]).astype(BF16)
        for g in groups:
            tb = g["t"].astype(BF16)
            if g["win"]:
                g["t"] = g["t"] + put(dot(rows(tb, *g["win"]), g["p"]), *g["win"])
            else:
                g["t"] = g["t"] + dot(tb, g["p"])
    for g in groups:
        g["tb"] = g["t"].astype(BF16)
        g["x0"] = dot(g["tb"], g["x"].astype(BF16))
    for g in groups:
        g["res"] = g["x"] - g["x0"] - dot(g["lh"], g["x0"].astype(BF16))
    out = []
    for g in groups:
        x = g["x0"] + dot(g["tb"], g["res"].astype(BF16))
        u4, w4, qe = x[:, :DV_C], x[:, DV_C:], g["qe"]
        wqe = [jnp.concatenate([w4[j * n:(j + 1) * n], qe[j * n:(j + 1) * n]], axis=0).astype(BF16)
               for j in range(DN_GROUP)]
        out.append((u4, wqe, g["ke"], g["aqk"], g["egl"]))
    return out


def _dn_advance(prep, s_ref):
    n = DN_CHUNK
    outs = []
    for gi, (u4, wqe, ke, aqk, egl) in enumerate(prep):
        vn, qs = [], []
        for j in range(DN_GROUP):
            wq = jnp.dot(wqe[j], s_ref[gi * DN_GROUP + j].astype(BF16), preferred_element_type=F32)
            vn.append(u4[j * n:(j + 1) * n] - wq[:n])
            qs.append(wq[n:])
        o4 = jnp.concatenate(qs, axis=0) + jnp.dot(aqk, jnp.concatenate(vn, axis=0).astype(BF16),
                                                   preferred_element_type=F32)
        for j in range(DN_GROUP):
            h = gi * DN_GROUP + j
            sl = slice(j * n, (j + 1) * n)
            s_ref[h] = s_ref[h] * egl[j * n:j * n + 1] + lax.dot_general(
                ke[sl], vn[j].astype(BF16), (((0,), (0,)), ((), ())), preferred_element_type=F32)
            outs.append(o4[sl])
    return outs


def _dn_scan_kernel(*refs, n_chunks, blocks_per_seq, zero_init, has_prev, write_state):
    refs = list(refs)
    fwd_in, bwd_in = refs[:4], refs[4:8]
    del refs[:8]
    s0f_ref, s0b_ref = (None, None) if zero_init else (refs.pop(0), refs.pop(0))
    if has_prev:
        del refs[:2]
    of_ref, ob_ref = refs.pop(0), refs.pop(0)
    sfo_ref, sbo_ref = (refs.pop(0), refs.pop(0)) if write_state else (None, None)
    sf_ref, sb_ref = refs
    i = pl.program_id(0)
    f_in_seq = lax.rem(i, blocks_per_seq)
    r_in_seq = lax.rem(pl.num_programs(0) - 1 - i, blocks_per_seq)

    @pl.when(f_in_seq == 0)
    def _():
        sf_ref[...] = jnp.zeros_like(sf_ref) if zero_init else s0f_ref[0]

    @pl.when(r_in_seq == blocks_per_seq - 1)
    def _():
        sb_ref[...] = jnp.zeros_like(sb_ref) if zero_init else s0b_ref[0]

    def chunk(io, c):
        rows = slice(c * DN_CHUNK, (c + 1) * DN_CHUNK)
        return [r[rows, :] for r in io]

    mf, mb = _dn_masks(False), _dn_masks(True)
    raw = [_dn_prepare(*chunk(fwd_in, c), mf, False) for c in range(n_chunks)]
    raw += [_dn_prepare(*chunk(bwd_in, c), mb, True) for c in range(n_chunks)]
    n_groups = H_C // DN_GROUP
    solved = _dn_solve([g for r in raw for g in r], mf[3])
    per_chunk = [solved[j * n_groups:(j + 1) * n_groups] for j in range(2 * n_chunks)]
    prep_f, prep_b = per_chunk[:n_chunks], per_chunk[n_chunks:]
    for step in range(n_chunks):
        for o_ref, prep, s_ref, c in ((of_ref, prep_f, sf_ref, step),
                                      (ob_ref, prep_b, sb_ref, n_chunks - 1 - step)):
            for h, o in enumerate(_dn_advance(prep[c], s_ref)):
                o_ref[c * DN_CHUNK:(c + 1) * DN_CHUNK, h * DV_C:(h + 1) * DV_C] = o.astype(BF16)

    if write_state:
        @pl.when(f_in_seq == blocks_per_seq - 1)
        def _():
            sfo_ref[0] = sf_ref[...]

        @pl.when(r_in_seq == 0)
        def _():
            sbo_ref[0] = sb_ref[...]


def _dn_scan(q, k, v, bg, s0_f, s0_b, prev, *, row0, n_seq, seq_len, ct, write_state):
    t = q.shape[0]
    bps = seq_len // ct
    nblk = n_seq * bps
    b0 = row0 // ct
    zero_init = s0_f is None
    fwd = lambda i: (b0 + i, 0)
    bwd = lambda i: (b0 + nblk - 1 - i, 0)
    seq_f = lambda i: (i // bps, 0, 0, 0)
    seq_b = lambda i: ((nblk - 1 - i) // bps, 0, 0, 0)
    in_specs, args = [], []
    for rows in (fwd, bwd):
        in_specs += [pl.BlockSpec((ct, DN_OUT), rows)] * 3 + [pl.BlockSpec((ct, 128), rows)]
        args += [q, k, v, bg]
    state_blk = (1, H_C, DK_C, DV_C)
    if not zero_init:
        in_specs += [pl.BlockSpec(state_blk, seq_f), pl.BlockSpec(state_blk, seq_b)]
        args += [s0_f, s0_b]
    aliases = {}
    if prev is not None:
        aliases = {len(args): 0, len(args) + 1: 1}
        in_specs += [pl.BlockSpec(memory_space=pl.ANY)] * 2
        args += list(prev)
    out_shape = [jax.ShapeDtypeStruct((t, DN_OUT), BF16)] * 2
    out_specs = [pl.BlockSpec((ct, DN_OUT), fwd), pl.BlockSpec((ct, DN_OUT), bwd)]
    if write_state:
        out_shape += [jax.ShapeDtypeStruct((n_seq,) + state_blk[1:], F32)] * 2
        out_specs += [pl.BlockSpec(state_blk, seq_f), pl.BlockSpec(state_blk, seq_b)]
    kern = functools.partial(_dn_scan_kernel, n_chunks=ct // DN_CHUNK, blocks_per_seq=bps,
                             zero_init=zero_init, has_prev=prev is not None, write_state=write_state)
    return pl.pallas_call(
        kern, out_shape=tuple(out_shape), grid=(nblk,), in_specs=in_specs, out_specs=tuple(out_specs),
        scratch_shapes=[pltpu.VMEM((H_C, DK_C, DV_C), F32)] * 2,
        input_output_aliases=aliases,
        compiler_params=pltpu.CompilerParams(dimension_semantics=("arbitrary",),
                                             vmem_limit_bytes=VMEM_LIMIT),
        name="dn_scan",
    )(*args)


def _dn_outproj_kernel(x_ref, of_ref, ob_ref, z_ref, mod_ref, ng_ref, w_ref, y_ref):
    o = of_ref[...].astype(F32) + ob_ref[...].astype(F32)
    parts = []
    for h in range(H_C):
        blk = o[:, h * DV_C:(h + 1) * DV_C]
        parts.append(blk * lax.rsqrt(jnp.mean(blk * blk, axis=-1, keepdims=True) + NORM_EPS))
    y = jnp.concatenate(parts, axis=1) * ng_ref[...] * jax.nn.silu(z_ref[...].astype(F32))
    gate = mod_ref[0][5:6]
    y_ref[...] = x_ref[...] + gate * jnp.dot(y.astype(BF16), w_ref[...], preferred_element_type=F32)


def _dn_outproj(x, o_f, o_b, z, mod, row_of_tile, norm_g, w_out, *, tm):
    t = x.shape[0]
    tile = lambda i: (i, 0)
    const = lambda i: (0, 0)
    return pl.pallas_call(
        _dn_outproj_kernel,
        out_shape=jax.ShapeDtypeStruct((t, D_MODEL), F32),
        grid=(t // tm,),
        in_specs=[pl.BlockSpec((tm, D_MODEL), tile), pl.BlockSpec((tm, DN_OUT), tile),
                  pl.BlockSpec((tm, DN_OUT), tile), pl.BlockSpec((tm, DN_OUT), tile),
                  pl.BlockSpec((1, 9, D_MODEL), lambda i: (row_of_tile(i), 0, 0)),
                  pl.BlockSpec((1, DN_OUT), const), pl.BlockSpec((DN_OUT, D_MODEL), const)],
        out_specs=pl.BlockSpec((tm, D_MODEL), tile),
        compiler_params=pltpu.CompilerParams(dimension_semantics=("arbitrary",),
                                             vmem_limit_bytes=VMEM_LIMIT),
        name="dn_outproj",
    )(x, o_f, o_b, z, mod, jnp.tile(norm_g.astype(F32), H_C).reshape(1, DN_OUT), w_out.astype(BF16))


def _dn_layer(x, mod, row_of_tile, g, w_in, w_out, conv_w, a_log, dt_bias, norm_g, s0_f, s0_b,
              *, tm, ct, tp, lp, ls):
    t = x.shape[0]
    q, k, v, z, bg = _dn_inproj(x, mod, row_of_tile, g, w_in, conv_w, a_log, dt_bias, tm=tm, tp=tp, lp=lp, ls=ls)
    bp, bs = tp // lp, (t - tp) // ls
    o_f, o_b, sf, sb = _dn_scan(q, k, v, bg, None, None, None, row0=0, n_seq=bp, seq_len=lp, ct=ct,
                                write_state=True)
    o_f, o_b = _dn_scan(q, k, v, bg, s0_f.astype(F32), s0_b.astype(F32), (o_f, o_b), row0=tp, n_seq=bs,
                        seq_len=ls, ct=ct, write_state=False)
    return _dn_outproj(x, o_f, o_b, z, mod, row_of_tile, norm_g, w_out, tm=tm), sf, sb


AH_Q = H_A * HD_A
AH_KV = KVH_A * HD_A
AH_U3 = 3 * HY_C


def _group_mean_sq(x, bd):
    sq = x * x
    hi = sq.astype(BF16)
    lo = (sq - hi.astype(F32)).astype(BF16)
    return (jnp.dot(hi, bd, preferred_element_type=F32) + jnp.dot(lo, bd, preferred_element_type=F32)) * (1.0 / HD_A)


def _rope_lanes(x, cos_t, sin_t):
    w = x.shape[1]
    lane = lax.broadcasted_iota(jnp.int32, x.shape, 1)
    low = lax.rem(lane, HD_A) < HD_A // 2
    partner = jnp.where(low, pltpu.roll(x, w - HD_A // 2, 1), pltpu.roll(x, HD_A // 2, 1))
    reps = w // cos_t.shape[1]
    return x * jnp.concatenate([cos_t] * reps, axis=1) + partner * jnp.concatenate([sin_t] * reps, axis=1)


def _rep4(x):
    lane = lax.broadcasted_iota(jnp.int32, x.shape, 1)
    sw = pltpu.roll(x, HD_A, 1)
    a = jnp.where(lane < HD_A, x, sw)
    b = jnp.where(lane < HD_A, sw, x)
    return jnp.concatenate([a, a, b, b], axis=1)


def _ah_inproj_kernel(xp_ref, x_ref, xn_ref, mod_ref, g_ref, w_ref, wut_ref, gains_ref, bdq_ref, cwb_ref,
                      *rest, tm, tile0, latent, tp, lp, ls):
    if latent:
        cos_ref, sin_ref, perm_ref = rest[:3]
        q_ref, kt_ref, vt_ref, u_ref, vb_ref = rest[8:]
    else:
        q_ref, kt_ref, vt_ref, u_ref, vb_ref, kn_ref, vn_ref = rest
    i = tile0 + pl.program_id(0)
    n_tiles_p = tp // tm
    xe = jnp.concatenate([xp_ref[...], x_ref[...], xn_ref[...]], axis=0)
    he = _ada_h(xe, g_ref[...], mod_ref[0], 1).astype(BF16)
    hm = he[8:tm + 8]
    p = jnp.dot(hm, w_ref[...], preferred_element_type=F32)
    q = p[:, :AH_Q]
    q = q * lax.rsqrt(_group_mean_sq(q, bdq_ref[...]) + NORM_EPS) * gains_ref[0:1, :]
    k = p[:, AH_Q:AH_Q + AH_KV]
    k = k * lax.rsqrt(_group_mean_sq(k, bdq_ref[:AH_KV, :AH_KV]) + NORM_EPS) * gains_ref[1:2, :AH_KV]
    v = p[:, AH_Q + AH_KV:]
    if latent:
        q = _rope_lanes(q, cos_ref[...], sin_ref[...])
        k = _rope_lanes(k, cos_ref[...], sin_ref[...])
    else:
        kn_ref[...] = k
        vn_ref[...] = v
    q_ref[...] = (q * (HD_A ** -0.5)).astype(BF16)
    kt_ref[...] = _rep4(k).astype(BF16)
    vt_ref[...] = _rep4(v).astype(BF16)
    halo = jnp.concatenate([he[0:8], he[tm + 8:tm + 16], jnp.zeros((112, D_MODEL), BF16)], axis=0)
    nt = (((1,), (1,)), ((), ()))

    def store(r0, c):
        u_ref[r0:r0 + HY_C, :] = c
        if r0 == 2 * HY_C:
            vb_ref[...] = c.astype(BF16)

    if not latent:
        first, last = _seq_edges(i, tm, n_tiles_p, tp, lp, ls, (1, tm), 1)
        lane = lax.broadcasted_iota(jnp.int32, (1, tm), 1)
        for j in range(AH_U3 // HY_C):
            wu = wut_ref[j * HY_C:(j + 1) * HY_C, :]
            u = lax.dot_general(wu, hm, nt, preferred_element_type=F32)
            uh = lax.dot_general(wu, halo, nt, preferred_element_type=F32)
            left = jnp.where(lane == 0, uh[:, 7:8], pltpu.roll(u, 1, 1))
            right = jnp.where(lane == tm - 1, uh[:, 8:9], pltpu.roll(u, tm - 1, 1))
            cwb = cwb_ref[j * HY_C:(j + 1) * HY_C, :]
            store(j * HY_C, jnp.where(first, 0.0, left) * cwb[:, 0:1] + u * cwb[:, 1:2]
                  + jnp.where(last, 0.0, right) * cwb[:, 2:3] + cwb[:, 3:4])

    else:
        half = tm // 2
        hs = jnp.dot(perm_ref[...], hm, preferred_element_type=F32).astype(BF16)
        t0 = i * tm - tp
        seq_start = lax.rem(t0, ls) == 0
        seq_end = lax.rem(t0 + tm, ls) == 0
        lane = lax.broadcasted_iota(jnp.int32, (1, half), 1)
        for j in range(AH_U3 // HY_C):
            wu = wut_ref[j * HY_C:(j + 1) * HY_C, :]
            u = lax.dot_general(wu, hs, nt, preferred_element_type=F32)
            uh = lax.dot_general(wu, halo, nt, preferred_element_type=F32)
            ev, od = u[:, :half], u[:, half:]
            prev_tok = jnp.where(seq_start, 0.0, uh[:, 7:8])
            next_tok = jnp.where(seq_end, 0.0, uh[:, 8:9])
            od_before = jnp.where(lane == 0, prev_tok, pltpu.roll(od, 1, 1))
            ev_after = jnp.where(lane == half - 1, next_tok, pltpu.roll(ev, half - 1, 1))
            cwb = cwb_ref[j * HY_C:(j + 1) * HY_C, :]
            w0, w1, w2, b = cwb[:, 0:1], cwb[:, 1:2], cwb[:, 2:3], cwb[:, 3:4]
            store(j * HY_C, jnp.concatenate([od_before * w0 + ev * w1 + od * w2 + b,
                                             ev * w0 + od * w1 + ev_after * w2 + b], axis=1))


def _split_perm(tm):
    tok = jnp.arange(tm)
    pos = jnp.where(tok % 2 == 0, tok // 2, tm // 2 + tok // 2)
    return (jnp.arange(tm)[:, None] == pos[None, :]).astype(BF16)


def _rope_tables(ls):
    rows = ls // GRID_W
    r, col = jnp.meshgrid(jnp.arange(rows), jnp.arange(GRID_W), indexing='ij')
    inv = ROPE_THETA ** (-jnp.arange(0, ROPE_AXIS_DIM, 2, dtype=F32) / ROPE_AXIS_DIM)
    ang = jnp.concatenate([r.reshape(-1, 1).astype(F32) * inv, col.reshape(-1, 1).astype(F32) * inv], axis=-1)
    cos, sin = jnp.cos(ang), jnp.sin(ang)
    return jnp.concatenate([cos, cos] * 2, axis=1), jnp.concatenate([-sin, sin] * 2, axis=1)


def _ah_inproj(x, mod, row_of_tile, g, w_in, q_norm, k_norm, conv_w, conv_b, rope, *, tm, tp, lp, ls):
    t = x.shape[0]
    w = w_in[:, :AH_Q + 2 * AH_KV].astype(BF16)
    wut = w_in[:, AH_Q + 2 * AH_KV:].T.astype(BF16)
    gains = jnp.stack([jnp.tile(q_norm.astype(F32), H_A), jnp.tile(k_norm.astype(F32), H_A)])
    gid = jnp.arange(AH_Q) // HD_A
    bdq = (gid[:, None] == gid[None, :]).astype(BF16)
    cwb = jnp.concatenate([conv_w.astype(F32).T, conv_b.astype(F32)[:, None],
                           jnp.zeros((AH_U3, 4), F32)], axis=1)
    n_tiles_p = tp // tm
    const = lambda i: (0, 0)
    bf = lambda n: jax.ShapeDtypeStruct((t, n), BF16)
    joint_shapes = (bf(AH_Q), bf(AH_Q), bf(AH_Q), jax.ShapeDtypeStruct((AH_U3, t), F32),
                    jax.ShapeDtypeStruct((HY_C, t), BF16))
    params = pltpu.CompilerParams(dimension_semantics=("arbitrary",), vmem_limit_bytes=VMEM_LIMIT)

    def call(tile0, n_tiles, latent, extra_in, extra_specs, extra_out, extra_out_specs, aliases):
        prev, nxt = _halo_specs(tm, t, tile0)
        tile = lambda i: (tile0 + i, 0)
        lanes = lambda i: (0, tile0 + i)
        joint_specs = (pl.BlockSpec((tm, AH_Q), tile),) * 3 + (pl.BlockSpec((AH_U3, tm), lanes),
                                                               pl.BlockSpec((HY_C, tm), lanes))
        return pl.pallas_call(
            functools.partial(_ah_inproj_kernel, tm=tm, tile0=tile0, latent=latent, tp=tp, lp=lp, ls=ls),
            out_shape=joint_shapes + extra_out, grid=(n_tiles,),
            in_specs=[prev, pl.BlockSpec((tm, D_MODEL), tile), nxt,
                      pl.BlockSpec((1, 9, D_MODEL), lambda i: (row_of_tile(tile0 + i), 0, 0)),
                      pl.BlockSpec((1, D_MODEL), const), pl.BlockSpec(w.shape, const),
                      pl.BlockSpec(wut.shape, const), pl.BlockSpec(gains.shape, const),
                      pl.BlockSpec(bdq.shape, const), pl.BlockSpec(cwb.shape, const)] + extra_specs,
            out_specs=joint_specs + extra_out_specs, input_output_aliases=aliases, compiler_params=params,
            name="ah_inproj_latent" if latent else "ah_inproj_ctx",
        )(x, x, x, mod, g.reshape(1, D_MODEL), w, wut, gains, bdq, cwb, *extra_in)

    kv_new = jax.ShapeDtypeStruct((tp, AH_KV), F32)
    kv_spec = pl.BlockSpec((tm, AH_KV), lambda i: (i, 0))
    q, kt, vt, ut, vb, kn, vn = call(0, n_tiles_p, False, [], [], (kv_new, kv_new), (kv_spec, kv_spec), {})
    rope_blk = lambda i: (lax.rem(i * tm, ls) // tm, 0)
    q, kt, vt, ut, vb = call(
        n_tiles_p, t // tm - n_tiles_p, True, [rope[0], rope[1], _split_perm(tm), q, kt, vt, ut, vb],
        [pl.BlockSpec((tm, 128), rope_blk)] * 2 + [pl.BlockSpec((tm, tm), const)]
        + [pl.BlockSpec(memory_space=pl.ANY)] * 5, (), (), {13 + j: j for j in range(5)})
    return q, kt, vt, kn, vn, ut, vb


def _attn_chains(chains, sink_ref, valid):
    tq = chains[0][0].shape[0]
    lane = lax.broadcasted_iota(jnp.int32, (tq, G_A * HD_A), 1)
    valid4 = None if valid is None else jnp.concatenate([valid] * G_A, axis=0)
    scores, sinks = [], []
    for q, kt, _, g in chains:
        qs = jnp.concatenate([jnp.where(lane // HD_A == j, q, jnp.zeros_like(q)) for j in range(G_A)], axis=0)
        s = lax.dot_general(qs, kt, (((1,), (1,)), ((), ())), preferred_element_type=F32)
        scores.append(s if valid4 is None else jnp.where(valid4, s, NEG_INF))
        sinks.append(jnp.concatenate([jnp.broadcast_to(sink_ref[g * G_A + j:g * G_A + j + 1, 0:1], (tq, 1))
                                      for j in range(G_A)], axis=0))
    maxes = [jnp.maximum(jnp.max(s, axis=-1, keepdims=True), sk) for s, sk in zip(scores, sinks)]
    probs = [jnp.exp(s - m) for s, m in zip(scores, maxes)]
    dens = [jnp.sum(p, axis=-1, keepdims=True) + jnp.exp(sk - m) for p, sk, m in zip(probs, sinks, maxes)]
    pvs = [jnp.dot(p.astype(BF16), c[2], preferred_element_type=F32) / d
           for p, c, d in zip(probs, chains, dens)]
    outs = []
    for pv in pvs:
        out = pv[:tq]
        for j in range(1, G_A):
            out = jnp.where(lane // HD_A == j, pv[j * tq:(j + 1) * tq], out)
        outs.append(out)
    return outs


def _attn_ctx_kernel(q_ref, kt_ref, vt_ref, sink_ref, o_ref, *, lp):
    w = G_A * HD_A
    where = [(slice(s0, s0 + lp), slice(g * w, (g + 1) * w), g)
             for s0 in range(0, q_ref.shape[0], lp) for g in range(KVH_A)]
    outs = _attn_chains([(q_ref[r, c], kt_ref[r, c], vt_ref[r, c], g) for r, c, g in where], sink_ref, None)
    for (r, c, _), o in zip(where, outs):
        o_ref[r, c] = o.astype(BF16)


def _attn_win_kernel(q_ref, kp_ref, km_ref, kn_ref, vp_ref, vm_ref, vn_ref, ck_ref, cv_ref, sink_ref,
                     prev_ref, o_ref, *, tq, ls):
    del prev_ref
    i = pl.program_id(1)
    n_ctx = ck_ref.shape[1]
    w = G_A * HD_A
    q_pos = i * tq + lax.broadcasted_iota(jnp.int32, (tq, n_ctx + tq + 2 * WINDOW), 0)
    col = lax.broadcasted_iota(jnp.int32, (tq, n_ctx + tq + 2 * WINDOW), 1)
    k_pos = i * tq - WINDOW + (col - n_ctx)
    valid = (col < n_ctx) | ((jnp.abs(q_pos - k_pos) <= WINDOW) & (k_pos >= 0) & (k_pos < ls))
    chains = []
    for g in range(KVH_A):
        cols = slice(g * w, (g + 1) * w)
        kt = jnp.concatenate([ck_ref[0, :, cols], kp_ref[:, cols], km_ref[:, cols], kn_ref[:, cols]], axis=0)
        vt = jnp.concatenate([cv_ref[0, :, cols], vp_ref[:, cols], vm_ref[:, cols], vn_ref[:, cols]], axis=0)
        chains.append((q_ref[:, cols], kt, vt, g))
    for g, o in enumerate(_attn_chains(chains, sink_ref, valid)):
        o_ref[:, g * w:(g + 1) * w] = o.astype(BF16)


def _attention(q, kt, vt, ck, cv, sink, *, tq, tp, lp, ls):
    t = q.shape[0]
    bp, bs = tp // lp, (t - tp) // ls
    sink_rows = jnp.broadcast_to(sink.astype(F32)[:, None], (H_A, 128))
    params = pltpu.CompilerParams(dimension_semantics=("arbitrary",), vmem_limit_bytes=VMEM_LIMIT)
    seq = lambda b: (b, 0)
    nsq = math.gcd(CTX_SEQS, bp)
    a = pl.pallas_call(
        functools.partial(_attn_ctx_kernel, lp=lp),
        out_shape=jax.ShapeDtypeStruct((t, AH_Q), BF16), grid=(bp // nsq,),
        in_specs=[pl.BlockSpec((nsq * lp, AH_Q), seq)] * 3 + [pl.BlockSpec((H_A, 128), lambda b: (0, 0))],
        out_specs=pl.BlockSpec((nsq * lp, AH_Q), seq), compiler_params=params, name="attn_ctx",
    )(q, kt, vt, sink_rows)
    nq = ls // tq
    wb = tq // WINDOW
    n128 = t // WINDOW
    main = lambda b, i: ((tp + b * ls) // tq + i, 0)
    prev = lambda b, i: (jnp.maximum((tp + b * ls) // WINDOW + i * wb - 1, 0), 0)
    nxt = lambda b, i: (jnp.minimum((tp + b * ls) // WINDOW + (i + 1) * wb, n128 - 1), 0)
    ctx = lambda b, i: (b, 0, 0)
    kv_specs = [pl.BlockSpec((WINDOW, AH_Q), prev), pl.BlockSpec((tq, AH_Q), main),
                pl.BlockSpec((WINDOW, AH_Q), nxt)]
    n_ctx = ck.shape[1]
    return pl.pallas_call(
        functools.partial(_attn_win_kernel, tq=tq, ls=ls),
        out_shape=jax.ShapeDtypeStruct((t, AH_Q), BF16), grid=(bs, nq),
        in_specs=[pl.BlockSpec((tq, AH_Q), main)] + kv_specs + kv_specs
        + [pl.BlockSpec((1, n_ctx, AH_Q), ctx)] * 2
        + [pl.BlockSpec((H_A, 128), lambda b, i: (0, 0)), pl.BlockSpec(memory_space=pl.ANY)],
        out_specs=pl.BlockSpec((tq, AH_Q), main),
        input_output_aliases={10: 0},
        compiler_params=pltpu.CompilerParams(dimension_semantics=("arbitrary", "arbitrary"),
                                             vmem_limit_bytes=VMEM_LIMIT),
        name="attn_win",
    )(q, kt, kt, kt, vt, vt, vt, ck, cv, sink_rows, a)


DFT_SUB = 64
DFT_HALF = 256


def _dft_gen_kernel(o_ref, cl_ref, sl_ref, *, length):
    a = pl.program_id(0)
    n2 = 4 * length

    def angles(n_vec, shape):
        k = lax.broadcasted_iota(jnp.int32, shape, 1)
        ph = lax.rem(n_vec * (2 * k + 1), n2)
        return ph.astype(F32) * (2.0 * math.pi / n2)

    @pl.when(a == 0)
    def _():
        th = angles(lax.broadcasted_iota(jnp.int32, (DFT_SUB, length), 0), (DFT_SUB, length))
        cl_ref[...] = jnp.cos(th)
        sl_ref[...] = jnp.sin(th)

    th = angles(jnp.full((8, length), a * DFT_SUB, jnp.int32), (8, length))[0:1]
    ch, sh = jnp.cos(th), jnp.sin(th)
    c = ch * cl_ref[...] - sh * sl_ref[...]
    ns = -(sh * cl_ref[...] + ch * sl_ref[...])
    for jt in range(length // DFT_HALF):
        src = slice(jt * DFT_HALF, (jt + 1) * DFT_HALF)
        o_ref[:, 2 * jt * DFT_HALF:(2 * jt + 1) * DFT_HALF] = c[:, src].astype(BF16)
        o_ref[:, (2 * jt + 1) * DFT_HALF:(2 * jt + 2) * DFT_HALF] = ns[:, src].astype(BF16)


def _dft_matrix(length):
    return pl.pallas_call(
        functools.partial(_dft_gen_kernel, length=length),
        out_shape=jax.ShapeDtypeStruct((length, 2 * length), BF16), grid=(length // DFT_SUB,),
        out_specs=pl.BlockSpec((DFT_SUB, 2 * length), lambda a: (a, 0)),
        scratch_shapes=[pltpu.VMEM((DFT_SUB, length), F32)] * 2,
        compiler_params=pltpu.CompilerParams(dimension_semantics=("arbitrary",),
                                             vmem_limit_bytes=VMEM_LIMIT),
        name="dft_gen",
    )()


HYF_RB = 256


def _hy_filter_kernel(w1_ref, w2_ref, w3f_ref, w3b_ref, cols_ref, f_ref, o_ref, a_ref, b_ref, *, length):
    rc, jt = pl.program_id(0), pl.program_id(1)

    @pl.when(jt == 0)
    def _():
        m = lax.broadcasted_iota(jnp.int32, (8, length), 1).astype(F32)[0:1]
        t = m * (1.0 / (length - 1))
        w = m * (2.0 * math.pi / length)
        band = lax.broadcasted_iota(jnp.int32, (HY_BANDS, 1), 0).astype(F32)
        fb = 1e-4 + band * ((HY_BANDS - 1 - 1e-4) / (HY_BANDS - 1))
        z = jnp.concatenate([t, jnp.cos(fb * w), -jnp.sin(fb * w),
                             jnp.zeros((40 - HY_EMB, length), F32)], axis=0)
        cols = cols_ref[...]
        h = jnp.sin(cols[:, 1:2] * (_bdot(w1_ref[...], z) + cols[:, 0:1]))
        h = jnp.sin(cols[:, 3:4] * (_bdot(w2_ref[...], h) + cols[:, 2:3]))
        n_out = 2 * HY_ORDER * HY_C
        row = (rc * HYF_RB + lax.broadcasted_iota(jnp.int32, (HYF_RB, 1), 0)).astype(F32)
        step = (HY_MAX_DECAY - HY_MIN_DECAY) / (n_out - 1)
        d_f = jnp.abs(HY_MIN_DECAY + row * step)
        d_b = jnp.abs(HY_MIN_DECAY + (row + HY_ORDER * HY_C) * step)
        hf = _bdot(w3f_ref[...], h) * jnp.exp(-t * d_f)
        hb = jnp.where(m == 0.0, 0.0, _bdot(w3b_ref[...], h) * jnp.exp(-t * d_b))
        ssq = jnp.sum(hf * hf, axis=-1, keepdims=True) + jnp.sum(hb * hb, axis=-1, keepdims=True)
        scale = lax.rsqrt(ssq + NORM_EPS) * (1.0 / length)
        a_ref[...] = ((hf + hb) * scale).astype(BF16)
        b_ref[...] = ((hf - hb) * scale).astype(BF16)

    o_ref[:, :DFT_HALF] = jnp.dot(a_ref[...], f_ref[:, :DFT_HALF], preferred_element_type=F32)
    o_ref[:, DFT_HALF:] = jnp.dot(b_ref[...], f_ref[:, DFT_HALF:], preferred_element_type=F32)


def _hy_filter(fmat, w1, b1, f1, w2, b2, f2, w3, *, length):
    w1t = jnp.pad(w1.astype(F32).T, ((0, 0), (0, 40 - HY_EMB)))
    w3t = w3.astype(F32).T
    cols = jnp.stack([b1, f1, b2, f2] + [jnp.zeros_like(b1)] * 4, axis=1).astype(F32)
    n_rows = HY_ORDER * HY_C
    const = lambda rc, jt: (0, 0)
    return pl.pallas_call(
        functools.partial(_hy_filter_kernel, length=length),
        out_shape=jax.ShapeDtypeStruct((n_rows, 2 * length), F32),
        grid=(n_rows // HYF_RB, length // DFT_HALF),
        in_specs=[pl.BlockSpec(w1t.shape, const), pl.BlockSpec((w2.shape[1], w2.shape[0]), const),
                  pl.BlockSpec((HYF_RB, w3t.shape[1]), lambda rc, jt: (rc, 0)),
                  pl.BlockSpec((HYF_RB, w3t.shape[1]), lambda rc, jt: (n_rows // HYF_RB + rc, 0)),
                  pl.BlockSpec(cols.shape, const),
                  pl.BlockSpec((length, 2 * DFT_HALF), lambda rc, jt: (0, jt))],
        out_specs=pl.BlockSpec((HYF_RB, 2 * DFT_HALF), lambda rc, jt: (rc, jt)),
        scratch_shapes=[pltpu.VMEM((HYF_RB, length), BF16)] * 2,
        compiler_params=pltpu.CompilerParams(dimension_semantics=("arbitrary", "arbitrary"),
                                             vmem_limit_bytes=VMEM_LIMIT),
        name="hy_filter",
    )(w1t, w2.astype(F32).T, w3t, w3t, cols, fmat)


HY_SEQS = 8


def _hy_fwd_kernel(z_ref, f_ref, k_ref, y_ref, *, nsq, length):
    kr, ki = k_ref[:, :DFT_HALF], k_ref[:, DFT_HALF:]
    for s in range(nsq):
        zt = jnp.dot(z_ref[:, s * length:(s + 1) * length], f_ref[...], preferred_element_type=F32)
        zr, zi = zt[:, :DFT_HALF], zt[:, DFT_HALF:]
        y_ref[s * HY_C:(s + 1) * HY_C, :DFT_HALF] = (zr * kr - zi * ki).astype(BF16)
        y_ref[s * HY_C:(s + 1) * HY_C, DFT_HALF:] = (zr * ki + zi * kr).astype(BF16)


def _hy_fwd(zb, fmat, kspec, order, *, lane0, n_seq, length):
    nsq = math.gcd(HY_SEQS, n_seq) if lane0 == 0 else 1
    sb0 = lane0 // length
    return pl.pallas_call(
        functools.partial(_hy_fwd_kernel, nsq=nsq, length=length),
        out_shape=jax.ShapeDtypeStruct((n_seq * HY_C, 2 * length), BF16),
        grid=(n_seq // nsq, length // DFT_HALF),
        in_specs=[pl.BlockSpec((HY_C, nsq * length), lambda s, jt: (0, sb0 + s)),
                  pl.BlockSpec((length, 2 * DFT_HALF), lambda s, jt: (0, jt)),
                  pl.BlockSpec((HY_C, 2 * DFT_HALF), lambda s, jt: (order, jt))],
        out_specs=pl.BlockSpec((nsq * HY_C, 2 * DFT_HALF), lambda s, jt: (s, jt)),
        compiler_params=pltpu.CompilerParams(dimension_semantics=("arbitrary", "arbitrary"),
                                             vmem_limit_bytes=VMEM_LIMIT),
        name="hy_fwd",
    )(zb, fmat, kspec)


def _hy_inv_kernel(y_ref, f_ref, gate_ref, z_ref, bias_ref, *rest, with_prev, with_bf16, nsq, tt):
    rest = list(rest)
    if with_prev:
        rest.pop(0)
    o_ref = rest.pop(0)
    ob_ref = rest.pop(0) if with_bf16 else None
    for s in range(nsq):
        lanes = slice(s * tt, (s + 1) * tt)
        y = lax.dot_general(y_ref[s * HY_C:(s + 1) * HY_C, :], f_ref[...], (((1,), (1,)), ((), ())),
                            preferred_element_type=F32)
        out = gate_ref[:, lanes] * (y + z_ref[:, lanes] * bias_ref[...])
        o_ref[:, lanes] = out
        if with_bf16:
            ob_ref[:, lanes] = out.astype(BF16)


def _hy_inv(yspec, fmat, gate_arr, gate_rb, z_arr, z_rb, bias, prev, *, lane0, n_seq, length, tt, t_total,
            with_bf16):
    nt = length // tt
    nsq = math.gcd(HY_SEQS, n_seq) if (nt == 1 and lane0 == 0) else 1
    lane_blk = lambda s, j: lane0 // tt + s * nt + j
    in_specs = [pl.BlockSpec((nsq * HY_C, 2 * length), lambda s, j: (s, 0)),
                pl.BlockSpec((tt, 2 * length), lambda s, j: (j, 0)),
                pl.BlockSpec((HY_C, nsq * tt), lambda s, j: (gate_rb, lane_blk(s, j))),
                pl.BlockSpec((HY_C, nsq * tt), lambda s, j: (z_rb, lane_blk(s, j))),
                pl.BlockSpec((HY_C, 1), lambda s, j: (0, 0))]
    args = [yspec, fmat, gate_arr, z_arr, bias.astype(F32).reshape(HY_C, 1)]
    aliases = {}
    if prev is not None:
        in_specs.append(pl.BlockSpec(memory_space=pl.ANY))
        aliases = {len(args): 0}
        args.append(prev)
    out_spec = pl.BlockSpec((HY_C, nsq * tt), lambda s, j: (0, lane_blk(s, j)))
    out_shape = [jax.ShapeDtypeStruct((HY_C, t_total), F32)]
    out_specs = [out_spec]
    if with_bf16:
        out_shape.append(jax.ShapeDtypeStruct((HY_C, t_total), BF16))
        out_specs.append(out_spec)
    res = pl.pallas_call(
        functools.partial(_hy_inv_kernel, with_prev=prev is not None, with_bf16=with_bf16, nsq=nsq, tt=tt),
        out_shape=tuple(out_shape), grid=(n_seq // nsq, nt), in_specs=in_specs, out_specs=tuple(out_specs),
        input_output_aliases=aliases,
        compiler_params=pltpu.CompilerParams(dimension_semantics=("arbitrary", "arbitrary"),
                                             vmem_limit_bytes=VMEM_LIMIT),
        name="hy_inv",
    )(*args)
    return res


def _hyena_group(ut, vb, fmat, kspec, hy_bias, y_prev, *, lane0, n_seq, length, t_total):
    tt = min(512, length)
    geo = dict(lane0=lane0, n_seq=n_seq, length=length)
    y1 = _hy_fwd(vb, fmat, kspec, 0, **geo)
    z2, z2b = _hy_inv(y1, fmat, ut, 0, ut, 2, hy_bias[0], None, tt=tt, t_total=t_total, with_bf16=True, **geo)
    y2 = _hy_fwd(z2b, fmat, kspec, 1, **geo)
    return _hy_inv(y2, fmat, ut, 1, z2, 0, hy_bias[1], y_prev, tt=tt, t_total=t_total, with_bf16=False,
                   **geo)[0]


HY_TILE = TOKEN_TILE


def _twiddle(jt, n):
    k0 = (jt * DFT_HALF + lax.broadcasted_iota(jnp.int32, (8, DFT_HALF), 1)).astype(F32)[0:1]
    ang = (k0 + 0.5) * (2.0 * math.pi / n)
    return jnp.cos(ang), -jnp.sin(ang)


def _cmul(ar, ai, br, bi):
    return ar * br - ai * bi, ar * bi + ai * br


def _hy_filter2_kernel(w1_ref, w2_ref, w3f_ref, w3b_ref, cols_ref, f_ref, ka_ref, kb_ref, *scr, length):
    rc, jt = pl.program_id(0), pl.program_id(1)
    half = length // 2

    @pl.when(jt == 0)
    def _():
        cols = cols_ref[...]
        n_out = 2 * HY_ORDER * HY_C
        row = (rc * HYF_RB + lax.broadcasted_iota(jnp.int32, (HYF_RB, 1), 0)).astype(F32)
        step = (HY_MAX_DECAY - HY_MIN_DECAY) / (n_out - 1)
        d_f = jnp.abs(HY_MIN_DECAY + row * step)
        d_b = jnp.abs(HY_MIN_DECAY + (row + HY_ORDER * HY_C) * step)
        band = lax.broadcasted_iota(jnp.int32, (HY_BANDS, 1), 0).astype(F32)
        fb = 1e-4 + band * ((HY_BANDS - 1 - 1e-4) / (HY_BANDS - 1))
        taps = []
        for parity in range(2):
            m = 2.0 * lax.broadcasted_iota(jnp.int32, (8, half), 1).astype(F32)[0:1] + parity
            t = m * (1.0 / (length - 1))
            w = m * (2.0 * math.pi / length)
            z = jnp.concatenate([t, jnp.cos(fb * w), -jnp.sin(fb * w),
                                 jnp.zeros((40 - HY_EMB, half), F32)], axis=0)
            h = jnp.sin(cols[:, 1:2] * (_bdot(w1_ref[...], z) + cols[:, 0:1]))
            h = jnp.sin(cols[:, 3:4] * (_bdot(w2_ref[...], h) + cols[:, 2:3]))
            hf = _bdot(w3f_ref[...], h) * jnp.exp(-t * d_f)
            hb = jnp.where(m == 0.0, 0.0, _bdot(w3b_ref[...], h) * jnp.exp(-t * d_b))
            taps.append((hf, hb))
        ssq = sum(jnp.sum(hf * hf, axis=-1, keepdims=True) + jnp.sum(hb * hb, axis=-1, keepdims=True)
                  for hf, hb in taps)
        scale = lax.rsqrt(ssq + NORM_EPS) * (1.0 / length)
        (hf, hb), (hfo, hbo) = taps
        scr[0][...] = ((hf + hb) * scale).astype(BF16)
        scr[1][...] = ((hf - hb) * scale).astype(BF16)
        scr[2][...] = (hfo * scale).astype(BF16)
        scr[3][...] = (hbo * scale).astype(BF16)

    dot = functools.partial(jnp.dot, preferred_element_type=F32)
    fc, fs = f_ref[:, :DFT_HALF], f_ref[:, DFT_HALF:]
    er, ei = dot(scr[0][...], fc), dot(scr[1][...], fs)
    tr, ti = _twiddle(jt, 2 * length)
    fr, fi = _cmul(tr, ti, dot(scr[2][...], fc), dot(scr[2][...], fs))
    gr, gi = _cmul(tr, -ti, dot(scr[3][...], fc), -dot(scr[3][...], fs))
    ka_ref[:, :DFT_HALF] = er + (fr + gr)
    ka_ref[:, DFT_HALF:] = ei + (fi + gi)
    kb_ref[:, :DFT_HALF] = er - (fr + gr)
    kb_ref[:, DFT_HALF:] = ei - (fi + gi)


def _hy_filter2(fhalf, w1, b1, f1, w2, b2, f2, w3, *, length):
    w1t = jnp.pad(w1.astype(F32).T, ((0, 0), (0, 40 - HY_EMB)))
    w3t = w3.astype(F32).T
    cols = jnp.stack([b1, f1, b2, f2] + [jnp.zeros_like(b1)] * 4, axis=1).astype(F32)
    n_rows = HY_ORDER * HY_C
    half = length // 2
    const = lambda rc, jt: (0, 0)
    out = jax.ShapeDtypeStruct((n_rows, length), F32)
    return pl.pallas_call(
        functools.partial(_hy_filter2_kernel, length=length),
        out_shape=(out, out), grid=(n_rows // HYF_RB, half // DFT_HALF),
        in_specs=[pl.BlockSpec(w1t.shape, const), pl.BlockSpec((w2.shape[1], w2.shape[0]), const),
                  pl.BlockSpec((HYF_RB, w3t.shape[1]), lambda rc, jt: (rc, 0)),
                  pl.BlockSpec((HYF_RB, w3t.shape[1]), lambda rc, jt: (n_rows // HYF_RB + rc, 0)),
                  pl.BlockSpec(cols.shape, const),
                  pl.BlockSpec((half, 2 * DFT_HALF), lambda rc, jt: (0, jt))],
        out_specs=(pl.BlockSpec((HYF_RB, 2 * DFT_HALF), lambda rc, jt: (rc, jt)),) * 2,
        scratch_shapes=[pltpu.VMEM((HYF_RB, half), BF16)] * 4,
        compiler_params=pltpu.CompilerParams(dimension_semantics=("arbitrary", "arbitrary"),
                                             vmem_limit_bytes=VMEM_LIMIT),
        name="hy_filter2",
    )(w1t, w2.astype(F32).T, w3t, w3t, cols, fhalf)


def _hy_fwd2_kernel(z_ref, f_ref, ka_ref, kb_ref, p_ref, q_ref, eo_ref, *, length, ft):
    jt = pl.program_id(1)
    half_tile = HY_TILE // 2

    @pl.when(jt == 0)
    def _():
        for j in range(length // HY_TILE):
            dst = slice(j * half_tile, (j + 1) * half_tile)
            eo_ref[:HY_C, dst] = z_ref[:, j * HY_TILE:j * HY_TILE + half_tile]
            eo_ref[HY_C:, dst] = z_ref[:, j * HY_TILE + half_tile:(j + 1) * HY_TILE]

    for u in range(ft):
        re = slice(2 * u * DFT_HALF, (2 * u + 1) * DFT_HALF)
        im = slice((2 * u + 1) * DFT_HALF, (2 * u + 2) * DFT_HALF)
        r = jnp.dot(eo_ref[...], f_ref[:, 2 * u * DFT_HALF:(2 * u + 2) * DFT_HALF],
                    preferred_element_type=F32)
        er, ei = r[:HY_C, :DFT_HALF], r[:HY_C, DFT_HALF:]
        tr, ti = _twiddle(jt * ft + u, 2 * length)
        pr, pi = _cmul(tr, ti, r[HY_C:, :DFT_HALF], r[HY_C:, DFT_HALF:])
        yar, yai = _cmul(er + pr, ei + pi, ka_ref[:, re], ka_ref[:, im])
        ybr, ybi = _cmul(er - pr, ei - pi, kb_ref[:, re], kb_ref[:, im])
        p_ref[:, re] = (yar + ybr).astype(BF16)
        p_ref[:, im] = (yai + ybi).astype(BF16)
        qr, qi = _cmul(yar - ybr, yai - ybi, tr, -ti)
        q_ref[:, re] = qr.astype(BF16)
        q_ref[:, im] = qi.astype(BF16)


HY_FT = 2


def _hy_fwd2(zb, fhalf, kspec, order, *, lane0, n_seq, length):
    sb0 = lane0 // length
    half = length // 2
    ft = math.gcd(HY_FT, half // DFT_HALF)
    width = 2 * DFT_HALF * ft
    out = jax.ShapeDtypeStruct((n_seq * HY_C, length), BF16)
    tile = pl.BlockSpec((HY_C, width), lambda s, jt: (s, jt))
    kblk = pl.BlockSpec((HY_C, width), lambda s, jt: (order, jt))
    return pl.pallas_call(
        functools.partial(_hy_fwd2_kernel, length=length, ft=ft),
        out_shape=(out, out), grid=(n_seq, 2 * half // width),
        in_specs=[pl.BlockSpec((HY_C, length), lambda s, jt: (0, sb0 + s)),
                  pl.BlockSpec((half, width), lambda s, jt: (0, jt)), kblk, kblk],
        out_specs=(tile, tile),
        scratch_shapes=[pltpu.VMEM((2 * HY_C, half), BF16)],
        compiler_params=pltpu.CompilerParams(dimension_semantics=("arbitrary", "arbitrary"),
                                             vmem_limit_bytes=VMEM_LIMIT),
        name="hy_fwd2",
    )(zb, fhalf, kspec[0], kspec[1])


def _hy_inv2_kernel(p_ref, q_ref, f_ref, gate_ref, z_ref, bias_ref, *rest, with_prev, with_bf16):
    rest = list(rest)
    if with_prev:
        rest.pop(0)
    o_ref = rest.pop(0)
    pq = jnp.concatenate([p_ref[...], q_ref[...]], axis=0)
    y = lax.dot_general(pq, f_ref[...], (((1,), (1,)), ((), ())), preferred_element_type=F32)
    y = jnp.concatenate([y[:HY_C], y[HY_C:]], axis=1)
    out = gate_ref[...] * (y + z_ref[...] * bias_ref[...])
    o_ref[...] = out
    if with_bf16:
        rest.pop(0)[...] = out.astype(BF16)


def _hy_inv2(pq, fhalf, gate_arr, gate_rb, z_arr, z_rb, bias, prev, *, lane0, n_seq, length, t_total,
             with_bf16):
    nt = length // HY_TILE
    lane_blk = lambda s, j: lane0 // HY_TILE + s * nt + j
    spec = pl.BlockSpec((HY_C, length), lambda s, j: (s, 0))
    in_specs = [spec, spec, pl.BlockSpec((HY_TILE // 2, length), lambda s, j: (j, 0)),
                pl.BlockSpec((HY_C, HY_TILE), lambda s, j: (gate_rb, lane_blk(s, j))),
                pl.BlockSpec((HY_C, HY_TILE), lambda s, j: (z_rb, lane_blk(s, j))),
                pl.BlockSpec((HY_C, 1), lambda s, j: (0, 0))]
    args = [pq[0], pq[1], fhalf, gate_arr, z_arr, bias.astype(F32).reshape(HY_C, 1)]
    aliases = {}
    if prev is not None:
        in_specs.append(pl.BlockSpec(memory_space=pl.ANY))
        aliases = {len(args): 0}
        args.append(prev)
    out_spec = pl.BlockSpec((HY_C, HY_TILE), lambda s, j: (0, lane_blk(s, j)))
    out_shape = [jax.ShapeDtypeStruct((HY_C, t_total), F32)]
    out_specs = [out_spec]
    if with_bf16:
        out_shape.append(jax.ShapeDtypeStruct((HY_C, t_total), BF16))
        out_specs.append(out_spec)
    return pl.pallas_call(
        functools.partial(_hy_inv2_kernel, with_prev=prev is not None, with_bf16=with_bf16),
        out_shape=tuple(out_shape), grid=(n_seq, nt), in_specs=in_specs, out_specs=tuple(out_specs),
        input_output_aliases=aliases,
        compiler_params=pltpu.CompilerParams(dimension_semantics=("arbitrary", "arbitrary"),
                                             vmem_limit_bytes=VMEM_LIMIT),
        name="hy_inv2",
    )(*args)


def _hyena_group2(ut, vb, fhalf, kspec, hy_bias, y_prev, *, lane0, n_seq, length, t_total):
    geo = dict(lane0=lane0, n_seq=n_seq, length=length)
    pq = _hy_fwd2(vb, fhalf, kspec, 0, **geo)
    z2, z2b = _hy_inv2(pq, fhalf, ut, 0, ut, 2, hy_bias[0], None, t_total=t_total, with_bf16=True, **geo)
    pq = _hy_fwd2(z2b, fhalf, kspec, 1, **geo)
    return _hy_inv2(pq, fhalf, ut, 1, z2, 0, hy_bias[1], y_prev, t_total=t_total, with_bf16=False, **geo)[0]


def _ah_outproj_kernel(x_ref, a_ref, yt_ref, mod_ref, wa_ref, wy_ref, perm_ref, o_ref, *, n_tiles_p):
    o = jnp.dot(a_ref[...], wa_ref[...], preferred_element_type=F32)
    yb = yt_ref[...].astype(BF16)
    nat = jnp.dot(yb, perm_ref[...], preferred_element_type=F32).astype(BF16)
    yb = jnp.where(pl.program_id(0) < n_tiles_p, yb, nat)
    o = o + lax.dot_general(yb, wy_ref[...], (((0,), (0,)), ((), ())), preferred_element_type=F32)
    o_ref[...] = x_ref[...] + mod_ref[0][5:6] * o


def _ah_outproj(x, a, yt, mod, row_of_tile, w_out, *, tm, tp):
    t = x.shape[0]
    tile = lambda i: (i, 0)
    const = lambda i: (0, 0)
    wa, wy = w_out[:AH_Q].astype(BF16), w_out[AH_Q:].astype(BF16)
    perm = _split_perm(tm)
    return pl.pallas_call(
        functools.partial(_ah_outproj_kernel, n_tiles_p=tp // tm),
        out_shape=jax.ShapeDtypeStruct((t, D_MODEL), F32), grid=(t // tm,),
        in_specs=[pl.BlockSpec((tm, D_MODEL), tile), pl.BlockSpec((tm, AH_Q), tile),
                  pl.BlockSpec((HY_C, tm), lambda i: (0, i)),
                  pl.BlockSpec((1, 9, D_MODEL), lambda i: (row_of_tile(i), 0, 0)),
                  pl.BlockSpec(wa.shape, const), pl.BlockSpec(wy.shape, const),
                  pl.BlockSpec((tm, tm), const)],
        out_specs=pl.BlockSpec((tm, D_MODEL), tile),
        compiler_params=pltpu.CompilerParams(dimension_semantics=("arbitrary",),
                                             vmem_limit_bytes=VMEM_LIMIT),
        name="ah_outproj",
    )(x, a, yt, mod, wa, wy, perm)


def _ah_layer(x, mod, row_of_tile, g, w_in, w_out, q_norm, k_norm, sink, conv_w, conv_b, hy_bias,
              ck, cv, rope, fmats, kspecs, *, tm, tq, tp, lp, ls):
    t = x.shape[0]
    q, kt, vt, kn, vn, ut, vb = _ah_inproj(x, mod, row_of_tile, g, w_in, q_norm, k_norm, conv_w, conv_b, rope,
                                           tm=tm, tp=tp, lp=lp, ls=ls)
    a = _attention(q, kt, vt, ck, cv, sink, tq=tq, tp=tp, lp=lp, ls=ls)
    yt = _hyena_group(ut, vb, fmats[0], kspecs[0], hy_bias, None, lane0=0, n_seq=tp // lp, length=lp, t_total=t)
    yt = _hyena_group2(ut, vb, fmats[1], kspecs[1], hy_bias, yt, lane0=tp, n_seq=(t - tp) // ls, length=ls,
                       t_total=t)
    return _ah_outproj(x, a, yt, mod, row_of_tile, w_out, tm=tm, tp=tp), kn[:tp], vn[:tp]


def _rep4_ctx(c):
    b, s = c.shape[:2]
    return jnp.broadcast_to(c[:, :, :, None, :], (b, s, KVH_A, G_A, HD_A)).reshape(b, s, AH_Q).astype(BF16)


def _ada_kernel(c_ref, w_ref, b_ref, o_ref):
    s = jax.nn.silu(c_ref[...]).astype(BF16)
    o_ref[0] = jnp.dot(s, w_ref[0].astype(BF16), preferred_element_type=F32) + b_ref[0]


def _ada_mod(cond, ada_w, ada_b):
    depth, _, n = ada_w.shape
    rows = 16
    cp = jnp.pad(cond.astype(F32), ((0, rows - cond.shape[0]), (0, 0)))
    out = pl.pallas_call(
        _ada_kernel, out_shape=jax.ShapeDtypeStruct((depth, rows, n), F32),
        grid=(depth, n // D_MODEL),
        in_specs=[pl.BlockSpec((rows, D_MODEL), lambda l, j: (0, 0)),
                  pl.BlockSpec((1, D_MODEL, D_MODEL), lambda l, j: (l, 0, j)),
                  pl.BlockSpec((1, 1, D_MODEL), lambda l, j: (l, 0, j))],
        out_specs=pl.BlockSpec((1, rows, D_MODEL), lambda l, j: (l, 0, j)),
        compiler_params=pltpu.CompilerParams(dimension_semantics=("arbitrary", "arbitrary"),
                                             vmem_limit_bytes=VMEM_LIMIT),
        name="ada_mod",
    )(cp, ada_w, ada_b.reshape(depth, 1, n))
    return out.reshape(depth, rows, n // D_MODEL, D_MODEL)


def kernel(x_prompt, x_sample, cache_k, cache_v, state_fwd, state_bwd, c, c_ctx, norm_g, ada_w, ada_b, ffn_w13, ffn_w2, mx_w_in, mx_w_out, q_norm, k_norm, attn_sink, hy_conv_w, hy_conv_b, hy_w1, hy_b1, hy_freq1, hy_w2, hy_b2, hy_freq2, hy_w3, hy_bias, dn_w_in, dn_w_out, dn_conv_w, dn_a_log, dn_dt_bias, dn_norm_g):
    bp, lp, _ = x_prompt.shape
    bs, ls, _ = x_sample.shape
    tp, ts = bp * lp, bs * ls
    assert tp % ls == 0
    tm = math.gcd(TOKEN_TILE, math.gcd(tp, ls))
    ct = math.gcd(DN_BLOCK, math.gcd(lp, ls))
    tq = math.gcd(ATTN_TQ, ls)
    cond = jnp.concatenate([c, c_ctx[None, :]], axis=0)
    mods = _ada_mod(cond, ada_w, ada_b)
    tf = math.gcd(FFN_TILE, math.gcd(tp, ls))
    tiles_p, tiles_pf = tp // tm, tp // tf

    def mod_row(tile):
        return lambda i: jnp.where(i < tp // tile, bs, (i - tp // tile) // (ls // tile))

    row_of_tile, row_f = mod_row(tm), mod_row(tf)

    rope = _rope_tables(ls)
    assert tm == HY_TILE and ls % tm == 0 and (ls // 2) % DFT_HALF == 0
    fmats = (_dft_matrix(lp), _dft_matrix(ls // 2))
    new_k, new_v, new_sf, new_sb = [], [], [], []
    for layer in range(DEPTH):
        mod = mods[layer]
        i = layer // 2
        w_a = (ffn_w13, ffn_w2, layer, 0)
        if layer == 0:
            x = _ffn(x_prompt.reshape(tp, D_MODEL), mod, row_f, norm_g[layer, 0], w_a, 0, tm=tf,
                     out_rows=tp + ts)
            x = _ffn(x_sample.reshape(ts, D_MODEL), mod, row_f, norm_g[layer, 0], w_a, 0, tm=tf,
                     out_rows=tp + ts, out_tile0=tiles_pf, prev=x)
        else:
            x = _ffn(x, mod, row_f, norm_g[layer, 0], w_a, 0, tm=tf)
        if layer % 2 == 0:
            kspecs = tuple(fn(f, hy_w1[i], hy_b1[i], hy_freq1[i], hy_w2[i], hy_b2[i], hy_freq2[i], hy_w3[i],
                              length=n) for fn, f, n in zip((_hy_filter, _hy_filter2), fmats, (lp, ls)))
            x, k_p, v_p = _ah_layer(x, mod, row_of_tile, norm_g[layer, 1], mx_w_in[i], mx_w_out[i], q_norm[i],
                                    k_norm[i], attn_sink[i], hy_conv_w[i], hy_conv_b[i], hy_bias[i],
                                    _rep4_ctx(cache_k[:, i]), _rep4_ctx(cache_v[:, i]), rope, fmats, kspecs,
                                    tm=tm, tq=tq, tp=tp, lp=lp, ls=ls)
            new_k.append(k_p.reshape(bp, lp, KVH_A, HD_A))
            new_v.append(v_p.reshape(bp, lp, KVH_A, HD_A))
        else:
            x, s_f, s_b = _dn_layer(x, mod, row_of_tile, norm_g[layer, 1], dn_w_in[i], dn_w_out[i],
                                    dn_conv_w[i], dn_a_log[i], dn_dt_bias[i], dn_norm_g[i],
                                    state_fwd[:, i], state_bwd[:, i], tm=tm, ct=ct, tp=tp, lp=lp, ls=ls)
            new_sf.append(s_f)
            new_sb.append(s_b)
        w_b = (ffn_w13, ffn_w2, layer, 1)
        if layer < DEPTH - 1:
            x = _ffn(x, mod, row_f, norm_g[layer, 2], w_b, 2, tm=tf)
        else:
            y_p = _ffn(x, mod, row_f, norm_g[layer, 2], w_b, 2, tm=tf, n_tiles=tiles_pf, out_rows=tp)
            y_s = _ffn(x, mod, row_f, norm_g[layer, 2], w_b, 2, tm=tf, in_tile0=tiles_pf,
                       n_tiles=ts // tf, out_rows=ts)

    return (y_p.reshape(bp, lp, D_MODEL), y_s.reshape(bs, ls, D_MODEL),
            jnp.stack(new_k, axis=1), jnp.stack(new_v, axis=1),
            jnp.stack(new_sf, axis=1), jnp.stack(new_sb, axis=1))
```
